```python
import math
import numpy as np
import jax
import jax.numpy as jnp
from jax import lax

D_MODEL = 1024
BATCH = 2
SEQ = 16384
DEPTH = 4

HEAD_DIM = 64
NSA_HEADS = 4
NSA_KV_HEADS = 1
CMP_BLOCK = 32
CMP_STRIDE = 16
SEL_BLOCK = 64
SEL_TOPK = 16
NSA_WINDOW = 512
DIL_PATTERNS = ((128, 1), (512, 4), (2048, 16))
DIL_HEADS_PER_GROUP = 2
DIL_HEADS = DIL_HEADS_PER_GROUP * len(DIL_PATTERNS)
SWA_HEADS = 6
SWA_KV_HEADS = 2
SWA_WINDOW = 128
BLOCK_Q = 128
D_FF = 2816
CONV_WIDTH = 3
ROPE_THETA = 10000.0
RMS_EPS = 1e-6
NEG = -1e30
TINY = 1e-30
FORCE = 1e4
COL_SIZES = ((NSA_HEADS * HEAD_DIM,) + (NSA_KV_HEADS * HEAD_DIM,) * 6 + (3 * NSA_HEADS,)
             + (DIL_HEADS * HEAD_DIM,) * 3
             + (SWA_HEADS * HEAD_DIM, SWA_KV_HEADS * HEAD_DIM, SWA_KV_HEADS * HEAD_DIM))
IN_COLS = sum(COL_SIZES)
MIX_WIDTH = (NSA_HEADS + DIL_HEADS + SWA_HEADS) * HEAD_DIM

kernel_name = 'hybrid_nsa_dilated_swa_convffn_trunk'


def _rmsnorm(x, w):
    xf = x.astype(jnp.float32)
    y = xf * lax.rsqrt(jnp.mean(xf * xf, axis=-1, keepdims=True) + RMS_EPS)
    return (y * w.astype(jnp.float32)).astype(x.dtype)


def _rope_tables(positions, dtype):
    inv = ROPE_THETA ** (-jnp.arange(0, HEAD_DIM, 2, dtype=jnp.float32) / HEAD_DIM)
    ang = positions.astype(jnp.float32)[..., None] * inv
    return (jnp.cos(ang)[:, :, None, :].astype(dtype),
            jnp.sin(ang)[:, :, None, :].astype(dtype))


def _apply_rope(t, cos, sin):
    t1, t2 = jnp.split(t, 2, axis=-1)
    return jnp.concatenate([t1 * cos - t2 * sin, t2 * cos + t1 * sin], axis=-1)


def _masked_softmax(s, mask):
    s = jnp.where(mask, s, NEG)
    m = jnp.max(s, axis=-1, keepdims=True)
    p = jnp.where(mask, jnp.exp(s - m), 0.0)
    return p / jnp.maximum(jnp.sum(p, axis=-1, keepdims=True), TINY)


def _band_blocks(t, n_prev, block):
    B, S = t.shape[:2]
    nb = S // block
    tb = t.reshape((B, nb, block) + t.shape[2:])
    tp = jnp.pad(tb, [(0, 0), (n_prev, 0)] + [(0, 0)] * (tb.ndim - 2))
    return jnp.concatenate([tp[:, i:i + nb] for i in range(n_prev + 1)], axis=2)


def _banded_attention(q, k, v, max_dist, block, sinks=None, return_lse=False):
    B, S, Hk, G, hd = q.shape
    block = math.gcd(block, S)
    nb = S // block
    n_prev = -(-max_dist // block)
    span = (n_prev + 1) * block
    kw = _band_blocks(k, n_prev, block)
    vw = _band_blocks(v, n_prev, block)
    qb = q.reshape(B, nb, block, Hk, G, hd)
    s = jnp.einsum('bnqhgd,bnkhd->bnhgqk', qb, kw).astype(jnp.float32) * (hd ** -0.5)
    dist = jnp.arange(block)[:, None] + n_prev * block - jnp.arange(span)[None, :]
    kpos = jnp.arange(nb)[:, None] * block - n_prev * block + jnp.arange(span)[None, :]
    mask = ((dist >= 0) & (dist <= max_dist))[None] & (kpos >= 0)[:, None, :]
    mask = mask[None, :, None, None]
    s = jnp.where(mask, s, NEG)
    m = jnp.max(s, axis=-1, keepdims=True)
    if sinks is not None:
        sk = sinks.astype(jnp.float32).reshape(1, 1, Hk, G, 1, 1)
        m = jnp.maximum(m, sk)
    p = jnp.where(mask, jnp.exp(s - m), 0.0)
    l = jnp.sum(p, axis=-1, keepdims=True)
    if sinks is not None:
        l = l + jnp.exp(sk - m)
    o = jnp.einsum('bnhgqk,bnkhd->bnqhgd', (p / l).astype(v.dtype), vw).reshape(B, S, Hk, G, hd)
    if not return_lse:
        return o
    lse = (m + jnp.log(l))[..., 0].transpose(0, 1, 4, 2, 3).reshape(B, S, Hk, G)
    return o, lse


def _compress(t, w, pe):
    B, S, Hk, hd = t.shape
    r = CMP_BLOCK // CMP_STRIDE
    n = S // CMP_STRIDE
    ts = t.reshape(B, n, CMP_STRIDE, Hk, hd)
    blocks = jnp.concatenate([ts[:, i:n - r + 1 + i] for i in range(r)], axis=2)
    return jnp.einsum('bnlhd,lde->bnhe', blocks + pe[:, None, :], w)


def _nsa(q, k_c, v_c, k_s, v_s, k_w, v_w, gates, w_ck, w_cv, pe_k, pe_v):
    B, S, Hk, G, hd = q.shape
    kc = _compress(k_c, w_ck, pe_k)
    vc = _compress(v_c, w_cv, pe_v)
    n_cmp = kc.shape[1]
    n_sel = S // SEL_BLOCK
    top = min(SEL_TOPK, n_sel)
    cmp_end = jnp.arange(n_cmp) * CMP_STRIDE + CMP_BLOCK - 1
    ci = jnp.arange(n_cmp)[:, None] * CMP_STRIDE
    sj = jnp.arange(n_sel)[None, :] * SEL_BLOCK
    overlap = ((ci < sj + SEL_BLOCK) & (ci + CMP_BLOCK > sj)).astype(jnp.float32)
    ks_blk = k_s.reshape(B, n_sel, SEL_BLOCK, Hk, hd).transpose(0, 3, 1, 2, 4)
    vs_blk = v_s.reshape(B, n_sel, SEL_BLOCK, Hk, hd).transpose(0, 3, 1, 2, 4)
    nb = S // BLOCK_Q
    q_blocks = q.reshape(B, nb, BLOCK_Q, Hk, G, hd).transpose(1, 0, 2, 3, 4, 5)
    scale = hd ** -0.5
    gather = jax.vmap(jax.vmap(lambda blk, ids: blk[ids]))

    def per_block(args):
        qb, b = args
        t = b * BLOCK_Q + jnp.arange(BLOCK_Q)
        s = jnp.einsum('bqhgd,bnhd->bhgqn', qb, kc).astype(jnp.float32) * scale
        p = _masked_softmax(s, cmp_end[None, :] <= t[:, None])
        o_cmp = jnp.einsum('bhgqn,bnhd->bqhgd', p.astype(vc.dtype), vc)
        imp = jnp.einsum('bhgqn,nj->bhqj', p, overlap)
        blk = jnp.arange(n_sel)[None, :]
        cur = (t // SEL_BLOCK)[:, None]
        forced = (blk == 0) | (blk == cur) | (blk == cur - 1)
        imp = jnp.where(forced, FORCE, imp)
        imp = jnp.where(blk <= cur, imp, NEG)
        val, idx = lax.top_k(imp, top)
        ks = gather(ks_blk, idx)
        vs = gather(vs_blk, idx)
        s2 = jnp.einsum('bqhgd,bhqkld->bhgqkl', qb, ks).astype(jnp.float32) * scale
        kpos = idx[..., None] * SEL_BLOCK + jnp.arange(SEL_BLOCK)
        m2 = (val[..., None] > NEG / 2) & (kpos <= t[:, None, None])
        n_k = top * SEL_BLOCK
        p2 = _masked_softmax(s2.reshape(B, Hk, G, BLOCK_Q, n_k), m2.reshape(B, Hk, 1, BLOCK_Q, n_k))
        o_sel = jnp.einsum('bhgqn,bhqnd->bqhgd', p2.astype(vs.dtype), vs.reshape(B, Hk, BLOCK_Q, n_k, hd))
        return o_cmp, o_sel

    o_cmp, o_sel = lax.map(per_block, (q_blocks, jnp.arange(nb)))
    o_cmp = o_cmp.transpose(1, 0, 2, 3, 4, 5).reshape(B, S, Hk, G, hd)
    o_sel = o_sel.transpose(1, 0, 2, 3, 4, 5).reshape(B, S, Hk, G, hd)
    o_win = _banded_attention(q, k_w, v_w, NSA_WINDOW - 1, BLOCK_Q)
    return gates[..., 0:1] * o_cmp + gates[..., 1:2] * o_sel + gates[..., 2:3] * o_win


def _dilated_group(q, k, v, window, dilation):
    B, S, H, hd = q.shape
    n = S // dilation

    def fold(t):
        return t.reshape(B, n, dilation, H, hd).transpose(0, 2, 1, 3, 4).reshape(B * dilation, n, H, hd)

    o, lse = _banded_attention(fold(q)[:, :, :, None], fold(k), fold(v), window // dilation, BLOCK_Q,
                               return_lse=True)
    o = o[:, :, :, 0].reshape(B, dilation, n, H, hd).transpose(0, 2, 1, 3, 4).reshape(B, S, H, hd)
    lse = lse[..., 0].reshape(B, dilation, n, H).transpose(0, 2, 1, 3).reshape(B, S, H)
    return o, lse


def _token_mixers(h, cos, sin, w_in, w_out, w_ck, w_cv, pe_k, pe_v, sinks):
    B, S, _ = h.shape
    proj = h @ w_in
    split_at = np.cumsum(COL_SIZES)[:-1].tolist()
    (a_q, a_kc, a_vc, a_ks, a_vs, a_kw, a_vw, a_g,
     b_q, b_k, b_v, c_q, c_k, c_v) = jnp.split(proj, split_at, axis=-1)

    def heads(t, n):
        return t.reshape(B, S, n, HEAD_DIM)

    def rope(t):
        return _apply_rope(t, cos, sin)

    ga = NSA_HEADS // NSA_KV_HEADS
    qa = rope(heads(a_q, NSA_HEADS)).reshape(B, S, NSA_KV_HEADS, ga, HEAD_DIM)
    gates = jax.nn.sigmoid(a_g).reshape(B, S, NSA_KV_HEADS, ga, 3)
    out_a = _nsa(qa,
                 rope(heads(a_kc, NSA_KV_HEADS)), heads(a_vc, NSA_KV_HEADS),
                 rope(heads(a_ks, NSA_KV_HEADS)), heads(a_vs, NSA_KV_HEADS),
                 rope(heads(a_kw, NSA_KV_HEADS)), heads(a_vw, NSA_KV_HEADS),
                 gates, w_ck, w_cv, pe_k, pe_v).reshape(B, S, NSA_HEADS * HEAD_DIM)

    qb, kb, vb = rope(heads(b_q, DIL_HEADS)), rope(heads(b_k, DIL_HEADS)), heads(b_v, DIL_HEADS)
    outs, lses = [], []
    for g, (win, dil) in enumerate(DIL_PATTERNS):
        sl = slice(g * DIL_HEADS_PER_GROUP, (g + 1) * DIL_HEADS_PER_GROUP)
        o, lse = _dilated_group(qb[:, :, sl], kb[:, :, sl], vb[:, :, sl], win, dil)
        outs.append(o)
        lses.append(lse)
    alpha = jax.nn.softmax(jnp.stack(lses, axis=0), axis=0)
    out_b = jnp.concatenate([o * alpha[g][..., None].astype(o.dtype) for g, o in enumerate(outs)],
                            axis=2).reshape(B, S, DIL_HEADS * HEAD_DIM)

    gc = SWA_HEADS // SWA_KV_HEADS
    qc = rope(heads(c_q, SWA_HEADS)).reshape(B, S, SWA_KV_HEADS, gc, HEAD_DIM)
    out_c = _banded_attention(qc, rope(heads(c_k, SWA_KV_HEADS)), heads(c_v, SWA_KV_HEADS),
                              SWA_WINDOW - 1, BLOCK_Q,
                              sinks=sinks.reshape(SWA_KV_HEADS, gc)).reshape(B, S, SWA_HEADS * HEAD_DIM)

    return jnp.concatenate([out_a, out_b, out_c], axis=-1) @ w_out


def _conv_ffn(h, w_gate, w_up, conv_w, conv_b, w_down):
    S = h.shape[1]
    g = h @ w_gate
    gp = jnp.pad(g, ((0, 0), (CONV_WIDTH - 1, 0), (0, 0)))
    acc = gp[:, 0:S] * conv_w[0]
    for i in range(1, CONV_WIDTH):
        acc = acc + gp[:, i:i + S] * conv_w[i]
    return (jax.nn.gelu(acc + conv_b, approximate=True) * (h @ w_up)) @ w_down


def setup_inputs(seed: int = 0) -> dict:
    key = jax.random.key(seed)
    ks = jax.random.split(key, 20)
    f32 = jnp.float32

    def nrm(k, shape, scale):
        return jax.random.normal(k, shape, f32) * scale

    x = nrm(ks[0], (BATCH, SEQ, D_MODEL), 1.0)
    c = nrm(ks[1], (BATCH, D_MODEL), 1.0)
    offset = jax.random.randint(ks[2], (BATCH, 1), 0, 4096, dtype=jnp.int32)
    positions = jnp.arange(SEQ, dtype=jnp.int32)[None, :] + offset
    w_in = nrm(ks[3], (DEPTH, D_MODEL, IN_COLS), D_MODEL ** -0.5)
    w_out = nrm(ks[4], (DEPTH, MIX_WIDTH, D_MODEL), MIX_WIDTH ** -0.5)
    w_ada = nrm(ks[5], (DEPTH, D_MODEL, 6 * D_MODEL), 0.5 * D_MODEL ** -0.5)
    b_ada = nrm(ks[6], (DEPTH, 6 * D_MODEL), 0.01)
    norm_w = 1.0 + nrm(ks[7], (DEPTH, 4, D_MODEL), 0.05)
    cmp_w_k = nrm(ks[8], (DEPTH, CMP_BLOCK, HEAD_DIM, HEAD_DIM), (CMP_BLOCK * HEAD_DIM) ** -0.5)
    cmp_w_v = nrm(ks[9], (DEPTH, CMP_BLOCK, HEAD_DIM, HEAD_DIM), (CMP_BLOCK * HEAD_DIM) ** -0.5)
    cmp_pe_k = nrm(ks[10], (DEPTH, CMP_BLOCK, HEAD_DIM), 0.5)
    cmp_pe_v = nrm(ks[11], (DEPTH, CMP_BLOCK, HEAD_DIM), 0.5)
    sinks = nrm(ks[12], (DEPTH, SWA_HEADS), 1.0)
    w_gate = nrm(ks[13], (DEPTH, D_MODEL, D_FF), D_MODEL ** -0.5)
    w_up = nrm(ks[14], (DEPTH, D_MODEL, D_FF), D_MODEL ** -0.5)
    conv_w = nrm(ks[15], (DEPTH, CONV_WIDTH, D_FF), CONV_WIDTH ** -0.5)
    conv_b = nrm(ks[16], (DEPTH, D_FF), 0.01)
    w_down = nrm(ks[17], (DEPTH, D_FF, D_MODEL), D_FF ** -0.5)
    return {'x': x, 'c': c, 'positions': positions, 'w_in': w_in, 'w_out': w_out,
            'w_ada': w_ada, 'b_ada': b_ada, 'norm_w': norm_w,
            'cmp_w_k': cmp_w_k, 'cmp_w_v': cmp_w_v, 'cmp_pe_k': cmp_pe_k, 'cmp_pe_v': cmp_pe_v,
            'sinks': sinks, 'w_gate': w_gate, 'w_up': w_up, 'conv_w': conv_w, 'conv_b': conv_b,
            'w_down': w_down}


def reference(x, c, positions, w_in, w_out, w_ada, b_ada, norm_w, cmp_w_k, cmp_w_v, cmp_pe_k, cmp_pe_v,
              sinks, w_gate, w_up, conv_w, conv_b, w_down):
    cos, sin = _rope_tables(positions, x.dtype)
    cond = jax.nn.silu(c)
    for layer in range(DEPTH):
        ada = cond @ w_ada[layer] + b_ada[layer]
        sh1, sc1, g1, sh2, sc2, g2 = [a[:, None, :] for a in jnp.split(ada, 6, axis=-1)]
        h = _rmsnorm(x, norm_w[layer, 0]) * (1 + sc1) + sh1
        h = _token_mixers(h, cos, sin, w_in[layer], w_out[layer], cmp_w_k[layer], cmp_w_v[layer],
                          cmp_pe_k[layer], cmp_pe_v[layer], sinks[layer])
        x = x + g1 * _rmsnorm(h, norm_w[layer, 1])
        h = _rmsnorm(x, norm_w[layer, 2]) * (1 + sc2) + sh2
        h = _conv_ffn(h, w_gate[layer], w_up[layer], conv_w[layer], conv_b[layer], w_down[layer])
        x = x + g2 * _rmsnorm(h, norm_w[layer, 3])
    return x
```

```python
import functools

import numpy as np
import jax
import jax.numpy as jnp
from jax import lax
from jax.experimental import pallas as pl
from jax.experimental.pallas import tpu as pltpu

F32 = jnp.float32
BF16 = jnp.bfloat16

HEAD_DIM = 64
HALF = HEAD_DIM // 2
NSA_HEADS = 4
CMP_BLOCK = 32
CMP_STRIDE = 16
SEL_BLOCK = 64
SEL_TOPK = 16
NSA_WINDOW = 512
DIL_PATTERNS = ((128, 1), (512, 4), (2048, 16))
DIL_HEADS = 6
SWA_HEADS = 6
SWA_KV_HEADS = 2
SWA_WINDOW = 128
ROPE_THETA = 10000.0
RMS_EPS = 1e-6
NEG = -1e30
FORCE = 1e4
CONV_WIDTH = 3

LANES = 128
TILE = 512
NSUB = TILE // LANES
VROWS = HEAD_DIM + 16
MEMBER_BIG = 2.0 ** 100
VMEM_LIMIT = 56 * 1024 * 1024

Q_ROWS = 1024
A_Q_BLK = 3
NK_ROWS = 768
NK_LANES = 768
V_ROWS = 640
G_ROWS = 16
W_ROWS = Q_ROWS + NK_ROWS + V_ROWS + G_ROWS
N_VPIECES = V_ROWS // HEAD_DIM


def _cp(sem):
    return pltpu.CompilerParams(dimension_semantics=sem, vmem_limit_bytes=VMEM_LIMIT)


def _const_spec(shape, index_map):
    return pl.BlockSpec(shape, index_map, pipeline_mode=pl.Buffered(1))


def _adaln_kernel(c_ref, w_ref, b_ref, o_ref):
    c = c_ref[...]
    cond = c * jax.nn.sigmoid(c)
    o_ref[...] = jnp.dot(cond, w_ref[...], preferred_element_type=F32,
                         precision=lax.Precision.HIGHEST) + b_ref[...]


def _adaln(c8, w_ada, b_ada):
    depth, d, six_d = w_ada.shape
    nblk = six_d // d
    return pl.pallas_call(
        _adaln_kernel,
        grid=(depth, nblk),
        in_specs=[pl.BlockSpec((8, d), lambda l, n: (0, 0)),
                  pl.BlockSpec((None, d, d), lambda l, n: (l, 0, n)),
                  pl.BlockSpec((None, 1, d), lambda l, n: (l, 0, n))],
        out_specs=pl.BlockSpec((None, 8, d), lambda l, n: (l, 0, n)),
        out_shape=jax.ShapeDtypeStruct((depth, 8, six_d), F32),
        compiler_params=_cp(("parallel", "parallel")),
        name="adaln",
    )(c8, w_ada, b_ada.reshape(depth, 1, six_d))


def _norm_mod_to_scratch(x_ref, a_ref, sh_ref, h_scr):
    for c in range(NSUB):
        sl = slice(c * LANES, (c + 1) * LANES)
        xs = x_ref[:, sl]
        ms = jnp.mean(xs * xs, axis=0, keepdims=True)
        h_scr[:, sl] = ((xs * lax.rsqrt(ms + RMS_EPS)) * a_ref[...] + sh_ref[...]).astype(BF16)


def _inproj_kernel(x_ref, a_ref, sh_ref, w_ref, cos_ref, sin_ref,
                   q_ref, kcvc_ref, nk_ref, v4_ref, gate_ref, h_scr):
    j = pl.program_id(1)
    _norm_mod_to_scratch(x_ref, a_ref, sh_ref, h_scr)
    h = h_scr[...]
    cos = cos_ref[...]
    sin = sin_ref[...]

    def proj(r0, r1):
        return jnp.dot(w_ref[r0:r1, :], h, preferred_element_type=F32)

    def rope(r, nh):
        outs = []
        for hh in range(nh):
            t1 = r[HEAD_DIM * hh:HEAD_DIM * hh + HALF]
            t2 = r[HEAD_DIM * hh + HALF:HEAD_DIM * (hh + 1)]
            outs.append(t1 * cos - t2 * sin)
            outs.append(t2 * cos + t1 * sin)
        return jnp.concatenate(outs, axis=0)

    for r0, r1 in ((0, 384), (384, 768), (768, 1024)):
        q_ref[r0:r1, :] = rope(proj(r0, r1), (r1 - r0) // HEAD_DIM).astype(BF16)

    base = Q_ROWS
    r = proj(base, base + 128)
    kcvc = jnp.concatenate([rope(r[0:64], 1), r[64:128]], axis=0)
    kcvc_ref[...] = kcvc.T.astype(BF16)

    r = proj(base + 128, base + 192)
    tok = j * TILE + lax.broadcasted_iota(jnp.int32, (HEAD_DIM, TILE), 1)
    row = lax.broadcasted_iota(jnp.int32, (HEAD_DIM, TILE), 0)
    member_cols = jnp.where(row == ((tok >> 6) & 15), MEMBER_BIG, 0.0).astype(F32)
    nk_ref[:, 0:128] = jnp.concatenate([rope(r, 1), member_cols], axis=0).T.astype(BF16)

    r = proj(base + 192, base + 256)
    nk_ref[:, 128:256] = jnp.concatenate([rope(r, 1), jnp.zeros((HEAD_DIM, TILE), F32)], axis=0).T.astype(BF16)

    r = proj(base + 256, base + 640)
    nk_ref[:, 256:640] = rope(r, 6).T.astype(BF16)

    r = proj(base + 640, base + 768)
    nk_ref[:, 640:768] = rope(r, 2).T.astype(BF16)

    base = Q_ROWS + NK_ROWS
    r = proj(base, base + V_ROWS).astype(BF16)
    ones = jnp.ones((VROWS - HEAD_DIM, LANES), BF16)
    for c in range(NSUB):
        for p in range(N_VPIECES):
            v4_ref[c, VROWS * p:VROWS * p + HEAD_DIM, :] = r[HEAD_DIM * p:HEAD_DIM * (p + 1),
                                                             c * LANES:(c + 1) * LANES]
            v4_ref[c, VROWS * p + HEAD_DIM:VROWS * (p + 1), :] = ones

    base = Q_ROWS + NK_ROWS + V_ROWS
    gate_ref[...] = jax.nn.sigmoid(proj(base, base + G_ROWS))


def _inproj(xT, a1, sh1, w_inT, layer, cosT, sinT):
    B, D, S = xT.shape
    nt = S // TILE
    return pl.pallas_call(
        _inproj_kernel,
        grid=(B, nt),
        in_specs=[pl.BlockSpec((None, D, TILE), lambda b, j: (b, 0, j)),
                  pl.BlockSpec((None, D, LANES), lambda b, j: (b, 0, 0)),
                  pl.BlockSpec((None, D, LANES), lambda b, j: (b, 0, 0)),
                  _const_spec((None, W_ROWS, D), lambda b, j: (layer, 0, 0)),
                  pl.BlockSpec((None, HALF, TILE), lambda b, j: (b, 0, j)),
                  pl.BlockSpec((None, HALF, TILE), lambda b, j: (b, 0, j))],
        out_specs=[pl.BlockSpec((None, Q_ROWS, TILE), lambda b, j: (b, 0, j)),
                   pl.BlockSpec((None, TILE, LANES), lambda b, j: (b, j, 0)),
                   pl.BlockSpec((None, TILE, NK_LANES), lambda b, j: (b, j, 0)),
                   pl.BlockSpec((None, NSUB, N_VPIECES * VROWS, LANES), lambda b, j: (b, j, 0, 0)),
                   pl.BlockSpec((None, G_ROWS, TILE), lambda b, j: (b, 0, j))],
        out_shape=[jax.ShapeDtypeStruct((B, Q_ROWS, S), BF16),
                   jax.ShapeDtypeStruct((B, S, LANES), BF16),
                   jax.ShapeDtypeStruct((B, S, NK_LANES), BF16),
                   jax.ShapeDtypeStruct((B, S // LANES, N_VPIECES * VROWS, LANES), BF16),
                   jax.ShapeDtypeStruct((B, G_ROWS, S), F32)],
        scratch_shapes=[pltpu.VMEM((D, TILE), BF16)],
        compiler_params=_cp(("parallel", "parallel")),
        name="inproj",
    )(xT, a1, sh1, w_inT, cosT, sinT)


def _compress_kernel(t_ref, w_ref, pe_ref, kc_ref, vc_ref):
    n = t_ref.shape[0]
    a = jnp.dot(t_ref[...], w_ref[...], preferred_element_type=F32)
    pc = jnp.dot(pe_ref[...], w_ref[...], preferred_element_type=F32)
    const = pc[0:1, 0:LANES] + pc[8:9, LANES:2 * LANES]
    cmp = a[:, 0:LANES] + pltpu.roll(a[:, LANES:2 * LANES], n - 1, 0) + const
    cmp_t = cmp.T
    ones = jnp.ones((VROWS - HEAD_DIM, LANES), BF16)
    for c in range(n // LANES):
        kc_ref[c] = cmp[c * LANES:(c + 1) * LANES].astype(BF16)
        vc_ref[c, 0:HEAD_DIM, :] = cmp_t[HEAD_DIM:2 * HEAD_DIM, c * LANES:(c + 1) * LANES].astype(BF16)
        vc_ref[c, HEAD_DIM:VROWS, :] = ones


def _compress(kcvc, wbig, pe2, layer):
    B, S, _ = kcvc.shape
    n = S // CMP_STRIDE
    nch = n // LANES
    tview = kcvc.reshape(B, n, CMP_STRIDE * LANES)
    return pl.pallas_call(
        _compress_kernel,
        grid=(B,),
        in_specs=[pl.BlockSpec((None, n, CMP_STRIDE * LANES), lambda b: (b, 0, 0)),
                  pl.BlockSpec((None, CMP_STRIDE * LANES, 2 * LANES), lambda b: (layer, 0, 0)),
                  pl.BlockSpec((None, 16, CMP_STRIDE * LANES), lambda b: (layer, 0, 0))],
        out_specs=[pl.BlockSpec((None, nch, LANES, LANES), lambda b: (b, 0, 0, 0)),
                   pl.BlockSpec((None, nch, VROWS, LANES), lambda b: (b, 0, 0, 0))],
        out_shape=[jax.ShapeDtypeStruct((B, nch, LANES, LANES), BF16),
                   jax.ShapeDtypeStruct((B, nch, VROWS, LANES), BF16)],
        compiler_params=_cp(("parallel",)),
        name="compress",
    )(tview, wbig, pe2)


def _stack_heads(q_ref, nh, lane_slice=slice(None)):
    return jnp.concatenate([q_ref[HEAD_DIM * h:HEAD_DIM * (h + 1), lane_slice] for h in range(nh)], axis=1)


def _cmp_kernel(q_ref, kc_ref, vc_ref, ov_ref, bias_ref, o_ref, mm_ref, s_scr, imp_scr, *, n_sel):
    i = pl.program_id(1)
    cd = i // 16
    nq = NSA_HEADS * LANES
    q = _stack_heads(q_ref, NSA_HEADS)

    def scores(c):
        return jnp.dot(kc_ref[c, :, 0:HEAD_DIM], q, preferred_element_type=F32)

    def pass1(c, m):
        s = scores(c)
        s_scr[c] = s
        return jnp.maximum(m, jnp.max(s, axis=0, keepdims=True))

    m = lax.fori_loop(0, cd, pass1, jnp.full((1, nq), NEG, F32))
    bias = bias_ref[...]
    sd = scores(cd) + jnp.concatenate([bias] * NSA_HEADS, axis=1)
    m = jnp.maximum(m, jnp.max(sd, axis=0, keepdims=True))
    valid = m > 0.5 * NEG

    imp_scr[...] = jnp.zeros(imp_scr.shape, F32)

    def accumulate(c, p, acc):
        acc = acc + jnp.dot(vc_ref[c], p, preferred_element_type=F32)
        r0 = pl.multiple_of(c * 32, 32)
        imp_scr[pl.ds(r0, 40), :] += jnp.dot(ov_ref[...], p, preferred_element_type=F32)
        return acc

    def pass2(c, acc):
        return accumulate(c, jnp.exp(s_scr[c] - m).astype(BF16), acc)

    acc = lax.fori_loop(0, cd, pass2, jnp.zeros((VROWS, nq), F32))
    acc = accumulate(cd, jnp.exp(sd - m).astype(BF16), acc)

    inv = jnp.where(valid, 1.0 / acc[HEAD_DIM:HEAD_DIM + 1], 0.0)
    o = acc[0:HEAD_DIM] * inv
    for h in range(NSA_HEADS):
        o_ref[HEAD_DIM * h:HEAD_DIM * (h + 1), :] = o[:, h * LANES:(h + 1) * LANES]

    imp = jnp.zeros((n_sel, LANES), F32)
    for h in range(NSA_HEADS):
        sl = slice(h * LANES, (h + 1) * LANES)
        imp = imp + imp_scr[0:n_sel, sl] * inv[:, sl]

    blk = lax.broadcasted_iota(jnp.int32, (n_sel, LANES), 0).astype(F32)
    t = i * LANES + lax.broadcasted_iota(jnp.int32, (n_sel, LANES), 1)
    cur = (t >> 6).astype(F32)
    forced = (blk == 0.0) | (blk == cur) | (blk == cur - 1.0)
    imp = jnp.where(forced, FORCE, imp)
    imp = jnp.where(blk <= cur, imp, NEG)

    def pick(_, carry):
        imp, chosen = carry
        mx = jnp.max(imp, axis=0, keepdims=True)
        first = jnp.min(jnp.where(imp == mx, blk, float(n_sel)), axis=0, keepdims=True)
        hit = blk == first
        return jnp.where(hit, -jnp.inf, imp), jnp.where(hit, 1.0, chosen)

    _, chosen = lax.fori_loop(0, min(SEL_TOPK, n_sel), pick, (imp, jnp.zeros((n_sel, LANES), F32)))
    member = (chosen > 0.5) & (blk <= cur)
    mm_ref[...] = jnp.where(member, 0.0, -1.0).astype(BF16)


def _cmp_topk(qT, kc4, vc4, ov, cmp_bias):
    B, _, S = qT.shape
    nblk = S // LANES
    nch = kc4.shape[1]
    n_sel = S // SEL_BLOCK
    nq = NSA_HEADS * LANES
    return pl.pallas_call(
        functools.partial(_cmp_kernel, n_sel=n_sel),
        grid=(B, nblk),
        in_specs=[pl.BlockSpec((None, NSA_HEADS * HEAD_DIM, LANES), lambda b, i: (b, A_Q_BLK, i)),
                  pl.BlockSpec((None, nch, LANES, LANES), lambda b, i: (b, 0, 0, 0)),
                  pl.BlockSpec((None, nch, VROWS, LANES), lambda b, i: (b, 0, 0, 0)),
                  pl.BlockSpec((40, LANES), lambda b, i: (0, 0)),
                  pl.BlockSpec((None, LANES, LANES), lambda b, i: (i % 16, 0, 0))],
        out_specs=[pl.BlockSpec((None, NSA_HEADS * HEAD_DIM, LANES), lambda b, i: (b, 0, i)),
                   pl.BlockSpec((None, n_sel, LANES), lambda b, i: (b, 0, i))],
        out_shape=[jax.ShapeDtypeStruct((B, NSA_HEADS * HEAD_DIM, S), F32),
                   jax.ShapeDtypeStruct((B, n_sel, S), BF16)],
        scratch_shapes=[pltpu.VMEM((nch, LANES, nq), F32),
                        pltpu.VMEM((32 * nch + 64, nq), F32)],
        compiler_params=_cp(("parallel", "parallel")),
        name="nsa_cmp_topk",
    )(qT, kc4, vc4, ov, cmp_bias)


def _sel_kernel(q_ref, mm_ref, ks_ref, v_ref, causal_ref, ocmp_ref, owin_ref, gate_ref, out_ref,
                qa_scr, mm_scr):
    i = pl.program_id(1)
    nq = NSA_HEADS * LANES
    qa_scr[0:HEAD_DIM, :] = _stack_heads(q_ref, NSA_HEADS)
    qa_scr[HEAD_DIM:LANES, :] = jnp.zeros((LANES - HEAD_DIM, nq), BF16)
    mm = mm_ref[...]
    for h in range(NSA_HEADS):
        mm_scr[:, h * LANES:(h + 1) * LANES] = mm

    def scores(c):
        g16 = pl.multiple_of((c // 8) * 16, 16)
        qa_scr[HEAD_DIM:HEAD_DIM + 16, :] = mm_scr[pl.ds(g16, 16), :]
        k = ks_ref[pl.ds(pl.multiple_of(c * LANES, LANES), LANES), :]
        return jnp.dot(k, qa_scr[...], preferred_element_type=F32)

    def update(c, s, m, acc):
        m_new = jnp.maximum(m, jnp.max(s, axis=0, keepdims=True))
        p = jnp.exp(s - m_new).astype(BF16)
        acc = acc * jnp.exp(m - m_new) + jnp.dot(v_ref[c], p, preferred_element_type=F32)
        return m_new, acc

    def body(c, carry):
        return update(c, scores(c), *carry)

    m, acc = lax.fori_loop(0, i, body, (jnp.full((1, nq), NEG, F32), jnp.zeros((VROWS, nq), F32)))
    m, acc = update(i, scores(i) + causal_ref[...], m, acc)
    o = acc[0:HEAD_DIM] * (1.0 / acc[HEAD_DIM:HEAD_DIM + 1])

    g = gate_ref[...]
    for h in range(NSA_HEADS):
        rs = slice(HEAD_DIM * h, HEAD_DIM * (h + 1))
        out = (g[3 * h:3 * h + 1] * ocmp_ref[rs, :] + g[3 * h + 1:3 * h + 2] * o[:, h * LANES:(h + 1) * LANES]
               + g[3 * h + 2:3 * h + 3] * owin_ref[rs, :])
        out_ref[rs, :] = out.astype(BF16)


def _sel_attend(qT, mm, nk, v4, causal4, ocmp, owin, gates):
    B, _, S = qT.shape
    nblk = S // LANES
    n_sel = S // SEL_BLOCK
    nq = NSA_HEADS * LANES
    ar = NSA_HEADS * HEAD_DIM
    return pl.pallas_call(
        _sel_kernel,
        grid=(B, nblk),
        in_specs=[pl.BlockSpec((None, ar, LANES), lambda b, i: (b, A_Q_BLK, i)),
                  pl.BlockSpec((None, n_sel, LANES), lambda b, i: (b, 0, i)),
                  pl.BlockSpec((None, S, LANES), lambda b, i: (b, 0, 0)),
                  pl.BlockSpec((None, nblk, VROWS, LANES), lambda b, i: (b, 0, 0, 0)),
                  pl.BlockSpec((LANES, nq), lambda b, i: (0, 0)),
                  pl.BlockSpec((None, ar, LANES), lambda b, i: (b, 0, i)),
                  pl.BlockSpec((None, ar, LANES), lambda b, i: (b, 0, i)),
                  pl.BlockSpec((None, G_ROWS, LANES), lambda b, i: (b, 0, i))],
        out_specs=pl.BlockSpec((None, ar, LANES), lambda b, i: (b, 0, i)),
        out_shape=jax.ShapeDtypeStruct((B, ar, S), BF16),
        scratch_shapes=[pltpu.VMEM((LANES, nq), BF16), pltpu.VMEM((n_sel, nq), BF16)],
        compiler_params=_cp(("parallel", "parallel")),
        name="nsa_sel",
    )(qT, mm, nk, v4, causal4, ocmp, owin, gates)


def _banded_kernel(*refs, hkv, grp, nprev, mid_bias, dynamic, has_sink, want_lse):
    q_ref, k_ref, v_ref, bias_ref = refs[:4]
    pos = 4
    sink_ref = None
    if has_sink:
        sink_ref = refs[pos]
        pos += 1
    o_ref = refs[pos]
    lse_ref = refs[pos + 1] if want_lse else None
    j = pl.program_id(1)

    def one(s, g, first):
        lanes = slice(s * LANES, (s + 1) * LANES)
        qg = jnp.concatenate([q_ref[HEAD_DIM * (g * grp + u):HEAD_DIM * (g * grp + u + 1), lanes]
                              for u in range(grp)], axis=1)
        parts = []
        for ci in range(nprev + 1):
            which = 0 if ci == 0 else (2 if ci == nprev else 1)
            if first:
                kc = s - nprev + ci
                if kc < 0:
                    continue
                kcc = kc
                row = kc * LANES
            else:
                kc = j * NSUB + s - nprev + ci
                kcc = jnp.maximum(kc, 0) if dynamic else kc
                row = pl.multiple_of(kcc * LANES, LANES)
            sc = jnp.dot(k_ref[pl.ds(row, LANES), HEAD_DIM * g:HEAD_DIM * (g + 1)], qg,
                         preferred_element_type=F32)
            if dynamic:
                sc = sc + bias_ref[jnp.where(kc >= 0, which, 3)]
            elif which != 1 or mid_bias:
                sc = sc + bias_ref[which]
            parts.append((kcc, sc))
        m = None
        for _, sc in parts:
            mc = jnp.max(sc, axis=0, keepdims=True)
            m = mc if m is None else jnp.maximum(m, mc)
        if has_sink:
            sk = sink_ref[g, 0:1, :]
            m = jnp.maximum(m, sk)
        acc = None
        for kcc, sc in parts:
            t = jnp.dot(v_ref[kcc, VROWS * g:VROWS * (g + 1), :], jnp.exp(sc - m).astype(BF16),
                        preferred_element_type=F32)
            acc = t if acc is None else acc + t
        l = acc[HEAD_DIM:HEAD_DIM + 1]
        if has_sink:
            l = l + jnp.exp(sk - m)
        o = acc[0:HEAD_DIM] * (1.0 / l)
        for u in range(grp):
            hq = g * grp + u
            o_ref[HEAD_DIM * hq:HEAD_DIM * (hq + 1), lanes] = o[:, u * LANES:(u + 1) * LANES].astype(o_ref.dtype)
        if want_lse:
            lse_ref[8 * g:8 * (g + 1), lanes] = jnp.broadcast_to(m + jnp.log(l), (8, LANES))

    def run(first):
        for s in range(NSUB):
            for g in range(hkv):
                one(s, g, first)

    if dynamic:
        run(False)
    else:
        pl.when(j == 0)(lambda: run(True))
        pl.when(j > 0)(lambda: run(False))


def _banded(qT, q_blk, nk, k_blk, v4, v_blk, bias, *, hkv, grp, nprev, mid_bias, dynamic,
            out_dtype, sinks=None, want_lse=False, name):
    B, _, S = qT.shape
    nt = S // TILE
    qrows = hkv * grp * HEAD_DIM
    in_specs = [pl.BlockSpec((None, qrows, TILE), lambda b, j: (b, q_blk, j)),
                pl.BlockSpec((None, S, LANES), lambda b, j: (b, 0, k_blk)),
                pl.BlockSpec((None, S // LANES, hkv * VROWS, LANES), lambda b, j: (b, 0, v_blk, 0)),
                pl.BlockSpec(bias.shape, lambda b, j: (0, 0, 0))]
    args = [qT, nk, v4, bias]
    if sinks is not None:
        in_specs.append(pl.BlockSpec(sinks.shape, lambda b, j: (0, 0, 0)))
        args.append(sinks)
    out_specs = [pl.BlockSpec((None, qrows, TILE), lambda b, j: (b, 0, j))]
    out_shape = [jax.ShapeDtypeStruct((B, qrows, S), out_dtype)]
    if want_lse:
        out_specs.append(pl.BlockSpec((None, 8 * hkv, TILE), lambda b, j: (b, 0, j)))
        out_shape.append(jax.ShapeDtypeStruct((B, 8 * hkv, S), F32))
    return pl.pallas_call(
        functools.partial(_banded_kernel, hkv=hkv, grp=grp, nprev=nprev, mid_bias=mid_bias,
                          dynamic=dynamic, has_sink=sinks is not None, want_lse=want_lse),
        grid=(B, nt),
        in_specs=in_specs,
        out_specs=out_specs,
        out_shape=out_shape,
        compiler_params=_cp(("parallel", "parallel")),
        name=name,
    )(*args)


def _outproj_kernel(a_ref, b0_ref, b1_ref, b2_ref, l0_ref, l1_ref, l2_ref, c_ref, w_ref, x_ref, gw_ref,
                    o_ref, mix_scr):
    b_refs = (b0_ref, b1_ref, b2_ref)
    l_refs = (l0_ref, l1_ref, l2_ref)
    for h in range(2):
        ls = [r[8 * h:8 * h + 1, :] for r in l_refs]
        mx = jnp.maximum(jnp.maximum(ls[0], ls[1]), ls[2])
        es = [jnp.exp(v - mx) for v in ls]
        inv = 1.0 / (es[0] + es[1] + es[2])
        for g in range(3):
            rs = slice(HEAD_DIM * h, HEAD_DIM * (h + 1))
            mix_scr[128 * g + HEAD_DIM * h:128 * g + HEAD_DIM * (h + 1), :] = (
                b_refs[g][rs, :] * (es[g] * inv)).astype(BF16)
    y = jnp.dot(w_ref[:, 0:256], a_ref[...], preferred_element_type=F32)
    y = y + jnp.dot(w_ref[:, 256:640], mix_scr[...], preferred_element_type=F32)
    y = y + jnp.dot(w_ref[:, 640:1024], c_ref[...], preferred_element_type=F32)
    ms = jnp.mean(y * y, axis=0, keepdims=True)
    yn = y * lax.rsqrt(ms + RMS_EPS)
    gw = gw_ref[...]
    for c in range(NSUB):
        sl = slice(c * LANES, (c + 1) * LANES)
        o_ref[:, sl] = x_ref[:, sl] + gw * yn[:, sl]


def _outproj(aT, bs, lses, cT, w_outT, layer, xT, gw):
    B, D, S = xT.shape
    nt = S // TILE
    tile = lambda rows: pl.BlockSpec((None, rows, TILE), lambda b, j: (b, 0, j))
    return pl.pallas_call(
        _outproj_kernel,
        grid=(B, nt),
        in_specs=[tile(256), tile(128), tile(128), tile(128), tile(16), tile(16), tile(16), tile(384),
                  _const_spec((None, D, D), lambda b, j: (layer, 0, 0)),
                  tile(D),
                  pl.BlockSpec((None, D, LANES), lambda b, j: (b, 0, 0))],
        out_specs=tile(D),
        out_shape=jax.ShapeDtypeStruct((B, D, S), F32),
        scratch_shapes=[pltpu.VMEM((384, TILE), BF16)],
        compiler_params=_cp(("parallel", "parallel")),
        name="outproj",
    )(aT, bs[0], bs[1], bs[2], lses[0], lses[1], lses[2], cT, w_outT, xT, gw)


def _ffn_kernel(x_ref, a_ref, sh_ref, wg_ref, wu_ref, wd_ref, cv_ref, gw_ref, o_ref,
                h_scr, carry_scr, y_scr, *, fchunk):
    j = pl.program_id(1)
    d_ff = wg_ref.shape[0]

    @pl.when(j == 0)
    def _():
        carry_scr[...] = jnp.zeros(carry_scr.shape, F32)

    _norm_mod_to_scratch(x_ref, a_ref, sh_ref, h_scr)
    h = h_scr[...]
    lane = lax.broadcasted_iota(jnp.int32, (fchunk, LANES), 1)
    for c in range(d_ff // fchunk):
        rs = slice(c * fchunk, (c + 1) * fchunk)
        g = jnp.dot(wg_ref[rs, :], h, preferred_element_type=F32)
        prev = carry_scr[rs, :]
        carry_scr[rs, :] = g[:, TILE - LANES:TILE]
        g1 = pltpu.roll(g, 1, 1)
        g2 = pltpu.roll(g, 2, 1)
        fix1 = jnp.where(lane < 1, pltpu.roll(prev, 1, 1), g1[:, 0:LANES])
        fix2 = jnp.where(lane < 2, pltpu.roll(prev, 2, 1), g2[:, 0:LANES])
        g1 = jnp.concatenate([fix1, g1[:, LANES:]], axis=1)
        g2 = jnp.concatenate([fix2, g2[:, LANES:]], axis=1)
        w0 = jnp.concatenate([cv_ref[0, rs, :]] * NSUB, axis=1)
        w1 = jnp.concatenate([cv_ref[1, rs, :]] * NSUB, axis=1)
        w2 = jnp.concatenate([cv_ref[2, rs, :]] * NSUB, axis=1)
        cb = jnp.concatenate([cv_ref[3, rs, :]] * NSUB, axis=1)
        acc = g2 * w0 + g1 * w1 + g * w2 + cb
        up = jnp.dot(wu_ref[rs, :], h, preferred_element_type=F32)
        act = (jax.nn.gelu(acc, approximate=True) * up).astype(BF16)
        part = jnp.dot(wd_ref[:, rs], act, preferred_element_type=F32)
        if c == 0:
            y_scr[...] = part
        else:
            y_scr[...] += part
    y = y_scr[...]
    ms = jnp.mean(y * y, axis=0, keepdims=True)
    yn = y * lax.rsqrt(ms + RMS_EPS)
    gw = gw_ref[...]
    for c in range(NSUB):
        sl = slice(c * LANES, (c + 1) * LANES)
        o_ref[:, sl] = x_ref[:, sl] + gw * yn[:, sl]


def _ffn(xT, a2, sh2, wgT, wuT, wdT, cv, layer, gw):
    B, D, S = xT.shape
    nt = S // TILE
    d_ff = wgT.shape[1]
    fchunk = 256
    return pl.pallas_call(
        functools.partial(_ffn_kernel, fchunk=fchunk),
        grid=(B, nt),
        in_specs=[pl.BlockSpec((None, D, TILE), lambda b, j: (b, 0, j)),
                  pl.BlockSpec((None, D, LANES), lambda b, j: (b, 0, 0)),
                  pl.BlockSpec((None, D, LANES), lambda b, j: (b, 0, 0)),
                  _const_spec((None, d_ff, D), lambda b, j: (layer, 0, 0)),
                  _const_spec((None, d_ff, D), lambda b, j: (layer, 0, 0)),
                  _const_spec((None, D, d_ff), lambda b, j: (layer, 0, 0)),
                  _const_spec((None, 4, d_ff, LANES), lambda b, j: (layer, 0, 0, 0)),
                  pl.BlockSpec((None, D, LANES), lambda b, j: (b, 0, 0))],
        out_specs=pl.BlockSpec((None, D, TILE), lambda b, j: (b, 0, j)),
        out_shape=jax.ShapeDtypeStruct((B, D, S), F32),
        scratch_shapes=[pltpu.VMEM((D, TILE), BF16), pltpu.VMEM((d_ff, LANES), F32),
                        pltpu.VMEM((D, TILE), F32)],
        compiler_params=_cp(("arbitrary", "arbitrary")),
        name="convffn",
    )(xT, a2, sh2, wgT, wuT, wdT, cv, gw)


def _band_bias(dilation, old_edge, reps):
    kk = np.arange(LANES)[:, None]
    qq = np.arange(LANES)[None, :]
    res = ((qq - kk) % dilation) == 0
    tabs = [res & (kk - qq >= old_edge), res, res & (kk <= qq), np.zeros_like(res)]
    out = np.stack([np.where(t, 0.0, NEG) for t in tabs]).astype(np.float32)
    return jnp.asarray(np.tile(out, (1, 1, reps)))


def _cmp_bias():
    nn = np.arange(LANES)[:, None]
    qq = np.arange(LANES)[None, :]
    tabs = [np.where(CMP_STRIDE * nn + CMP_BLOCK - 1 <= LANES * r + qq, 0.0, NEG) for r in range(16)]
    return jnp.asarray(np.stack(tabs).astype(np.float32))


def _overlap_rows():
    jj = np.arange(40)[:, None]
    nn = np.arange(LANES)[None, :]
    return jnp.asarray(((nn >= 4 * jj - 1) & (nn <= 4 * jj + 3)).astype(np.float32), dtype=BF16)


_IN_COL_ORDER = ((1804, 2188), (652, 1036), (0, 256),
                 (256, 320), (320, 384), (384, 448), (512, 576), (1036, 1420), (2188, 2316),
                 (448, 512), (576, 640), (1420, 1804), (2316, 2444),
                 (640, 652))


def _prep_w_in(w_in):
    cols = np.concatenate([np.arange(a, b) for a, b in _IN_COL_ORDER])
    wt = jnp.swapaxes(w_in[:, :, cols], 1, 2)
    scale = np.ones((wt.shape[1], 1), np.float32)
    scale[:Q_ROWS] = HEAD_DIM ** -0.5
    wt = wt * scale
    wt = jnp.pad(wt, ((0, 0), (0, W_ROWS - wt.shape[1]), (0, 0)))
    return wt.astype(BF16)


def _prep_compress(w_ck, w_cv, pe_k, pe_v):
    L = w_ck.shape[0]
    half = CMP_BLOCK // 2

    def big(lo):
        wk = w_ck[:, lo:lo + half]
        wv = w_cv[:, lo:lo + half]
        z = jnp.zeros_like(wk)
        top = jnp.concatenate([wk, z], axis=-1)
        bot = jnp.concatenate([z, wv], axis=-1)
        return jnp.concatenate([top, bot], axis=2).reshape(L, half * LANES, LANES)

    wbig = jnp.concatenate([big(0), big(half)], axis=-1).astype(BF16)
    pe = jnp.concatenate([pe_k, pe_v], axis=-1)
    pe2 = jnp.zeros((L, 16, half * LANES), F32)
    pe2 = pe2.at[:, 0].set(pe[:, :half].reshape(L, -1)).at[:, 8].set(pe[:, half:].reshape(L, -1))
    return wbig, pe2.astype(BF16)


def _lane_bcast(v):
    return jnp.broadcast_to(v[..., None], v.shape + (LANES,))


def kernel(x, c, positions, w_in, w_out, w_ada, b_ada, norm_w, cmp_w_k, cmp_w_v, cmp_pe_k, cmp_pe_v,
           sinks, w_gate, w_up, conv_w, conv_b, w_down):
    B, S, D = x.shape
    depth = w_in.shape[0]
    assert S % 2048 == 0 and D == 1024 and w_in.shape[2] == 2444

    inv = ROPE_THETA ** (-jnp.arange(0, HEAD_DIM, 2, dtype=F32) / HEAD_DIM)
    ang = positions.astype(F32)[:, None, :] * inv[None, :, None]
    cosT, sinT = jnp.cos(ang), jnp.sin(ang)

    c8 = jnp.pad(c, ((0, 8 - B), (0, 0)))
    ada = _adaln(c8, w_ada, b_ada)[:, :B]
    sh1, sc1, g1, sh2, sc2, g2 = [ada[:, :, k * D:(k + 1) * D] for k in range(6)]
    nw = norm_w[:, :, None, :]
    a1 = _lane_bcast(nw[:, 0] * (1 + sc1))
    gw1 = _lane_bcast(g1 * nw[:, 1])
    a2 = _lane_bcast(nw[:, 2] * (1 + sc2))
    gw2 = _lane_bcast(g2 * nw[:, 3])
    sh1b, sh2b = _lane_bcast(sh1), _lane_bcast(sh2)

    w_inT = _prep_w_in(w_in)
    w_outT = jnp.swapaxes(w_out, 1, 2).astype(BF16)
    wgT = jnp.swapaxes(w_gate, 1, 2).astype(BF16)
    wuT = jnp.swapaxes(w_up, 1, 2).astype(BF16)
    wdT = jnp.swapaxes(w_down, 1, 2).astype(BF16)
    cv = _lane_bcast(jnp.concatenate([conv_w, conv_b[:, None, :]], axis=1))
    wbig, pe2 = _prep_compress(cmp_w_k, cmp_w_v, cmp_pe_k, cmp_pe_v)
    sink_tab = jnp.broadcast_to(
        jnp.repeat(sinks.reshape(depth, SWA_KV_HEADS, SWA_HEADS // SWA_KV_HEADS), LANES, axis=-1)[:, :, None, :],
        (depth, SWA_KV_HEADS, 8, LANES * (SWA_HEADS // SWA_KV_HEADS)))

    ov = _overlap_rows()
    cmp_bias = _cmp_bias()
    causal4 = _band_bias(1, 1, NSA_HEADS)[2]
    bias_win = _band_bias(1, 1, NSA_HEADS)
    bias_swa = _band_bias(1, 1, SWA_HEADS // SWA_KV_HEADS)
    bias_dil = [_band_bias(d, 0, 1) for _, d in DIL_PATTERNS]

    xT = jnp.swapaxes(x, 1, 2)
    for l in range(depth):
        qT, kcvc, nk, v4, gates = _inproj(xT, a1[l], sh1b[l], w_inT, l, cosT, sinT)
        kc4, vc4 = _compress(kcvc, wbig, pe2, l)
        ocmp, mm = _cmp_topk(qT, kc4, vc4, ov, cmp_bias)
        owin, = _banded(qT, A_Q_BLK, nk, 1, v4, 1, bias_win, hkv=1, grp=NSA_HEADS, nprev=NSA_WINDOW // LANES,
                        mid_bias=False, dynamic=False, out_dtype=F32, name="nsa_win")
        aT = _sel_attend(qT, mm, nk, v4, causal4, ocmp, owin, gates)
        bs, lses = [], []
        for gi, (win, dil) in enumerate(DIL_PATTERNS):
            o, lse = _banded(qT, 3 + gi, nk, 2 + gi, v4, 1 + gi, bias_dil[gi], hkv=2, grp=1, nprev=dil,
                             mid_bias=dil > 1, dynamic=dil > NSUB, out_dtype=F32, want_lse=True,
                             name="dil%d" % dil)
            bs.append(o)
            lses.append(lse)
        cT, = _banded(qT, 0, nk, 5, v4, 4, bias_swa, hkv=SWA_KV_HEADS, grp=SWA_HEADS // SWA_KV_HEADS,
                      nprev=SWA_WINDOW // LANES, mid_bias=False, dynamic=False, out_dtype=BF16,
                      sinks=sink_tab[l], name="swa")
        xT = _outproj(aT, bs, lses, cT, w_outT, l, xT, gw1[l])
        xT = _ffn(xT, a2[l], sh2b[l], wgT, wuT, wdT, cv, l, gw2[l])
    return jnp.swapaxes(xT, 1, 2)
```

```python
import functools

import numpy as np
import jax
import jax.numpy as jnp
from jax import lax
from jax.experimental import pallas as pl
from jax.experimental.pallas import tpu as pltpu

F32 = jnp.float32
BF16 = jnp.bfloat16

HEAD_DIM = 64
HALF = HEAD_DIM // 2
NSA_HEADS = 4
CMP_BLOCK = 32
CMP_STRIDE = 16
SEL_BLOCK = 64
SEL_TOPK = 16
NSA_WINDOW = 512
DIL_PATTERNS = ((128, 1), (512, 4), (2048, 16))
DIL_HEADS = 6
SWA_HEADS = 6
SWA_KV_HEADS = 2
SWA_WINDOW = 128
ROPE_THETA = 10000.0
RMS_EPS = 1e-6
NEG = -1e30
FORCE = 1e4
CONV_WIDTH = 3

LANES = 128
TILE = 512
NSUB = TILE // LANES
VROWS = HEAD_DIM + 16
MEMBER_BIG = 2.0 ** 100
VMEM_LIMIT = 56 * 1024 * 1024

Q_ROWS = 1024
A_Q_BLK = 3
NK_ROWS = 768
NK_LANES = 768
V_ROWS = 640
G_ROWS = 16
W_ROWS = Q_ROWS + NK_ROWS + V_ROWS + G_ROWS
N_VPIECES = V_ROWS // HEAD_DIM


def _cp(sem):
    return pltpu.CompilerParams(dimension_semantics=sem, vmem_limit_bytes=VMEM_LIMIT)


def _const_spec(shape, index_map):
    return pl.BlockSpec(shape, index_map, pipeline_mode=pl.Buffered(1))


def _adaln_kernel(c_ref, w_ref, b_ref, o_ref):
    c = c_ref[...]
    cond = c * jax.nn.sigmoid(c)
    o_ref[...] = jnp.dot(cond, w_ref[...], preferred_element_type=F32,
                         precision=lax.Precision.HIGHEST) + b_ref[...]


def _adaln(c8, w_ada, b_ada):
    depth, d, six_d = w_ada.shape
    nblk = six_d // d
    return pl.pallas_call(
        _adaln_kernel,
        grid=(depth, nblk),
        in_specs=[pl.BlockSpec((8, d), lambda l, n: (0, 0)),
                  pl.BlockSpec((None, d, d), lambda l, n: (l, 0, n)),
                  pl.BlockSpec((None, 1, d), lambda l, n: (l, 0, n))],
        out_specs=pl.BlockSpec((None, 8, d), lambda l, n: (l, 0, n)),
        out_shape=jax.ShapeDtypeStruct((depth, 8, six_d), F32),
        compiler_params=_cp(("parallel", "parallel")),
        name="adaln",
    )(c8, w_ada, b_ada.reshape(depth, 1, six_d))


def _norm_mod_to_scratch(x_ref, a_ref, sh_ref, h_scr):
    for c in range(NSUB):
        sl = slice(c * LANES, (c + 1) * LANES)
        xs = x_ref[:, sl]
        ms = jnp.mean(xs * xs, axis=0, keepdims=True)
        h_scr[:, sl] = ((xs * lax.rsqrt(ms + RMS_EPS)) * a_ref[...] + sh_ref[...]).astype(BF16)


def _inproj_kernel(x_ref, a_ref, sh_ref, w_ref, cos_ref, sin_ref,
                   q_ref, kcvc_ref, nk_ref, v4_ref, gate_ref, h_scr):
    j = pl.program_id(1)
    _norm_mod_to_scratch(x_ref, a_ref, sh_ref, h_scr)
    h = h_scr[...]
    cos = cos_ref[...]
    sin = sin_ref[...]

    def proj(r0, r1):
        return jnp.dot(w_ref[r0:r1, :], h, preferred_element_type=F32)

    def rope(r, nh):
        outs = []
        for hh in range(nh):
            t1 = r[HEAD_DIM * hh:HEAD_DIM * hh + HALF]
            t2 = r[HEAD_DIM * hh + HALF:HEAD_DIM * (hh + 1)]
            outs.append(t1 * cos - t2 * sin)
            outs.append(t2 * cos + t1 * sin)
        return jnp.concatenate(outs, axis=0)

    for r0, r1 in ((0, 384), (384, 768), (768, 1024)):
        q_ref[r0:r1, :] = rope(proj(r0, r1), (r1 - r0) // HEAD_DIM).astype(BF16)

    base = Q_ROWS
    r = proj(base, base + 128)
    kcvc = jnp.concatenate([rope(r[0:64], 1), r[64:128]], axis=0)
    kcvc_ref[...] = kcvc.T.astype(BF16)

    r = proj(base + 128, base + 192)
    tok = j * TILE + lax.broadcasted_iota(jnp.int32, (HEAD_DIM, TILE), 1)
    row = lax.broadcasted_iota(jnp.int32, (HEAD_DIM, TILE), 0)
    member_cols = jnp.where(row == ((tok >> 6) & 15), MEMBER_BIG, 0.0).astype(F32)
    nk_ref[:, 0:128] = jnp.concatenate([rope(r, 1), member_cols], axis=0).T.astype(BF16)

    r = proj(base + 192, base + 256)
    nk_ref[:, 128:256] = jnp.concatenate([rope(r, 1), jnp.zeros((HEAD_DIM, TILE), F32)], axis=0).T.astype(BF16)

    r = proj(base + 256, base + 640)
    nk_ref[:, 256:640] = rope(r, 6).T.astype(BF16)

    r = proj(base + 640, base + 768)
    nk_ref[:, 640:768] = rope(r, 2).T.astype(BF16)

    base = Q_ROWS + NK_ROWS
    r = proj(base, base + V_ROWS).astype(BF16)
    ones = jnp.ones((VROWS - HEAD_DIM, LANES), BF16)
    for c in range(NSUB):
        for p in range(N_VPIECES):
            v4_ref[c, VROWS * p:VROWS * p + HEAD_DIM, :] = r[HEAD_DIM * p:HEAD_DIM * (p + 1),
                                                             c * LANES:(c + 1) * LANES]
            v4_ref[c, VROWS * p + HEAD_DIM:VROWS * (p + 1), :] = ones

    base = Q_ROWS + NK_ROWS + V_ROWS
    gate_ref[...] = jax.nn.sigmoid(proj(base, base + G_ROWS))


def _inproj(xT, a1, sh1, w_inT, layer, cosT, sinT):
    B, D, S = xT.shape
    nt = S // TILE
    return pl.pallas_call(
        _inproj_kernel,
        grid=(B, nt),
        in_specs=[pl.BlockSpec((None, D, TILE), lambda b, j: (b, 0, j)),
                  pl.BlockSpec((None, D, LANES), lambda b, j: (b, 0, 0)),
                  pl.BlockSpec((None, D, LANES), lambda b, j: (b, 0, 0)),
                  _const_spec((None, W_ROWS, D), lambda b, j: (layer, 0, 0)),
                  pl.BlockSpec((None, HALF, TILE), lambda b, j: (b, 0, j)),
                  pl.BlockSpec((None, HALF, TILE), lambda b, j: (b, 0, j))],
        out_specs=[pl.BlockSpec((None, Q_ROWS, TILE), lambda b, j: (b, 0, j)),
                   pl.BlockSpec((None, TILE, LANES), lambda b, j: (b, j, 0)),
                   pl.BlockSpec((None, TILE, NK_LANES), lambda b, j: (b, j, 0)),
                   pl.BlockSpec((None, NSUB, N_VPIECES * VROWS, LANES), lambda b, j: (b, j, 0, 0)),
                   pl.BlockSpec((None, G_ROWS, TILE), lambda b, j: (b, 0, j))],
        out_shape=[jax.ShapeDtypeStruct((B, Q_ROWS, S), BF16),
                   jax.ShapeDtypeStruct((B, S, LANES), BF16),
                   jax.ShapeDtypeStruct((B, S, NK_LANES), BF16),
                   jax.ShapeDtypeStruct((B, S // LANES, N_VPIECES * VROWS, LANES), BF16),
                   jax.ShapeDtypeStruct((B, G_ROWS, S), F32)],
        scratch_shapes=[pltpu.VMEM((D, TILE), BF16)],
        compiler_params=_cp(("parallel", "parallel")),
        name="inproj",
    )(xT, a1, sh1, w_inT, cosT, sinT)


def _compress_kernel(t_ref, w_ref, pe_ref, kc_ref, vc_ref):
    n = t_ref.shape[0]
    a = jnp.dot(t_ref[...], w_ref[...], preferred_element_type=F32)
    pc = jnp.dot(pe_ref[...], w_ref[...], preferred_element_type=F32)
    const = pc[0:1, 0:LANES] + pc[8:9, LANES:2 * LANES]
    cmp = a[:, 0:LANES] + pltpu.roll(a[:, LANES:2 * LANES], n - 1, 0) + const
    cmp_t = cmp.T
    ones = jnp.ones((VROWS - HEAD_DIM, LANES), BF16)
    for c in range(n // LANES):
        kc_ref[c] = cmp[c * LANES:(c + 1) * LANES].astype(BF16)
        vc_ref[c, 0:HEAD_DIM, :] = cmp_t[HEAD_DIM:2 * HEAD_DIM, c * LANES:(c + 1) * LANES].astype(BF16)
        vc_ref[c, HEAD_DIM:VROWS, :] = ones


def _compress(kcvc, wbig, pe2, layer):
    B, S, _ = kcvc.shape
    n = S // CMP_STRIDE
    nch = n // LANES
    tview = kcvc.reshape(B, n, CMP_STRIDE * LANES)
    return pl.pallas_call(
        _compress_kernel,
        grid=(B,),
        in_specs=[pl.BlockSpec((None, n, CMP_STRIDE * LANES), lambda b: (b, 0, 0)),
                  pl.BlockSpec((None, CMP_STRIDE * LANES, 2 * LANES), lambda b: (layer, 0, 0)),
                  pl.BlockSpec((None, 16, CMP_STRIDE * LANES), lambda b: (layer, 0, 0))],
        out_specs=[pl.BlockSpec((None, nch, LANES, LANES), lambda b: (b, 0, 0, 0)),
                   pl.BlockSpec((None, nch, VROWS, LANES), lambda b: (b, 0, 0, 0))],
        out_shape=[jax.ShapeDtypeStruct((B, nch, LANES, LANES), BF16),
                   jax.ShapeDtypeStruct((B, nch, VROWS, LANES), BF16)],
        compiler_params=_cp(("parallel",)),
        name="compress",
    )(tview, wbig, pe2)


def _stack_heads(q_ref, nh, lane_slice=slice(None)):
    return jnp.concatenate([q_ref[HEAD_DIM * h:HEAD_DIM * (h + 1), lane_slice] for h in range(nh)], axis=1)


def _cmp_kernel(q_ref, kc_ref, vc_ref, ov_ref, bias_ref, o_ref, mm_ref, imp_scr, *, n_sel):
    i = pl.program_id(1)
    cd = i // 16
    nch = kc_ref.shape[0]
    q = _stack_heads(q_ref, NSA_HEADS)
    edge = bias_ref[...]

    ss = []
    for c in range(nch):
        b = jnp.where(c < cd, 0.0, jnp.where(c == cd, edge, NEG))
        s = jnp.dot(kc_ref[c, :, 0:HEAD_DIM], q, preferred_element_type=F32)
        ss.append(s + jnp.concatenate([b] * NSA_HEADS, axis=1))
    m = jnp.max(functools.reduce(jnp.maximum, ss), axis=0, keepdims=True)
    valid = m > 0.5 * NEG

    imp_scr[...] = jnp.zeros(imp_scr.shape, F32)
    acc = None
    for c in range(nch):
        p = jnp.exp(ss[c] - m).astype(BF16)
        t = jnp.dot(vc_ref[c], p, preferred_element_type=F32)
        acc = t if acc is None else acc + t
        imp_scr[32 * c:32 * c + 40, :] += jnp.dot(ov_ref[...], p, preferred_element_type=F32)

    inv = jnp.where(valid, 1.0 / acc[HEAD_DIM:HEAD_DIM + 1], 0.0)
    o = acc[0:HEAD_DIM] * inv
    for h in range(NSA_HEADS):
        o_ref[HEAD_DIM * h:HEAD_DIM * (h + 1), :] = o[:, h * LANES:(h + 1) * LANES]

    imp = jnp.zeros((n_sel, LANES), F32)
    for h in range(NSA_HEADS):
        sl = slice(h * LANES, (h + 1) * LANES)
        imp = imp + imp_scr[0:n_sel, sl] * inv[:, sl]

    blk = lax.broadcasted_iota(jnp.int32, (n_sel, LANES), 0).astype(F32)
    t = i * LANES + lax.broadcasted_iota(jnp.int32, (n_sel, LANES), 1)
    cur = (t >> 6).astype(F32)
    forced = (blk == 0.0) | (blk == cur) | (blk == cur - 1.0)
    imp = jnp.where(forced, FORCE, imp)
    imp = jnp.where(blk <= cur, imp, NEG)

    def pick(_, carry):
        imp, chosen = carry
        mx = jnp.max(imp, axis=0, keepdims=True)
        first = jnp.min(jnp.where(imp == mx, blk, float(n_sel)), axis=0, keepdims=True)
        hit = blk == first
        return jnp.where(hit, -jnp.inf, imp), jnp.where(hit, 1.0, chosen)

    _, chosen = lax.fori_loop(0, min(SEL_TOPK, n_sel), pick, (imp, jnp.zeros((n_sel, LANES), F32)))
    member = (chosen > 0.5) & (blk <= cur)
    mm_ref[...] = jnp.where(member, 0.0, -1.0).astype(BF16)


def _cmp_topk(qT, kc4, vc4, ov, cmp_bias):
    B, _, S = qT.shape
    nblk = S // LANES
    nch = kc4.shape[1]
    n_sel = S // SEL_BLOCK
    nq = NSA_HEADS * LANES
    return pl.pallas_call(
        functools.partial(_cmp_kernel, n_sel=n_sel),
        grid=(B, nblk),
        in_specs=[pl.BlockSpec((None, NSA_HEADS * HEAD_DIM, LANES), lambda b, i: (b, A_Q_BLK, i)),
                  pl.BlockSpec((None, nch, LANES, LANES), lambda b, i: (b, 0, 0, 0)),
                  pl.BlockSpec((None, nch, VROWS, LANES), lambda b, i: (b, 0, 0, 0)),
                  pl.BlockSpec((40, LANES), lambda b, i: (0, 0)),
                  pl.BlockSpec((None, LANES, LANES), lambda b, i: (i % 16, 0, 0))],
        out_specs=[pl.BlockSpec((None, NSA_HEADS * HEAD_DIM, LANES), lambda b, i: (b, 0, i)),
                   pl.BlockSpec((None, n_sel, LANES), lambda b, i: (b, 0, i))],
        out_shape=[jax.ShapeDtypeStruct((B, NSA_HEADS * HEAD_DIM, S), F32),
                   jax.ShapeDtypeStruct((B, n_sel, S), BF16)],
        scratch_shapes=[pltpu.VMEM((32 * nch + 64, nq), F32)],
        compiler_params=_cp(("parallel", "parallel")),
        name="nsa_cmp_topk",
    )(qT, kc4, vc4, ov, cmp_bias)


SEL_GROUP = 8


def _sel_kernel(q_ref, mm_ref, ks_ref, v_ref, causal_ref, ocmp_ref, owin_ref, gate_ref, out_ref,
                qa_scr, mm_scr, s_scr):
    i = pl.program_id(1)
    nq = NSA_HEADS * LANES
    qa_scr[0:HEAD_DIM, :] = _stack_heads(q_ref, NSA_HEADS)
    qa_scr[HEAD_DIM:LANES, :] = jnp.zeros((LANES - HEAD_DIM, nq), BF16)
    mm = mm_ref[...]
    for h in range(NSA_HEADS):
        mm_scr[:, h * LANES:(h + 1) * LANES] = mm

    def group_scores(gi):
        qa_scr[HEAD_DIM:HEAD_DIM + 16, :] = mm_scr[pl.ds(pl.multiple_of(gi * 16, 16), 16), :]
        qa = qa_scr[...]
        out = []
        for u in range(SEL_GROUP):
            row = pl.multiple_of((gi * SEL_GROUP + u) * LANES, LANES)
            out.append(jnp.dot(ks_ref[pl.ds(row, LANES), :], qa, preferred_element_type=F32))
        return out

    def update(gi, ss, m, acc):
        m_new = jnp.maximum(m, jnp.max(functools.reduce(jnp.maximum, ss), axis=0, keepdims=True))
        pv = None
        for u, s in enumerate(ss):
            t = jnp.dot(v_ref[gi * SEL_GROUP + u], jnp.exp(s - m_new).astype(BF16), preferred_element_type=F32)
            pv = t if pv is None else pv + t
        return m_new, acc * jnp.exp(m - m_new) + pv

    def body(gi, carry):
        return update(gi, group_scores(gi), *carry)

    last = i // SEL_GROUP
    m, acc = lax.fori_loop(0, last, body, (jnp.full((1, nq), NEG, F32), jnp.zeros((VROWS, nq), F32)))
    for u, s in enumerate(group_scores(last)):
        s_scr[u] = s
    diag = i % SEL_GROUP
    s_scr[diag] = s_scr[diag] + causal_ref[...]
    m, acc = update(last, [s_scr[u] for u in range(SEL_GROUP)], m, acc)
    o = acc[0:HEAD_DIM] * (1.0 / acc[HEAD_DIM:HEAD_DIM + 1])

    g = gate_ref[...]
    for h in range(NSA_HEADS):
        rs = slice(HEAD_DIM * h, HEAD_DIM * (h + 1))
        out = (g[3 * h:3 * h + 1] * ocmp_ref[rs, :] + g[3 * h + 1:3 * h + 2] * o[:, h * LANES:(h + 1) * LANES]
               + g[3 * h + 2:3 * h + 3] * owin_ref[rs, :])
        out_ref[rs, :] = out.astype(BF16)


def _sel_attend(qT, mm, nk, v4, causal4, ocmp, owin, gates):
    B, _, S = qT.shape
    nblk = S // LANES
    n_sel = S // SEL_BLOCK
    nq = NSA_HEADS * LANES
    ar = NSA_HEADS * HEAD_DIM
    return pl.pallas_call(
        _sel_kernel,
        grid=(B, nblk),
        in_specs=[pl.BlockSpec((None, ar, LANES), lambda b, i: (b, A_Q_BLK, i)),
                  pl.BlockSpec((None, n_sel, LANES), lambda b, i: (b, 0, i)),
                  pl.BlockSpec((None, S, LANES), lambda b, i: (b, 0, 0)),
                  pl.BlockSpec((None, nblk, VROWS, LANES), lambda b, i: (b, 0, 0, 0)),
                  pl.BlockSpec((LANES, nq), lambda b, i: (0, 0)),
                  pl.BlockSpec((None, ar, LANES), lambda b, i: (b, 0, i)),
                  pl.BlockSpec((None, ar, LANES), lambda b, i: (b, 0, i)),
                  pl.BlockSpec((None, G_ROWS, LANES), lambda b, i: (b, 0, i))],
        out_specs=pl.BlockSpec((None, ar, LANES), lambda b, i: (b, 0, i)),
        out_shape=jax.ShapeDtypeStruct((B, ar, S), BF16),
        scratch_shapes=[pltpu.VMEM((LANES, nq), BF16), pltpu.VMEM((n_sel, nq), BF16),
                        pltpu.VMEM((SEL_GROUP, LANES, nq), F32)],
        compiler_params=_cp(("parallel", "parallel")),
        name="nsa_sel",
    )(qT, mm, nk, v4, causal4, ocmp, owin, gates)


def _banded_kernel(*refs, hkv, grp, nprev, mid_bias, dynamic, has_sink, want_lse):
    q_ref, k_ref, v_ref, bias_ref = refs[:4]
    pos = 4
    sink_ref = None
    if has_sink:
        sink_ref = refs[pos]
        pos += 1
    o_ref = refs[pos]
    lse_ref = refs[pos + 1] if want_lse else None
    j = pl.program_id(1)

    def one(s, g, first):
        lanes = slice(s * LANES, (s + 1) * LANES)
        qg = jnp.concatenate([q_ref[HEAD_DIM * (g * grp + u):HEAD_DIM * (g * grp + u + 1), lanes]
                              for u in range(grp)], axis=1)
        parts = []
        for ci in range(nprev + 1):
            which = 0 if ci == 0 else (2 if ci == nprev else 1)
            if first:
                kc = s - nprev + ci
                if kc < 0:
                    continue
                kcc = kc
                row = kc * LANES
            else:
                kc = j * NSUB + s - nprev + ci
                kcc = jnp.maximum(kc, 0) if dynamic else kc
                row = pl.multiple_of(kcc * LANES, LANES)
            sc = jnp.dot(k_ref[pl.ds(row, LANES), HEAD_DIM * g:HEAD_DIM * (g + 1)], qg,
                         preferred_element_type=F32)
            if dynamic:
                sc = sc + bias_ref[jnp.where(kc >= 0, which, 3)]
            elif which != 1 or mid_bias:
                sc = sc + bias_ref[which]
            parts.append((kcc, sc))
        m = None
        for _, sc in parts:
            mc = jnp.max(sc, axis=0, keepdims=True)
            m = mc if m is None else jnp.maximum(m, mc)
        if has_sink:
            sk = sink_ref[g, 0:1, :]
            m = jnp.maximum(m, sk)
        acc = None
        for kcc, sc in parts:
            t = jnp.dot(v_ref[kcc, VROWS * g:VROWS * (g + 1), :], jnp.exp(sc - m).astype(BF16),
                        preferred_element_type=F32)
            acc = t if acc is None else acc + t
        l = acc[HEAD_DIM:HEAD_DIM + 1]
        if has_sink:
            l = l + jnp.exp(sk - m)
        o = acc[0:HEAD_DIM] * (1.0 / l)
        for u in range(grp):
            hq = g * grp + u
            o_ref[HEAD_DIM * hq:HEAD_DIM * (hq + 1), lanes] = o[:, u * LANES:(u + 1) * LANES].astype(o_ref.dtype)
        if want_lse:
            lse_ref[8 * g:8 * (g + 1), lanes] = jnp.broadcast_to(m + jnp.log(l), (8, LANES))

    def run(first):
        for s in range(NSUB):
            for g in range(hkv):
                one(s, g, first)

    if dynamic:
        run(False)
    else:
        pl.when(j == 0)(lambda: run(True))
        pl.when(j > 0)(lambda: run(False))


def _banded(qT, q_blk, nk, k_blk, v4, v_blk, bias, *, hkv, grp, nprev, mid_bias, dynamic,
            out_dtype, sinks=None, want_lse=False, name):
    B, _, S = qT.shape
    nt = S // TILE
    qrows = hkv * grp * HEAD_DIM
    in_specs = [pl.BlockSpec((None, qrows, TILE), lambda b, j: (b, q_blk, j)),
                pl.BlockSpec((None, S, LANES), lambda b, j: (b, 0, k_blk)),
                pl.BlockSpec((None, S // LANES, hkv * VROWS, LANES), lambda b, j: (b, 0, v_blk, 0)),
                pl.BlockSpec(bias.shape, lambda b, j: (0, 0, 0))]
    args = [qT, nk, v4, bias]
    if sinks is not None:
        in_specs.append(pl.BlockSpec(sinks.shape, lambda b, j: (0, 0, 0)))
        args.append(sinks)
    out_specs = [pl.BlockSpec((None, qrows, TILE), lambda b, j: (b, 0, j))]
    out_shape = [jax.ShapeDtypeStruct((B, qrows, S), out_dtype)]
    if want_lse:
        out_specs.append(pl.BlockSpec((None, 8 * hkv, TILE), lambda b, j: (b, 0, j)))
        out_shape.append(jax.ShapeDtypeStruct((B, 8 * hkv, S), F32))
    return pl.pallas_call(
        functools.partial(_banded_kernel, hkv=hkv, grp=grp, nprev=nprev, mid_bias=mid_bias,
                          dynamic=dynamic, has_sink=sinks is not None, want_lse=want_lse),
        grid=(B, nt),
        in_specs=in_specs,
        out_specs=out_specs,
        out_shape=out_shape,
        compiler_params=_cp(("parallel", "parallel")),
        name=name,
    )(*args)


def _outproj_kernel(a_ref, b0_ref, b1_ref, b2_ref, l0_ref, l1_ref, l2_ref, c_ref, w_ref, x_ref, gw_ref,
                    o_ref, mix_scr):
    b_refs = (b0_ref, b1_ref, b2_ref)
    l_refs = (l0_ref, l1_ref, l2_ref)
    for h in range(2):
        ls = [r[8 * h:8 * h + 1, :] for r in l_refs]
        mx = jnp.maximum(jnp.maximum(ls[0], ls[1]), ls[2])
        es = [jnp.exp(v - mx) for v in ls]
        inv = 1.0 / (es[0] + es[1] + es[2])
        for g in range(3):
            rs = slice(HEAD_DIM * h, HEAD_DIM * (h + 1))
            mix_scr[128 * g + HEAD_DIM * h:128 * g + HEAD_DIM * (h + 1), :] = (
                b_refs[g][rs, :] * (es[g] * inv)).astype(BF16)
    y = jnp.dot(w_ref[:, 0:256], a_ref[...], preferred_element_type=F32)
    y = y + jnp.dot(w_ref[:, 256:640], mix_scr[...], preferred_element_type=F32)
    y = y + jnp.dot(w_ref[:, 640:1024], c_ref[...], preferred_element_type=F32)
    ms = jnp.mean(y * y, axis=0, keepdims=True)
    yn = y * lax.rsqrt(ms + RMS_EPS)
    gw = gw_ref[...]
    for c in range(NSUB):
        sl = slice(c * LANES, (c + 1) * LANES)
        o_ref[:, sl] = x_ref[:, sl] + gw * yn[:, sl]


def _outproj(aT, bs, lses, cT, w_outT, layer, xT, gw):
    B, D, S = xT.shape
    nt = S // TILE
    tile = lambda rows: pl.BlockSpec((None, rows, TILE), lambda b, j: (b, 0, j))
    return pl.pallas_call(
        _outproj_kernel,
        grid=(B, nt),
        in_specs=[tile(256), tile(128), tile(128), tile(128), tile(16), tile(16), tile(16), tile(384),
                  _const_spec((None, D, D), lambda b, j: (layer, 0, 0)),
                  tile(D),
                  pl.BlockSpec((None, D, LANES), lambda b, j: (b, 0, 0))],
        out_specs=tile(D),
        out_shape=jax.ShapeDtypeStruct((B, D, S), F32),
        scratch_shapes=[pltpu.VMEM((384, TILE), BF16)],
        compiler_params=_cp(("parallel", "parallel")),
        name="outproj",
    )(aT, bs[0], bs[1], bs[2], lses[0], lses[1], lses[2], cT, w_outT, xT, gw)


def _ffn_kernel(x_ref, a_ref, sh_ref, wg_ref, wu_ref, wd_ref, cv_ref, gw_ref, o_ref,
                h_scr, carry_scr, y_scr, *, fchunk):
    j = pl.program_id(1)
    d_ff = wg_ref.shape[0]

    @pl.when(j == 0)
    def _():
        carry_scr[...] = jnp.zeros(carry_scr.shape, F32)

    _norm_mod_to_scratch(x_ref, a_ref, sh_ref, h_scr)
    h = h_scr[...]
    lane = lax.broadcasted_iota(jnp.int32, (fchunk, LANES), 1)
    for c in range(d_ff // fchunk):
        rs = slice(c * fchunk, (c + 1) * fchunk)
        g = jnp.dot(wg_ref[rs, :], h, preferred_element_type=F32)
        prev = carry_scr[rs, :]
        carry_scr[rs, :] = g[:, TILE - LANES:TILE]
        g1 = pltpu.roll(g, 1, 1)
        g2 = pltpu.roll(g, 2, 1)
        fix1 = jnp.where(lane < 1, pltpu.roll(prev, 1, 1), g1[:, 0:LANES])
        fix2 = jnp.where(lane < 2, pltpu.roll(prev, 2, 1), g2[:, 0:LANES])
        g1 = jnp.concatenate([fix1, g1[:, LANES:]], axis=1)
        g2 = jnp.concatenate([fix2, g2[:, LANES:]], axis=1)
        w0 = jnp.concatenate([cv_ref[0, rs, :]] * NSUB, axis=1)
        w1 = jnp.concatenate([cv_ref[1, rs, :]] * NSUB, axis=1)
        w2 = jnp.concatenate([cv_ref[2, rs, :]] * NSUB, axis=1)
        cb = jnp.concatenate([cv_ref[3, rs, :]] * NSUB, axis=1)
        acc = g2 * w0 + g1 * w1 + g * w2 + cb
        up = jnp.dot(wu_ref[rs, :], h, preferred_element_type=F32)
        act = (jax.nn.gelu(acc, approximate=True) * up).astype(BF16)
        part = jnp.dot(wd_ref[:, rs], act, preferred_element_type=F32)
        if c == 0:
            y_scr[...] = part
        else:
            y_scr[...] += part
    y = y_scr[...]
    ms = jnp.mean(y * y, axis=0, keepdims=True)
    yn = y * lax.rsqrt(ms + RMS_EPS)
    gw = gw_ref[...]
    for c in range(NSUB):
        sl = slice(c * LANES, (c + 1) * LANES)
        o_ref[:, sl] = x_ref[:, sl] + gw * yn[:, sl]


def _ffn(xT, a2, sh2, wgT, wuT, wdT, cv, layer, gw):
    B, D, S = xT.shape
    nt = S // TILE
    d_ff = wgT.shape[1]
    fchunk = 256
    return pl.pallas_call(
        functools.partial(_ffn_kernel, fchunk=fchunk),
        grid=(B, nt),
        in_specs=[pl.BlockSpec((None, D, TILE), lambda b, j: (b, 0, j)),
                  pl.BlockSpec((None, D, LANES), lambda b, j: (b, 0, 0)),
                  pl.BlockSpec((None, D, LANES), lambda b, j: (b, 0, 0)),
                  _const_spec((None, d_ff, D), lambda b, j: (layer, 0, 0)),
                  _const_spec((None, d_ff, D), lambda b, j: (layer, 0, 0)),
                  _const_spec((None, D, d_ff), lambda b, j: (layer, 0, 0)),
                  _const_spec((None, 4, d_ff, LANES), lambda b, j: (layer, 0, 0, 0)),
                  pl.BlockSpec((None, D, LANES), lambda b, j: (b, 0, 0))],
        out_specs=pl.BlockSpec((None, D, TILE), lambda b, j: (b, 0, j)),
        out_shape=jax.ShapeDtypeStruct((B, D, S), F32),
        scratch_shapes=[pltpu.VMEM((D, TILE), BF16), pltpu.VMEM((d_ff, LANES), F32),
                        pltpu.VMEM((D, TILE), F32)],
        compiler_params=_cp(("arbitrary", "arbitrary")),
        name="convffn",
    )(xT, a2, sh2, wgT, wuT, wdT, cv, gw)


def _band_bias(dilation, old_edge, reps):
    kk = np.arange(LANES)[:, None]
    qq = np.arange(LANES)[None, :]
    res = ((qq - kk) % dilation) == 0
    tabs = [res & (kk - qq >= old_edge), res, res & (kk <= qq), np.zeros_like(res)]
    out = np.stack([np.where(t, 0.0, NEG) for t in tabs]).astype(np.float32)
    return jnp.asarray(np.tile(out, (1, 1, reps)))


def _cmp_bias():
    nn = np.arange(LANES)[:, None]
    qq = np.arange(LANES)[None, :]
    tabs = [np.where(CMP_STRIDE * nn + CMP_BLOCK - 1 <= LANES * r + qq, 0.0, NEG) for r in range(16)]
    return jnp.asarray(np.stack(tabs).astype(np.float32))


def _overlap_rows():
    jj = np.arange(40)[:, None]
    nn = np.arange(LANES)[None, :]
    return jnp.asarray(((nn >= 4 * jj - 1) & (nn <= 4 * jj + 3)).astype(np.float32), dtype=BF16)


_IN_COL_ORDER = ((1804, 2188), (652, 1036), (0, 256),
                 (256, 320), (320, 384), (384, 448), (512, 576), (1036, 1420), (2188, 2316),
                 (448, 512), (576, 640), (1420, 1804), (2316, 2444),
                 (640, 652))


def _prep_w_in(w_in):
    cols = np.concatenate([np.arange(a, b) for a, b in _IN_COL_ORDER])
    wt = jnp.swapaxes(w_in[:, :, cols], 1, 2)
    scale = np.ones((wt.shape[1], 1), np.float32)
    scale[:Q_ROWS] = HEAD_DIM ** -0.5
    wt = wt * scale
    wt = jnp.pad(wt, ((0, 0), (0, W_ROWS - wt.shape[1]), (0, 0)))
    return wt.astype(BF16)


def _prep_compress(w_ck, w_cv, pe_k, pe_v):
    L = w_ck.shape[0]
    half = CMP_BLOCK // 2

    def big(lo):
        wk = w_ck[:, lo:lo + half]
        wv = w_cv[:, lo:lo + half]
        z = jnp.zeros_like(wk)
        top = jnp.concatenate([wk, z], axis=-1)
        bot = jnp.concatenate([z, wv], axis=-1)
        return jnp.concatenate([top, bot], axis=2).reshape(L, half * LANES, LANES)

    wbig = jnp.concatenate([big(0), big(half)], axis=-1).astype(BF16)
    pe = jnp.concatenate([pe_k, pe_v], axis=-1)
    pe2 = jnp.zeros((L, 16, half * LANES), F32)
    pe2 = pe2.at[:, 0].set(pe[:, :half].reshape(L, -1)).at[:, 8].set(pe[:, half:].reshape(L, -1))
    return wbig, pe2.astype(BF16)


def _lane_bcast(v):
    return jnp.broadcast_to(v[..., None], v.shape + (LANES,))


def kernel(x, c, positions, w_in, w_out, w_ada, b_ada, norm_w, cmp_w_k, cmp_w_v, cmp_pe_k, cmp_pe_v,
           sinks, w_gate, w_up, conv_w, conv_b, w_down):
    B, S, D = x.shape
    depth = w_in.shape[0]
    assert S % 2048 == 0 and D == 1024 and w_in.shape[2] == 2444

    inv = ROPE_THETA ** (-jnp.arange(0, HEAD_DIM, 2, dtype=F32) / HEAD_DIM)
    ang = positions.astype(F32)[:, None, :] * inv[None, :, None]
    cosT, sinT = jnp.cos(ang), jnp.sin(ang)

    c8 = jnp.pad(c, ((0, 8 - B), (0, 0)))
    ada = _adaln(c8, w_ada, b_ada)[:, :B]
    sh1, sc1, g1, sh2, sc2, g2 = [ada[:, :, k * D:(k + 1) * D] for k in range(6)]
    nw = norm_w[:, :, None, :]
    a1 = _lane_bcast(nw[:, 0] * (1 + sc1))
    gw1 = _lane_bcast(g1 * nw[:, 1])
    a2 = _lane_bcast(nw[:, 2] * (1 + sc2))
    gw2 = _lane_bcast(g2 * nw[:, 3])
    sh1b, sh2b = _lane_bcast(sh1), _lane_bcast(sh2)

    w_inT = _prep_w_in(w_in)
    w_outT = jnp.swapaxes(w_out, 1, 2).astype(BF16)
    wgT = jnp.swapaxes(w_gate, 1, 2).astype(BF16)
    wuT = jnp.swapaxes(w_up, 1, 2).astype(BF16)
    wdT = jnp.swapaxes(w_down, 1, 2).astype(BF16)
    cv = _lane_bcast(jnp.concatenate([conv_w, conv_b[:, None, :]], axis=1))
    wbig, pe2 = _prep_compress(cmp_w_k, cmp_w_v, cmp_pe_k, cmp_pe_v)
    sink_tab = jnp.broadcast_to(
        jnp.repeat(sinks.reshape(depth, SWA_KV_HEADS, SWA_HEADS // SWA_KV_HEADS), LANES, axis=-1)[:, :, None, :],
        (depth, SWA_KV_HEADS, 8, LANES * (SWA_HEADS // SWA_KV_HEADS)))

    ov = _overlap_rows()
    cmp_bias = _cmp_bias()
    causal4 = _band_bias(1, 1, NSA_HEADS)[2]
    bias_win = _band_bias(1, 1, NSA_HEADS)
    bias_swa = _band_bias(1, 1, SWA_HEADS // SWA_KV_HEADS)
    bias_dil = [_band_bias(d, 0, 1) for _, d in DIL_PATTERNS]

    xT = jnp.swapaxes(x, 1, 2)
    for l in range(depth):
        qT, kcvc, nk, v4, gates = _inproj(xT, a1[l], sh1b[l], w_inT, l, cosT, sinT)
        kc4, vc4 = _compress(kcvc, wbig, pe2, l)
        ocmp, mm = _cmp_topk(qT, kc4, vc4, ov, cmp_bias)
        owin, = _banded(qT, A_Q_BLK, nk, 1, v4, 1, bias_win, hkv=1, grp=NSA_HEADS, nprev=NSA_WINDOW // LANES,
                        mid_bias=False, dynamic=False, out_dtype=F32, name="nsa_win")
        aT = _sel_attend(qT, mm, nk, v4, causal4, ocmp, owin, gates)
        bs, lses = [], []
        for gi, (win, dil) in enumerate(DIL_PATTERNS):
            o, lse = _banded(qT, 3 + gi, nk, 2 + gi, v4, 1 + gi, bias_dil[gi], hkv=2, grp=1, nprev=dil,
                             mid_bias=dil > 1, dynamic=dil > NSUB, out_dtype=F32, want_lse=True,
                             name="dil%d" % dil)
            bs.append(o)
            lses.append(lse)
        cT, = _banded(qT, 0, nk, 5, v4, 4, bias_swa, hkv=SWA_KV_HEADS, grp=SWA_HEADS // SWA_KV_HEADS,
                      nprev=SWA_WINDOW // LANES, mid_bias=False, dynamic=False, out_dtype=BF16,
                      sinks=sink_tab[l], name="swa")
        xT = _outproj(aT, bs, lses, cT, w_outT, l, xT, gw1[l])
        xT = _ffn(xT, a2[l], sh2b[l], wgT, wuT, wdT, cv, l, gw2[l])
    return jnp.swapaxes(xT, 1, 2)
```

```python
import functools

import numpy as np
import jax
import jax.numpy as jnp
from jax import lax
from jax.experimental import pallas as pl
from jax.experimental.pallas import tpu as pltpu

F32 = jnp.float32
BF16 = jnp.bfloat16

HEAD_DIM = 64
HALF = HEAD_DIM // 2
NSA_HEADS = 4
CMP_BLOCK = 32
CMP_STRIDE = 16
SEL_BLOCK = 64
SEL_TOPK = 16
NSA_WINDOW = 512
DIL_PATTERNS = ((128, 1), (512, 4), (2048, 16))
DIL_HEADS = 6
SWA_HEADS = 6
SWA_KV_HEADS = 2
SWA_WINDOW = 128
ROPE_THETA = 10000.0
RMS_EPS = 1e-6
NEG = -1e30
FORCE = 1e4
CONV_WIDTH = 3

LANES = 128
TILE = 512
NSUB = TILE // LANES
VROWS = HEAD_DIM + 16
MEMBER_BIG = 2.0 ** 100
VMEM_LIMIT = 56 * 1024 * 1024

Q_ROWS = 1024
A_Q_BLK = 3
NK_ROWS = 768
NK_LANES = 768
V_ROWS = 640
G_ROWS = 16
W_ROWS = Q_ROWS + NK_ROWS + V_ROWS + G_ROWS
N_VPIECES = V_ROWS // HEAD_DIM


def _cp(sem):
    return pltpu.CompilerParams(dimension_semantics=sem, vmem_limit_bytes=VMEM_LIMIT)


def _const_spec(shape, index_map):
    return pl.BlockSpec(shape, index_map, pipeline_mode=pl.Buffered(1))


def _adaln_kernel(c_ref, w_ref, b_ref, o_ref):
    c = c_ref[...]
    cond = c * jax.nn.sigmoid(c)
    o_ref[...] = jnp.dot(cond, w_ref[...], preferred_element_type=F32,
                         precision=lax.Precision.HIGHEST) + b_ref[...]


def _adaln(c8, w_ada, b_ada):
    depth, d, six_d = w_ada.shape
    nblk = six_d // d
    return pl.pallas_call(
        _adaln_kernel,
        grid=(depth, nblk),
        in_specs=[pl.BlockSpec((8, d), lambda l, n: (0, 0)),
                  pl.BlockSpec((None, d, d), lambda l, n: (l, 0, n)),
                  pl.BlockSpec((None, 1, d), lambda l, n: (l, 0, n))],
        out_specs=pl.BlockSpec((None, 8, d), lambda l, n: (l, 0, n)),
        out_shape=jax.ShapeDtypeStruct((depth, 8, six_d), F32),
        compiler_params=_cp(("parallel", "parallel")),
        name="adaln",
    )(c8, w_ada, b_ada.reshape(depth, 1, six_d))


def _norm_mod_to_scratch(x_ref, a_ref, sh_ref, h_scr):
    for c in range(NSUB):
        sl = slice(c * LANES, (c + 1) * LANES)
        xs = x_ref[:, sl]
        ms = jnp.mean(xs * xs, axis=0, keepdims=True)
        h_scr[:, sl] = ((xs * lax.rsqrt(ms + RMS_EPS)) * a_ref[...] + sh_ref[...]).astype(BF16)


def _inproj_kernel(x_ref, a_ref, sh_ref, w_ref, cos_ref, sin_ref,
                   q_ref, kcvc_ref, nk_ref, v4_ref, gate_ref, h_scr):
    j = pl.program_id(1)
    _norm_mod_to_scratch(x_ref, a_ref, sh_ref, h_scr)
    h = h_scr[...]
    cos = cos_ref[...]
    sin = sin_ref[...]

    def proj(r0, r1):
        return jnp.dot(w_ref[r0:r1, :], h, preferred_element_type=F32)

    def rope(r, nh):
        outs = []
        for hh in range(nh):
            t1 = r[HEAD_DIM * hh:HEAD_DIM * hh + HALF]
            t2 = r[HEAD_DIM * hh + HALF:HEAD_DIM * (hh + 1)]
            outs.append(t1 * cos - t2 * sin)
            outs.append(t2 * cos + t1 * sin)
        return jnp.concatenate(outs, axis=0)

    for r0, r1 in ((0, 384), (384, 768), (768, 1024)):
        q_ref[r0:r1, :] = rope(proj(r0, r1), (r1 - r0) // HEAD_DIM).astype(BF16)

    base = Q_ROWS
    r = proj(base, base + 128)
    kcvc = jnp.concatenate([rope(r[0:64], 1), r[64:128]], axis=0)
    kcvc_ref[...] = kcvc.T.astype(BF16)

    r = proj(base + 128, base + 192)
    tok = j * TILE + lax.broadcasted_iota(jnp.int32, (HEAD_DIM, TILE), 1)
    row = lax.broadcasted_iota(jnp.int32, (HEAD_DIM, TILE), 0)
    member_cols = jnp.where(row == ((tok >> 6) & 15), MEMBER_BIG, 0.0).astype(F32)
    nk_ref[:, 0:128] = jnp.concatenate([rope(r, 1), member_cols], axis=0).T.astype(BF16)

    r = proj(base + 192, base + 256)
    nk_ref[:, 128:256] = jnp.concatenate([rope(r, 1), jnp.zeros((HEAD_DIM, TILE), F32)], axis=0).T.astype(BF16)

    r = proj(base + 256, base + 640)
    nk_ref[:, 256:640] = rope(r, 6).T.astype(BF16)

    r = proj(base + 640, base + 768)
    nk_ref[:, 640:768] = rope(r, 2).T.astype(BF16)

    base = Q_ROWS + NK_ROWS
    r = proj(base, base + V_ROWS).astype(BF16)
    ones = jnp.ones((VROWS - HEAD_DIM, LANES), BF16)
    for c in range(NSUB):
        for p in range(N_VPIECES):
            v4_ref[c, VROWS * p:VROWS * p + HEAD_DIM, :] = r[HEAD_DIM * p:HEAD_DIM * (p + 1),
                                                             c * LANES:(c + 1) * LANES]
            v4_ref[c, VROWS * p + HEAD_DIM:VROWS * (p + 1), :] = ones

    base = Q_ROWS + NK_ROWS + V_ROWS
    gate_ref[...] = jax.nn.sigmoid(proj(base, base + G_ROWS))


def _inproj(xT, a1, sh1, w_inT, layer, cosT, sinT):
    B, D, S = xT.shape
    nt = S // TILE
    return pl.pallas_call(
        _inproj_kernel,
        grid=(B, nt),
        in_specs=[pl.BlockSpec((None, D, TILE), lambda b, j: (b, 0, j)),
                  pl.BlockSpec((None, D, LANES), lambda b, j: (b, 0, 0)),
                  pl.BlockSpec((None, D, LANES), lambda b, j: (b, 0, 0)),
                  _const_spec((None, W_ROWS, D), lambda b, j: (layer, 0, 0)),
                  pl.BlockSpec((None, HALF, TILE), lambda b, j: (b, 0, j)),
                  pl.BlockSpec((None, HALF, TILE), lambda b, j: (b, 0, j))],
        out_specs=[pl.BlockSpec((None, Q_ROWS, TILE), lambda b, j: (b, 0, j)),
                   pl.BlockSpec((None, TILE, LANES), lambda b, j: (b, j, 0)),
                   pl.BlockSpec((None, TILE, NK_LANES), lambda b, j: (b, j, 0)),
                   pl.BlockSpec((None, NSUB, N_VPIECES * VROWS, LANES), lambda b, j: (b, j, 0, 0)),
                   pl.BlockSpec((None, G_ROWS, TILE), lambda b, j: (b, 0, j))],
        out_shape=[jax.ShapeDtypeStruct((B, Q_ROWS, S), BF16),
                   jax.ShapeDtypeStruct((B, S, LANES), BF16),
                   jax.ShapeDtypeStruct((B, S, NK_LANES), BF16),
                   jax.ShapeDtypeStruct((B, S // LANES, N_VPIECES * VROWS, LANES), BF16),
                   jax.ShapeDtypeStruct((B, G_ROWS, S), F32)],
        scratch_shapes=[pltpu.VMEM((D, TILE), BF16)],
        compiler_params=_cp(("parallel", "parallel")),
        name="inproj",
    )(xT, a1, sh1, w_inT, cosT, sinT)


def _compress_kernel(t_ref, w_ref, pe_ref, kc_ref, vc_ref):
    n = t_ref.shape[0]
    a = jnp.dot(t_ref[...], w_ref[...], preferred_element_type=F32)
    pc = jnp.dot(pe_ref[...], w_ref[...], preferred_element_type=F32)
    const = pc[0:1, 0:LANES] + pc[8:9, LANES:2 * LANES]
    cmp = a[:, 0:LANES] + pltpu.roll(a[:, LANES:2 * LANES], n - 1, 0) + const
    cmp_t = cmp.T
    ones = jnp.ones((VROWS - HEAD_DIM, LANES), BF16)
    for c in range(n // LANES):
        kc_ref[c] = cmp[c * LANES:(c + 1) * LANES].astype(BF16)
        vc_ref[c, 0:HEAD_DIM, :] = cmp_t[HEAD_DIM:2 * HEAD_DIM, c * LANES:(c + 1) * LANES].astype(BF16)
        vc_ref[c, HEAD_DIM:VROWS, :] = ones


def _compress(kcvc, wbig, pe2, layer):
    B, S, _ = kcvc.shape
    n = S // CMP_STRIDE
    nch = n // LANES
    tview = kcvc.reshape(B, n, CMP_STRIDE * LANES)
    return pl.pallas_call(
        _compress_kernel,
        grid=(B,),
        in_specs=[pl.BlockSpec((None, n, CMP_STRIDE * LANES), lambda b: (b, 0, 0)),
                  pl.BlockSpec((None, CMP_STRIDE * LANES, 2 * LANES), lambda b: (layer, 0, 0)),
                  pl.BlockSpec((None, 16, CMP_STRIDE * LANES), lambda b: (layer, 0, 0))],
        out_specs=[pl.BlockSpec((None, nch, LANES, LANES), lambda b: (b, 0, 0, 0)),
                   pl.BlockSpec((None, nch, VROWS, LANES), lambda b: (b, 0, 0, 0))],
        out_shape=[jax.ShapeDtypeStruct((B, nch, LANES, LANES), BF16),
                   jax.ShapeDtypeStruct((B, nch, VROWS, LANES), BF16)],
        compiler_params=_cp(("parallel",)),
        name="compress",
    )(tview, wbig, pe2)


def _stack_heads(q_ref, nh, lane_slice=slice(None)):
    return jnp.concatenate([q_ref[HEAD_DIM * h:HEAD_DIM * (h + 1), lane_slice] for h in range(nh)], axis=1)


def _cmp_kernel(q_ref, kc_ref, vc_ref, ov_ref, bias_ref, o_ref, mm_ref, imp_scr, *, n_sel):
    i = pl.program_id(1)
    cd = i // 16
    nch = kc_ref.shape[0]
    q = _stack_heads(q_ref, NSA_HEADS)
    edge = bias_ref[...]

    ss = []
    for c in range(nch):
        b = jnp.where(c < cd, 0.0, jnp.where(c == cd, edge, NEG))
        s = jnp.dot(kc_ref[c, :, 0:HEAD_DIM], q, preferred_element_type=F32)
        ss.append(s + jnp.concatenate([b] * NSA_HEADS, axis=1))
    m = jnp.max(functools.reduce(jnp.maximum, ss), axis=0, keepdims=True)
    valid = m > 0.5 * NEG

    imp_scr[...] = jnp.zeros(imp_scr.shape, F32)
    acc = None
    for c in range(nch):
        p = jnp.exp(ss[c] - m).astype(BF16)
        t = jnp.dot(vc_ref[c], p, preferred_element_type=F32)
        acc = t if acc is None else acc + t
        imp_scr[32 * c:32 * c + 40, :] += jnp.dot(ov_ref[...], p, preferred_element_type=F32)

    inv = jnp.where(valid, 1.0 / acc[HEAD_DIM:HEAD_DIM + 1], 0.0)
    o = acc[0:HEAD_DIM] * inv
    for h in range(NSA_HEADS):
        o_ref[HEAD_DIM * h:HEAD_DIM * (h + 1), :] = o[:, h * LANES:(h + 1) * LANES]

    imp = jnp.zeros((n_sel, LANES), F32)
    for h in range(NSA_HEADS):
        sl = slice(h * LANES, (h + 1) * LANES)
        imp = imp + imp_scr[0:n_sel, sl] * inv[:, sl]

    blk = lax.broadcasted_iota(jnp.int32, (n_sel, LANES), 0).astype(F32)
    t = i * LANES + lax.broadcasted_iota(jnp.int32, (n_sel, LANES), 1)
    cur = (t >> 6).astype(F32)
    forced = (blk == 0.0) | (blk == cur) | (blk == cur - 1.0)
    imp = jnp.where(forced, FORCE, imp)
    imp = jnp.where(blk <= cur, imp, NEG)

    def pick(_, carry):
        imp, chosen = carry
        mx = jnp.max(imp, axis=0, keepdims=True)
        first = jnp.min(jnp.where(imp == mx, blk, float(n_sel)), axis=0, keepdims=True)
        hit = blk == first
        return jnp.where(hit, -jnp.inf, imp), jnp.where(hit, 1.0, chosen)

    _, chosen = lax.fori_loop(0, min(SEL_TOPK, n_sel), pick, (imp, jnp.zeros((n_sel, LANES), F32)))
    member = (chosen > 0.5) & (blk <= cur)
    mm_ref[...] = jnp.where(member, 0.0, -1.0).astype(BF16)


def _cmp_topk(qT, kc4, vc4, ov, cmp_bias):
    B, _, S = qT.shape
    nblk = S // LANES
    nch = kc4.shape[1]
    n_sel = S // SEL_BLOCK
    nq = NSA_HEADS * LANES
    return pl.pallas_call(
        functools.partial(_cmp_kernel, n_sel=n_sel),
        grid=(B, nblk),
        in_specs=[pl.BlockSpec((None, NSA_HEADS * HEAD_DIM, LANES), lambda b, i: (b, A_Q_BLK, i)),
                  pl.BlockSpec((None, nch, LANES, LANES), lambda b, i: (b, 0, 0, 0)),
                  pl.BlockSpec((None, nch, VROWS, LANES), lambda b, i: (b, 0, 0, 0)),
                  pl.BlockSpec((40, LANES), lambda b, i: (0, 0)),
                  pl.BlockSpec((None, LANES, LANES), lambda b, i: (i % 16, 0, 0))],
        out_specs=[pl.BlockSpec((None, NSA_HEADS * HEAD_DIM, LANES), lambda b, i: (b, 0, i)),
                   pl.BlockSpec((None, n_sel, LANES), lambda b, i: (b, 0, i))],
        out_shape=[jax.ShapeDtypeStruct((B, NSA_HEADS * HEAD_DIM, S), F32),
                   jax.ShapeDtypeStruct((B, n_sel, S), BF16)],
        scratch_shapes=[pltpu.VMEM((32 * nch + 64, nq), F32)],
        compiler_params=_cp(("parallel", "parallel")),
        name="nsa_cmp_topk",
    )(qT, kc4, vc4, ov, cmp_bias)


SEL_GROUP = 8


SEL_TQ = 2 * LANES


def _sel_kernel(q_ref, mm_ref, ks_ref, v_ref, causal_ref, ocmp_ref, owin_ref, gate_ref, out_ref,
                qa_scr, mm_scr, s_scr):
    i = pl.program_id(1)
    nq = NSA_HEADS * SEL_TQ
    qa_scr[0:HEAD_DIM, :] = _stack_heads(q_ref, NSA_HEADS)
    qa_scr[HEAD_DIM:LANES, :] = jnp.zeros((LANES - HEAD_DIM, nq), BF16)
    mm = mm_ref[...]
    for h in range(NSA_HEADS):
        mm_scr[:, h * SEL_TQ:(h + 1) * SEL_TQ] = mm

    gkeys = SEL_GROUP * LANES

    def group_scores(gi):
        qa_scr[HEAD_DIM:HEAD_DIM + 16, :] = mm_scr[pl.ds(pl.multiple_of(gi * 16, 16), 16), :]
        row = pl.multiple_of(gi * gkeys, gkeys)
        return jnp.dot(ks_ref[pl.ds(row, gkeys), :], qa_scr[...], preferred_element_type=F32)

    def softmax_pv(gi, m, acc, mg):
        m_new = jnp.maximum(m, mg)
        p = jnp.exp(s_scr[...] - m_new).astype(BF16)
        vcat = jnp.concatenate([v_ref[gi * SEL_GROUP + u] for u in range(SEL_GROUP)], axis=1)
        return m_new, acc * jnp.exp(m - m_new), jnp.dot(vcat, p, preferred_element_type=F32)

    def body(gi, carry):
        m, acc, mg = carry
        m, acc, pv = softmax_pv(gi, m, acc, mg)
        nxt = group_scores(gi + 1)
        s_scr[...] = nxt
        return m, acc + pv, jnp.max(nxt, axis=0, keepdims=True)

    s0 = group_scores(0)
    s_scr[...] = s0
    c0 = i * (SEL_TQ // LANES)
    last = c0 // SEL_GROUP
    m, acc, _ = lax.fori_loop(0, last, body, (jnp.full((1, nq), NEG, F32), jnp.zeros((VROWS, nq), F32),
                                              jnp.max(s0, axis=0, keepdims=True)))
    drow = pl.multiple_of((c0 % SEL_GROUP) * LANES, SEL_TQ)
    s_scr[pl.ds(drow, SEL_TQ), :] += causal_ref[...]
    m, acc, pv = softmax_pv(last, m, acc, jnp.max(s_scr[...], axis=0, keepdims=True))
    acc = acc + pv
    o = acc[0:HEAD_DIM] * (1.0 / acc[HEAD_DIM:HEAD_DIM + 1])

    g = gate_ref[...]
    for h in range(NSA_HEADS):
        rs = slice(HEAD_DIM * h, HEAD_DIM * (h + 1))
        out = (g[3 * h:3 * h + 1] * ocmp_ref[rs, :] + g[3 * h + 1:3 * h + 2] * o[:, h * SEL_TQ:(h + 1) * SEL_TQ]
               + g[3 * h + 2:3 * h + 3] * owin_ref[rs, :])
        out_ref[rs, :] = out.astype(BF16)


def _sel_attend(qT, mm, nk, v4, causal4, ocmp, owin, gates):
    B, _, S = qT.shape
    nstep = S // SEL_TQ
    n_sel = S // SEL_BLOCK
    nq = NSA_HEADS * SEL_TQ
    ar = NSA_HEADS * HEAD_DIM
    return pl.pallas_call(
        _sel_kernel,
        grid=(B, nstep),
        in_specs=[pl.BlockSpec((None, ar, SEL_TQ), lambda b, i: (b, A_Q_BLK, i)),
                  pl.BlockSpec((None, n_sel, SEL_TQ), lambda b, i: (b, 0, i)),
                  pl.BlockSpec((None, S, LANES), lambda b, i: (b, 0, 0)),
                  pl.BlockSpec((None, S // LANES, VROWS, LANES), lambda b, i: (b, 0, 0, 0)),
                  pl.BlockSpec((SEL_TQ, nq), lambda b, i: (0, 0)),
                  pl.BlockSpec((None, ar, SEL_TQ), lambda b, i: (b, 0, i)),
                  pl.BlockSpec((None, ar, SEL_TQ), lambda b, i: (b, 0, i)),
                  pl.BlockSpec((None, G_ROWS, SEL_TQ), lambda b, i: (b, 0, i))],
        out_specs=pl.BlockSpec((None, ar, SEL_TQ), lambda b, i: (b, 0, i)),
        out_shape=jax.ShapeDtypeStruct((B, ar, S), BF16),
        scratch_shapes=[pltpu.VMEM((LANES, nq), BF16), pltpu.VMEM((n_sel, nq), BF16),
                        pltpu.VMEM((SEL_GROUP * LANES, nq), F32)],
        compiler_params=_cp(("parallel", "parallel")),
        name="nsa_sel",
    )(qT, mm, nk, v4, causal4, ocmp, owin, gates)


def _banded_kernel(*refs, hkv, grp, nprev, mid_bias, dynamic, has_sink, want_lse):
    q_ref, k_ref, v_ref, bias_ref = refs[:4]
    pos = 4
    sink_ref = None
    if has_sink:
        sink_ref = refs[pos]
        pos += 1
    o_ref = refs[pos]
    lse_ref = refs[pos + 1] if want_lse else None
    j = pl.program_id(1)

    def one(s, g, first):
        lanes = slice(s * LANES, (s + 1) * LANES)
        qg = jnp.concatenate([q_ref[HEAD_DIM * (g * grp + u):HEAD_DIM * (g * grp + u + 1), lanes]
                              for u in range(grp)], axis=1)
        parts = []
        for ci in range(nprev + 1):
            which = 0 if ci == 0 else (2 if ci == nprev else 1)
            if first:
                kc = s - nprev + ci
                if kc < 0:
                    continue
                kcc = kc
                row = kc * LANES
            else:
                kc = j * NSUB + s - nprev + ci
                kcc = jnp.maximum(kc, 0) if dynamic else kc
                row = pl.multiple_of(kcc * LANES, LANES)
            sc = jnp.dot(k_ref[pl.ds(row, LANES), HEAD_DIM * g:HEAD_DIM * (g + 1)], qg,
                         preferred_element_type=F32)
            if dynamic:
                sc = sc + bias_ref[jnp.where(kc >= 0, which, 3)]
            elif which != 1 or mid_bias:
                sc = sc + bias_ref[which]
            parts.append((kcc, sc))
        m = None
        for _, sc in parts:
            mc = jnp.max(sc, axis=0, keepdims=True)
            m = mc if m is None else jnp.maximum(m, mc)
        if has_sink:
            sk = sink_ref[g, 0:1, :]
            m = jnp.maximum(m, sk)
        acc = None
        for kcc, sc in parts:
            t = jnp.dot(v_ref[kcc, VROWS * g:VROWS * (g + 1), :], jnp.exp(sc - m).astype(BF16),
                        preferred_element_type=F32)
            acc = t if acc is None else acc + t
        l = acc[HEAD_DIM:HEAD_DIM + 1]
        if has_sink:
            l = l + jnp.exp(sk - m)
        o = acc[0:HEAD_DIM] * (1.0 / l)
        for u in range(grp):
            hq = g * grp + u
            o_ref[HEAD_DIM * hq:HEAD_DIM * (hq + 1), lanes] = o[:, u * LANES:(u + 1) * LANES].astype(o_ref.dtype)
        if want_lse:
            lse_ref[8 * g:8 * (g + 1), lanes] = jnp.broadcast_to(m + jnp.log(l), (8, LANES))

    def run(first):
        for s in range(NSUB):
            for g in range(hkv):
                one(s, g, first)

    if dynamic:
        run(False)
    else:
        pl.when(j == 0)(lambda: run(True))
        pl.when(j > 0)(lambda: run(False))


def _banded(qT, q_blk, nk, k_blk, v4, v_blk, bias, *, hkv, grp, nprev, mid_bias, dynamic,
            out_dtype, sinks=None, want_lse=False, name):
    B, _, S = qT.shape
    nt = S // TILE
    qrows = hkv * grp * HEAD_DIM
    in_specs = [pl.BlockSpec((None, qrows, TILE), lambda b, j: (b, q_blk, j)),
                pl.BlockSpec((None, S, LANES), lambda b, j: (b, 0, k_blk)),
                pl.BlockSpec((None, S // LANES, hkv * VROWS, LANES), lambda b, j: (b, 0, v_blk, 0)),
                pl.BlockSpec(bias.shape, lambda b, j: (0, 0, 0))]
    args = [qT, nk, v4, bias]
    if sinks is not None:
        in_specs.append(pl.BlockSpec(sinks.shape, lambda b, j: (0, 0, 0)))
        args.append(sinks)
    out_specs = [pl.BlockSpec((None, qrows, TILE), lambda b, j: (b, 0, j))]
    out_shape = [jax.ShapeDtypeStruct((B, qrows, S), out_dtype)]
    if want_lse:
        out_specs.append(pl.BlockSpec((None, 8 * hkv, TILE), lambda b, j: (b, 0, j)))
        out_shape.append(jax.ShapeDtypeStruct((B, 8 * hkv, S), F32))
    return pl.pallas_call(
        functools.partial(_banded_kernel, hkv=hkv, grp=grp, nprev=nprev, mid_bias=mid_bias,
                          dynamic=dynamic, has_sink=sinks is not None, want_lse=want_lse),
        grid=(B, nt),
        in_specs=in_specs,
        out_specs=out_specs,
        out_shape=out_shape,
        compiler_params=_cp(("parallel", "parallel")),
        name=name,
    )(*args)


def _outproj_kernel(a_ref, b0_ref, b1_ref, b2_ref, l0_ref, l1_ref, l2_ref, c_ref, w_ref, x_ref, gw_ref,
                    o_ref, mix_scr):
    b_refs = (b0_ref, b1_ref, b2_ref)
    l_refs = (l0_ref, l1_ref, l2_ref)
    for h in range(2):
        ls = [r[8 * h:8 * h + 1, :] for r in l_refs]
        mx = jnp.maximum(jnp.maximum(ls[0], ls[1]), ls[2])
        es = [jnp.exp(v - mx) for v in ls]
        inv = 1.0 / (es[0] + es[1] + es[2])
        for g in range(3):
            rs = slice(HEAD_DIM * h, HEAD_DIM * (h + 1))
            mix_scr[128 * g + HEAD_DIM * h:128 * g + HEAD_DIM * (h + 1), :] = (
                b_refs[g][rs, :] * (es[g] * inv)).astype(BF16)
    y = jnp.dot(w_ref[:, 0:256], a_ref[...], preferred_element_type=F32)
    y = y + jnp.dot(w_ref[:, 256:640], mix_scr[...], preferred_element_type=F32)
    y = y + jnp.dot(w_ref[:, 640:1024], c_ref[...], preferred_element_type=F32)
    ms = jnp.mean(y * y, axis=0, keepdims=True)
    yn = y * lax.rsqrt(ms + RMS_EPS)
    gw = gw_ref[...]
    for c in range(NSUB):
        sl = slice(c * LANES, (c + 1) * LANES)
        o_ref[:, sl] = x_ref[:, sl] + gw * yn[:, sl]


def _outproj(aT, bs, lses, cT, w_outT, layer, xT, gw):
    B, D, S = xT.shape
    nt = S // TILE
    tile = lambda rows: pl.BlockSpec((None, rows, TILE), lambda b, j: (b, 0, j))
    return pl.pallas_call(
        _outproj_kernel,
        grid=(B, nt),
        in_specs=[tile(256), tile(128), tile(128), tile(128), tile(16), tile(16), tile(16), tile(384),
                  _const_spec((None, D, D), lambda b, j: (layer, 0, 0)),
                  tile(D),
                  pl.BlockSpec((None, D, LANES), lambda b, j: (b, 0, 0))],
        out_specs=tile(D),
        out_shape=jax.ShapeDtypeStruct((B, D, S), F32),
        scratch_shapes=[pltpu.VMEM((384, TILE), BF16)],
        compiler_params=_cp(("parallel", "parallel")),
        name="outproj",
    )(aT, bs[0], bs[1], bs[2], lses[0], lses[1], lses[2], cT, w_outT, xT, gw)


def _ffn_kernel(x_ref, a_ref, sh_ref, wg_ref, wu_ref, wd_ref, cv_ref, gw_ref, o_ref,
                h_scr, carry_scr, y_scr, *, fchunk):
    j = pl.program_id(1)
    d_ff = wg_ref.shape[0]

    @pl.when(j == 0)
    def _():
        carry_scr[...] = jnp.zeros(carry_scr.shape, F32)

    _norm_mod_to_scratch(x_ref, a_ref, sh_ref, h_scr)
    h = h_scr[...]
    lane = lax.broadcasted_iota(jnp.int32, (fchunk, LANES), 1)
    for c in range(d_ff // fchunk):
        rs = slice(c * fchunk, (c + 1) * fchunk)
        g = jnp.dot(wg_ref[rs, :], h, preferred_element_type=F32)
        prev = carry_scr[rs, :]
        carry_scr[rs, :] = g[:, TILE - LANES:TILE]
        g1 = pltpu.roll(g, 1, 1)
        g2 = pltpu.roll(g, 2, 1)
        fix1 = jnp.where(lane < 1, pltpu.roll(prev, 1, 1), g1[:, 0:LANES])
        fix2 = jnp.where(lane < 2, pltpu.roll(prev, 2, 1), g2[:, 0:LANES])
        g1 = jnp.concatenate([fix1, g1[:, LANES:]], axis=1)
        g2 = jnp.concatenate([fix2, g2[:, LANES:]], axis=1)
        w0 = jnp.concatenate([cv_ref[0, rs, :]] * NSUB, axis=1)
        w1 = jnp.concatenate([cv_ref[1, rs, :]] * NSUB, axis=1)
        w2 = jnp.concatenate([cv_ref[2, rs, :]] * NSUB, axis=1)
        cb = jnp.concatenate([cv_ref[3, rs, :]] * NSUB, axis=1)
        acc = g2 * w0 + g1 * w1 + g * w2 + cb
        up = jnp.dot(wu_ref[rs, :], h, preferred_element_type=F32)
        act = (jax.nn.gelu(acc, approximate=True) * up).astype(BF16)
        part = jnp.dot(wd_ref[:, rs], act, preferred_element_type=F32)
        if c == 0:
            y_scr[...] = part
        else:
            y_scr[...] += part
    y = y_scr[...]
    ms = jnp.mean(y * y, axis=0, keepdims=True)
    yn = y * lax.rsqrt(ms + RMS_EPS)
    gw = gw_ref[...]
    for c in range(NSUB):
        sl = slice(c * LANES, (c + 1) * LANES)
        o_ref[:, sl] = x_ref[:, sl] + gw * yn[:, sl]


def _ffn(xT, a2, sh2, wgT, wuT, wdT, cv, layer, gw):
    B, D, S = xT.shape
    nt = S // TILE
    d_ff = wgT.shape[1]
    fchunk = 256
    return pl.pallas_call(
        functools.partial(_ffn_kernel, fchunk=fchunk),
        grid=(B, nt),
        in_specs=[pl.BlockSpec((None, D, TILE), lambda b, j: (b, 0, j)),
                  pl.BlockSpec((None, D, LANES), lambda b, j: (b, 0, 0)),
                  pl.BlockSpec((None, D, LANES), lambda b, j: (b, 0, 0)),
                  _const_spec((None, d_ff, D), lambda b, j: (layer, 0, 0)),
                  _const_spec((None, d_ff, D), lambda b, j: (layer, 0, 0)),
                  _const_spec((None, D, d_ff), lambda b, j: (layer, 0, 0)),
                  _const_spec((None, 4, d_ff, LANES), lambda b, j: (layer, 0, 0, 0)),
                  pl.BlockSpec((None, D, LANES), lambda b, j: (b, 0, 0))],
        out_specs=pl.BlockSpec((None, D, TILE), lambda b, j: (b, 0, j)),
        out_shape=jax.ShapeDtypeStruct((B, D, S), F32),
        scratch_shapes=[pltpu.VMEM((D, TILE), BF16), pltpu.VMEM((d_ff, LANES), F32),
                        pltpu.VMEM((D, TILE), F32)],
        compiler_params=_cp(("arbitrary", "arbitrary")),
        name="convffn",
    )(xT, a2, sh2, wgT, wuT, wdT, cv, gw)


def _band_bias(dilation, old_edge, reps):
    kk = np.arange(LANES)[:, None]
    qq = np.arange(LANES)[None, :]
    res = ((qq - kk) % dilation) == 0
    tabs = [res & (kk - qq >= old_edge), res, res & (kk <= qq), np.zeros_like(res)]
    out = np.stack([np.where(t, 0.0, NEG) for t in tabs]).astype(np.float32)
    return jnp.asarray(np.tile(out, (1, 1, reps)))


def _cmp_bias():
    nn = np.arange(LANES)[:, None]
    qq = np.arange(LANES)[None, :]
    tabs = [np.where(CMP_STRIDE * nn + CMP_BLOCK - 1 <= LANES * r + qq, 0.0, NEG) for r in range(16)]
    return jnp.asarray(np.stack(tabs).astype(np.float32))


def _overlap_rows():
    jj = np.arange(40)[:, None]
    nn = np.arange(LANES)[None, :]
    return jnp.asarray(((nn >= 4 * jj - 1) & (nn <= 4 * jj + 3)).astype(np.float32), dtype=BF16)


_IN_COL_ORDER = ((1804, 2188), (652, 1036), (0, 256),
                 (256, 320), (320, 384), (384, 448), (512, 576), (1036, 1420), (2188, 2316),
                 (448, 512), (576, 640), (1420, 1804), (2316, 2444),
                 (640, 652))


def _prep_w_in(w_in):
    cols = np.concatenate([np.arange(a, b) for a, b in _IN_COL_ORDER])
    wt = jnp.swapaxes(w_in[:, :, cols], 1, 2)
    scale = np.ones((wt.shape[1], 1), np.float32)
    scale[:Q_ROWS] = HEAD_DIM ** -0.5
    wt = wt * scale
    wt = jnp.pad(wt, ((0, 0), (0, W_ROWS - wt.shape[1]), (0, 0)))
    return wt.astype(BF16)


def _prep_compress(w_ck, w_cv, pe_k, pe_v):
    L = w_ck.shape[0]
    half = CMP_BLOCK // 2

    def big(lo):
        wk = w_ck[:, lo:lo + half]
        wv = w_cv[:, lo:lo + half]
        z = jnp.zeros_like(wk)
        top = jnp.concatenate([wk, z], axis=-1)
        bot = jnp.concatenate([z, wv], axis=-1)
        return jnp.concatenate([top, bot], axis=2).reshape(L, half * LANES, LANES)

    wbig = jnp.concatenate([big(0), big(half)], axis=-1).astype(BF16)
    pe = jnp.concatenate([pe_k, pe_v], axis=-1)
    pe2 = jnp.zeros((L, 16, half * LANES), F32)
    pe2 = pe2.at[:, 0].set(pe[:, :half].reshape(L, -1)).at[:, 8].set(pe[:, half:].reshape(L, -1))
    return wbig, pe2.astype(BF16)


def _lane_bcast(v):
    return jnp.broadcast_to(v[..., None], v.shape + (LANES,))


def kernel(x, c, positions, w_in, w_out, w_ada, b_ada, norm_w, cmp_w_k, cmp_w_v, cmp_pe_k, cmp_pe_v,
           sinks, w_gate, w_up, conv_w, conv_b, w_down):
    B, S, D = x.shape
    depth = w_in.shape[0]
    assert S % 2048 == 0 and D == 1024 and w_in.shape[2] == 2444

    inv = ROPE_THETA ** (-jnp.arange(0, HEAD_DIM, 2, dtype=F32) / HEAD_DIM)
    ang = positions.astype(F32)[:, None, :] * inv[None, :, None]
    cosT, sinT = jnp.cos(ang), jnp.sin(ang)

    c8 = jnp.pad(c, ((0, 8 - B), (0, 0)))
    ada = _adaln(c8, w_ada, b_ada)[:, :B]
    sh1, sc1, g1, sh2, sc2, g2 = [ada[:, :, k * D:(k + 1) * D] for k in range(6)]
    nw = norm_w[:, :, None, :]
    a1 = _lane_bcast(nw[:, 0] * (1 + sc1))
    gw1 = _lane_bcast(g1 * nw[:, 1])
    a2 = _lane_bcast(nw[:, 2] * (1 + sc2))
    gw2 = _lane_bcast(g2 * nw[:, 3])
    sh1b, sh2b = _lane_bcast(sh1), _lane_bcast(sh2)

    w_inT = _prep_w_in(w_in)
    w_outT = jnp.swapaxes(w_out, 1, 2).astype(BF16)
    wgT = jnp.swapaxes(w_gate, 1, 2).astype(BF16)
    wuT = jnp.swapaxes(w_up, 1, 2).astype(BF16)
    wdT = jnp.swapaxes(w_down, 1, 2).astype(BF16)
    cv = _lane_bcast(jnp.concatenate([conv_w, conv_b[:, None, :]], axis=1))
    wbig, pe2 = _prep_compress(cmp_w_k, cmp_w_v, cmp_pe_k, cmp_pe_v)
    sink_tab = jnp.broadcast_to(
        jnp.repeat(sinks.reshape(depth, SWA_KV_HEADS, SWA_HEADS // SWA_KV_HEADS), LANES, axis=-1)[:, :, None, :],
        (depth, SWA_KV_HEADS, 8, LANES * (SWA_HEADS // SWA_KV_HEADS)))

    ov = _overlap_rows()
    cmp_bias = _cmp_bias()
    kk = np.arange(SEL_TQ)[:, None]
    causal4 = jnp.asarray(np.tile(np.where(kk <= kk.T, 0.0, NEG).astype(np.float32), (1, NSA_HEADS)))
    bias_win = _band_bias(1, 1, NSA_HEADS)
    bias_swa = _band_bias(1, 1, SWA_HEADS // SWA_KV_HEADS)
    bias_dil = [_band_bias(d, 0, 1) for _, d in DIL_PATTERNS]

    xT = jnp.swapaxes(x, 1, 2)
    for l in range(depth):
        qT, kcvc, nk, v4, gates = _inproj(xT, a1[l], sh1b[l], w_inT, l, cosT, sinT)
        kc4, vc4 = _compress(kcvc, wbig, pe2, l)
        ocmp, mm = _cmp_topk(qT, kc4, vc4, ov, cmp_bias)
        owin, = _banded(qT, A_Q_BLK, nk, 1, v4, 1, bias_win, hkv=1, grp=NSA_HEADS, nprev=NSA_WINDOW // LANES,
                        mid_bias=False, dynamic=False, out_dtype=F32, name="nsa_win")
        aT = _sel_attend(qT, mm, nk, v4, causal4, ocmp, owin, gates)
        bs, lses = [], []
        for gi, (win, dil) in enumerate(DIL_PATTERNS):
            o, lse = _banded(qT, 3 + gi, nk, 2 + gi, v4, 1 + gi, bias_dil[gi], hkv=2, grp=1, nprev=dil,
                             mid_bias=dil > 1, dynamic=dil > NSUB, out_dtype=F32, want_lse=True,
                             name="dil%d" % dil)
            bs.append(o)
            lses.append(lse)
        cT, = _banded(qT, 0, nk, 5, v4, 4, bias_swa, hkv=SWA_KV_HEADS, grp=SWA_HEADS // SWA_KV_HEADS,
                      nprev=SWA_WINDOW // LANES, mid_bias=False, dynamic=False, out_dtype=BF16,
                      sinks=sink_tab[l], name="swa")
        xT = _outproj(aT, bs, lses, cT, w_outT, l, xT, gw1[l])
        xT = _ffn(xT, a2[l], sh2b[l], wgT, wuT, wdT, cv, l, gw2[l])
    return jnp.swapaxes(xT, 1, 2)
```

```python
import functools

import numpy as np
import jax
import jax.numpy as jnp
from jax import lax
from jax.experimental import pallas as pl
from jax.experimental.pallas import tpu as pltpu

F32 = jnp.float32
BF16 = jnp.bfloat16

HEAD_DIM = 64
HALF = HEAD_DIM // 2
NSA_HEADS = 4
CMP_BLOCK = 32
CMP_STRIDE = 16
SEL_BLOCK = 64
SEL_TOPK = 16
NSA_WINDOW = 512
DIL_PATTERNS = ((128, 1), (512, 4), (2048, 16))
DIL_HEADS = 6
SWA_HEADS = 6
SWA_KV_HEADS = 2
SWA_WINDOW = 128
ROPE_THETA = 10000.0
RMS_EPS = 1e-6
NEG = -1e30
FORCE = 1e4
CONV_WIDTH = 3

LANES = 128
TILE = 512
NSUB = TILE // LANES
VROWS = HEAD_DIM + 16
MEMBER_BIG = 2.0 ** 100
VMEM_LIMIT = 56 * 1024 * 1024

Q_ROWS = 1024
A_Q_BLK = 3
NK_ROWS = 768
NK_LANES = 768
V_ROWS = 640
G_ROWS = 16
W_ROWS = Q_ROWS + NK_ROWS + V_ROWS + G_ROWS
N_VPIECES = V_ROWS // HEAD_DIM


def _cp(sem):
    return pltpu.CompilerParams(dimension_semantics=sem, vmem_limit_bytes=VMEM_LIMIT)


def _const_spec(shape, index_map):
    return pl.BlockSpec(shape, index_map, pipeline_mode=pl.Buffered(1))


def _adaln_kernel(c_ref, w_ref, b_ref, o_ref):
    c = c_ref[...]
    cond = c * jax.nn.sigmoid(c)
    o_ref[...] = jnp.dot(cond, w_ref[...], preferred_element_type=F32,
                         precision=lax.Precision.HIGHEST) + b_ref[...]


def _adaln(c8, w_ada, b_ada):
    depth, d, six_d = w_ada.shape
    nblk = six_d // d
    return pl.pallas_call(
        _adaln_kernel,
        grid=(depth, nblk),
        in_specs=[pl.BlockSpec((8, d), lambda l, n: (0, 0)),
                  pl.BlockSpec((None, d, d), lambda l, n: (l, 0, n)),
                  pl.BlockSpec((None, 1, d), lambda l, n: (l, 0, n))],
        out_specs=pl.BlockSpec((None, 8, d), lambda l, n: (l, 0, n)),
        out_shape=jax.ShapeDtypeStruct((depth, 8, six_d), F32),
        compiler_params=_cp(("parallel", "parallel")),
        name="adaln",
    )(c8, w_ada, b_ada.reshape(depth, 1, six_d))


def _norm_mod_to_scratch(x_ref, a_ref, sh_ref, h_scr):
    for c in range(NSUB):
        sl = slice(c * LANES, (c + 1) * LANES)
        xs = x_ref[:, sl]
        ms = jnp.mean(xs * xs, axis=0, keepdims=True)
        h_scr[:, sl] = ((xs * lax.rsqrt(ms + RMS_EPS)) * a_ref[...] + sh_ref[...]).astype(BF16)


def _inproj_kernel(x_ref, a_ref, sh_ref, w_ref, cos_ref, sin_ref,
                   q_ref, kcvc_ref, nk_ref, v4_ref, gate_ref, h_scr):
    j = pl.program_id(1)
    _norm_mod_to_scratch(x_ref, a_ref, sh_ref, h_scr)
    h = h_scr[...]
    cos = cos_ref[...]
    sin = sin_ref[...]

    def proj(r0, r1):
        return jnp.dot(w_ref[r0:r1, :], h, preferred_element_type=F32)

    def rope(r, nh):
        outs = []
        for hh in range(nh):
            t1 = r[HEAD_DIM * hh:HEAD_DIM * hh + HALF]
            t2 = r[HEAD_DIM * hh + HALF:HEAD_DIM * (hh + 1)]
            outs.append(t1 * cos - t2 * sin)
            outs.append(t2 * cos + t1 * sin)
        return jnp.concatenate(outs, axis=0)

    for r0, r1 in ((0, 384), (384, 768), (768, 1024)):
        q_ref[r0:r1, :] = rope(proj(r0, r1), (r1 - r0) // HEAD_DIM).astype(BF16)

    base = Q_ROWS
    r = proj(base, base + 128)
    kcvc = jnp.concatenate([rope(r[0:64], 1), r[64:128]], axis=0)
    kcvc_ref[...] = kcvc.T.astype(BF16)

    r = proj(base + 128, base + 192)
    tok = j * TILE + lax.broadcasted_iota(jnp.int32, (HEAD_DIM, TILE), 1)
    row = lax.broadcasted_iota(jnp.int32, (HEAD_DIM, TILE), 0)
    member_cols = jnp.where(row == ((tok >> 6) & 15), MEMBER_BIG, 0.0).astype(F32)
    nk_ref[:, 0:128] = jnp.concatenate([rope(r, 1), member_cols], axis=0).T.astype(BF16)

    r = proj(base + 192, base + 256)
    nk_ref[:, 128:256] = jnp.concatenate([rope(r, 1), jnp.zeros((HEAD_DIM, TILE), F32)], axis=0).T.astype(BF16)

    r = proj(base + 256, base + 640)
    nk_ref[:, 256:640] = rope(r, 6).T.astype(BF16)

    r = proj(base + 640, base + 768)
    nk_ref[:, 640:768] = rope(r, 2).T.astype(BF16)

    base = Q_ROWS + NK_ROWS
    r = proj(base, base + V_ROWS).astype(BF16)
    ones = jnp.ones((VROWS - HEAD_DIM, LANES), BF16)
    for c in range(NSUB):
        for p in range(N_VPIECES):
            v4_ref[c, VROWS * p:VROWS * p + HEAD_DIM, :] = r[HEAD_DIM * p:HEAD_DIM * (p + 1),
                                                             c * LANES:(c + 1) * LANES]
            v4_ref[c, VROWS * p + HEAD_DIM:VROWS * (p + 1), :] = ones

    base = Q_ROWS + NK_ROWS + V_ROWS
    gate_ref[...] = jax.nn.sigmoid(proj(base, base + G_ROWS))


def _inproj(xT, a1, sh1, w_inT, layer, cosT, sinT):
    B, D, S = xT.shape
    nt = S // TILE
    return pl.pallas_call(
        _inproj_kernel,
        grid=(B, nt),
        in_specs=[pl.BlockSpec((None, D, TILE), lambda b, j: (b, 0, j)),
                  pl.BlockSpec((None, D, LANES), lambda b, j: (b, 0, 0)),
                  pl.BlockSpec((None, D, LANES), lambda b, j: (b, 0, 0)),
                  _const_spec((None, W_ROWS, D), lambda b, j: (layer, 0, 0)),
                  pl.BlockSpec((None, HALF, TILE), lambda b, j: (b, 0, j)),
                  pl.BlockSpec((None, HALF, TILE), lambda b, j: (b, 0, j))],
        out_specs=[pl.BlockSpec((None, Q_ROWS, TILE), lambda b, j: (b, 0, j)),
                   pl.BlockSpec((None, TILE, LANES), lambda b, j: (b, j, 0)),
                   pl.BlockSpec((None, TILE, NK_LANES), lambda b, j: (b, j, 0)),
                   pl.BlockSpec((None, NSUB, N_VPIECES * VROWS, LANES), lambda b, j: (b, j, 0, 0)),
                   pl.BlockSpec((None, G_ROWS, TILE), lambda b, j: (b, 0, j))],
        out_shape=[jax.ShapeDtypeStruct((B, Q_ROWS, S), BF16),
                   jax.ShapeDtypeStruct((B, S, LANES), BF16),
                   jax.ShapeDtypeStruct((B, S, NK_LANES), BF16),
                   jax.ShapeDtypeStruct((B, S // LANES, N_VPIECES * VROWS, LANES), BF16),
                   jax.ShapeDtypeStruct((B, G_ROWS, S), F32)],
        scratch_shapes=[pltpu.VMEM((D, TILE), BF16)],
        compiler_params=_cp(("parallel", "parallel")),
        name="inproj",
    )(xT, a1, sh1, w_inT, cosT, sinT)


def _compress_kernel(t_ref, w_ref, pe_ref, kc_ref, vc_ref):
    n = t_ref.shape[0]
    a = jnp.dot(t_ref[...], w_ref[...], preferred_element_type=F32)
    pc = jnp.dot(pe_ref[...], w_ref[...], preferred_element_type=F32)
    const = pc[0:1, 0:LANES] + pc[8:9, LANES:2 * LANES]
    cmp = a[:, 0:LANES] + pltpu.roll(a[:, LANES:2 * LANES], n - 1, 0) + const
    cmp_t = cmp.T
    ones = jnp.ones((VROWS - HEAD_DIM, LANES), BF16)
    for c in range(n // LANES):
        kc_ref[c] = cmp[c * LANES:(c + 1) * LANES].astype(BF16)
        vc_ref[c, 0:HEAD_DIM, :] = cmp_t[HEAD_DIM:2 * HEAD_DIM, c * LANES:(c + 1) * LANES].astype(BF16)
        vc_ref[c, HEAD_DIM:VROWS, :] = ones


def _compress(kcvc, wbig, pe2, layer):
    B, S, _ = kcvc.shape
    n = S // CMP_STRIDE
    nch = n // LANES
    tview = kcvc.reshape(B, n, CMP_STRIDE * LANES)
    return pl.pallas_call(
        _compress_kernel,
        grid=(B,),
        in_specs=[pl.BlockSpec((None, n, CMP_STRIDE * LANES), lambda b: (b, 0, 0)),
                  pl.BlockSpec((None, CMP_STRIDE * LANES, 2 * LANES), lambda b: (layer, 0, 0)),
                  pl.BlockSpec((None, 16, CMP_STRIDE * LANES), lambda b: (layer, 0, 0))],
        out_specs=[pl.BlockSpec((None, nch, LANES, LANES), lambda b: (b, 0, 0, 0)),
                   pl.BlockSpec((None, nch, VROWS, LANES), lambda b: (b, 0, 0, 0))],
        out_shape=[jax.ShapeDtypeStruct((B, nch, LANES, LANES), BF16),
                   jax.ShapeDtypeStruct((B, nch, VROWS, LANES), BF16)],
        compiler_params=_cp(("parallel",)),
        name="compress",
    )(tview, wbig, pe2)


def _stack_heads(q_ref, nh, lane_slice=slice(None)):
    return jnp.concatenate([q_ref[HEAD_DIM * h:HEAD_DIM * (h + 1), lane_slice] for h in range(nh)], axis=1)


def _cmp_kernel(q_ref, kc_ref, vc_ref, ov_ref, bias_ref, o_ref, mm_ref, imp_scr, *, n_sel):
    i = pl.program_id(1)
    cd = i // 16
    nch = kc_ref.shape[0]
    q = _stack_heads(q_ref, NSA_HEADS)
    edge = bias_ref[...]

    ss = []
    for c in range(nch):
        b = jnp.where(c < cd, 0.0, jnp.where(c == cd, edge, NEG))
        s = jnp.dot(kc_ref[c, :, 0:HEAD_DIM], q, preferred_element_type=F32)
        ss.append(s + jnp.concatenate([b] * NSA_HEADS, axis=1))
    m = jnp.max(functools.reduce(jnp.maximum, ss), axis=0, keepdims=True)
    valid = m > 0.5 * NEG

    imp_scr[...] = jnp.zeros(imp_scr.shape, F32)
    acc = None
    for c in range(nch):
        p = jnp.exp(ss[c] - m).astype(BF16)
        t = jnp.dot(vc_ref[c], p, preferred_element_type=F32)
        acc = t if acc is None else acc + t
        imp_scr[32 * c:32 * c + 40, :] += jnp.dot(ov_ref[...], p, preferred_element_type=F32)

    inv = jnp.where(valid, 1.0 / acc[HEAD_DIM:HEAD_DIM + 1], 0.0)
    o = acc[0:HEAD_DIM] * inv
    for h in range(NSA_HEADS):
        o_ref[HEAD_DIM * h:HEAD_DIM * (h + 1), :] = o[:, h * LANES:(h + 1) * LANES]

    imp = jnp.zeros((n_sel, LANES), F32)
    for h in range(NSA_HEADS):
        sl = slice(h * LANES, (h + 1) * LANES)
        imp = imp + imp_scr[0:n_sel, sl] * inv[:, sl]

    blk = lax.broadcasted_iota(jnp.int32, (n_sel, LANES), 0).astype(F32)
    t = i * LANES + lax.broadcasted_iota(jnp.int32, (n_sel, LANES), 1)
    cur = (t >> 6).astype(F32)
    forced = (blk == 0.0) | (blk == cur) | (blk == cur - 1.0)
    imp = jnp.where(forced, FORCE, imp)
    imp = jnp.where(blk <= cur, imp, NEG)

    def pick(_, carry):
        imp, chosen = carry
        mx = jnp.max(imp, axis=0, keepdims=True)
        first = jnp.min(jnp.where(imp == mx, blk, float(n_sel)), axis=0, keepdims=True)
        hit = blk == first
        return jnp.where(hit, -jnp.inf, imp), jnp.where(hit, 1.0, chosen)

    _, chosen = lax.fori_loop(0, min(SEL_TOPK, n_sel), pick, (imp, jnp.zeros((n_sel, LANES), F32)))
    member = (chosen > 0.5) & (blk <= cur)
    mm_ref[...] = jnp.where(member, 0.0, -1.0).astype(BF16)


def _cmp_topk(qT, kc4, vc4, ov, cmp_bias):
    B, _, S = qT.shape
    nblk = S // LANES
    nch = kc4.shape[1]
    n_sel = S // SEL_BLOCK
    nq = NSA_HEADS * LANES
    return pl.pallas_call(
        functools.partial(_cmp_kernel, n_sel=n_sel),
        grid=(B, nblk),
        in_specs=[pl.BlockSpec((None, NSA_HEADS * HEAD_DIM, LANES), lambda b, i: (b, A_Q_BLK, i)),
                  pl.BlockSpec((None, nch, LANES, LANES), lambda b, i: (b, 0, 0, 0)),
                  pl.BlockSpec((None, nch, VROWS, LANES), lambda b, i: (b, 0, 0, 0)),
                  pl.BlockSpec((40, LANES), lambda b, i: (0, 0)),
                  pl.BlockSpec((None, LANES, LANES), lambda b, i: (i % 16, 0, 0))],
        out_specs=[pl.BlockSpec((None, NSA_HEADS * HEAD_DIM, LANES), lambda b, i: (b, 0, i)),
                   pl.BlockSpec((None, n_sel, LANES), lambda b, i: (b, 0, i))],
        out_shape=[jax.ShapeDtypeStruct((B, NSA_HEADS * HEAD_DIM, S), F32),
                   jax.ShapeDtypeStruct((B, n_sel, S), BF16)],
        scratch_shapes=[pltpu.VMEM((32 * nch + 64, nq), F32)],
        compiler_params=_cp(("parallel", "parallel")),
        name="nsa_cmp_topk",
    )(qT, kc4, vc4, ov, cmp_bias)


SEL_GROUP = 8


SEL_TQ = 2 * LANES


def _sel_kernel(q_ref, mm_ref, ks_ref, v_ref, causal_ref, ocmp_ref, owin_ref, gate_ref, out_ref,
                qa_scr, mm_scr, s_scr):
    i = pl.program_id(1)
    nq = NSA_HEADS * SEL_TQ
    qa_scr[0:HEAD_DIM, :] = _stack_heads(q_ref, NSA_HEADS)
    qa_scr[HEAD_DIM:LANES, :] = jnp.zeros((LANES - HEAD_DIM, nq), BF16)
    mm = mm_ref[...]
    for h in range(NSA_HEADS):
        mm_scr[:, h * SEL_TQ:(h + 1) * SEL_TQ] = mm

    gkeys = SEL_GROUP * LANES

    def load_query(gi):
        qa_scr[HEAD_DIM:HEAD_DIM + 16, :] = mm_scr[pl.ds(pl.multiple_of(gi * 16, 16), 16), :]
        return qa_scr[...]

    def chunk_scores(gi, u, qa):
        row = pl.multiple_of((gi * SEL_GROUP + u) * LANES, LANES)
        return jnp.dot(ks_ref[pl.ds(row, LANES), :], qa, preferred_element_type=F32)

    def weighted_values(gi, ps):
        vcat = jnp.concatenate([v_ref[gi * SEL_GROUP + u] for u in range(SEL_GROUP)], axis=1)
        return jnp.dot(vcat, jnp.concatenate(ps, axis=0), preferred_element_type=F32)

    def body(gi, carry):
        m, acc, mg = carry
        m_new = jnp.maximum(m, mg)
        qa = load_query(gi + 1)
        ps, mx = [], None
        for u in range(SEL_GROUP):
            rows = slice(u * LANES, (u + 1) * LANES)
            ps.append(jnp.exp(s_scr[rows, :] - m_new).astype(BF16))
            nxt = chunk_scores(gi + 1, u, qa)
            s_scr[rows, :] = nxt
            mx = nxt if mx is None else jnp.maximum(mx, nxt)
        acc = acc * jnp.exp(m - m_new) + weighted_values(gi, ps)
        return m_new, acc, jnp.max(mx, axis=0, keepdims=True)

    qa = load_query(0)
    mx = None
    for u in range(SEL_GROUP):
        s0 = chunk_scores(0, u, qa)
        s_scr[u * LANES:(u + 1) * LANES, :] = s0
        mx = s0 if mx is None else jnp.maximum(mx, s0)
    c0 = i * (SEL_TQ // LANES)
    last = c0 // SEL_GROUP
    m, acc, _ = lax.fori_loop(0, last, body, (jnp.full((1, nq), NEG, F32), jnp.zeros((VROWS, nq), F32),
                                              jnp.max(mx, axis=0, keepdims=True)))
    drow = pl.multiple_of((c0 % SEL_GROUP) * LANES, SEL_TQ)
    s_scr[pl.ds(drow, SEL_TQ), :] += causal_ref[...]
    s = s_scr[...]
    m_new = jnp.maximum(m, jnp.max(s, axis=0, keepdims=True))
    acc = acc * jnp.exp(m - m_new) + weighted_values(last, [jnp.exp(s - m_new).astype(BF16)])
    o = acc[0:HEAD_DIM] * (1.0 / acc[HEAD_DIM:HEAD_DIM + 1])

    g = gate_ref[...]
    for h in range(NSA_HEADS):
        rs = slice(HEAD_DIM * h, HEAD_DIM * (h + 1))
        out = (g[3 * h:3 * h + 1] * ocmp_ref[rs, :] + g[3 * h + 1:3 * h + 2] * o[:, h * SEL_TQ:(h + 1) * SEL_TQ]
               + g[3 * h + 2:3 * h + 3] * owin_ref[rs, :])
        out_ref[rs, :] = out.astype(BF16)


def _sel_attend(qT, mm, nk, v4, causal4, ocmp, owin, gates):
    B, _, S = qT.shape
    nstep = S // SEL_TQ
    n_sel = S // SEL_BLOCK
    nq = NSA_HEADS * SEL_TQ
    ar = NSA_HEADS * HEAD_DIM
    return pl.pallas_call(
        _sel_kernel,
        grid=(B, nstep),
        in_specs=[pl.BlockSpec((None, ar, SEL_TQ), lambda b, i: (b, A_Q_BLK, i)),
                  pl.BlockSpec((None, n_sel, SEL_TQ), lambda b, i: (b, 0, i)),
                  pl.BlockSpec((None, S, LANES), lambda b, i: (b, 0, 0)),
                  pl.BlockSpec((None, S // LANES, VROWS, LANES), lambda b, i: (b, 0, 0, 0)),
                  pl.BlockSpec((SEL_TQ, nq), lambda b, i: (0, 0)),
                  pl.BlockSpec((None, ar, SEL_TQ), lambda b, i: (b, 0, i)),
                  pl.BlockSpec((None, ar, SEL_TQ), lambda b, i: (b, 0, i)),
                  pl.BlockSpec((None, G_ROWS, SEL_TQ), lambda b, i: (b, 0, i))],
        out_specs=pl.BlockSpec((None, ar, SEL_TQ), lambda b, i: (b, 0, i)),
        out_shape=jax.ShapeDtypeStruct((B, ar, S), BF16),
        scratch_shapes=[pltpu.VMEM((LANES, nq), BF16), pltpu.VMEM((n_sel, nq), BF16),
                        pltpu.VMEM((SEL_GROUP * LANES, nq), F32)],
        compiler_params=_cp(("parallel", "parallel")),
        name="nsa_sel",
    )(qT, mm, nk, v4, causal4, ocmp, owin, gates)


def _banded_kernel(*refs, hkv, grp, nprev, mid_bias, dynamic, has_sink, want_lse):
    q_ref, k_ref, v_ref, bias_ref = refs[:4]
    pos = 4
    sink_ref = None
    if has_sink:
        sink_ref = refs[pos]
        pos += 1
    o_ref = refs[pos]
    lse_ref = refs[pos + 1] if want_lse else None
    j = pl.program_id(1)

    def one(s, g, first):
        lanes = slice(s * LANES, (s + 1) * LANES)
        qg = jnp.concatenate([q_ref[HEAD_DIM * (g * grp + u):HEAD_DIM * (g * grp + u + 1), lanes]
                              for u in range(grp)], axis=1)
        parts = []
        for ci in range(nprev + 1):
            which = 0 if ci == 0 else (2 if ci == nprev else 1)
            if first:
                kc = s - nprev + ci
                if kc < 0:
                    continue
                kcc = kc
                row = kc * LANES
            else:
                kc = j * NSUB + s - nprev + ci
                kcc = jnp.maximum(kc, 0) if dynamic else kc
                row = pl.multiple_of(kcc * LANES, LANES)
            sc = jnp.dot(k_ref[pl.ds(row, LANES), HEAD_DIM * g:HEAD_DIM * (g + 1)], qg,
                         preferred_element_type=F32)
            if dynamic:
                sc = sc + bias_ref[jnp.where(kc >= 0, which, 3)]
            elif which != 1 or mid_bias:
                sc = sc + bias_ref[which]
            parts.append((kcc, sc))
        m = None
        for _, sc in parts:
            mc = jnp.max(sc, axis=0, keepdims=True)
            m = mc if m is None else jnp.maximum(m, mc)
        if has_sink:
            sk = sink_ref[g, 0:1, :]
            m = jnp.maximum(m, sk)
        acc = None
        for kcc, sc in parts:
            t = jnp.dot(v_ref[kcc, VROWS * g:VROWS * (g + 1), :], jnp.exp(sc - m).astype(BF16),
                        preferred_element_type=F32)
            acc = t if acc is None else acc + t
        l = acc[HEAD_DIM:HEAD_DIM + 1]
        if has_sink:
            l = l + jnp.exp(sk - m)
        o = acc[0:HEAD_DIM] * (1.0 / l)
        for u in range(grp):
            hq = g * grp + u
            o_ref[HEAD_DIM * hq:HEAD_DIM * (hq + 1), lanes] = o[:, u * LANES:(u + 1) * LANES].astype(o_ref.dtype)
        if want_lse:
            lse_ref[8 * g:8 * (g + 1), lanes] = jnp.broadcast_to(m + jnp.log(l), (8, LANES))

    def run(first):
        for s in range(NSUB):
            for g in range(hkv):
                one(s, g, first)

    if dynamic:
        run(False)
    else:
        pl.when(j == 0)(lambda: run(True))
        pl.when(j > 0)(lambda: run(False))


def _banded(qT, q_blk, nk, k_blk, v4, v_blk, bias, *, hkv, grp, nprev, mid_bias, dynamic,
            out_dtype, sinks=None, want_lse=False, name):
    B, _, S = qT.shape
    nt = S // TILE
    qrows = hkv * grp * HEAD_DIM
    in_specs = [pl.BlockSpec((None, qrows, TILE), lambda b, j: (b, q_blk, j)),
                pl.BlockSpec((None, S, LANES), lambda b, j: (b, 0, k_blk)),
                pl.BlockSpec((None, S // LANES, hkv * VROWS, LANES), lambda b, j: (b, 0, v_blk, 0)),
                pl.BlockSpec(bias.shape, lambda b, j: (0, 0, 0))]
    args = [qT, nk, v4, bias]
    if sinks is not None:
        in_specs.append(pl.BlockSpec(sinks.shape, lambda b, j: (0, 0, 0)))
        args.append(sinks)
    out_specs = [pl.BlockSpec((None, qrows, TILE), lambda b, j: (b, 0, j))]
    out_shape = [jax.ShapeDtypeStruct((B, qrows, S), out_dtype)]
    if want_lse:
        out_specs.append(pl.BlockSpec((None, 8 * hkv, TILE), lambda b, j: (b, 0, j)))
        out_shape.append(jax.ShapeDtypeStruct((B, 8 * hkv, S), F32))
    return pl.pallas_call(
        functools.partial(_banded_kernel, hkv=hkv, grp=grp, nprev=nprev, mid_bias=mid_bias,
                          dynamic=dynamic, has_sink=sinks is not None, want_lse=want_lse),
        grid=(B, nt),
        in_specs=in_specs,
        out_specs=out_specs,
        out_shape=out_shape,
        compiler_params=_cp(("parallel", "parallel")),
        name=name,
    )(*args)


def _outproj_kernel(a_ref, b0_ref, b1_ref, b2_ref, l0_ref, l1_ref, l2_ref, c_ref, w_ref, x_ref, gw_ref,
                    o_ref, mix_scr):
    b_refs = (b0_ref, b1_ref, b2_ref)
    l_refs = (l0_ref, l1_ref, l2_ref)
    for h in range(2):
        ls = [r[8 * h:8 * h + 1, :] for r in l_refs]
        mx = jnp.maximum(jnp.maximum(ls[0], ls[1]), ls[2])
        es = [jnp.exp(v - mx) for v in ls]
        inv = 1.0 / (es[0] + es[1] + es[2])
        for g in range(3):
            rs = slice(HEAD_DIM * h, HEAD_DIM * (h + 1))
            mix_scr[128 * g + HEAD_DIM * h:128 * g + HEAD_DIM * (h + 1), :] = (
                b_refs[g][rs, :] * (es[g] * inv)).astype(BF16)
    y = jnp.dot(w_ref[:, 0:256], a_ref[...], preferred_element_type=F32)
    y = y + jnp.dot(w_ref[:, 256:640], mix_scr[...], preferred_element_type=F32)
    y = y + jnp.dot(w_ref[:, 640:1024], c_ref[...], preferred_element_type=F32)
    ms = jnp.mean(y * y, axis=0, keepdims=True)
    yn = y * lax.rsqrt(ms + RMS_EPS)
    gw = gw_ref[...]
    for c in range(NSUB):
        sl = slice(c * LANES, (c + 1) * LANES)
        o_ref[:, sl] = x_ref[:, sl] + gw * yn[:, sl]


def _outproj(aT, bs, lses, cT, w_outT, layer, xT, gw):
    B, D, S = xT.shape
    nt = S // TILE
    tile = lambda rows: pl.BlockSpec((None, rows, TILE), lambda b, j: (b, 0, j))
    return pl.pallas_call(
        _outproj_kernel,
        grid=(B, nt),
        in_specs=[tile(256), tile(128), tile(128), tile(128), tile(16), tile(16), tile(16), tile(384),
                  _const_spec((None, D, D), lambda b, j: (layer, 0, 0)),
                  tile(D),
                  pl.BlockSpec((None, D, LANES), lambda b, j: (b, 0, 0))],
        out_specs=tile(D),
        out_shape=jax.ShapeDtypeStruct((B, D, S), F32),
        scratch_shapes=[pltpu.VMEM((384, TILE), BF16)],
        compiler_params=_cp(("parallel", "parallel")),
        name="outproj",
    )(aT, bs[0], bs[1], bs[2], lses[0], lses[1], lses[2], cT, w_outT, xT, gw)


def _ffn_kernel(x_ref, a_ref, sh_ref, wg_ref, wu_ref, wd_ref, cv_ref, gw_ref, o_ref,
                h_scr, carry_scr, act_scr, *, fchunk):
    j = pl.program_id(1)
    d_ff = wg_ref.shape[0]

    @pl.when(j == 0)
    def _():
        carry_scr[...] = jnp.zeros(carry_scr.shape, F32)

    _norm_mod_to_scratch(x_ref, a_ref, sh_ref, h_scr)
    h = h_scr[...]
    lane = lax.broadcasted_iota(jnp.int32, (fchunk, LANES), 1)
    for c in range(d_ff // fchunk):
        rs = slice(c * fchunk, (c + 1) * fchunk)
        g = jnp.dot(wg_ref[rs, :], h, preferred_element_type=F32)
        prev = carry_scr[rs, :]
        carry_scr[rs, :] = g[:, TILE - LANES:TILE]
        g1 = pltpu.roll(g, 1, 1)
        g2 = pltpu.roll(g, 2, 1)
        fix1 = jnp.where(lane < 1, pltpu.roll(prev, 1, 1), g1[:, 0:LANES])
        fix2 = jnp.where(lane < 2, pltpu.roll(prev, 2, 1), g2[:, 0:LANES])
        g1 = jnp.concatenate([fix1, g1[:, LANES:]], axis=1)
        g2 = jnp.concatenate([fix2, g2[:, LANES:]], axis=1)
        w0 = jnp.concatenate([cv_ref[0, rs, :]] * NSUB, axis=1)
        w1 = jnp.concatenate([cv_ref[1, rs, :]] * NSUB, axis=1)
        w2 = jnp.concatenate([cv_ref[2, rs, :]] * NSUB, axis=1)
        cb = jnp.concatenate([cv_ref[3, rs, :]] * NSUB, axis=1)
        acc = g2 * w0 + g1 * w1 + g * w2 + cb
        up = jnp.dot(wu_ref[rs, :], h, preferred_element_type=F32)
        act_scr[rs, :] = (jax.nn.gelu(acc, approximate=True) * up).astype(BF16)
    y = jnp.dot(wd_ref[...], act_scr[...], preferred_element_type=F32)
    ms = jnp.mean(y * y, axis=0, keepdims=True)
    yn = y * lax.rsqrt(ms + RMS_EPS)
    gw = gw_ref[...]
    for c in range(NSUB):
        sl = slice(c * LANES, (c + 1) * LANES)
        o_ref[:, sl] = x_ref[:, sl] + gw * yn[:, sl]


def _ffn(xT, a2, sh2, wgT, wuT, wdT, cv, layer, gw):
    B, D, S = xT.shape
    nt = S // TILE
    d_ff = wgT.shape[1]
    fchunk = 256
    return pl.pallas_call(
        functools.partial(_ffn_kernel, fchunk=fchunk),
        grid=(B, nt),
        in_specs=[pl.BlockSpec((None, D, TILE), lambda b, j: (b, 0, j)),
                  pl.BlockSpec((None, D, LANES), lambda b, j: (b, 0, 0)),
                  pl.BlockSpec((None, D, LANES), lambda b, j: (b, 0, 0)),
                  _const_spec((None, d_ff, D), lambda b, j: (layer, 0, 0)),
                  _const_spec((None, d_ff, D), lambda b, j: (layer, 0, 0)),
                  _const_spec((None, D, d_ff), lambda b, j: (layer, 0, 0)),
                  _const_spec((None, 4, d_ff, LANES), lambda b, j: (layer, 0, 0, 0)),
                  pl.BlockSpec((None, D, LANES), lambda b, j: (b, 0, 0))],
        out_specs=pl.BlockSpec((None, D, TILE), lambda b, j: (b, 0, j)),
        out_shape=jax.ShapeDtypeStruct((B, D, S), F32),
        scratch_shapes=[pltpu.VMEM((D, TILE), BF16), pltpu.VMEM((d_ff, LANES), F32),
                        pltpu.VMEM((d_ff, TILE), BF16)],
        compiler_params=_cp(("arbitrary", "arbitrary")),
        name="convffn",
    )(xT, a2, sh2, wgT, wuT, wdT, cv, gw)


def _band_bias(dilation, old_edge, reps):
    kk = np.arange(LANES)[:, None]
    qq = np.arange(LANES)[None, :]
    res = ((qq - kk) % dilation) == 0
    tabs = [res & (kk - qq >= old_edge), res, res & (kk <= qq), np.zeros_like(res)]
    out = np.stack([np.where(t, 0.0, NEG) for t in tabs]).astype(np.float32)
    return jnp.asarray(np.tile(out, (1, 1, reps)))


def _cmp_bias():
    nn = np.arange(LANES)[:, None]
    qq = np.arange(LANES)[None, :]
    tabs = [np.where(CMP_STRIDE * nn + CMP_BLOCK - 1 <= LANES * r + qq, 0.0, NEG) for r in range(16)]
    return jnp.asarray(np.stack(tabs).astype(np.float32))


def _overlap_rows():
    jj = np.arange(40)[:, None]
    nn = np.arange(LANES)[None, :]
    return jnp.asarray(((nn >= 4 * jj - 1) & (nn <= 4 * jj + 3)).astype(np.float32), dtype=BF16)


_IN_COL_ORDER = ((1804, 2188), (652, 1036), (0, 256),
                 (256, 320), (320, 384), (384, 448), (512, 576), (1036, 1420), (2188, 2316),
                 (448, 512), (576, 640), (1420, 1804), (2316, 2444),
                 (640, 652))


def _prep_w_in(w_in):
    cols = np.concatenate([np.arange(a, b) for a, b in _IN_COL_ORDER])
    wt = jnp.swapaxes(w_in[:, :, cols], 1, 2)
    scale = np.ones((wt.shape[1], 1), np.float32)
    scale[:Q_ROWS] = HEAD_DIM ** -0.5
    wt = wt * scale
    wt = jnp.pad(wt, ((0, 0), (0, W_ROWS - wt.shape[1]), (0, 0)))
    return wt.astype(BF16)


def _prep_compress(w_ck, w_cv, pe_k, pe_v):
    L = w_ck.shape[0]
    half = CMP_BLOCK // 2

    def big(lo):
        wk = w_ck[:, lo:lo + half]
        wv = w_cv[:, lo:lo + half]
        z = jnp.zeros_like(wk)
        top = jnp.concatenate([wk, z], axis=-1)
        bot = jnp.concatenate([z, wv], axis=-1)
        return jnp.concatenate([top, bot], axis=2).reshape(L, half * LANES, LANES)

    wbig = jnp.concatenate([big(0), big(half)], axis=-1).astype(BF16)
    pe = jnp.concatenate([pe_k, pe_v], axis=-1)
    pe2 = jnp.zeros((L, 16, half * LANES), F32)
    pe2 = pe2.at[:, 0].set(pe[:, :half].reshape(L, -1)).at[:, 8].set(pe[:, half:].reshape(L, -1))
    return wbig, pe2.astype(BF16)


def _lane_bcast(v):
    return jnp.broadcast_to(v[..., None], v.shape + (LANES,))


def kernel(x, c, positions, w_in, w_out, w_ada, b_ada, norm_w, cmp_w_k, cmp_w_v, cmp_pe_k, cmp_pe_v,
           sinks, w_gate, w_up, conv_w, conv_b, w_down):
    B, S, D = x.shape
    depth = w_in.shape[0]
    assert S % 2048 == 0 and D == 1024 and w_in.shape[2] == 2444

    inv = ROPE_THETA ** (-jnp.arange(0, HEAD_DIM, 2, dtype=F32) / HEAD_DIM)
    ang = positions.astype(F32)[:, None, :] * inv[None, :, None]
    cosT, sinT = jnp.cos(ang), jnp.sin(ang)

    c8 = jnp.pad(c, ((0, 8 - B), (0, 0)))
    ada = _adaln(c8, w_ada, b_ada)[:, :B]
    sh1, sc1, g1, sh2, sc2, g2 = [ada[:, :, k * D:(k + 1) * D] for k in range(6)]
    nw = norm_w[:, :, None, :]
    a1 = _lane_bcast(nw[:, 0] * (1 + sc1))
    gw1 = _lane_bcast(g1 * nw[:, 1])
    a2 = _lane_bcast(nw[:, 2] * (1 + sc2))
    gw2 = _lane_bcast(g2 * nw[:, 3])
    sh1b, sh2b = _lane_bcast(sh1), _lane_bcast(sh2)

    w_inT = _prep_w_in(w_in)
    w_outT = jnp.swapaxes(w_out, 1, 2).astype(BF16)
    wgT = jnp.swapaxes(w_gate, 1, 2).astype(BF16)
    wuT = jnp.swapaxes(w_up, 1, 2).astype(BF16)
    wdT = jnp.swapaxes(w_down, 1, 2).astype(BF16)
    cv = _lane_bcast(jnp.concatenate([conv_w, conv_b[:, None, :]], axis=1))
    wbig, pe2 = _prep_compress(cmp_w_k, cmp_w_v, cmp_pe_k, cmp_pe_v)
    sink_tab = jnp.broadcast_to(
        jnp.repeat(sinks.reshape(depth, SWA_KV_HEADS, SWA_HEADS // SWA_KV_HEADS), LANES, axis=-1)[:, :, None, :],
        (depth, SWA_KV_HEADS, 8, LANES * (SWA_HEADS // SWA_KV_HEADS)))

    ov = _overlap_rows()
    cmp_bias = _cmp_bias()
    kk = np.arange(SEL_TQ)[:, None]
    causal4 = jnp.asarray(np.tile(np.where(kk <= kk.T, 0.0, NEG).astype(np.float32), (1, NSA_HEADS)))
    bias_win = _band_bias(1, 1, NSA_HEADS)
    bias_swa = _band_bias(1, 1, SWA_HEADS // SWA_KV_HEADS)
    bias_dil = [_band_bias(d, 0, 1) for _, d in DIL_PATTERNS]

    xT = jnp.swapaxes(x, 1, 2)
    for l in range(depth):
        qT, kcvc, nk, v4, gates = _inproj(xT, a1[l], sh1b[l], w_inT, l, cosT, sinT)
        kc4, vc4 = _compress(kcvc, wbig, pe2, l)
        ocmp, mm = _cmp_topk(qT, kc4, vc4, ov, cmp_bias)
        owin, = _banded(qT, A_Q_BLK, nk, 1, v4, 1, bias_win, hkv=1, grp=NSA_HEADS, nprev=NSA_WINDOW // LANES,
                        mid_bias=False, dynamic=False, out_dtype=F32, name="nsa_win")
        aT = _sel_attend(qT, mm, nk, v4, causal4, ocmp, owin, gates)
        bs, lses = [], []
        for gi, (win, dil) in enumerate(DIL_PATTERNS):
            o, lse = _banded(qT, 3 + gi, nk, 2 + gi, v4, 1 + gi, bias_dil[gi], hkv=2, grp=1, nprev=dil,
                             mid_bias=dil > 1, dynamic=dil > NSUB, out_dtype=F32, want_lse=True,
                             name="dil%d" % dil)
            bs.append(o)
            lses.append(lse)
        cT, = _banded(qT, 0, nk, 5, v4, 4, bias_swa, hkv=SWA_KV_HEADS, grp=SWA_HEADS // SWA_KV_HEADS,
                      nprev=SWA_WINDOW // LANES, mid_bias=False, dynamic=False, out_dtype=BF16,
                      sinks=sink_tab[l], name="swa")
        xT = _outproj(aT, bs, lses, cT, w_outT, l, xT, gw1[l])
        xT = _ffn(xT, a2[l], sh2b[l], wgT, wuT, wdT, cv, l, gw2[l])
    return jnp.swapaxes(xT, 1, 2)
```

```python
import functools

import numpy as np
import jax
import jax.numpy as jnp
from jax import lax
from jax.experimental import pallas as pl
from jax.experimental.pallas import tpu as pltpu

F32 = jnp.float32
BF16 = jnp.bfloat16

HEAD_DIM = 64
HALF = HEAD_DIM // 2
NSA_HEADS = 4
CMP_BLOCK = 32
CMP_STRIDE = 16
SEL_BLOCK = 64
SEL_TOPK = 16
NSA_WINDOW = 512
DIL_PATTERNS = ((128, 1), (512, 4), (2048, 16))
DIL_HEADS = 6
SWA_HEADS = 6
SWA_KV_HEADS = 2
SWA_WINDOW = 128
ROPE_THETA = 10000.0
RMS_EPS = 1e-6
NEG = -1e30
FORCE = 1e4
CONV_WIDTH = 3

LANES = 128
TILE = 512
NSUB = TILE // LANES
VROWS = HEAD_DIM + 16
MEMBER_BIG = 2.0 ** 100
VMEM_LIMIT = 56 * 1024 * 1024

Q_ROWS = 1024
A_Q_BLK = 3
NK_ROWS = 768
NK_LANES = 768
V_ROWS = 640
G_ROWS = 16
W_ROWS = Q_ROWS + NK_ROWS + V_ROWS + G_ROWS
N_VPIECES = V_ROWS // HEAD_DIM


def _cp(sem):
    return pltpu.CompilerParams(dimension_semantics=sem, vmem_limit_bytes=VMEM_LIMIT)


def _const_spec(shape, index_map):
    return pl.BlockSpec(shape, index_map, pipeline_mode=pl.Buffered(1))


def _adaln_kernel(c_ref, w_ref, b_ref, o_ref):
    c = c_ref[...]
    cond = c * jax.nn.sigmoid(c)
    o_ref[...] = jnp.dot(cond, w_ref[...], preferred_element_type=F32,
                         precision=lax.Precision.HIGHEST) + b_ref[...]


def _adaln(c8, w_ada, b_ada):
    depth, d, six_d = w_ada.shape
    nblk = six_d // d
    return pl.pallas_call(
        _adaln_kernel,
        grid=(depth, nblk),
        in_specs=[pl.BlockSpec((8, d), lambda l, n: (0, 0)),
                  pl.BlockSpec((None, d, d), lambda l, n: (l, 0, n)),
                  pl.BlockSpec((None, 1, d), lambda l, n: (l, 0, n))],
        out_specs=pl.BlockSpec((None, 8, d), lambda l, n: (l, 0, n)),
        out_shape=jax.ShapeDtypeStruct((depth, 8, six_d), F32),
        compiler_params=_cp(("parallel", "parallel")),
        name="adaln",
    )(c8, w_ada, b_ada.reshape(depth, 1, six_d))


def _norm_mod_to_scratch(x_ref, a_ref, sh_ref, h_scr):
    for c in range(NSUB):
        sl = slice(c * LANES, (c + 1) * LANES)
        xs = x_ref[:, sl]
        ms = jnp.mean(xs * xs, axis=0, keepdims=True)
        h_scr[:, sl] = ((xs * lax.rsqrt(ms + RMS_EPS)) * a_ref[...] + sh_ref[...]).astype(BF16)


def _inproj_kernel(x_ref, a_ref, sh_ref, w_ref, cos_ref, sin_ref,
                   q_ref, kcvc_ref, nk_ref, v4_ref, gate_ref, h_scr):
    j = pl.program_id(1)
    _norm_mod_to_scratch(x_ref, a_ref, sh_ref, h_scr)
    h = h_scr[...]
    cos = cos_ref[...]
    sin = sin_ref[...]

    def proj(r0, r1):
        return jnp.dot(w_ref[r0:r1, :], h, preferred_element_type=F32)

    def rope(r, nh):
        outs = []
        for hh in range(nh):
            t1 = r[HEAD_DIM * hh:HEAD_DIM * hh + HALF]
            t2 = r[HEAD_DIM * hh + HALF:HEAD_DIM * (hh + 1)]
            outs.append(t1 * cos - t2 * sin)
            outs.append(t2 * cos + t1 * sin)
        return jnp.concatenate(outs, axis=0)

    for r0, r1 in ((0, 384), (384, 768), (768, 1024)):
        q_ref[r0:r1, :] = rope(proj(r0, r1), (r1 - r0) // HEAD_DIM).astype(BF16)

    base = Q_ROWS
    r = proj(base, base + 128)
    kcvc = jnp.concatenate([rope(r[0:64], 1), r[64:128]], axis=0)
    kcvc_ref[...] = kcvc.T.astype(BF16)

    r = proj(base + 128, base + 192)
    tok = j * TILE + lax.broadcasted_iota(jnp.int32, (HEAD_DIM, TILE), 1)
    row = lax.broadcasted_iota(jnp.int32, (HEAD_DIM, TILE), 0)
    member_cols = jnp.where(row == ((tok >> 6) & 15), MEMBER_BIG, 0.0).astype(F32)
    nk_ref[:, 0:128] = jnp.concatenate([rope(r, 1), member_cols], axis=0).T.astype(BF16)

    r = proj(base + 192, base + 256)
    nk_ref[:, 128:256] = jnp.concatenate([rope(r, 1), jnp.zeros((HEAD_DIM, TILE), F32)], axis=0).T.astype(BF16)

    r = proj(base + 256, base + 640)
    nk_ref[:, 256:640] = rope(r, 6).T.astype(BF16)

    r = proj(base + 640, base + 768)
    nk_ref[:, 640:768] = rope(r, 2).T.astype(BF16)

    base = Q_ROWS + NK_ROWS
    r = proj(base, base + V_ROWS).astype(BF16)
    ones = jnp.ones((VROWS - HEAD_DIM, LANES), BF16)
    for c in range(NSUB):
        for p in range(N_VPIECES):
            v4_ref[c, VROWS * p:VROWS * p + HEAD_DIM, :] = r[HEAD_DIM * p:HEAD_DIM * (p + 1),
                                                             c * LANES:(c + 1) * LANES]
            v4_ref[c, VROWS * p + HEAD_DIM:VROWS * (p + 1), :] = ones

    base = Q_ROWS + NK_ROWS + V_ROWS
    gate_ref[...] = jax.nn.sigmoid(proj(base, base + G_ROWS))


def _inproj(xT, a1, sh1, w_inT, layer, cosT, sinT):
    B, D, S = xT.shape
    nt = S // TILE
    return pl.pallas_call(
        _inproj_kernel,
        grid=(B, nt),
        in_specs=[pl.BlockSpec((None, D, TILE), lambda b, j: (b, 0, j)),
                  pl.BlockSpec((None, D, LANES), lambda b, j: (b, 0, 0)),
                  pl.BlockSpec((None, D, LANES), lambda b, j: (b, 0, 0)),
                  _const_spec((None, W_ROWS, D), lambda b, j: (layer, 0, 0)),
                  pl.BlockSpec((None, HALF, TILE), lambda b, j: (b, 0, j)),
                  pl.BlockSpec((None, HALF, TILE), lambda b, j: (b, 0, j))],
        out_specs=[pl.BlockSpec((None, Q_ROWS, TILE), lambda b, j: (b, 0, j)),
                   pl.BlockSpec((None, TILE, LANES), lambda b, j: (b, j, 0)),
                   pl.BlockSpec((None, TILE, NK_LANES), lambda b, j: (b, j, 0)),
                   pl.BlockSpec((None, NSUB, N_VPIECES * VROWS, LANES), lambda b, j: (b, j, 0, 0)),
                   pl.BlockSpec((None, G_ROWS, TILE), lambda b, j: (b, 0, j))],
        out_shape=[jax.ShapeDtypeStruct((B, Q_ROWS, S), BF16),
                   jax.ShapeDtypeStruct((B, S, LANES), BF16),
                   jax.ShapeDtypeStruct((B, S, NK_LANES), BF16),
                   jax.ShapeDtypeStruct((B, S // LANES, N_VPIECES * VROWS, LANES), BF16),
                   jax.ShapeDtypeStruct((B, G_ROWS, S), F32)],
        scratch_shapes=[pltpu.VMEM((D, TILE), BF16)],
        compiler_params=_cp(("parallel", "parallel")),
        name="inproj",
    )(xT, a1, sh1, w_inT, cosT, sinT)


def _compress_kernel(t_ref, w_ref, pe_ref, kc_ref, vc_ref):
    n = t_ref.shape[0]
    a = jnp.dot(t_ref[...], w_ref[...], preferred_element_type=F32)
    pc = jnp.dot(pe_ref[...], w_ref[...], preferred_element_type=F32)
    const = pc[0:1, 0:LANES] + pc[8:9, LANES:2 * LANES]
    cmp = a[:, 0:LANES] + pltpu.roll(a[:, LANES:2 * LANES], n - 1, 0) + const
    cmp_t = cmp.T
    ones = jnp.ones((VROWS - HEAD_DIM, LANES), BF16)
    for c in range(n // LANES):
        kc_ref[c] = cmp[c * LANES:(c + 1) * LANES].astype(BF16)
        vc_ref[c, 0:HEAD_DIM, :] = cmp_t[HEAD_DIM:2 * HEAD_DIM, c * LANES:(c + 1) * LANES].astype(BF16)
        vc_ref[c, HEAD_DIM:VROWS, :] = ones


def _compress(kcvc, wbig, pe2, layer):
    B, S, _ = kcvc.shape
    n = S // CMP_STRIDE
    nch = n // LANES
    tview = kcvc.reshape(B, n, CMP_STRIDE * LANES)
    return pl.pallas_call(
        _compress_kernel,
        grid=(B,),
        in_specs=[pl.BlockSpec((None, n, CMP_STRIDE * LANES), lambda b: (b, 0, 0)),
                  pl.BlockSpec((None, CMP_STRIDE * LANES, 2 * LANES), lambda b: (layer, 0, 0)),
                  pl.BlockSpec((None, 16, CMP_STRIDE * LANES), lambda b: (layer, 0, 0))],
        out_specs=[pl.BlockSpec((None, nch, LANES, LANES), lambda b: (b, 0, 0, 0)),
                   pl.BlockSpec((None, nch, VROWS, LANES), lambda b: (b, 0, 0, 0))],
        out_shape=[jax.ShapeDtypeStruct((B, nch, LANES, LANES), BF16),
                   jax.ShapeDtypeStruct((B, nch, VROWS, LANES), BF16)],
        compiler_params=_cp(("parallel",)),
        name="compress",
    )(tview, wbig, pe2)


def _stack_heads(q_ref, nh, lane_slice=slice(None)):
    return jnp.concatenate([q_ref[HEAD_DIM * h:HEAD_DIM * (h + 1), lane_slice] for h in range(nh)], axis=1)


CMP_UNITS = 2
CMP_TQ = CMP_UNITS * LANES


def _cmp_kernel(q_ref, kc_ref, vc_ref, ov_ref, bias_ref, o_ref, mm_ref, imp_scr, *, n_sel, nstep):
    i = pl.program_id(1)
    nch_all = kc_ref.shape[0]
    parts = 4 if nch_all % 4 == 0 else (2 if nch_all % 2 == 0 else 1)
    for v in range(parts):
        pl.when(i // (nstep // parts) == v)(functools.partial(
            _cmp_body, q_ref, kc_ref, vc_ref, ov_ref, bias_ref, o_ref, mm_ref, imp_scr,
            nch=nch_all * (v + 1) // parts, nrows=n_sel * (v + 1) // parts, n_sel=n_sel))


def _cmp_body(q_ref, kc_ref, vc_ref, ov_ref, bias_ref, o_ref, mm_ref, imp_scr, *, nch, nrows, n_sel):
    i = pl.program_id(1)
    cd = (i * CMP_UNITS) // 16

    def scores(u):
        q = _stack_heads(q_ref, NSA_HEADS, slice(u * LANES, (u + 1) * LANES))
        edge = bias_ref[u]
        ss = []
        for c in range(nch):
            b = jnp.where(c < cd, 0.0, jnp.where(c == cd, edge, NEG))
            s = jnp.dot(kc_ref[c, :, 0:HEAD_DIM], q, preferred_element_type=F32)
            ss.append(s + jnp.concatenate([b] * NSA_HEADS, axis=1))
        return ss

    def finish(u, ss):
        lanes = slice(u * LANES, (u + 1) * LANES)
        m = jnp.max(functools.reduce(jnp.maximum, ss), axis=0, keepdims=True)
        valid = m > 0.5 * NEG
        imp_scr[u, 0:32 * nch + 8, :] = jnp.zeros((32 * nch + 8, imp_scr.shape[2]), F32)
        acc = None
        for c in range(nch):
            p = jnp.exp(ss[c] - m).astype(BF16)
            t = jnp.dot(vc_ref[c], p, preferred_element_type=F32)
            acc = t if acc is None else acc + t
            imp_scr[u, 32 * c:32 * c + 40, :] += jnp.dot(ov_ref[...], p, preferred_element_type=F32)
        inv = jnp.where(valid, 1.0 / acc[HEAD_DIM:HEAD_DIM + 1], 0.0)
        o = acc[0:HEAD_DIM] * inv
        for h in range(NSA_HEADS):
            o_ref[HEAD_DIM * h:HEAD_DIM * (h + 1), lanes] = o[:, h * LANES:(h + 1) * LANES]
        imp = jnp.zeros((nrows, LANES), F32)
        for h in range(NSA_HEADS):
            sl = slice(h * LANES, (h + 1) * LANES)
            imp = imp + imp_scr[u, 0:nrows, sl] * inv[:, sl]
        return imp

    pending = scores(0)
    imps = []
    for u in range(1, CMP_UNITS):
        nxt = scores(u)
        imps.append(finish(u - 1, pending))
        pending = nxt
    imps.append(finish(CMP_UNITS - 1, pending))

    blk = lax.broadcasted_iota(jnp.int32, (nrows, LANES), 0).astype(F32)

    def pick(_, imp):
        mx = jnp.max(imp, axis=0, keepdims=True)
        first = jnp.min(jnp.where(imp == mx, blk, float(nrows)), axis=0, keepdims=True)
        return jnp.where(blk == first, -jnp.inf, imp)

    for u in range(CMP_UNITS):
        t = i * CMP_TQ + u * LANES + lax.broadcasted_iota(jnp.int32, (nrows, LANES), 1)
        cur = (t >> 6).astype(F32)
        forced = (blk == 0.0) | (blk == cur) | (blk == cur - 1.0)
        imp = jnp.where(forced, FORCE, imps[u])
        imp = jnp.where(blk <= cur, imp, NEG)
        imp = lax.fori_loop(0, min(SEL_TOPK, nrows), pick, imp)
        member = (imp == -jnp.inf) & (blk <= cur)
        mm_ref[0:nrows, u * LANES:(u + 1) * LANES] = jnp.where(member, 0.0, -1.0).astype(BF16)
    if nrows < n_sel:
        mm_ref[nrows:n_sel, :] = jnp.full((n_sel - nrows, CMP_TQ), -1.0, BF16)


def _cmp_topk(qT, kc4, vc4, ov, cmp_bias):
    B, _, S = qT.shape
    nstep = S // CMP_TQ
    nch = kc4.shape[1]
    n_sel = S // SEL_BLOCK
    nq = NSA_HEADS * LANES
    nbias = cmp_bias.shape[0] // CMP_UNITS
    return pl.pallas_call(
        functools.partial(_cmp_kernel, n_sel=n_sel, nstep=nstep),
        grid=(B, nstep),
        in_specs=[pl.BlockSpec((None, NSA_HEADS * HEAD_DIM, CMP_TQ), lambda b, i: (b, A_Q_BLK, i)),
                  pl.BlockSpec((None, nch, LANES, LANES), lambda b, i: (b, 0, 0, 0)),
                  pl.BlockSpec((None, nch, VROWS, LANES), lambda b, i: (b, 0, 0, 0)),
                  pl.BlockSpec((40, LANES), lambda b, i: (0, 0)),
                  pl.BlockSpec((None, CMP_UNITS, LANES, LANES), lambda b, i: (i % nbias, 0, 0, 0))],
        out_specs=[pl.BlockSpec((None, NSA_HEADS * HEAD_DIM, CMP_TQ), lambda b, i: (b, 0, i)),
                   pl.BlockSpec((None, n_sel, CMP_TQ), lambda b, i: (b, 0, i))],
        out_shape=[jax.ShapeDtypeStruct((B, NSA_HEADS * HEAD_DIM, S), F32),
                   jax.ShapeDtypeStruct((B, n_sel, S), BF16)],
        scratch_shapes=[pltpu.VMEM((CMP_UNITS, 32 * nch + 64, nq), F32)],
        compiler_params=_cp(("parallel", "parallel")),
        name="nsa_cmp_topk",
    )(qT, kc4, vc4, ov, cmp_bias.reshape(nbias, CMP_UNITS, LANES, LANES))


SEL_GROUP = 8


SEL_TQ = 2 * LANES


def _sel_kernel(q_ref, mm_ref, ks_ref, v_ref, causal_ref, ocmp_ref, owin_ref, gate_ref, out_ref,
                qa_scr, mm_scr, s_scr):
    i = pl.program_id(1)
    nq = NSA_HEADS * SEL_TQ
    qa_scr[0:HEAD_DIM, :] = _stack_heads(q_ref, NSA_HEADS)
    qa_scr[HEAD_DIM:LANES, :] = jnp.zeros((LANES - HEAD_DIM, nq), BF16)
    mm = mm_ref[...]
    for h in range(NSA_HEADS):
        mm_scr[:, h * SEL_TQ:(h + 1) * SEL_TQ] = mm

    gkeys = SEL_GROUP * LANES

    def load_query(gi):
        qa_scr[HEAD_DIM:HEAD_DIM + 16, :] = mm_scr[pl.ds(pl.multiple_of(gi * 16, 16), 16), :]
        return qa_scr[...]

    def chunk_scores(gi, u, qa):
        row = pl.multiple_of((gi * SEL_GROUP + u) * LANES, LANES)
        return jnp.dot(ks_ref[pl.ds(row, LANES), :], qa, preferred_element_type=F32)

    def weighted_values(gi, ps):
        vcat = jnp.concatenate([v_ref[gi * SEL_GROUP + u] for u in range(SEL_GROUP)], axis=1)
        return jnp.dot(vcat, jnp.concatenate(ps, axis=0), preferred_element_type=F32)

    def body(gi, carry):
        m, acc, mg = carry
        m_new = jnp.maximum(m, mg)
        qa = load_query(gi + 1)
        ps, mx = [], None
        for u in range(SEL_GROUP):
            rows = slice(u * LANES, (u + 1) * LANES)
            ps.append(jnp.exp(s_scr[rows, :] - m_new).astype(BF16))
            nxt = chunk_scores(gi + 1, u, qa)
            s_scr[rows, :] = nxt
            mx = nxt if mx is None else jnp.maximum(mx, nxt)
        acc = acc * jnp.exp(m - m_new) + weighted_values(gi, ps)
        return m_new, acc, jnp.max(mx, axis=0, keepdims=True)

    qa = load_query(0)
    mx = None
    for u in range(SEL_GROUP):
        s0 = chunk_scores(0, u, qa)
        s_scr[u * LANES:(u + 1) * LANES, :] = s0
        mx = s0 if mx is None else jnp.maximum(mx, s0)
    c0 = i * (SEL_TQ // LANES)
    last = c0 // SEL_GROUP
    m, acc, _ = lax.fori_loop(0, last, body, (jnp.full((1, nq), NEG, F32), jnp.zeros((VROWS, nq), F32),
                                              jnp.max(mx, axis=0, keepdims=True)))
    drow = pl.multiple_of((c0 % SEL_GROUP) * LANES, SEL_TQ)
    s_scr[pl.ds(drow, SEL_TQ), :] += causal_ref[...]
    s = s_scr[...]
    m_new = jnp.maximum(m, jnp.max(s, axis=0, keepdims=True))
    acc = acc * jnp.exp(m - m_new) + weighted_values(last, [jnp.exp(s - m_new).astype(BF16)])
    o = acc[0:HEAD_DIM] * (1.0 / acc[HEAD_DIM:HEAD_DIM + 1])

    g = gate_ref[...]
    for h in range(NSA_HEADS):
        rs = slice(HEAD_DIM * h, HEAD_DIM * (h + 1))
        out = (g[3 * h:3 * h + 1] * ocmp_ref[rs, :] + g[3 * h + 1:3 * h + 2] * o[:, h * SEL_TQ:(h + 1) * SEL_TQ]
               + g[3 * h + 2:3 * h + 3] * owin_ref[rs, :])
        out_ref[rs, :] = out.astype(BF16)


def _sel_attend(qT, mm, nk, v4, causal4, ocmp, owin, gates):
    B, _, S = qT.shape
    nstep = S // SEL_TQ
    n_sel = S // SEL_BLOCK
    nq = NSA_HEADS * SEL_TQ
    ar = NSA_HEADS * HEAD_DIM
    return pl.pallas_call(
        _sel_kernel,
        grid=(B, nstep),
        in_specs=[pl.BlockSpec((None, ar, SEL_TQ), lambda b, i: (b, A_Q_BLK, i)),
                  pl.BlockSpec((None, n_sel, SEL_TQ), lambda b, i: (b, 0, i)),
                  pl.BlockSpec((None, S, LANES), lambda b, i: (b, 0, 0)),
                  pl.BlockSpec((None, S // LANES, VROWS, LANES), lambda b, i: (b, 0, 0, 0)),
                  pl.BlockSpec((SEL_TQ, nq), lambda b, i: (0, 0)),
                  pl.BlockSpec((None, ar, SEL_TQ), lambda b, i: (b, 0, i)),
                  pl.BlockSpec((None, ar, SEL_TQ), lambda b, i: (b, 0, i)),
                  pl.BlockSpec((None, G_ROWS, SEL_TQ), lambda b, i: (b, 0, i))],
        out_specs=pl.BlockSpec((None, ar, SEL_TQ), lambda b, i: (b, 0, i)),
        out_shape=jax.ShapeDtypeStruct((B, ar, S), BF16),
        scratch_shapes=[pltpu.VMEM((LANES, nq), BF16), pltpu.VMEM((n_sel, nq), BF16),
                        pltpu.VMEM((SEL_GROUP * LANES, nq), F32)],
        compiler_params=_cp(("parallel", "parallel")),
        name="nsa_sel",
    )(qT, mm, nk, v4, causal4, ocmp, owin, gates)


def _banded_kernel(*refs, hkv, grp, nprev, mid_bias, dynamic, has_sink, want_lse):
    q_ref, k_ref, v_ref, bias_ref = refs[:4]
    pos = 4
    sink_ref = None
    if has_sink:
        sink_ref = refs[pos]
        pos += 1
    o_ref = refs[pos]
    lse_ref = refs[pos + 1] if want_lse else None
    j = pl.program_id(1)

    def scores(s, g, first):
        lanes = slice(s * LANES, (s + 1) * LANES)
        qg = jnp.concatenate([q_ref[HEAD_DIM * (g * grp + u):HEAD_DIM * (g * grp + u + 1), lanes]
                              for u in range(grp)], axis=1)
        parts = []
        for ci in range(nprev + 1):
            which = 0 if ci == 0 else (2 if ci == nprev else 1)
            if first:
                kc = s - nprev + ci
                if kc < 0:
                    continue
                kcc = kc
                row = kc * LANES
            else:
                kc = j * NSUB + s - nprev + ci
                kcc = jnp.maximum(kc, 0) if dynamic else kc
                row = pl.multiple_of(kcc * LANES, LANES)
            sc = jnp.dot(k_ref[pl.ds(row, LANES), HEAD_DIM * g:HEAD_DIM * (g + 1)], qg,
                         preferred_element_type=F32)
            if dynamic:
                sc = sc + bias_ref[jnp.where(kc >= 0, which, 3)]
            elif which != 1 or mid_bias:
                sc = sc + bias_ref[which]
            parts.append((kcc, sc))
        return parts

    def finish(s, g, parts):
        lanes = slice(s * LANES, (s + 1) * LANES)
        m = None
        for _, sc in parts:
            mc = jnp.max(sc, axis=0, keepdims=True)
            m = mc if m is None else jnp.maximum(m, mc)
        if has_sink:
            sk = sink_ref[g, 0:1, :]
            m = jnp.maximum(m, sk)
        acc = None
        for kcc, sc in parts:
            t = jnp.dot(v_ref[kcc, VROWS * g:VROWS * (g + 1), :], jnp.exp(sc - m).astype(BF16),
                        preferred_element_type=F32)
            acc = t if acc is None else acc + t
        l = acc[HEAD_DIM:HEAD_DIM + 1]
        if has_sink:
            l = l + jnp.exp(sk - m)
        o = acc[0:HEAD_DIM] * (1.0 / l)
        for u in range(grp):
            hq = g * grp + u
            o_ref[HEAD_DIM * hq:HEAD_DIM * (hq + 1), lanes] = o[:, u * LANES:(u + 1) * LANES].astype(o_ref.dtype)
        if want_lse:
            lse_ref[8 * g:8 * (g + 1), lanes] = jnp.broadcast_to(m + jnp.log(l), (8, LANES))

    def run(first):
        pending = None
        for s in range(NSUB):
            for g in range(hkv):
                parts = scores(s, g, first)
                if pending is not None:
                    finish(*pending)
                pending = (s, g, parts)
        finish(*pending)

    if dynamic:
        run(False)
    else:
        pl.when(j == 0)(lambda: run(True))
        pl.when(j > 0)(lambda: run(False))


def _banded(qT, q_blk, nk, k_blk, v4, v_blk, bias, *, hkv, grp, nprev, mid_bias, dynamic,
            out_dtype, sinks=None, want_lse=False, name):
    B, _, S = qT.shape
    nt = S // TILE
    qrows = hkv * grp * HEAD_DIM
    in_specs = [pl.BlockSpec((None, qrows, TILE), lambda b, j: (b, q_blk, j)),
                pl.BlockSpec((None, S, LANES), lambda b, j: (b, 0, k_blk)),
                pl.BlockSpec((None, S // LANES, hkv * VROWS, LANES), lambda b, j: (b, 0, v_blk, 0)),
                pl.BlockSpec(bias.shape, lambda b, j: (0, 0, 0))]
    args = [qT, nk, v4, bias]
    if sinks is not None:
        in_specs.append(pl.BlockSpec(sinks.shape, lambda b, j: (0, 0, 0)))
        args.append(sinks)
    out_specs = [pl.BlockSpec((None, qrows, TILE), lambda b, j: (b, 0, j))]
    out_shape = [jax.ShapeDtypeStruct((B, qrows, S), out_dtype)]
    if want_lse:
        out_specs.append(pl.BlockSpec((None, 8 * hkv, TILE), lambda b, j: (b, 0, j)))
        out_shape.append(jax.ShapeDtypeStruct((B, 8 * hkv, S), F32))
    return pl.pallas_call(
        functools.partial(_banded_kernel, hkv=hkv, grp=grp, nprev=nprev, mid_bias=mid_bias,
                          dynamic=dynamic, has_sink=sinks is not None, want_lse=want_lse),
        grid=(B, nt),
        in_specs=in_specs,
        out_specs=out_specs,
        out_shape=out_shape,
        compiler_params=_cp(("parallel", "parallel")),
        name=name,
    )(*args)


def _outproj_kernel(a_ref, b0_ref, b1_ref, b2_ref, l0_ref, l1_ref, l2_ref, c_ref, w_ref, x_ref, gw_ref,
                    o_ref, mix_scr):
    b_refs = (b0_ref, b1_ref, b2_ref)
    l_refs = (l0_ref, l1_ref, l2_ref)
    for h in range(2):
        ls = [r[8 * h:8 * h + 1, :] for r in l_refs]
        mx = jnp.maximum(jnp.maximum(ls[0], ls[1]), ls[2])
        es = [jnp.exp(v - mx) for v in ls]
        inv = 1.0 / (es[0] + es[1] + es[2])
        for g in range(3):
            rs = slice(HEAD_DIM * h, HEAD_DIM * (h + 1))
            mix_scr[128 * g + HEAD_DIM * h:128 * g + HEAD_DIM * (h + 1), :] = (
                b_refs[g][rs, :] * (es[g] * inv)).astype(BF16)
    y = jnp.dot(w_ref[:, 0:256], a_ref[...], preferred_element_type=F32)
    y = y + jnp.dot(w_ref[:, 256:640], mix_scr[...], preferred_element_type=F32)
    y = y + jnp.dot(w_ref[:, 640:1024], c_ref[...], preferred_element_type=F32)
    ms = jnp.mean(y * y, axis=0, keepdims=True)
    yn = y * lax.rsqrt(ms + RMS_EPS)
    gw = gw_ref[...]
    for c in range(NSUB):
        sl = slice(c * LANES, (c + 1) * LANES)
        o_ref[:, sl] = x_ref[:, sl] + gw * yn[:, sl]


def _outproj(aT, bs, lses, cT, w_outT, layer, xT, gw):
    B, D, S = xT.shape
    nt = S // TILE
    tile = lambda rows: pl.BlockSpec((None, rows, TILE), lambda b, j: (b, 0, j))
    return pl.pallas_call(
        _outproj_kernel,
        grid=(B, nt),
        in_specs=[tile(256), tile(128), tile(128), tile(128), tile(16), tile(16), tile(16), tile(384),
                  _const_spec((None, D, D), lambda b, j: (layer, 0, 0)),
                  tile(D),
                  pl.BlockSpec((None, D, LANES), lambda b, j: (b, 0, 0))],
        out_specs=tile(D),
        out_shape=jax.ShapeDtypeStruct((B, D, S), F32),
        scratch_shapes=[pltpu.VMEM((384, TILE), BF16)],
        compiler_params=_cp(("parallel", "parallel")),
        name="outproj",
    )(aT, bs[0], bs[1], bs[2], lses[0], lses[1], lses[2], cT, w_outT, xT, gw)


def _ffn_kernel(x_ref, a_ref, sh_ref, wg_ref, wu_ref, wd_ref, cv_ref, gw_ref, o_ref,
                h_scr, carry_scr, act_scr, *, fchunk):
    j = pl.program_id(1)
    d_ff = wg_ref.shape[0]

    @pl.when(j == 0)
    def _():
        carry_scr[...] = jnp.zeros(carry_scr.shape, F32)

    _norm_mod_to_scratch(x_ref, a_ref, sh_ref, h_scr)
    h = h_scr[...]
    lane = lax.broadcasted_iota(jnp.int32, (fchunk, LANES), 1)
    for c in range(d_ff // fchunk):
        rs = slice(c * fchunk, (c + 1) * fchunk)
        g = jnp.dot(wg_ref[rs, :], h, preferred_element_type=F32)
        prev = carry_scr[rs, :]
        carry_scr[rs, :] = g[:, TILE - LANES:TILE]
        g1 = pltpu.roll(g, 1, 1)
        g2 = pltpu.roll(g, 2, 1)
        fix1 = jnp.where(lane < 1, pltpu.roll(prev, 1, 1), g1[:, 0:LANES])
        fix2 = jnp.where(lane < 2, pltpu.roll(prev, 2, 1), g2[:, 0:LANES])
        g1 = jnp.concatenate([fix1, g1[:, LANES:]], axis=1)
        g2 = jnp.concatenate([fix2, g2[:, LANES:]], axis=1)
        w0 = jnp.concatenate([cv_ref[0, rs, :]] * NSUB, axis=1)
        w1 = jnp.concatenate([cv_ref[1, rs, :]] * NSUB, axis=1)
        w2 = jnp.concatenate([cv_ref[2, rs, :]] * NSUB, axis=1)
        cb = jnp.concatenate([cv_ref[3, rs, :]] * NSUB, axis=1)
        acc = g2 * w0 + g1 * w1 + g * w2 + cb
        up = jnp.dot(wu_ref[rs, :], h, preferred_element_type=F32)
        act_scr[rs, :] = (jax.nn.gelu(acc, approximate=True) * up).astype(BF16)
    y = jnp.dot(wd_ref[...], act_scr[...], preferred_element_type=F32)
    ms = jnp.mean(y * y, axis=0, keepdims=True)
    yn = y * lax.rsqrt(ms + RMS_EPS)
    gw = gw_ref[...]
    for c in range(NSUB):
        sl = slice(c * LANES, (c + 1) * LANES)
        o_ref[:, sl] = x_ref[:, sl] + gw * yn[:, sl]


def _ffn(xT, a2, sh2, wgT, wuT, wdT, cv, layer, gw):
    B, D, S = xT.shape
    nt = S // TILE
    d_ff = wgT.shape[1]
    fchunk = 256
    return pl.pallas_call(
        functools.partial(_ffn_kernel, fchunk=fchunk),
        grid=(B, nt),
        in_specs=[pl.BlockSpec((None, D, TILE), lambda b, j: (b, 0, j)),
                  pl.BlockSpec((None, D, LANES), lambda b, j: (b, 0, 0)),
                  pl.BlockSpec((None, D, LANES), lambda b, j: (b, 0, 0)),
                  _const_spec((None, d_ff, D), lambda b, j: (layer, 0, 0)),
                  _const_spec((None, d_ff, D), lambda b, j: (layer, 0, 0)),
                  _const_spec((None, D, d_ff), lambda b, j: (layer, 0, 0)),
                  _const_spec((None, 4, d_ff, LANES), lambda b, j: (layer, 0, 0, 0)),
                  pl.BlockSpec((None, D, LANES), lambda b, j: (b, 0, 0))],
        out_specs=pl.BlockSpec((None, D, TILE), lambda b, j: (b, 0, j)),
        out_shape=jax.ShapeDtypeStruct((B, D, S), F32),
        scratch_shapes=[pltpu.VMEM((D, TILE), BF16), pltpu.VMEM((d_ff, LANES), F32),
                        pltpu.VMEM((d_ff, TILE), BF16)],
        compiler_params=_cp(("arbitrary", "arbitrary")),
        name="convffn",
    )(xT, a2, sh2, wgT, wuT, wdT, cv, gw)


def _band_bias(dilation, old_edge, reps):
    kk = np.arange(LANES)[:, None]
    qq = np.arange(LANES)[None, :]
    res = ((qq - kk) % dilation) == 0
    tabs = [res & (kk - qq >= old_edge), res, res & (kk <= qq), np.zeros_like(res)]
    out = np.stack([np.where(t, 0.0, NEG) for t in tabs]).astype(np.float32)
    return jnp.asarray(np.tile(out, (1, 1, reps)))


def _cmp_bias():
    nn = np.arange(LANES)[:, None]
    qq = np.arange(LANES)[None, :]
    tabs = [np.where(CMP_STRIDE * nn + CMP_BLOCK - 1 <= LANES * r + qq, 0.0, NEG) for r in range(16)]
    return jnp.asarray(np.stack(tabs).astype(np.float32))


def _overlap_rows():
    jj = np.arange(40)[:, None]
    nn = np.arange(LANES)[None, :]
    return jnp.asarray(((nn >= 4 * jj - 1) & (nn <= 4 * jj + 3)).astype(np.float32), dtype=BF16)


_IN_COL_ORDER = ((1804, 2188), (652, 1036), (0, 256),
                 (256, 320), (320, 384), (384, 448), (512, 576), (1036, 1420), (2188, 2316),
                 (448, 512), (576, 640), (1420, 1804), (2316, 2444),
                 (640, 652))


def _prep_w_in(w_in):
    cols = np.concatenate([np.arange(a, b) for a, b in _IN_COL_ORDER])
    wt = jnp.swapaxes(w_in[:, :, cols], 1, 2)
    scale = np.ones((wt.shape[1], 1), np.float32)
    scale[:Q_ROWS] = HEAD_DIM ** -0.5
    wt = wt * scale
    wt = jnp.pad(wt, ((0, 0), (0, W_ROWS - wt.shape[1]), (0, 0)))
    return wt.astype(BF16)


def _prep_compress(w_ck, w_cv, pe_k, pe_v):
    L = w_ck.shape[0]
    half = CMP_BLOCK // 2

    def big(lo):
        wk = w_ck[:, lo:lo + half]
        wv = w_cv[:, lo:lo + half]
        z = jnp.zeros_like(wk)
        top = jnp.concatenate([wk, z], axis=-1)
        bot = jnp.concatenate([z, wv], axis=-1)
        return jnp.concatenate([top, bot], axis=2).reshape(L, half * LANES, LANES)

    wbig = jnp.concatenate([big(0), big(half)], axis=-1).astype(BF16)
    pe = jnp.concatenate([pe_k, pe_v], axis=-1)
    pe2 = jnp.zeros((L, 16, half * LANES), F32)
    pe2 = pe2.at[:, 0].set(pe[:, :half].reshape(L, -1)).at[:, 8].set(pe[:, half:].reshape(L, -1))
    return wbig, pe2.astype(BF16)


def _lane_bcast(v):
    return jnp.broadcast_to(v[..., None], v.shape + (LANES,))


def kernel(x, c, positions, w_in, w_out, w_ada, b_ada, norm_w, cmp_w_k, cmp_w_v, cmp_pe_k, cmp_pe_v,
           sinks, w_gate, w_up, conv_w, conv_b, w_down):
    B, S, D = x.shape
    depth = w_in.shape[0]
    assert S % 2048 == 0 and D == 1024 and w_in.shape[2] == 2444

    inv = ROPE_THETA ** (-jnp.arange(0, HEAD_DIM, 2, dtype=F32) / HEAD_DIM)
    ang = positions.astype(F32)[:, None, :] * inv[None, :, None]
    cosT, sinT = jnp.cos(ang), jnp.sin(ang)

    c8 = jnp.pad(c, ((0, 8 - B), (0, 0)))
    ada = _adaln(c8, w_ada, b_ada)[:, :B]
    sh1, sc1, g1, sh2, sc2, g2 = [ada[:, :, k * D:(k + 1) * D] for k in range(6)]
    nw = norm_w[:, :, None, :]
    a1 = _lane_bcast(nw[:, 0] * (1 + sc1))
    gw1 = _lane_bcast(g1 * nw[:, 1])
    a2 = _lane_bcast(nw[:, 2] * (1 + sc2))
    gw2 = _lane_bcast(g2 * nw[:, 3])
    sh1b, sh2b = _lane_bcast(sh1), _lane_bcast(sh2)

    w_inT = _prep_w_in(w_in)
    w_outT = jnp.swapaxes(w_out, 1, 2).astype(BF16)
    wgT = jnp.swapaxes(w_gate, 1, 2).astype(BF16)
    wuT = jnp.swapaxes(w_up, 1, 2).astype(BF16)
    wdT = jnp.swapaxes(w_down, 1, 2).astype(BF16)
    cv = _lane_bcast(jnp.concatenate([conv_w, conv_b[:, None, :]], axis=1))
    wbig, pe2 = _prep_compress(cmp_w_k, cmp_w_v, cmp_pe_k, cmp_pe_v)
    sink_tab = jnp.broadcast_to(
        jnp.repeat(sinks.reshape(depth, SWA_KV_HEADS, SWA_HEADS // SWA_KV_HEADS), LANES, axis=-1)[:, :, None, :],
        (depth, SWA_KV_HEADS, 8, LANES * (SWA_HEADS // SWA_KV_HEADS)))

    ov = _overlap_rows()
    cmp_bias = _cmp_bias()
    kk = np.arange(SEL_TQ)[:, None]
    causal4 = jnp.asarray(np.tile(np.where(kk <= kk.T, 0.0, NEG).astype(np.float32), (1, NSA_HEADS)))
    bias_win = _band_bias(1, 1, NSA_HEADS)
    bias_swa = _band_bias(1, 1, SWA_HEADS // SWA_KV_HEADS)
    bias_dil = [_band_bias(d, 0, 1) for _, d in DIL_PATTERNS]

    xT = jnp.swapaxes(x, 1, 2)
    for l in range(depth):
        qT, kcvc, nk, v4, gates = _inproj(xT, a1[l], sh1b[l], w_inT, l, cosT, sinT)
        kc4, vc4 = _compress(kcvc, wbig, pe2, l)
        ocmp, mm = _cmp_topk(qT, kc4, vc4, ov, cmp_bias)
        owin, = _banded(qT, A_Q_BLK, nk, 1, v4, 1, bias_win, hkv=1, grp=NSA_HEADS, nprev=NSA_WINDOW // LANES,
                        mid_bias=False, dynamic=False, out_dtype=F32, name="nsa_win")
        aT = _sel_attend(qT, mm, nk, v4, causal4, ocmp, owin, gates)
        bs, lses = [], []
        for gi, (win, dil) in enumerate(DIL_PATTERNS):
            o, lse = _banded(qT, 3 + gi, nk, 2 + gi, v4, 1 + gi, bias_dil[gi], hkv=2, grp=1, nprev=dil,
                             mid_bias=dil > 1, dynamic=dil > NSUB, out_dtype=F32, want_lse=True,
                             name="dil%d" % dil)
            bs.append(o)
            lses.append(lse)
        cT, = _banded(qT, 0, nk, 5, v4, 4, bias_swa, hkv=SWA_KV_HEADS, grp=SWA_HEADS // SWA_KV_HEADS,
                      nprev=SWA_WINDOW // LANES, mid_bias=False, dynamic=False, out_dtype=BF16,
                      sinks=sink_tab[l], name="swa")
        xT = _outproj(aT, bs, lses, cT, w_outT, l, xT, gw1[l])
        xT = _ffn(xT, a2[l], sh2b[l], wgT, wuT, wdT, cv, l, gw2[l])
    return jnp.swapaxes(xT, 1, 2)
```

```python
import functools

import numpy as np
import jax
import jax.numpy as jnp
from jax import lax
from jax.experimental import pallas as pl
from jax.experimental.pallas import tpu as pltpu

F32 = jnp.float32
BF16 = jnp.bfloat16

HEAD_DIM = 64
HALF = HEAD_DIM // 2
NSA_HEADS = 4
CMP_BLOCK = 32
CMP_STRIDE = 16
SEL_BLOCK = 64
SEL_TOPK = 16
NSA_WINDOW = 512
DIL_PATTERNS = ((128, 1), (512, 4), (2048, 16))
DIL_HEADS = 6
SWA_HEADS = 6
SWA_KV_HEADS = 2
SWA_WINDOW = 128
ROPE_THETA = 10000.0
RMS_EPS = 1e-6
NEG = -1e30
FORCE = 1e4
CONV_WIDTH = 3

LANES = 128
TILE = 512
NSUB = TILE // LANES
VROWS = HEAD_DIM + 16
MEMBER_BIG = 2.0 ** 100
VMEM_LIMIT = 56 * 1024 * 1024

Q_ROWS = 1024
A_Q_BLK = 3
NK_ROWS = 768
NK_LANES = 768
V_ROWS = 640
G_ROWS = 16
W_ROWS = Q_ROWS + NK_ROWS + V_ROWS + G_ROWS
N_VPIECES = V_ROWS // HEAD_DIM


def _cp(sem):
    return pltpu.CompilerParams(dimension_semantics=sem, vmem_limit_bytes=VMEM_LIMIT)


def _const_spec(shape, index_map):
    return pl.BlockSpec(shape, index_map, pipeline_mode=pl.Buffered(1))


def _adaln_kernel(c_ref, w_ref, b_ref, o_ref):
    c = c_ref[...]
    cond = c * jax.nn.sigmoid(c)
    o_ref[...] = jnp.dot(cond, w_ref[...], preferred_element_type=F32,
                         precision=lax.Precision.HIGHEST) + b_ref[...]


def _adaln(c8, w_ada, b_ada):
    depth, d, six_d = w_ada.shape
    nblk = six_d // d
    return pl.pallas_call(
        _adaln_kernel,
        grid=(depth, nblk),
        in_specs=[pl.BlockSpec((8, d), lambda l, n: (0, 0)),
                  pl.BlockSpec((None, d, d), lambda l, n: (l, 0, n)),
                  pl.BlockSpec((None, 1, d), lambda l, n: (l, 0, n))],
        out_specs=pl.BlockSpec((None, 8, d), lambda l, n: (l, 0, n)),
        out_shape=jax.ShapeDtypeStruct((depth, 8, six_d), F32),
        compiler_params=_cp(("parallel", "parallel")),
        name="adaln",
    )(c8, w_ada, b_ada.reshape(depth, 1, six_d))


def _norm_mod_to_scratch(x_ref, a_ref, sh_ref, h_scr):
    for c in range(NSUB):
        sl = slice(c * LANES, (c + 1) * LANES)
        xs = x_ref[:, sl]
        ms = jnp.mean(xs * xs, axis=0, keepdims=True)
        h_scr[:, sl] = ((xs * lax.rsqrt(ms + RMS_EPS)) * a_ref[...] + sh_ref[...]).astype(BF16)


def _inproj_kernel(x_ref, a_ref, sh_ref, w_ref, cos_ref, sin_ref,
                   q_ref, kcvc_ref, nk_ref, v4_ref, gate_ref, h_scr):
    j = pl.program_id(1)
    _norm_mod_to_scratch(x_ref, a_ref, sh_ref, h_scr)
    h = h_scr[...]
    cos = cos_ref[...]
    sin = sin_ref[...]

    def proj(r0, r1):
        return jnp.dot(w_ref[r0:r1, :], h, preferred_element_type=F32)

    def rope(r, nh):
        outs = []
        for hh in range(nh):
            t1 = r[HEAD_DIM * hh:HEAD_DIM * hh + HALF]
            t2 = r[HEAD_DIM * hh + HALF:HEAD_DIM * (hh + 1)]
            outs.append(t1 * cos - t2 * sin)
            outs.append(t2 * cos + t1 * sin)
        return jnp.concatenate(outs, axis=0)

    for r0, r1 in ((0, 384), (384, 768), (768, 1024)):
        q_ref[r0:r1, :] = rope(proj(r0, r1), (r1 - r0) // HEAD_DIM).astype(BF16)

    base = Q_ROWS
    r = proj(base, base + 128)
    kcvc = jnp.concatenate([rope(r[0:64], 1), r[64:128]], axis=0)
    kcvc_ref[...] = kcvc.T.astype(BF16)

    r = proj(base + 128, base + 192)
    tok = j * TILE + lax.broadcasted_iota(jnp.int32, (HEAD_DIM, TILE), 1)
    row = lax.broadcasted_iota(jnp.int32, (HEAD_DIM, TILE), 0)
    member_cols = jnp.where(row == ((tok >> 6) & 15), MEMBER_BIG, 0.0).astype(F32)
    nk_ref[:, 0:128] = jnp.concatenate([rope(r, 1), member_cols], axis=0).T.astype(BF16)

    r = proj(base + 192, base + 256)
    nk_ref[:, 128:256] = jnp.concatenate([rope(r, 1), jnp.zeros((HEAD_DIM, TILE), F32)], axis=0).T.astype(BF16)

    r = proj(base + 256, base + 640)
    nk_ref[:, 256:640] = rope(r, 6).T.astype(BF16)

    r = proj(base + 640, base + 768)
    nk_ref[:, 640:768] = rope(r, 2).T.astype(BF16)

    base = Q_ROWS + NK_ROWS
    r = proj(base, base + V_ROWS).astype(BF16)
    ones = jnp.ones((VROWS - HEAD_DIM, LANES), BF16)
    for c in range(NSUB):
        for p in range(N_VPIECES):
            v4_ref[c, VROWS * p:VROWS * p + HEAD_DIM, :] = r[HEAD_DIM * p:HEAD_DIM * (p + 1),
                                                             c * LANES:(c + 1) * LANES]
            v4_ref[c, VROWS * p + HEAD_DIM:VROWS * (p + 1), :] = ones

    base = Q_ROWS + NK_ROWS + V_ROWS
    gate_ref[...] = jax.nn.sigmoid(proj(base, base + G_ROWS))


def _inproj(xT, a1, sh1, w_inT, layer, cosT, sinT):
    B, D, S = xT.shape
    nt = S // TILE
    return pl.pallas_call(
        _inproj_kernel,
        grid=(B, nt),
        in_specs=[pl.BlockSpec((None, D, TILE), lambda b, j: (b, 0, j)),
                  pl.BlockSpec((None, D, LANES), lambda b, j: (b, 0, 0)),
                  pl.BlockSpec((None, D, LANES), lambda b, j: (b, 0, 0)),
                  _const_spec((None, W_ROWS, D), lambda b, j: (layer, 0, 0)),
                  pl.BlockSpec((None, HALF, TILE), lambda b, j: (b, 0, j)),
                  pl.BlockSpec((None, HALF, TILE), lambda b, j: (b, 0, j))],
        out_specs=[pl.BlockSpec((None, Q_ROWS, TILE), lambda b, j: (b, 0, j)),
                   pl.BlockSpec((None, TILE, LANES), lambda b, j: (b, j, 0)),
                   pl.BlockSpec((None, TILE, NK_LANES), lambda b, j: (b, j, 0)),
                   pl.BlockSpec((None, NSUB, N_VPIECES * VROWS, LANES), lambda b, j: (b, j, 0, 0)),
                   pl.BlockSpec((None, G_ROWS, TILE), lambda b, j: (b, 0, j))],
        out_shape=[jax.ShapeDtypeStruct((B, Q_ROWS, S), BF16),
                   jax.ShapeDtypeStruct((B, S, LANES), BF16),
                   jax.ShapeDtypeStruct((B, S, NK_LANES), BF16),
                   jax.ShapeDtypeStruct((B, S // LANES, N_VPIECES * VROWS, LANES), BF16),
                   jax.ShapeDtypeStruct((B, G_ROWS, S), F32)],
        scratch_shapes=[pltpu.VMEM((D, TILE), BF16)],
        compiler_params=_cp(("parallel", "parallel")),
        name="inproj",
    )(xT, a1, sh1, w_inT, cosT, sinT)


def _compress_kernel(t_ref, w_ref, pe_ref, kc_ref, vc_ref):
    n = t_ref.shape[0]
    a = jnp.dot(t_ref[...], w_ref[...], preferred_element_type=F32)
    pc = jnp.dot(pe_ref[...], w_ref[...], preferred_element_type=F32)
    const = pc[0:1, 0:LANES] + pc[8:9, LANES:2 * LANES]
    cmp = a[:, 0:LANES] + pltpu.roll(a[:, LANES:2 * LANES], n - 1, 0) + const
    cmp_t = cmp.T
    ones = jnp.ones((VROWS - HEAD_DIM, LANES), BF16)
    for c in range(n // LANES):
        kc_ref[c] = cmp[c * LANES:(c + 1) * LANES].astype(BF16)
        vc_ref[c, 0:HEAD_DIM, :] = cmp_t[HEAD_DIM:2 * HEAD_DIM, c * LANES:(c + 1) * LANES].astype(BF16)
        vc_ref[c, HEAD_DIM:VROWS, :] = ones


def _compress(kcvc, wbig, pe2, layer):
    B, S, _ = kcvc.shape
    n = S // CMP_STRIDE
    nch = n // LANES
    tview = kcvc.reshape(B, n, CMP_STRIDE * LANES)
    return pl.pallas_call(
        _compress_kernel,
        grid=(B,),
        in_specs=[pl.BlockSpec((None, n, CMP_STRIDE * LANES), lambda b: (b, 0, 0)),
                  pl.BlockSpec((None, CMP_STRIDE * LANES, 2 * LANES), lambda b: (layer, 0, 0)),
                  pl.BlockSpec((None, 16, CMP_STRIDE * LANES), lambda b: (layer, 0, 0))],
        out_specs=[pl.BlockSpec((None, nch, LANES, LANES), lambda b: (b, 0, 0, 0)),
                   pl.BlockSpec((None, nch, VROWS, LANES), lambda b: (b, 0, 0, 0))],
        out_shape=[jax.ShapeDtypeStruct((B, nch, LANES, LANES), BF16),
                   jax.ShapeDtypeStruct((B, nch, VROWS, LANES), BF16)],
        compiler_params=_cp(("parallel",)),
        name="compress",
    )(tview, wbig, pe2)


def _stack_heads(q_ref, nh, lane_slice=slice(None)):
    return jnp.concatenate([q_ref[HEAD_DIM * h:HEAD_DIM * (h + 1), lane_slice] for h in range(nh)], axis=1)


CMP_UNITS = 2
CMP_TQ = CMP_UNITS * LANES


def _cmp_kernel(q_ref, kc_ref, vc_ref, ov_ref, bias_ref, o_ref, mm_ref, imp_scr, *, n_sel, nstep):
    i = pl.program_id(1)
    nch_all = kc_ref.shape[0]
    parts = 4 if nch_all % 4 == 0 else (2 if nch_all % 2 == 0 else 1)
    for v in range(parts):
        pl.when(i // (nstep // parts) == v)(functools.partial(
            _cmp_body, q_ref, kc_ref, vc_ref, ov_ref, bias_ref, o_ref, mm_ref, imp_scr,
            nch=nch_all * (v + 1) // parts, nrows=n_sel * (v + 1) // parts, n_sel=n_sel))


def _cmp_body(q_ref, kc_ref, vc_ref, ov_ref, bias_ref, o_ref, mm_ref, imp_scr, *, nch, nrows, n_sel):
    i = pl.program_id(1)
    cd = (i * CMP_UNITS) // 16

    def scores(u):
        q = _stack_heads(q_ref, NSA_HEADS, slice(u * LANES, (u + 1) * LANES))
        edge = bias_ref[u]
        ss = []
        for c in range(nch):
            b = jnp.where(c < cd, 0.0, jnp.where(c == cd, edge, NEG))
            s = jnp.dot(kc_ref[c, :, 0:HEAD_DIM], q, preferred_element_type=F32)
            ss.append(s + jnp.concatenate([b] * NSA_HEADS, axis=1))
        return ss

    def finish(u, ss):
        lanes = slice(u * LANES, (u + 1) * LANES)
        m = jnp.max(functools.reduce(jnp.maximum, ss), axis=0, keepdims=True)
        valid = m > 0.5 * NEG
        imp_scr[u, 0:32 * nch + 8, :] = jnp.zeros((32 * nch + 8, imp_scr.shape[2]), F32)
        acc = None
        for c in range(nch):
            p = jnp.exp(ss[c] - m).astype(BF16)
            t = jnp.dot(vc_ref[c], p, preferred_element_type=F32)
            acc = t if acc is None else acc + t
            imp_scr[u, 32 * c:32 * c + 40, :] += jnp.dot(ov_ref[...], p, preferred_element_type=F32)
        inv = jnp.where(valid, 1.0 / acc[HEAD_DIM:HEAD_DIM + 1], 0.0)
        o = acc[0:HEAD_DIM] * inv
        for h in range(NSA_HEADS):
            o_ref[HEAD_DIM * h:HEAD_DIM * (h + 1), lanes] = o[:, h * LANES:(h + 1) * LANES]
        imp = jnp.zeros((nrows, LANES), F32)
        for h in range(NSA_HEADS):
            sl = slice(h * LANES, (h + 1) * LANES)
            imp = imp + imp_scr[u, 0:nrows, sl] * inv[:, sl]
        return imp

    pending = scores(0)
    imps = []
    for u in range(1, CMP_UNITS):
        nxt = scores(u)
        imps.append(finish(u - 1, pending))
        pending = nxt
    imps.append(finish(CMP_UNITS - 1, pending))

    blk = lax.broadcasted_iota(jnp.int32, (nrows, LANES), 0).astype(F32)

    def pick(_, imp):
        mx = jnp.max(imp, axis=0, keepdims=True)
        first = jnp.min(jnp.where(imp == mx, blk, float(nrows)), axis=0, keepdims=True)
        return jnp.where(blk == first, -jnp.inf, imp)

    for u in range(CMP_UNITS):
        t = i * CMP_TQ + u * LANES + lax.broadcasted_iota(jnp.int32, (nrows, LANES), 1)
        cur = (t >> 6).astype(F32)
        forced = (blk == 0.0) | (blk == cur) | (blk == cur - 1.0)
        imp = jnp.where(forced, FORCE, imps[u])
        imp = jnp.where(blk <= cur, imp, NEG)
        imp = lax.fori_loop(0, min(SEL_TOPK, nrows), pick, imp)
        member = (imp == -jnp.inf) & (blk <= cur)
        mm_ref[0:nrows, u * LANES:(u + 1) * LANES] = jnp.where(member, 0.0, -1.0).astype(BF16)
    if nrows < n_sel:
        mm_ref[nrows:n_sel, :] = jnp.full((n_sel - nrows, CMP_TQ), -1.0, BF16)


def _cmp_topk(qT, kc4, vc4, ov, cmp_bias):
    B, _, S = qT.shape
    nstep = S // CMP_TQ
    nch = kc4.shape[1]
    n_sel = S // SEL_BLOCK
    nq = NSA_HEADS * LANES
    nbias = cmp_bias.shape[0] // CMP_UNITS
    return pl.pallas_call(
        functools.partial(_cmp_kernel, n_sel=n_sel, nstep=nstep),
        grid=(B, nstep),
        in_specs=[pl.BlockSpec((None, NSA_HEADS * HEAD_DIM, CMP_TQ), lambda b, i: (b, A_Q_BLK, i)),
                  pl.BlockSpec((None, nch, LANES, LANES), lambda b, i: (b, 0, 0, 0)),
                  pl.BlockSpec((None, nch, VROWS, LANES), lambda b, i: (b, 0, 0, 0)),
                  pl.BlockSpec((40, LANES), lambda b, i: (0, 0)),
                  pl.BlockSpec((None, CMP_UNITS, LANES, LANES), lambda b, i: (i % nbias, 0, 0, 0))],
        out_specs=[pl.BlockSpec((None, NSA_HEADS * HEAD_DIM, CMP_TQ), lambda b, i: (b, 0, i)),
                   pl.BlockSpec((None, n_sel, CMP_TQ), lambda b, i: (b, 0, i))],
        out_shape=[jax.ShapeDtypeStruct((B, NSA_HEADS * HEAD_DIM, S), F32),
                   jax.ShapeDtypeStruct((B, n_sel, S), BF16)],
        scratch_shapes=[pltpu.VMEM((CMP_UNITS, 32 * nch + 64, nq), F32)],
        compiler_params=_cp(("parallel", "parallel")),
        name="nsa_cmp_topk",
    )(qT, kc4, vc4, ov, cmp_bias.reshape(nbias, CMP_UNITS, LANES, LANES))


SEL_GROUP = 8


SEL_TQ = 2 * LANES


def _sel_kernel(q_ref, mm_ref, ks_ref, v_ref, causal_ref, ocmp_ref, owin_ref, gate_ref, out_ref,
                qa_scr, mm_scr, s_scr):
    i = pl.program_id(1)
    nq = NSA_HEADS * SEL_TQ
    qa_scr[0:HEAD_DIM, :] = _stack_heads(q_ref, NSA_HEADS)
    qa_scr[HEAD_DIM:LANES, :] = jnp.zeros((LANES - HEAD_DIM, nq), BF16)
    mm = mm_ref[...]
    for h in range(NSA_HEADS):
        mm_scr[:, h * SEL_TQ:(h + 1) * SEL_TQ] = mm

    gkeys = SEL_GROUP * LANES

    def load_query(gi):
        qa_scr[HEAD_DIM:HEAD_DIM + 16, :] = mm_scr[pl.ds(pl.multiple_of(gi * 16, 16), 16), :]
        return qa_scr[...]

    def chunk_scores(gi, u, qa):
        row = pl.multiple_of((gi * SEL_GROUP + u) * LANES, LANES)
        return jnp.dot(ks_ref[pl.ds(row, LANES), :], qa, preferred_element_type=F32)

    def weighted_values(gi, ps):
        vcat = jnp.concatenate([v_ref[gi * SEL_GROUP + u] for u in range(SEL_GROUP)], axis=1)
        return jnp.dot(vcat, jnp.concatenate(ps, axis=0), preferred_element_type=F32)

    def body(gi, carry):
        m, acc, mg = carry
        m_new = jnp.maximum(m, mg)
        qa = load_query(gi + 1)
        ps, mx = [], None
        for u in range(SEL_GROUP):
            rows = slice(u * LANES, (u + 1) * LANES)
            ps.append(jnp.exp(s_scr[rows, :] - m_new).astype(BF16))
            nxt = chunk_scores(gi + 1, u, qa)
            s_scr[rows, :] = nxt
            mx = nxt if mx is None else jnp.maximum(mx, nxt)
        acc = acc * jnp.exp(m - m_new) + weighted_values(gi, ps)
        return m_new, acc, jnp.max(mx, axis=0, keepdims=True)

    qa = load_query(0)
    mx = None
    for u in range(SEL_GROUP):
        s0 = chunk_scores(0, u, qa)
        s_scr[u * LANES:(u + 1) * LANES, :] = s0
        mx = s0 if mx is None else jnp.maximum(mx, s0)
    c0 = i * (SEL_TQ // LANES)
    last = c0 // SEL_GROUP
    m, acc, _ = lax.fori_loop(0, last, body, (jnp.full((1, nq), NEG, F32), jnp.zeros((VROWS, nq), F32),
                                              jnp.max(mx, axis=0, keepdims=True)))
    drow = pl.multiple_of((c0 % SEL_GROUP) * LANES, SEL_TQ)
    s_scr[pl.ds(drow, SEL_TQ), :] += causal_ref[...]
    s = s_scr[...]
    m_new = jnp.maximum(m, jnp.max(s, axis=0, keepdims=True))
    acc = acc * jnp.exp(m - m_new) + weighted_values(last, [jnp.exp(s - m_new).astype(BF16)])
    o = acc[0:HEAD_DIM] * (1.0 / acc[HEAD_DIM:HEAD_DIM + 1])

    g = gate_ref[...]
    for h in range(NSA_HEADS):
        rs = slice(HEAD_DIM * h, HEAD_DIM * (h + 1))
        out = (g[3 * h:3 * h + 1] * ocmp_ref[rs, :] + g[3 * h + 1:3 * h + 2] * o[:, h * SEL_TQ:(h + 1) * SEL_TQ]
               + g[3 * h + 2:3 * h + 3] * owin_ref[rs, :])
        out_ref[rs, :] = out.astype(BF16)


def _sel_attend(qT, mm, nk, v4, causal4, ocmp, owin, gates):
    B, _, S = qT.shape
    nstep = S // SEL_TQ
    n_sel = S // SEL_BLOCK
    nq = NSA_HEADS * SEL_TQ
    ar = NSA_HEADS * HEAD_DIM
    return pl.pallas_call(
        _sel_kernel,
        grid=(B, nstep),
        in_specs=[pl.BlockSpec((None, ar, SEL_TQ), lambda b, i: (b, A_Q_BLK, i)),
                  pl.BlockSpec((None, n_sel, SEL_TQ), lambda b, i: (b, 0, i)),
                  pl.BlockSpec((None, S, LANES), lambda b, i: (b, 0, 0)),
                  pl.BlockSpec((None, S // LANES, VROWS, LANES), lambda b, i: (b, 0, 0, 0)),
                  pl.BlockSpec((SEL_TQ, nq), lambda b, i: (0, 0)),
                  pl.BlockSpec((None, ar, SEL_TQ), lambda b, i: (b, 0, i)),
                  pl.BlockSpec((None, ar, SEL_TQ), lambda b, i: (b, 0, i)),
                  pl.BlockSpec((None, G_ROWS, SEL_TQ), lambda b, i: (b, 0, i))],
        out_specs=pl.BlockSpec((None, ar, SEL_TQ), lambda b, i: (b, 0, i)),
        out_shape=jax.ShapeDtypeStruct((B, ar, S), BF16),
        scratch_shapes=[pltpu.VMEM((LANES, nq), BF16), pltpu.VMEM((n_sel, nq), BF16),
                        pltpu.VMEM((SEL_GROUP * LANES, nq), F32)],
        compiler_params=_cp(("parallel", "parallel")),
        name="nsa_sel",
    )(qT, mm, nk, v4, causal4, ocmp, owin, gates)


def _banded_kernel(*refs, hkv, grp, nprev, mid_bias, dynamic, has_sink, want_lse):
    q_ref, k_ref, v_ref, bias_ref = refs[:4]
    pos = 4
    sink_ref = None
    if has_sink:
        sink_ref = refs[pos]
        pos += 1
    o_ref = refs[pos]
    lse_ref = refs[pos + 1] if want_lse else None
    j = pl.program_id(1)

    def scores(s, g, first):
        lanes = slice(s * LANES, (s + 1) * LANES)
        qg = jnp.concatenate([q_ref[HEAD_DIM * (g * grp + u):HEAD_DIM * (g * grp + u + 1), lanes]
                              for u in range(grp)], axis=1)
        parts = []
        for ci in range(nprev + 1):
            which = 0 if ci == 0 else (2 if ci == nprev else 1)
            if first:
                kc = s - nprev + ci
                if kc < 0:
                    continue
                kcc = kc
                row = kc * LANES
            else:
                kc = j * NSUB + s - nprev + ci
                kcc = jnp.maximum(kc, 0) if dynamic else kc
                row = pl.multiple_of(kcc * LANES, LANES)
            sc = jnp.dot(k_ref[pl.ds(row, LANES), HEAD_DIM * g:HEAD_DIM * (g + 1)], qg,
                         preferred_element_type=F32)
            if dynamic:
                sc = sc + bias_ref[jnp.where(kc >= 0, which, 3)]
            elif which != 1 or mid_bias:
                sc = sc + bias_ref[which]
            parts.append((kcc, sc))
        return parts

    def finish(s, g, parts):
        lanes = slice(s * LANES, (s + 1) * LANES)
        m = None
        for _, sc in parts:
            mc = jnp.max(sc, axis=0, keepdims=True)
            m = mc if m is None else jnp.maximum(m, mc)
        if has_sink:
            sk = sink_ref[g, 0:1, :]
            m = jnp.maximum(m, sk)
        acc = None
        for kcc, sc in parts:
            t = jnp.dot(v_ref[kcc, VROWS * g:VROWS * (g + 1), :], jnp.exp(sc - m).astype(BF16),
                        preferred_element_type=F32)
            acc = t if acc is None else acc + t
        l = acc[HEAD_DIM:HEAD_DIM + 1]
        if has_sink:
            l = l + jnp.exp(sk - m)
        o = acc[0:HEAD_DIM] * (1.0 / l)
        for u in range(grp):
            hq = g * grp + u
            o_ref[HEAD_DIM * hq:HEAD_DIM * (hq + 1), lanes] = o[:, u * LANES:(u + 1) * LANES].astype(o_ref.dtype)
        if want_lse:
            lse_ref[8 * g:8 * (g + 1), lanes] = jnp.broadcast_to(m + jnp.log(l), (8, LANES))

    def run(first):
        pending = None
        for s in range(NSUB):
            for g in range(hkv):
                parts = scores(s, g, first)
                if pending is not None:
                    finish(*pending)
                pending = (s, g, parts)
        finish(*pending)

    if dynamic:
        run(False)
    else:
        pl.when(j == 0)(lambda: run(True))
        pl.when(j > 0)(lambda: run(False))


def _banded(qT, q_blk, nk, k_blk, v4, v_blk, bias, *, hkv, grp, nprev, mid_bias, dynamic,
            out_dtype, sinks=None, want_lse=False, name):
    B, _, S = qT.shape
    nt = S // TILE
    qrows = hkv * grp * HEAD_DIM
    in_specs = [pl.BlockSpec((None, qrows, TILE), lambda b, j: (b, q_blk, j)),
                pl.BlockSpec((None, S, LANES), lambda b, j: (b, 0, k_blk)),
                pl.BlockSpec((None, S // LANES, hkv * VROWS, LANES), lambda b, j: (b, 0, v_blk, 0)),
                pl.BlockSpec(bias.shape, lambda b, j: (0, 0, 0))]
    args = [qT, nk, v4, bias]
    if sinks is not None:
        in_specs.append(pl.BlockSpec(sinks.shape, lambda b, j: (0, 0, 0)))
        args.append(sinks)
    out_specs = [pl.BlockSpec((None, qrows, TILE), lambda b, j: (b, 0, j))]
    out_shape = [jax.ShapeDtypeStruct((B, qrows, S), out_dtype)]
    if want_lse:
        out_specs.append(pl.BlockSpec((None, 8 * hkv, TILE), lambda b, j: (b, 0, j)))
        out_shape.append(jax.ShapeDtypeStruct((B, 8 * hkv, S), F32))
    return pl.pallas_call(
        functools.partial(_banded_kernel, hkv=hkv, grp=grp, nprev=nprev, mid_bias=mid_bias,
                          dynamic=dynamic, has_sink=sinks is not None, want_lse=want_lse),
        grid=(B, nt),
        in_specs=in_specs,
        out_specs=out_specs,
        out_shape=out_shape,
        compiler_params=_cp(("parallel", "parallel")),
        name=name,
    )(*args)


def _outproj_kernel(a_ref, b0_ref, b1_ref, b2_ref, l0_ref, l1_ref, l2_ref, c_ref, w_ref, x_ref, gw_ref,
                    o_ref, mix_scr):
    b_refs = (b0_ref, b1_ref, b2_ref)
    l_refs = (l0_ref, l1_ref, l2_ref)
    for h in range(2):
        ls = [r[8 * h:8 * h + 1, :] for r in l_refs]
        mx = jnp.maximum(jnp.maximum(ls[0], ls[1]), ls[2])
        es = [jnp.exp(v - mx) for v in ls]
        inv = 1.0 / (es[0] + es[1] + es[2])
        for g in range(3):
            rs = slice(HEAD_DIM * h, HEAD_DIM * (h + 1))
            mix_scr[128 * g + HEAD_DIM * h:128 * g + HEAD_DIM * (h + 1), :] = (
                b_refs[g][rs, :] * (es[g] * inv)).astype(BF16)
    y = jnp.dot(w_ref[:, 0:256], a_ref[...], preferred_element_type=F32)
    y = y + jnp.dot(w_ref[:, 256:640], mix_scr[...], preferred_element_type=F32)
    y = y + jnp.dot(w_ref[:, 640:1024], c_ref[...], preferred_element_type=F32)
    ms = jnp.mean(y * y, axis=0, keepdims=True)
    yn = y * lax.rsqrt(ms + RMS_EPS)
    gw = gw_ref[...]
    for c in range(NSUB):
        sl = slice(c * LANES, (c + 1) * LANES)
        o_ref[:, sl] = x_ref[:, sl] + gw * yn[:, sl]


def _mix_ffn_kernel(a_ref, b0_ref, b1_ref, b2_ref, l0_ref, l1_ref, l2_ref, c_ref, wo_ref, x_ref, gw1_ref,
                    a2_ref, sh2_ref, wg_ref, wu_ref, wd_ref, cv_ref, gw2_ref, o_ref,
                    mix_scr, x1_scr, h_scr, carry_scr, act_scr, *, fchunk):
    _outproj_kernel(a_ref, b0_ref, b1_ref, b2_ref, l0_ref, l1_ref, l2_ref, c_ref, wo_ref, x_ref, gw1_ref,
                    x1_scr, mix_scr)
    _ffn_kernel(x1_scr, a2_ref, sh2_ref, wg_ref, wu_ref, wd_ref, cv_ref, gw2_ref, o_ref,
                h_scr, carry_scr, act_scr, fchunk=fchunk)


def _mix_ffn(aT, bs, lses, cT, w_outT, xT, gw1, a2, sh2, wgT, wuT, wdT, cv, gw2, layer):
    B, D, S = xT.shape
    nt = S // TILE
    d_ff = wgT.shape[1]
    tile = lambda rows: pl.BlockSpec((None, rows, TILE), lambda b, j: (b, 0, j))
    vec = pl.BlockSpec((None, D, LANES), lambda b, j: (b, 0, 0))
    return pl.pallas_call(
        functools.partial(_mix_ffn_kernel, fchunk=256),
        grid=(B, nt),
        in_specs=[tile(256), tile(128), tile(128), tile(128), tile(16), tile(16), tile(16), tile(384),
                  _const_spec((None, D, D), lambda b, j: (layer, 0, 0)),
                  tile(D), vec, vec, vec,
                  _const_spec((None, d_ff, D), lambda b, j: (layer, 0, 0)),
                  _const_spec((None, d_ff, D), lambda b, j: (layer, 0, 0)),
                  _const_spec((None, D, d_ff), lambda b, j: (layer, 0, 0)),
                  _const_spec((None, 4, d_ff, LANES), lambda b, j: (layer, 0, 0, 0)),
                  vec],
        out_specs=tile(D),
        out_shape=jax.ShapeDtypeStruct((B, D, S), F32),
        scratch_shapes=[pltpu.VMEM((384, TILE), BF16), pltpu.VMEM((D, TILE), F32),
                        pltpu.VMEM((D, TILE), BF16), pltpu.VMEM((d_ff, LANES), F32),
                        pltpu.VMEM((d_ff, TILE), BF16)],
        compiler_params=_cp(("arbitrary", "arbitrary")),
        name="mix_ffn",
    )(aT, bs[0], bs[1], bs[2], lses[0], lses[1], lses[2], cT, w_outT, xT, gw1,
      a2, sh2, wgT, wuT, wdT, cv, gw2)


def _ffn_kernel(x_ref, a_ref, sh_ref, wg_ref, wu_ref, wd_ref, cv_ref, gw_ref, o_ref,
                h_scr, carry_scr, act_scr, *, fchunk):
    j = pl.program_id(1)
    d_ff = wg_ref.shape[0]

    @pl.when(j == 0)
    def _():
        carry_scr[...] = jnp.zeros(carry_scr.shape, F32)

    _norm_mod_to_scratch(x_ref, a_ref, sh_ref, h_scr)
    h = h_scr[...]
    lane = lax.broadcasted_iota(jnp.int32, (fchunk, LANES), 1)
    for c in range(d_ff // fchunk):
        rs = slice(c * fchunk, (c + 1) * fchunk)
        g = jnp.dot(wg_ref[rs, :], h, preferred_element_type=F32)
        prev = carry_scr[rs, :]
        carry_scr[rs, :] = g[:, TILE - LANES:TILE]
        g1 = pltpu.roll(g, 1, 1)
        g2 = pltpu.roll(g, 2, 1)
        fix1 = jnp.where(lane < 1, pltpu.roll(prev, 1, 1), g1[:, 0:LANES])
        fix2 = jnp.where(lane < 2, pltpu.roll(prev, 2, 1), g2[:, 0:LANES])
        g1 = jnp.concatenate([fix1, g1[:, LANES:]], axis=1)
        g2 = jnp.concatenate([fix2, g2[:, LANES:]], axis=1)
        w0 = jnp.concatenate([cv_ref[0, rs, :]] * NSUB, axis=1)
        w1 = jnp.concatenate([cv_ref[1, rs, :]] * NSUB, axis=1)
        w2 = jnp.concatenate([cv_ref[2, rs, :]] * NSUB, axis=1)
        cb = jnp.concatenate([cv_ref[3, rs, :]] * NSUB, axis=1)
        acc = g2 * w0 + g1 * w1 + g * w2 + cb
        up = jnp.dot(wu_ref[rs, :], h, preferred_element_type=F32)
        act_scr[rs, :] = (jax.nn.gelu(acc, approximate=True) * up).astype(BF16)
    y = jnp.dot(wd_ref[...], act_scr[...], preferred_element_type=F32)
    ms = jnp.mean(y * y, axis=0, keepdims=True)
    yn = y * lax.rsqrt(ms + RMS_EPS)
    gw = gw_ref[...]
    for c in range(NSUB):
        sl = slice(c * LANES, (c + 1) * LANES)
        o_ref[:, sl] = x_ref[:, sl] + gw * yn[:, sl]


def _band_bias(dilation, old_edge, reps):
    kk = np.arange(LANES)[:, None]
    qq = np.arange(LANES)[None, :]
    res = ((qq - kk) % dilation) == 0
    tabs = [res & (kk - qq >= old_edge), res, res & (kk <= qq), np.zeros_like(res)]
    out = np.stack([np.where(t, 0.0, NEG) for t in tabs]).astype(np.float32)
    return jnp.asarray(np.tile(out, (1, 1, reps)))


def _cmp_bias():
    nn = np.arange(LANES)[:, None]
    qq = np.arange(LANES)[None, :]
    tabs = [np.where(CMP_STRIDE * nn + CMP_BLOCK - 1 <= LANES * r + qq, 0.0, NEG) for r in range(16)]
    return jnp.asarray(np.stack(tabs).astype(np.float32))


def _overlap_rows():
    jj = np.arange(40)[:, None]
    nn = np.arange(LANES)[None, :]
    return jnp.asarray(((nn >= 4 * jj - 1) & (nn <= 4 * jj + 3)).astype(np.float32), dtype=BF16)


_IN_COL_ORDER = ((1804, 2188), (652, 1036), (0, 256),
                 (256, 320), (320, 384), (384, 448), (512, 576), (1036, 1420), (2188, 2316),
                 (448, 512), (576, 640), (1420, 1804), (2316, 2444),
                 (640, 652))


def _prep_w_in(w_in):
    wt = jnp.swapaxes(jnp.concatenate([w_in[:, :, a:b] for a, b in _IN_COL_ORDER], axis=2), 1, 2)
    scale = np.ones((wt.shape[1], 1), np.float32)
    scale[:Q_ROWS] = HEAD_DIM ** -0.5
    wt = wt * scale
    wt = jnp.pad(wt, ((0, 0), (0, W_ROWS - wt.shape[1]), (0, 0)))
    return wt.astype(BF16)


def _prep_compress(w_ck, w_cv, pe_k, pe_v):
    L = w_ck.shape[0]
    half = CMP_BLOCK // 2

    def big(lo):
        wk = w_ck[:, lo:lo + half]
        wv = w_cv[:, lo:lo + half]
        z = jnp.zeros_like(wk)
        top = jnp.concatenate([wk, z], axis=-1)
        bot = jnp.concatenate([z, wv], axis=-1)
        return jnp.concatenate([top, bot], axis=2).reshape(L, half * LANES, LANES)

    wbig = jnp.concatenate([big(0), big(half)], axis=-1).astype(BF16)
    pe = jnp.concatenate([pe_k, pe_v], axis=-1)
    pe2 = jnp.zeros((L, 16, half * LANES), F32)
    pe2 = pe2.at[:, 0].set(pe[:, :half].reshape(L, -1)).at[:, 8].set(pe[:, half:].reshape(L, -1))
    return wbig, pe2.astype(BF16)


def _lane_bcast(v):
    return jnp.broadcast_to(v[..., None], v.shape + (LANES,))


def kernel(x, c, positions, w_in, w_out, w_ada, b_ada, norm_w, cmp_w_k, cmp_w_v, cmp_pe_k, cmp_pe_v,
           sinks, w_gate, w_up, conv_w, conv_b, w_down):
    B, S, D = x.shape
    depth = w_in.shape[0]
    assert S % 2048 == 0 and D == 1024 and w_in.shape[2] == 2444

    inv = ROPE_THETA ** (-jnp.arange(0, HEAD_DIM, 2, dtype=F32) / HEAD_DIM)
    ang = positions.astype(F32)[:, None, :] * inv[None, :, None]
    cosT, sinT = jnp.cos(ang), jnp.sin(ang)

    c8 = jnp.pad(c, ((0, 8 - B), (0, 0)))
    ada = _adaln(c8, w_ada, b_ada)[:, :B]
    sh1, sc1, g1, sh2, sc2, g2 = [ada[:, :, k * D:(k + 1) * D] for k in range(6)]
    nw = norm_w[:, :, None, :]
    a1 = _lane_bcast(nw[:, 0] * (1 + sc1))
    gw1 = _lane_bcast(g1 * nw[:, 1])
    a2 = _lane_bcast(nw[:, 2] * (1 + sc2))
    gw2 = _lane_bcast(g2 * nw[:, 3])
    sh1b, sh2b = _lane_bcast(sh1), _lane_bcast(sh2)

    w_inT = _prep_w_in(w_in)
    w_outT = jnp.swapaxes(w_out, 1, 2).astype(BF16)
    wgT = jnp.swapaxes(w_gate, 1, 2).astype(BF16)
    wuT = jnp.swapaxes(w_up, 1, 2).astype(BF16)
    wdT = jnp.swapaxes(w_down, 1, 2).astype(BF16)
    cv = _lane_bcast(jnp.concatenate([conv_w, conv_b[:, None, :]], axis=1))
    wbig, pe2 = _prep_compress(cmp_w_k, cmp_w_v, cmp_pe_k, cmp_pe_v)
    sink_tab = jnp.broadcast_to(
        jnp.repeat(sinks.reshape(depth, SWA_KV_HEADS, SWA_HEADS // SWA_KV_HEADS), LANES, axis=-1)[:, :, None, :],
        (depth, SWA_KV_HEADS, 8, LANES * (SWA_HEADS // SWA_KV_HEADS)))

    ov = _overlap_rows()
    cmp_bias = _cmp_bias()
    kk = np.arange(SEL_TQ)[:, None]
    causal4 = jnp.asarray(np.tile(np.where(kk <= kk.T, 0.0, NEG).astype(np.float32), (1, NSA_HEADS)))
    bias_win = _band_bias(1, 1, NSA_HEADS)
    bias_swa = _band_bias(1, 1, SWA_HEADS // SWA_KV_HEADS)
    bias_dil = [_band_bias(d, 0, 1) for _, d in DIL_PATTERNS]

    xT = jnp.swapaxes(x, 1, 2)
    for l in range(depth):
        qT, kcvc, nk, v4, gates = _inproj(xT, a1[l], sh1b[l], w_inT, l, cosT, sinT)
        kc4, vc4 = _compress(kcvc, wbig, pe2, l)
        ocmp, mm = _cmp_topk(qT, kc4, vc4, ov, cmp_bias)
        owin, = _banded(qT, A_Q_BLK, nk, 1, v4, 1, bias_win, hkv=1, grp=NSA_HEADS, nprev=NSA_WINDOW // LANES,
                        mid_bias=False, dynamic=False, out_dtype=F32, name="nsa_win")
        aT = _sel_attend(qT, mm, nk, v4, causal4, ocmp, owin, gates)
        bs, lses = [], []
        for gi, (win, dil) in enumerate(DIL_PATTERNS):
            o, lse = _banded(qT, 3 + gi, nk, 2 + gi, v4, 1 + gi, bias_dil[gi], hkv=2, grp=1, nprev=dil,
                             mid_bias=dil > 1, dynamic=dil > NSUB, out_dtype=F32, want_lse=True,
                             name="dil%d" % dil)
            bs.append(o)
            lses.append(lse)
        cT, = _banded(qT, 0, nk, 5, v4, 4, bias_swa, hkv=SWA_KV_HEADS, grp=SWA_HEADS // SWA_KV_HEADS,
                      nprev=SWA_WINDOW // LANES, mid_bias=False, dynamic=False, out_dtype=BF16,
                      sinks=sink_tab[l], name="swa")
        xT = _mix_ffn(aT, bs, lses, cT, w_outT, xT, gw1[l], a2[l], sh2b[l], wgT, wuT, wdT, cv, gw2[l], l)
    return jnp.swapaxes(xT, 1, 2)
```

```python
import functools

import numpy as np
import jax
import jax.numpy as jnp
from jax import lax
from jax.experimental import pallas as pl
from jax.experimental.pallas import tpu as pltpu

F32 = jnp.float32
BF16 = jnp.bfloat16

HEAD_DIM = 64
HALF = HEAD_DIM // 2
NSA_HEADS = 4
CMP_BLOCK = 32
CMP_STRIDE = 16
SEL_BLOCK = 64
SEL_TOPK = 16
NSA_WINDOW = 512
DIL_PATTERNS = ((128, 1), (512, 4), (2048, 16))
DIL_HEADS = 6
SWA_HEADS = 6
SWA_KV_HEADS = 2
SWA_WINDOW = 128
ROPE_THETA = 10000.0
RMS_EPS = 1e-6
NEG = -1e30
FORCE = 1e4
CONV_WIDTH = 3

LANES = 128
TILE = 512
NSUB = TILE // LANES
VROWS = HEAD_DIM + 16
MEMBER_BIG = 2.0 ** 100
VMEM_LIMIT = 56 * 1024 * 1024

Q_ROWS = 1024
A_Q_BLK = 3
NK_ROWS = 768
NK_LANES = 768
V_ROWS = 640
G_ROWS = 16
W_ROWS = Q_ROWS + NK_ROWS + V_ROWS + G_ROWS
N_VPIECES = V_ROWS // HEAD_DIM


def _cp(sem):
    return pltpu.CompilerParams(dimension_semantics=sem, vmem_limit_bytes=VMEM_LIMIT)


def _const_spec(shape, index_map):
    return pl.BlockSpec(shape, index_map, pipeline_mode=pl.Buffered(1))


def _adaln_kernel(c_ref, w_ref, b_ref, o_ref):
    c = c_ref[...]
    cond = c * jax.nn.sigmoid(c)
    o_ref[...] = jnp.dot(cond, w_ref[...], preferred_element_type=F32,
                         precision=lax.Precision.HIGHEST) + b_ref[...]


def _adaln(c8, w_ada, b_ada):
    depth, d, six_d = w_ada.shape
    nblk = six_d // d
    return pl.pallas_call(
        _adaln_kernel,
        grid=(depth, nblk),
        in_specs=[pl.BlockSpec((8, d), lambda l, n: (0, 0)),
                  pl.BlockSpec((None, d, d), lambda l, n: (l, 0, n)),
                  pl.BlockSpec((None, 1, d), lambda l, n: (l, 0, n))],
        out_specs=pl.BlockSpec((None, 8, d), lambda l, n: (l, 0, n)),
        out_shape=jax.ShapeDtypeStruct((depth, 8, six_d), F32),
        compiler_params=_cp(("parallel", "parallel")),
        name="adaln",
    )(c8, w_ada, b_ada.reshape(depth, 1, six_d))


def _norm_mod_to_scratch(x_ref, a_ref, sh_ref, h_scr):
    for c in range(NSUB):
        sl = slice(c * LANES, (c + 1) * LANES)
        xs = x_ref[:, sl]
        ms = jnp.mean(xs * xs, axis=0, keepdims=True)
        h_scr[:, sl] = ((xs * lax.rsqrt(ms + RMS_EPS)) * a_ref[...] + sh_ref[...]).astype(BF16)


def _inproj_kernel(x_ref, a_ref, sh_ref, w_ref, cos_ref, sin_ref,
                   q_ref, kcvc_ref, nk_ref, v4_ref, gate_ref, h_scr):
    j = pl.program_id(1)
    _norm_mod_to_scratch(x_ref, a_ref, sh_ref, h_scr)
    h = h_scr[...]
    cos = cos_ref[...]
    sin = sin_ref[...]

    def proj(r0, r1):
        return jnp.dot(w_ref[r0:r1, :], h, preferred_element_type=F32)

    def rope(r, nh):
        outs = []
        for hh in range(nh):
            t1 = r[HEAD_DIM * hh:HEAD_DIM * hh + HALF]
            t2 = r[HEAD_DIM * hh + HALF:HEAD_DIM * (hh + 1)]
            outs.append(t1 * cos - t2 * sin)
            outs.append(t2 * cos + t1 * sin)
        return jnp.concatenate(outs, axis=0)

    for r0, r1 in ((0, 384), (384, 768), (768, 1024)):
        q_ref[r0:r1, :] = rope(proj(r0, r1), (r1 - r0) // HEAD_DIM).astype(BF16)

    base = Q_ROWS
    r = proj(base, base + 128)
    kcvc = jnp.concatenate([rope(r[0:64], 1), r[64:128]], axis=0)
    kcvc_ref[...] = kcvc.T.astype(BF16)

    r = proj(base + 128, base + 192)
    tok = j * TILE + lax.broadcasted_iota(jnp.int32, (HEAD_DIM, TILE), 1)
    row = lax.broadcasted_iota(jnp.int32, (HEAD_DIM, TILE), 0)
    member_cols = jnp.where(row == ((tok >> 6) & 15), MEMBER_BIG, 0.0).astype(F32)
    nk_ref[:, 0:128] = jnp.concatenate([rope(r, 1), member_cols], axis=0).T.astype(BF16)

    r = proj(base + 192, base + 256)
    nk_ref[:, 128:256] = jnp.concatenate([rope(r, 1), jnp.zeros((HEAD_DIM, TILE), F32)], axis=0).T.astype(BF16)

    r = proj(base + 256, base + 640)
    nk_ref[:, 256:640] = rope(r, 6).T.astype(BF16)

    r = proj(base + 640, base + 768)
    nk_ref[:, 640:768] = rope(r, 2).T.astype(BF16)

    base = Q_ROWS + NK_ROWS
    r = proj(base, base + V_ROWS).astype(BF16)
    ones = jnp.ones((VROWS - HEAD_DIM, LANES), BF16)
    for c in range(NSUB):
        for p in range(N_VPIECES):
            v4_ref[c, VROWS * p:VROWS * p + HEAD_DIM, :] = r[HEAD_DIM * p:HEAD_DIM * (p + 1),
                                                             c * LANES:(c + 1) * LANES]
            v4_ref[c, VROWS * p + HEAD_DIM:VROWS * (p + 1), :] = ones

    base = Q_ROWS + NK_ROWS + V_ROWS
    gate_ref[...] = jax.nn.sigmoid(proj(base, base + G_ROWS))


def _inproj(xT, a1, sh1, w_inT, layer, cosT, sinT):
    B, D, S = xT.shape
    nt = S // TILE
    return pl.pallas_call(
        _inproj_kernel,
        grid=(B, nt),
        in_specs=[pl.BlockSpec((None, D, TILE), lambda b, j: (b, 0, j)),
                  pl.BlockSpec((None, D, LANES), lambda b, j: (b, 0, 0)),
                  pl.BlockSpec((None, D, LANES), lambda b, j: (b, 0, 0)),
                  _const_spec((None, W_ROWS, D), lambda b, j: (layer, 0, 0)),
                  pl.BlockSpec((None, HALF, TILE), lambda b, j: (b, 0, j)),
                  pl.BlockSpec((None, HALF, TILE), lambda b, j: (b, 0, j))],
        out_specs=[pl.BlockSpec((None, Q_ROWS, TILE), lambda b, j: (b, 0, j)),
                   pl.BlockSpec((None, TILE, LANES), lambda b, j: (b, j, 0)),
                   pl.BlockSpec((None, TILE, NK_LANES), lambda b, j: (b, j, 0)),
                   pl.BlockSpec((None, NSUB, N_VPIECES * VROWS, LANES), lambda b, j: (b, j, 0, 0)),
                   pl.BlockSpec((None, G_ROWS, TILE), lambda b, j: (b, 0, j))],
        out_shape=[jax.ShapeDtypeStruct((B, Q_ROWS, S), BF16),
                   jax.ShapeDtypeStruct((B, S, LANES), BF16),
                   jax.ShapeDtypeStruct((B, S, NK_LANES), BF16),
                   jax.ShapeDtypeStruct((B, S // LANES, N_VPIECES * VROWS, LANES), BF16),
                   jax.ShapeDtypeStruct((B, G_ROWS, S), F32)],
        scratch_shapes=[pltpu.VMEM((D, TILE), BF16)],
        compiler_params=_cp(("parallel", "parallel")),
        name="inproj",
    )(xT, a1, sh1, w_inT, cosT, sinT)


def _compress_kernel(t_ref, w_ref, pe_ref, kc_ref, vc_ref):
    n = t_ref.shape[0]
    a = jnp.dot(t_ref[...], w_ref[...], preferred_element_type=F32)
    pc = jnp.dot(pe_ref[...], w_ref[...], preferred_element_type=F32)
    const = pc[0:1, 0:LANES] + pc[8:9, LANES:2 * LANES]
    cmp = a[:, 0:LANES] + pltpu.roll(a[:, LANES:2 * LANES], n - 1, 0) + const
    cmp_t = cmp.T
    ones = jnp.ones((VROWS - HEAD_DIM, LANES), BF16)
    for c in range(n // LANES):
        kc_ref[c] = cmp[c * LANES:(c + 1) * LANES].astype(BF16)
        vc_ref[c, 0:HEAD_DIM, :] = cmp_t[HEAD_DIM:2 * HEAD_DIM, c * LANES:(c + 1) * LANES].astype(BF16)
        vc_ref[c, HEAD_DIM:VROWS, :] = ones


def _compress(kcvc, wbig, pe2, layer):
    B, S, _ = kcvc.shape
    n = S // CMP_STRIDE
    nch = n // LANES
    tview = kcvc.reshape(B, n, CMP_STRIDE * LANES)
    return pl.pallas_call(
        _compress_kernel,
        grid=(B,),
        in_specs=[pl.BlockSpec((None, n, CMP_STRIDE * LANES), lambda b: (b, 0, 0)),
                  pl.BlockSpec((None, CMP_STRIDE * LANES, 2 * LANES), lambda b: (layer, 0, 0)),
                  pl.BlockSpec((None, 16, CMP_STRIDE * LANES), lambda b: (layer, 0, 0))],
        out_specs=[pl.BlockSpec((None, nch, LANES, LANES), lambda b: (b, 0, 0, 0)),
                   pl.BlockSpec((None, nch, VROWS, LANES), lambda b: (b, 0, 0, 0))],
        out_shape=[jax.ShapeDtypeStruct((B, nch, LANES, LANES), BF16),
                   jax.ShapeDtypeStruct((B, nch, VROWS, LANES), BF16)],
        compiler_params=_cp(("parallel",)),
        name="compress",
    )(tview, wbig, pe2)


def _stack_heads(q_ref, nh, lane_slice=slice(None)):
    return jnp.concatenate([q_ref[HEAD_DIM * h:HEAD_DIM * (h + 1), lane_slice] for h in range(nh)], axis=1)


CMP_UNITS = 2
CMP_TQ = CMP_UNITS * LANES


def _cmp_kernel(q_ref, kc_ref, vc_ref, ov_ref, bias_ref, o_ref, mm_ref, imp_scr, *, n_sel, nstep):
    i = pl.program_id(1)
    nch_all = kc_ref.shape[0]
    parts = 4 if nch_all % 4 == 0 else (2 if nch_all % 2 == 0 else 1)
    for v in range(parts):
        pl.when(i // (nstep // parts) == v)(functools.partial(
            _cmp_body, q_ref, kc_ref, vc_ref, ov_ref, bias_ref, o_ref, mm_ref, imp_scr,
            nch=nch_all * (v + 1) // parts, nrows=n_sel * (v + 1) // parts, n_sel=n_sel))


def _cmp_body(q_ref, kc_ref, vc_ref, ov_ref, bias_ref, o_ref, mm_ref, imp_scr, *, nch, nrows, n_sel):
    i = pl.program_id(1)
    cd = (i * CMP_UNITS) // 16

    def scores(u):
        q = _stack_heads(q_ref, NSA_HEADS, slice(u * LANES, (u + 1) * LANES))
        edge = bias_ref[u]
        ss = []
        for c in range(nch):
            b = jnp.where(c < cd, 0.0, jnp.where(c == cd, edge, NEG))
            s = jnp.dot(kc_ref[c, :, 0:HEAD_DIM], q, preferred_element_type=F32)
            ss.append(s + jnp.concatenate([b] * NSA_HEADS, axis=1))
        return ss

    def finish(u, ss):
        lanes = slice(u * LANES, (u + 1) * LANES)
        m = jnp.max(functools.reduce(jnp.maximum, ss), axis=0, keepdims=True)
        valid = m > 0.5 * NEG
        imp_scr[u, 0:32 * nch + 8, :] = jnp.zeros((32 * nch + 8, imp_scr.shape[2]), F32)
        acc = None
        for c in range(nch):
            p = jnp.exp(ss[c] - m).astype(BF16)
            t = jnp.dot(vc_ref[c], p, preferred_element_type=F32)
            acc = t if acc is None else acc + t
            imp_scr[u, 32 * c:32 * c + 40, :] += jnp.dot(ov_ref[...], p, preferred_element_type=F32)
        inv = jnp.where(valid, 1.0 / acc[HEAD_DIM:HEAD_DIM + 1], 0.0)
        o = acc[0:HEAD_DIM] * inv
        for h in range(NSA_HEADS):
            o_ref[HEAD_DIM * h:HEAD_DIM * (h + 1), lanes] = o[:, h * LANES:(h + 1) * LANES]
        imp = jnp.zeros((nrows, LANES), F32)
        for h in range(NSA_HEADS):
            sl = slice(h * LANES, (h + 1) * LANES)
            imp = imp + imp_scr[u, 0:nrows, sl] * inv[:, sl]
        return imp

    pending = scores(0)
    imps = []
    for u in range(1, CMP_UNITS):
        nxt = scores(u)
        imps.append(finish(u - 1, pending))
        pending = nxt
    imps.append(finish(CMP_UNITS - 1, pending))

    blk = lax.broadcasted_iota(jnp.int32, (nrows, LANES), 0).astype(F32)

    def pick(_, imp):
        mx = jnp.max(imp, axis=0, keepdims=True)
        first = jnp.min(jnp.where(imp == mx, blk, float(nrows)), axis=0, keepdims=True)
        return jnp.where(blk == first, -jnp.inf, imp)

    for u in range(CMP_UNITS):
        t = i * CMP_TQ + u * LANES + lax.broadcasted_iota(jnp.int32, (nrows, LANES), 1)
        cur = (t >> 6).astype(F32)
        forced = (blk == 0.0) | (blk == cur) | (blk == cur - 1.0)
        imp = jnp.where(forced, FORCE, imps[u])
        imp = jnp.where(blk <= cur, imp, NEG)
        imp = lax.fori_loop(0, min(SEL_TOPK, nrows), pick, imp)
        member = (imp == -jnp.inf) & (blk <= cur)
        mm_ref[0:nrows, u * LANES:(u + 1) * LANES] = jnp.where(member, 0.0, -1.0).astype(BF16)
    if nrows < n_sel:
        mm_ref[nrows:n_sel, :] = jnp.full((n_sel - nrows, CMP_TQ), -1.0, BF16)


def _cmp_topk(qT, kc4, vc4, ov, cmp_bias):
    B, _, S = qT.shape
    nstep = S // CMP_TQ
    nch = kc4.shape[1]
    n_sel = S // SEL_BLOCK
    nq = NSA_HEADS * LANES
    nbias = cmp_bias.shape[0] // CMP_UNITS
    return pl.pallas_call(
        functools.partial(_cmp_kernel, n_sel=n_sel, nstep=nstep),
        grid=(B, nstep),
        in_specs=[pl.BlockSpec((None, NSA_HEADS * HEAD_DIM, CMP_TQ), lambda b, i: (b, A_Q_BLK, i)),
                  pl.BlockSpec((None, nch, LANES, LANES), lambda b, i: (b, 0, 0, 0)),
                  pl.BlockSpec((None, nch, VROWS, LANES), lambda b, i: (b, 0, 0, 0)),
                  pl.BlockSpec((40, LANES), lambda b, i: (0, 0)),
                  pl.BlockSpec((None, CMP_UNITS, LANES, LANES), lambda b, i: (i % nbias, 0, 0, 0))],
        out_specs=[pl.BlockSpec((None, NSA_HEADS * HEAD_DIM, CMP_TQ), lambda b, i: (b, 0, i)),
                   pl.BlockSpec((None, n_sel, CMP_TQ), lambda b, i: (b, 0, i))],
        out_shape=[jax.ShapeDtypeStruct((B, NSA_HEADS * HEAD_DIM, S), F32),
                   jax.ShapeDtypeStruct((B, n_sel, S), BF16)],
        scratch_shapes=[pltpu.VMEM((CMP_UNITS, 32 * nch + 64, nq), F32)],
        compiler_params=_cp(("parallel", "parallel")),
        name="nsa_cmp_topk",
    )(qT, kc4, vc4, ov, cmp_bias.reshape(nbias, CMP_UNITS, LANES, LANES))


SEL_GROUP = 8
SEL_TQ = 2 * LANES


def _sel_kernel(q_ref, mm_ref, ks_ref, v_ref, causal_ref, ocmp_ref, owin_ref, gate_ref, out_ref,
                qa_scr, mm_scr, s_scr):
    i = pl.program_id(1)
    nq = NSA_HEADS * SEL_TQ
    qa_scr[0:HEAD_DIM, :] = _stack_heads(q_ref, NSA_HEADS)
    qa_scr[HEAD_DIM:LANES, :] = jnp.zeros((LANES - HEAD_DIM, nq), BF16)
    mm = mm_ref[...]
    for h in range(NSA_HEADS):
        mm_scr[:, h * SEL_TQ:(h + 1) * SEL_TQ] = mm

    gkeys = SEL_GROUP * LANES

    def load_query(gi):
        rg = (gi * SEL_GROUP) // 8
        qa_scr[HEAD_DIM:HEAD_DIM + 16, :] = mm_scr[pl.ds(pl.multiple_of(rg * 16, 16), 16), :]
        return qa_scr[...]

    def chunk_scores(gi, u, qa):
        row = pl.multiple_of((gi * SEL_GROUP + u) * LANES, LANES)
        return jnp.dot(ks_ref[pl.ds(row, LANES), :], qa, preferred_element_type=F32)

    def weighted_values(gi, ps):
        vcat = jnp.concatenate([v_ref[gi * SEL_GROUP + u] for u in range(SEL_GROUP)], axis=1)
        return jnp.dot(vcat, jnp.concatenate(ps, axis=0), preferred_element_type=F32)

    def body(gi, carry):
        m, acc, mg = carry
        m_new = jnp.maximum(m, mg)
        qa = load_query(gi + 1)
        ps, mx = [], None
        for u in range(SEL_GROUP):
            rows = slice(u * LANES, (u + 1) * LANES)
            ps.append(jnp.exp(s_scr[rows, :] - m_new).astype(BF16))
            nxt = chunk_scores(gi + 1, u, qa)
            s_scr[rows, :] = nxt
            mx = nxt if mx is None else jnp.maximum(mx, nxt)
        acc = acc * jnp.exp(m - m_new) + weighted_values(gi, ps)
        return m_new, acc, jnp.max(mx, axis=0, keepdims=True)

    qa = load_query(0)
    mx = None
    for u in range(SEL_GROUP):
        s0 = chunk_scores(0, u, qa)
        s_scr[u * LANES:(u + 1) * LANES, :] = s0
        mx = s0 if mx is None else jnp.maximum(mx, s0)
    c0 = i * (SEL_TQ // LANES)
    last = c0 // SEL_GROUP
    m, acc, _ = lax.fori_loop(0, last, body, (jnp.full((1, nq), NEG, F32), jnp.zeros((VROWS, nq), F32),
                                              jnp.max(mx, axis=0, keepdims=True)))
    drow = pl.multiple_of((c0 % SEL_GROUP) * LANES, SEL_TQ)
    s_scr[pl.ds(drow, SEL_TQ), :] += causal_ref[...]
    s = s_scr[...]
    m_new = jnp.maximum(m, jnp.max(s, axis=0, keepdims=True))
    acc = acc * jnp.exp(m - m_new) + weighted_values(last, [jnp.exp(s - m_new).astype(BF16)])
    o = acc[0:HEAD_DIM] * (1.0 / acc[HEAD_DIM:HEAD_DIM + 1])

    g = gate_ref[...]
    for h in range(NSA_HEADS):
        rs = slice(HEAD_DIM * h, HEAD_DIM * (h + 1))
        out = (g[3 * h:3 * h + 1] * ocmp_ref[rs, :] + g[3 * h + 1:3 * h + 2] * o[:, h * SEL_TQ:(h + 1) * SEL_TQ]
               + g[3 * h + 2:3 * h + 3] * owin_ref[rs, :])
        out_ref[rs, :] = out.astype(BF16)


def _sel_attend(qT, mm, nk, v4, causal4, ocmp, owin, gates):
    B, _, S = qT.shape
    nstep = S // SEL_TQ
    n_sel = S // SEL_BLOCK
    nq = NSA_HEADS * SEL_TQ
    ar = NSA_HEADS * HEAD_DIM
    return pl.pallas_call(
        _sel_kernel,
        grid=(B, nstep),
        in_specs=[pl.BlockSpec((None, ar, SEL_TQ), lambda b, i: (b, A_Q_BLK, i)),
                  pl.BlockSpec((None, n_sel, SEL_TQ), lambda b, i: (b, 0, i)),
                  pl.BlockSpec((None, S, LANES), lambda b, i: (b, 0, 0)),
                  pl.BlockSpec((None, S // LANES, VROWS, LANES), lambda b, i: (b, 0, 0, 0)),
                  _const_spec((SEL_TQ, nq), lambda b, i: (0, 0)),
                  pl.BlockSpec((None, ar, SEL_TQ), lambda b, i: (b, 0, i)),
                  pl.BlockSpec((None, ar, SEL_TQ), lambda b, i: (b, 0, i)),
                  pl.BlockSpec((None, G_ROWS, SEL_TQ), lambda b, i: (b, 0, i))],
        out_specs=pl.BlockSpec((None, ar, SEL_TQ), lambda b, i: (b, 0, i)),
        out_shape=jax.ShapeDtypeStruct((B, ar, S), BF16),
        scratch_shapes=[pltpu.VMEM((LANES, nq), BF16), pltpu.VMEM((n_sel, nq), BF16),
                        pltpu.VMEM((SEL_GROUP * LANES, nq), F32)],
        compiler_params=_cp(("parallel", "parallel")),
        name="nsa_sel",
    )(qT, mm, nk, v4, causal4, ocmp, owin, gates)


def _banded_kernel(*refs, hkv, grp, nprev, span, zero_bias, dynamic, has_sink, want_lse):
    q_ref, k_ref, v_ref, bias_ref = refs[:4]
    pos = 4
    sink_ref = None
    if has_sink:
        sink_ref = refs[pos]
        pos += 1
    o_ref = refs[pos]
    lse_ref = refs[pos + 1] if want_lse else None
    j = pl.program_id(1)

    width = span * LANES
    nchunk = nprev + span

    def scores(s, g, first):
        lanes = slice(s * width, (s + 1) * width)
        qg = jnp.concatenate([q_ref[HEAD_DIM * (g * grp + u):HEAD_DIM * (g * grp + u + 1), lanes]
                              for u in range(grp)], axis=1)
        parts = []
        for ci in range(nchunk):
            if first:
                kc = s * span - nprev + ci
                if kc < 0:
                    continue
                kcc = kc
                row = kc * LANES
            else:
                kc = j * NSUB + s * span - nprev + ci
                kcc = jnp.maximum(kc, 0) if dynamic else kc
                row = pl.multiple_of(kcc * LANES, LANES)
            sc = jnp.dot(k_ref[pl.ds(row, LANES), HEAD_DIM * g:HEAD_DIM * (g + 1)], qg,
                         preferred_element_type=F32)
            if dynamic:
                sc = sc + bias_ref[jnp.where(kc >= 0, ci, nchunk)]
            elif not zero_bias[ci]:
                sc = sc + bias_ref[ci]
            parts.append((kcc, sc))
        return parts

    def finish(s, g, parts):
        lanes = slice(s * width, (s + 1) * width)
        m = None
        for _, sc in parts:
            mc = jnp.max(sc, axis=0, keepdims=True)
            m = mc if m is None else jnp.maximum(m, mc)
        if has_sink:
            sk = sink_ref[g, 0:1, :]
            m = jnp.maximum(m, sk)
        acc = None
        for kcc, sc in parts:
            t = jnp.dot(v_ref[kcc, VROWS * g:VROWS * (g + 1), :], jnp.exp(sc - m).astype(BF16),
                        preferred_element_type=F32)
            acc = t if acc is None else acc + t
        l = acc[HEAD_DIM:HEAD_DIM + 1]
        if has_sink:
            l = l + jnp.exp(sk - m)
        o = acc[0:HEAD_DIM] * (1.0 / l)
        for u in range(grp):
            hq = g * grp + u
            o_ref[HEAD_DIM * hq:HEAD_DIM * (hq + 1), lanes] = o[:, u * width:(u + 1) * width].astype(o_ref.dtype)
        if want_lse:
            lse_ref[8 * g:8 * (g + 1), lanes] = jnp.broadcast_to(m + jnp.log(l), (8, width))

    def run(first):
        pending = None
        for s in range(NSUB // span):
            for g in range(hkv):
                parts = scores(s, g, first)
                if pending is not None:
                    finish(*pending)
                pending = (s, g, parts)
        finish(*pending)

    if dynamic:
        run(False)
    else:
        pl.when(j == 0)(lambda: run(True))
        pl.when(j > 0)(lambda: run(False))


def _banded(qT, q_blk, nk, k_blk, v4, v_blk, bias_and_zero, *, hkv, grp, nprev, span, dynamic,
            out_dtype, sinks=None, want_lse=False, name):
    bias, zero_bias = bias_and_zero
    assert span == 1 or (grp == 1 and sinks is None)
    B, _, S = qT.shape
    nt = S // TILE
    qrows = hkv * grp * HEAD_DIM
    in_specs = [pl.BlockSpec((None, qrows, TILE), lambda b, j: (b, q_blk, j)),
                pl.BlockSpec((None, S, LANES), lambda b, j: (b, 0, k_blk)),
                pl.BlockSpec((None, S // LANES, hkv * VROWS, LANES), lambda b, j: (b, 0, v_blk, 0)),
                pl.BlockSpec(bias.shape, lambda b, j: (0, 0, 0))]
    args = [qT, nk, v4, bias]
    if sinks is not None:
        in_specs.append(pl.BlockSpec(sinks.shape, lambda b, j: (0, 0, 0)))
        args.append(sinks)
    out_specs = [pl.BlockSpec((None, qrows, TILE), lambda b, j: (b, 0, j))]
    out_shape = [jax.ShapeDtypeStruct((B, qrows, S), out_dtype)]
    if want_lse:
        out_specs.append(pl.BlockSpec((None, 8 * hkv, TILE), lambda b, j: (b, 0, j)))
        out_shape.append(jax.ShapeDtypeStruct((B, 8 * hkv, S), F32))
    return pl.pallas_call(
        functools.partial(_banded_kernel, hkv=hkv, grp=grp, nprev=nprev, span=span, zero_bias=zero_bias,
                          dynamic=dynamic, has_sink=sinks is not None, want_lse=want_lse),
        grid=(B, nt),
        in_specs=in_specs,
        out_specs=out_specs,
        out_shape=out_shape,
        compiler_params=_cp(("parallel", "parallel")),
        name=name,
    )(*args)


def _post_norm_residual(y, x_ref, gw_ref, o_ref):
    ms = jnp.mean(y * y, axis=0, keepdims=True)
    yn = y * lax.rsqrt(ms + RMS_EPS)
    gw = gw_ref[...]
    for c in range(NSUB):
        sl = slice(c * LANES, (c + 1) * LANES)
        o_ref[:, sl] = x_ref[:, sl] + gw * yn[:, sl]


def _mix_ffn_kernel(a_ref, b0_ref, b1_ref, b2_ref, l0_ref, l1_ref, l2_ref, c_ref, wo_ref, x_ref, gw1_ref,
                    a2_ref, sh2_ref, wg_ref, wu_ref, wd_ref, cv_ref, gw2_ref, o_ref,
                    mix_scr, x1_scr, h_scr, carry_scr, act_scr, *, fchunk):
    s = pl.program_id(1)
    cur = s % 2
    prv = 1 - cur
    d_ff = wg_ref.shape[0]

    @pl.when(s == 0)
    def _():
        carry_scr[...] = jnp.zeros(carry_scr.shape, F32)
        h_scr[1] = jnp.zeros(h_scr.shape[1:], BF16)
        x1_scr[1] = jnp.zeros(x1_scr.shape[1:], F32)

    h = h_scr[prv]
    lane = lax.broadcasted_iota(jnp.int32, (fchunk, LANES), 1)
    for c in range(d_ff // fchunk):
        rs = slice(c * fchunk, (c + 1) * fchunk)
        g = jnp.dot(wg_ref[rs, :], h, preferred_element_type=F32)
        prev = carry_scr[rs, :]
        carry_scr[rs, :] = g[:, TILE - LANES:TILE]
        g1 = pltpu.roll(g, 1, 1)
        g2 = pltpu.roll(g, 2, 1)
        fix1 = jnp.where(lane < 1, pltpu.roll(prev, 1, 1), g1[:, 0:LANES])
        fix2 = jnp.where(lane < 2, pltpu.roll(prev, 2, 1), g2[:, 0:LANES])
        g1 = jnp.concatenate([fix1, g1[:, LANES:]], axis=1)
        g2 = jnp.concatenate([fix2, g2[:, LANES:]], axis=1)
        w0 = jnp.concatenate([cv_ref[0, rs, :]] * NSUB, axis=1)
        w1 = jnp.concatenate([cv_ref[1, rs, :]] * NSUB, axis=1)
        w2 = jnp.concatenate([cv_ref[2, rs, :]] * NSUB, axis=1)
        cb = jnp.concatenate([cv_ref[3, rs, :]] * NSUB, axis=1)
        acc = g2 * w0 + g1 * w1 + g * w2 + cb
        up = jnp.dot(wu_ref[rs, :], h, preferred_element_type=F32)
        act_scr[rs, :] = (jax.nn.gelu(acc, approximate=True) * up).astype(BF16)

    b_refs = (b0_ref, b1_ref, b2_ref)
    l_refs = (l0_ref, l1_ref, l2_ref)
    for hh in range(2):
        ls = [r[8 * hh:8 * hh + 1, :] for r in l_refs]
        mx = jnp.maximum(jnp.maximum(ls[0], ls[1]), ls[2])
        es = [jnp.exp(v - mx) for v in ls]
        inv = 1.0 / (es[0] + es[1] + es[2])
        for g in range(3):
            rs = slice(HEAD_DIM * hh, HEAD_DIM * (hh + 1))
            mix_scr[128 * g + HEAD_DIM * hh:128 * g + HEAD_DIM * (hh + 1), :] = (
                b_refs[g][rs, :] * (es[g] * inv)).astype(BF16)
    y1 = jnp.dot(wo_ref[:, 0:256], a_ref[...], preferred_element_type=F32)
    y1 = y1 + jnp.dot(wo_ref[:, 256:640], mix_scr[...], preferred_element_type=F32)
    y1 = y1 + jnp.dot(wo_ref[:, 640:1024], c_ref[...], preferred_element_type=F32)

    y2 = jnp.dot(wd_ref[...], act_scr[...], preferred_element_type=F32)

    _post_norm_residual(y1, x_ref, gw1_ref, x1_scr.at[cur])
    _norm_mod_to_scratch(x1_scr.at[cur], a2_ref, sh2_ref, h_scr.at[cur])
    _post_norm_residual(y2, x1_scr.at[prv], gw2_ref, o_ref)


def _mix_ffn(aT, bs, lses, cT, w_outT, xT, gw1, a2, sh2, wgT, wuT, wdT, cv, gw2, layer):
    B, D, S = xT.shape
    nt = S // TILE
    d_ff = wgT.shape[1]
    tile = lambda rows: pl.BlockSpec((None, rows, TILE), lambda b, s: (b, 0, jnp.minimum(s, nt - 1)))
    vec = pl.BlockSpec((None, D, LANES), lambda b, s: (b, 0, 0))
    return pl.pallas_call(
        functools.partial(_mix_ffn_kernel, fchunk=256),
        grid=(B, nt + 1),
        in_specs=[tile(256), tile(128), tile(128), tile(128), tile(16), tile(16), tile(16), tile(384),
                  _const_spec((None, D, D), lambda b, s: (layer, 0, 0)),
                  tile(D), vec, vec, vec,
                  _const_spec((None, d_ff, D), lambda b, s: (layer, 0, 0)),
                  _const_spec((None, d_ff, D), lambda b, s: (layer, 0, 0)),
                  _const_spec((None, D, d_ff), lambda b, s: (layer, 0, 0)),
                  _const_spec((None, 4, d_ff, LANES), lambda b, s: (layer, 0, 0, 0)),
                  vec],
        out_specs=pl.BlockSpec((None, D, TILE), lambda b, s: (b, 0, jnp.maximum(s - 1, 0))),
        out_shape=jax.ShapeDtypeStruct((B, D, S), F32),
        scratch_shapes=[pltpu.VMEM((384, TILE), BF16), pltpu.VMEM((2, D, TILE), F32),
                        pltpu.VMEM((2, D, TILE), BF16), pltpu.VMEM((d_ff, LANES), F32),
                        pltpu.VMEM((d_ff, TILE), BF16)],
        compiler_params=_cp(("arbitrary", "arbitrary")),
        name="mix_ffn",
    )(aT, bs[0], bs[1], bs[2], lses[0], lses[1], lses[2], cT, w_outT, xT, gw1,
      a2, sh2, wgT, wuT, wdT, cv, gw2)


def _band_bias(dilation, old_edge, reps, nprev, span):
    kk = np.arange(LANES)[:, None]
    qq = np.arange(LANES)[None, :]
    res = ((qq - kk) % dilation) == 0
    none = np.zeros_like(res)

    def role(ci):
        if ci < 0 or ci > nprev:
            return none
        return res & (kk - qq >= old_edge) if ci == 0 else (res & (kk <= qq) if ci == nprev else res)

    tabs = [np.concatenate([role(u - a) for a in range(span)], axis=1) for u in range(nprev + span)]
    zero = tuple(bool(t.all()) for t in tabs)
    tabs.append(np.concatenate([none] * span, axis=1))
    out = np.stack([np.where(t, 0.0, NEG) for t in tabs]).astype(np.float32)
    return jnp.asarray(np.tile(out, (1, 1, reps))), zero


def _cmp_bias():
    nn = np.arange(LANES)[:, None]
    qq = np.arange(LANES)[None, :]
    tabs = [np.where(CMP_STRIDE * nn + CMP_BLOCK - 1 <= LANES * r + qq, 0.0, NEG) for r in range(16)]
    return jnp.asarray(np.stack(tabs).astype(np.float32))


def _overlap_rows():
    jj = np.arange(40)[:, None]
    nn = np.arange(LANES)[None, :]
    return jnp.asarray(((nn >= 4 * jj - 1) & (nn <= 4 * jj + 3)).astype(np.float32), dtype=BF16)


_IN_COL_ORDER = ((1804, 2188), (652, 1036), (0, 256),
                 (256, 320), (320, 384), (384, 448), (512, 576), (1036, 1420), (2188, 2316),
                 (448, 512), (576, 640), (1420, 1804), (2316, 2444),
                 (640, 652))


def _prep_w_in(w_in):
    wt = jnp.swapaxes(jnp.concatenate([w_in[:, :, a:b] for a, b in _IN_COL_ORDER], axis=2), 1, 2)
    scale = np.ones((wt.shape[1], 1), np.float32)
    scale[:Q_ROWS] = HEAD_DIM ** -0.5
    wt = wt * scale
    wt = jnp.pad(wt, ((0, 0), (0, W_ROWS - wt.shape[1]), (0, 0)))
    return wt.astype(BF16)


def _prep_compress(w_ck, w_cv, pe_k, pe_v):
    L = w_ck.shape[0]
    half = CMP_BLOCK // 2

    def big(lo):
        wk = w_ck[:, lo:lo + half]
        wv = w_cv[:, lo:lo + half]
        z = jnp.zeros_like(wk)
        top = jnp.concatenate([wk, z], axis=-1)
        bot = jnp.concatenate([z, wv], axis=-1)
        return jnp.concatenate([top, bot], axis=2).reshape(L, half * LANES, LANES)

    wbig = jnp.concatenate([big(0), big(half)], axis=-1).astype(BF16)
    pe = jnp.concatenate([pe_k, pe_v], axis=-1)
    pe2 = jnp.zeros((L, 16, half * LANES), F32)
    pe2 = pe2.at[:, 0].set(pe[:, :half].reshape(L, -1)).at[:, 8].set(pe[:, half:].reshape(L, -1))
    return wbig, pe2.astype(BF16)


def _lane_bcast(v):
    return jnp.broadcast_to(v[..., None], v.shape + (LANES,))


def kernel(x, c, positions, w_in, w_out, w_ada, b_ada, norm_w, cmp_w_k, cmp_w_v, cmp_pe_k, cmp_pe_v,
           sinks, w_gate, w_up, conv_w, conv_b, w_down):
    B, S, D = x.shape
    depth = w_in.shape[0]
    assert S % 2048 == 0 and D == 1024 and w_in.shape[2] == 2444

    inv = ROPE_THETA ** (-jnp.arange(0, HEAD_DIM, 2, dtype=F32) / HEAD_DIM)
    ang = positions.astype(F32)[:, None, :] * inv[None, :, None]
    cosT, sinT = jnp.cos(ang), jnp.sin(ang)

    c8 = jnp.pad(c, ((0, 8 - B), (0, 0)))
    ada = _adaln(c8, w_ada, b_ada)[:, :B]
    sh1, sc1, g1, sh2, sc2, g2 = [ada[:, :, k * D:(k + 1) * D] for k in range(6)]
    nw = norm_w[:, :, None, :]
    a1 = _lane_bcast(nw[:, 0] * (1 + sc1))
    gw1 = _lane_bcast(g1 * nw[:, 1])
    a2 = _lane_bcast(nw[:, 2] * (1 + sc2))
    gw2 = _lane_bcast(g2 * nw[:, 3])
    sh1b, sh2b = _lane_bcast(sh1), _lane_bcast(sh2)

    w_inT = _prep_w_in(w_in)
    w_outT = jnp.swapaxes(w_out, 1, 2).astype(BF16)
    wgT = jnp.swapaxes(w_gate, 1, 2).astype(BF16)
    wuT = jnp.swapaxes(w_up, 1, 2).astype(BF16)
    wdT = jnp.swapaxes(w_down, 1, 2).astype(BF16)
    cv = _lane_bcast(jnp.concatenate([conv_w, conv_b[:, None, :]], axis=1))
    wbig, pe2 = _prep_compress(cmp_w_k, cmp_w_v, cmp_pe_k, cmp_pe_v)
    sink_tab = jnp.broadcast_to(
        jnp.repeat(sinks.reshape(depth, SWA_KV_HEADS, SWA_HEADS // SWA_KV_HEADS), LANES, axis=-1)[:, :, None, :],
        (depth, SWA_KV_HEADS, 8, LANES * (SWA_HEADS // SWA_KV_HEADS)))

    ov = _overlap_rows()
    cmp_bias = _cmp_bias()
    kk = np.arange(SEL_TQ)[:, None]
    causal4 = jnp.asarray(np.tile(np.where(kk <= kk.T, 0.0, NEG).astype(np.float32), (1, NSA_HEADS)))
    bias_win = _band_bias(1, 1, NSA_HEADS, NSA_WINDOW // LANES, 1)
    bias_swa = _band_bias(1, 1, SWA_HEADS // SWA_KV_HEADS, SWA_WINDOW // LANES, 1)
    dil_span = [2 if d >= NSUB else 1 for _, d in DIL_PATTERNS]
    bias_dil = [_band_bias(d, 0, 1, d, sp) for (_, d), sp in zip(DIL_PATTERNS, dil_span)]

    xT = jnp.swapaxes(x, 1, 2)
    for l in range(depth):
        qT, kcvc, nk, v4, gates = _inproj(xT, a1[l], sh1b[l], w_inT, l, cosT, sinT)
        kc4, vc4 = _compress(kcvc, wbig, pe2, l)
        ocmp, mm = _cmp_topk(qT, kc4, vc4, ov, cmp_bias)
        owin, = _banded(qT, A_Q_BLK, nk, 1, v4, 1, bias_win, hkv=1, grp=NSA_HEADS, nprev=NSA_WINDOW // LANES,
                        span=1, dynamic=False, out_dtype=F32, name="nsa_win")
        aT = _sel_attend(qT, mm, nk, v4, causal4, ocmp, owin, gates)
        bs, lses = [], []
        for gi, (win, dil) in enumerate(DIL_PATTERNS):
            o, lse = _banded(qT, 3 + gi, nk, 2 + gi, v4, 1 + gi, bias_dil[gi], hkv=2, grp=1, nprev=dil,
                             span=dil_span[gi], dynamic=dil > NSUB, out_dtype=F32, want_lse=True,
                             name="dil%d" % dil)
            bs.append(o)
            lses.append(lse)
        cT, = _banded(qT, 0, nk, 5, v4, 4, bias_swa, hkv=SWA_KV_HEADS, grp=SWA_HEADS // SWA_KV_HEADS,
                      nprev=SWA_WINDOW // LANES, span=1, dynamic=False, out_dtype=BF16,
                      sinks=sink_tab[l], name="swa")
        xT = _mix_ffn(aT, bs, lses, cT, w_outT, xT, gw1[l], a2[l], sh2b[l], wgT, wuT, wdT, cv, gw2[l], l)
    return jnp.swapaxes(xT, 1, 2)
```

```python
import functools

import numpy as np
import jax
import jax.numpy as jnp
from jax import lax
from jax.experimental import pallas as pl
from jax.experimental.pallas import tpu as pltpu

F32 = jnp.float32
BF16 = jnp.bfloat16

HEAD_DIM = 64
HALF = HEAD_DIM // 2
NSA_HEADS = 4
CMP_BLOCK = 32
CMP_STRIDE = 16
SEL_BLOCK = 64
SEL_TOPK = 16
NSA_WINDOW = 512
DIL_PATTERNS = ((128, 1), (512, 4), (2048, 16))
DIL_HEADS = 6
SWA_HEADS = 6
SWA_KV_HEADS = 2
SWA_WINDOW = 128
ROPE_THETA = 10000.0
RMS_EPS = 1e-6
NEG = -1e30
FORCE = 1e4
CONV_WIDTH = 3

LANES = 128
TILE = 512
NSUB = TILE // LANES
VROWS = HEAD_DIM + 16
MEMBER_BIG = 2.0 ** 100
VMEM_LIMIT = 56 * 1024 * 1024

Q_ROWS = 1024
A_Q_BLK = 3
NK_ROWS = 768
NK_LANES = 768
V_ROWS = 640
G_ROWS = 16
W_ROWS = Q_ROWS + NK_ROWS + V_ROWS + G_ROWS
N_VPIECES = V_ROWS // HEAD_DIM


def _cp(sem):
    return pltpu.CompilerParams(dimension_semantics=sem, vmem_limit_bytes=VMEM_LIMIT)


def _const_spec(shape, index_map):
    return pl.BlockSpec(shape, index_map, pipeline_mode=pl.Buffered(1))


def _adaln_kernel(c_ref, w_ref, b_ref, o_ref):
    c = c_ref[...]
    cond = c * jax.nn.sigmoid(c)
    o_ref[...] = jnp.dot(cond, w_ref[...], preferred_element_type=F32,
                         precision=lax.Precision.HIGHEST) + b_ref[...]


def _adaln(c8, w_ada, b_ada):
    depth, d, six_d = w_ada.shape
    nblk = six_d // d
    return pl.pallas_call(
        _adaln_kernel,
        grid=(depth, nblk),
        in_specs=[pl.BlockSpec((8, d), lambda l, n: (0, 0)),
                  pl.BlockSpec((None, d, d), lambda l, n: (l, 0, n)),
                  pl.BlockSpec((None, 1, d), lambda l, n: (l, 0, n))],
        out_specs=pl.BlockSpec((None, 8, d), lambda l, n: (l, 0, n)),
        out_shape=jax.ShapeDtypeStruct((depth, 8, six_d), F32),
        compiler_params=_cp(("parallel", "parallel")),
        name="adaln",
    )(c8, w_ada, b_ada.reshape(depth, 1, six_d))


def _norm_mod_to_scratch(x_ref, a_ref, sh_ref, h_scr):
    for c in range(NSUB):
        sl = slice(c * LANES, (c + 1) * LANES)
        xs = x_ref[:, sl]
        ms = jnp.mean(xs * xs, axis=0, keepdims=True)
        h_scr[:, sl] = ((xs * lax.rsqrt(ms + RMS_EPS)) * a_ref[...] + sh_ref[...]).astype(BF16)


def _inproj_kernel(x_ref, a_ref, sh_ref, w_ref, cos_ref, sin_ref,
                   q_ref, kcvc_ref, nk_ref, v4_ref, gate_ref, *rest, token_major):
    j = pl.program_id(1)
    if token_major:
        xt_ref, h_scr = rest
        for c in range(NSUB):
            xt_ref[:, c * LANES:(c + 1) * LANES] = x_ref[c * LANES:(c + 1) * LANES, :].T
        x_ref = xt_ref
    else:
        h_scr, = rest
    _norm_mod_to_scratch(x_ref, a_ref, sh_ref, h_scr)
    h = h_scr[...]
    cos = cos_ref[...]
    sin = sin_ref[...]

    def proj(r0, r1):
        return jnp.dot(w_ref[r0:r1, :], h, preferred_element_type=F32)

    def rope(r, nh):
        outs = []
        for hh in range(nh):
            t1 = r[HEAD_DIM * hh:HEAD_DIM * hh + HALF]
            t2 = r[HEAD_DIM * hh + HALF:HEAD_DIM * (hh + 1)]
            outs.append(t1 * cos - t2 * sin)
            outs.append(t2 * cos + t1 * sin)
        return jnp.concatenate(outs, axis=0)

    for r0, r1 in ((0, 384), (384, 768), (768, 1024)):
        q_ref[r0:r1, :] = rope(proj(r0, r1), (r1 - r0) // HEAD_DIM).astype(BF16)

    base = Q_ROWS
    r = proj(base, base + 128)
    kcvc = jnp.concatenate([rope(r[0:64], 1), r[64:128]], axis=0)
    kcvc_ref[...] = kcvc.T.astype(BF16)

    r = proj(base + 128, base + 192)
    tok = j * TILE + lax.broadcasted_iota(jnp.int32, (HEAD_DIM, TILE), 1)
    row = lax.broadcasted_iota(jnp.int32, (HEAD_DIM, TILE), 0)
    member_cols = jnp.where(row == ((tok >> 6) & 15), MEMBER_BIG, 0.0).astype(F32)
    nk_ref[:, 0:128] = jnp.concatenate([rope(r, 1), member_cols], axis=0).T.astype(BF16)

    r = proj(base + 192, base + 256)
    nk_ref[:, 128:256] = jnp.concatenate([rope(r, 1), jnp.zeros((HEAD_DIM, TILE), F32)], axis=0).T.astype(BF16)

    r = proj(base + 256, base + 640)
    nk_ref[:, 256:640] = rope(r, 6).T.astype(BF16)

    r = proj(base + 640, base + 768)
    nk_ref[:, 640:768] = rope(r, 2).T.astype(BF16)

    base = Q_ROWS + NK_ROWS
    r = proj(base, base + V_ROWS).astype(BF16)
    ones = jnp.ones((VROWS - HEAD_DIM, LANES), BF16)
    for c in range(NSUB):
        for p in range(N_VPIECES):
            v4_ref[c, VROWS * p:VROWS * p + HEAD_DIM, :] = r[HEAD_DIM * p:HEAD_DIM * (p + 1),
                                                             c * LANES:(c + 1) * LANES]
            v4_ref[c, VROWS * p + HEAD_DIM:VROWS * (p + 1), :] = ones

    base = Q_ROWS + NK_ROWS + V_ROWS
    gate_ref[...] = jax.nn.sigmoid(proj(base, base + G_ROWS))


def _inproj(x, a1, sh1, w_inT, layer, cosT, sinT, token_major):
    if token_major:
        B, S, D = x.shape
        x_spec = pl.BlockSpec((None, TILE, D), lambda b, j: (b, j, 0))
    else:
        B, D, S = x.shape
        x_spec = pl.BlockSpec((None, D, TILE), lambda b, j: (b, 0, j))
    nt = S // TILE
    out_specs = [pl.BlockSpec((None, Q_ROWS, TILE), lambda b, j: (b, 0, j)),
                 pl.BlockSpec((None, TILE, LANES), lambda b, j: (b, j, 0)),
                 pl.BlockSpec((None, TILE, NK_LANES), lambda b, j: (b, j, 0)),
                 pl.BlockSpec((None, NSUB, N_VPIECES * VROWS, LANES), lambda b, j: (b, j, 0, 0)),
                 pl.BlockSpec((None, G_ROWS, TILE), lambda b, j: (b, 0, j))]
    out_shape = [jax.ShapeDtypeStruct((B, Q_ROWS, S), BF16),
                 jax.ShapeDtypeStruct((B, S, LANES), BF16),
                 jax.ShapeDtypeStruct((B, S, NK_LANES), BF16),
                 jax.ShapeDtypeStruct((B, S // LANES, N_VPIECES * VROWS, LANES), BF16),
                 jax.ShapeDtypeStruct((B, G_ROWS, S), F32)]
    if token_major:
        out_specs.append(pl.BlockSpec((None, D, TILE), lambda b, j: (b, 0, j)))
        out_shape.append(jax.ShapeDtypeStruct((B, D, S), F32))
    return pl.pallas_call(
        functools.partial(_inproj_kernel, token_major=token_major),
        grid=(B, nt),
        in_specs=[x_spec,
                  pl.BlockSpec((None, D, LANES), lambda b, j: (b, 0, 0)),
                  pl.BlockSpec((None, D, LANES), lambda b, j: (b, 0, 0)),
                  _const_spec((None, W_ROWS, D), lambda b, j: (layer, 0, 0)),
                  pl.BlockSpec((None, HALF, TILE), lambda b, j: (b, 0, j)),
                  pl.BlockSpec((None, HALF, TILE), lambda b, j: (b, 0, j))],
        out_specs=out_specs,
        out_shape=out_shape,
        scratch_shapes=[pltpu.VMEM((D, TILE), BF16)],
        compiler_params=_cp(("parallel", "parallel")),
        name="inproj",
    )(x, a1, sh1, w_inT, cosT, sinT)


def _compress_kernel(t_ref, w_ref, pe_ref, kc_ref, vc_ref):
    n = t_ref.shape[0]
    a = jnp.dot(t_ref[...], w_ref[...], preferred_element_type=F32)
    pc = jnp.dot(pe_ref[...], w_ref[...], preferred_element_type=F32)
    const = pc[0:1, 0:LANES] + pc[8:9, LANES:2 * LANES]
    cmp = a[:, 0:LANES] + pltpu.roll(a[:, LANES:2 * LANES], n - 1, 0) + const
    cmp_t = cmp.T
    ones = jnp.ones((VROWS - HEAD_DIM, LANES), BF16)
    for c in range(n // LANES):
        kc_ref[c] = cmp[c * LANES:(c + 1) * LANES].astype(BF16)
        vc_ref[c, 0:HEAD_DIM, :] = cmp_t[HEAD_DIM:2 * HEAD_DIM, c * LANES:(c + 1) * LANES].astype(BF16)
        vc_ref[c, HEAD_DIM:VROWS, :] = ones


def _compress(kcvc, wbig, pe2, layer):
    B, S, _ = kcvc.shape
    n = S // CMP_STRIDE
    nch = n // LANES
    tview = kcvc.reshape(B, n, CMP_STRIDE * LANES)
    return pl.pallas_call(
        _compress_kernel,
        grid=(B,),
        in_specs=[pl.BlockSpec((None, n, CMP_STRIDE * LANES), lambda b: (b, 0, 0)),
                  pl.BlockSpec((None, CMP_STRIDE * LANES, 2 * LANES), lambda b: (layer, 0, 0)),
                  pl.BlockSpec((None, 16, CMP_STRIDE * LANES), lambda b: (layer, 0, 0))],
        out_specs=[pl.BlockSpec((None, nch, LANES, LANES), lambda b: (b, 0, 0, 0)),
                   pl.BlockSpec((None, nch, VROWS, LANES), lambda b: (b, 0, 0, 0))],
        out_shape=[jax.ShapeDtypeStruct((B, nch, LANES, LANES), BF16),
                   jax.ShapeDtypeStruct((B, nch, VROWS, LANES), BF16)],
        compiler_params=_cp(("parallel",)),
        name="compress",
    )(tview, wbig, pe2)


def _stack_heads(q_ref, nh, lane_slice=slice(None)):
    return jnp.concatenate([q_ref[HEAD_DIM * h:HEAD_DIM * (h + 1), lane_slice] for h in range(nh)], axis=1)


CMP_UNITS = 2
CMP_TQ = CMP_UNITS * LANES


def _cmp_kernel(q_ref, kc_ref, vc_ref, ov_ref, bias_ref, o_ref, mm_ref, imp_scr, *, n_sel, nstep):
    i = pl.program_id(1)
    nch_all = kc_ref.shape[0]
    parts = 4 if nch_all % 4 == 0 else (2 if nch_all % 2 == 0 else 1)
    for v in range(parts):
        pl.when(i // (nstep // parts) == v)(functools.partial(
            _cmp_body, q_ref, kc_ref, vc_ref, ov_ref, bias_ref, o_ref, mm_ref, imp_scr,
            nch=nch_all * (v + 1) // parts, nrows=n_sel * (v + 1) // parts, n_sel=n_sel))


def _cmp_body(q_ref, kc_ref, vc_ref, ov_ref, bias_ref, o_ref, mm_ref, imp_scr, *, nch, nrows, n_sel):
    i = pl.program_id(1)
    cd = (i * CMP_UNITS) // 16

    def scores(u):
        q = _stack_heads(q_ref, NSA_HEADS, slice(u * LANES, (u + 1) * LANES))
        edge = bias_ref[u]
        ss = []
        for c in range(nch):
            b = jnp.where(c < cd, 0.0, jnp.where(c == cd, edge, NEG))
            s = jnp.dot(kc_ref[c, :, 0:HEAD_DIM], q, preferred_element_type=F32)
            ss.append(s + jnp.concatenate([b] * NSA_HEADS, axis=1))
        return ss

    def finish(u, ss):
        lanes = slice(u * LANES, (u + 1) * LANES)
        m = jnp.max(functools.reduce(jnp.maximum, ss), axis=0, keepdims=True)
        valid = m > 0.5 * NEG
        imp_scr[u, 0:32 * nch + 8, :] = jnp.zeros((32 * nch + 8, imp_scr.shape[2]), F32)
        acc = None
        for c in range(nch):
            p = jnp.exp(ss[c] - m).astype(BF16)
            t = jnp.dot(vc_ref[c], p, preferred_element_type=F32)
            acc = t if acc is None else acc + t
            imp_scr[u, 32 * c:32 * c + 40, :] += jnp.dot(ov_ref[...], p, preferred_element_type=F32)
        inv = jnp.where(valid, 1.0 / acc[HEAD_DIM:HEAD_DIM + 1], 0.0)
        o = acc[0:HEAD_DIM] * inv
        for h in range(NSA_HEADS):
            o_ref[HEAD_DIM * h:HEAD_DIM * (h + 1), lanes] = o[:, h * LANES:(h + 1) * LANES]
        imp = jnp.zeros((nrows, LANES), F32)
        for h in range(NSA_HEADS):
            sl = slice(h * LANES, (h + 1) * LANES)
            imp = imp + imp_scr[u, 0:nrows, sl] * inv[:, sl]
        return imp

    pending = scores(0)
    imps = []
    for u in range(1, CMP_UNITS):
        nxt = scores(u)
        imps.append(finish(u - 1, pending))
        pending = nxt
    imps.append(finish(CMP_UNITS - 1, pending))

    blk = lax.broadcasted_iota(jnp.int32, (nrows, LANES), 0).astype(F32)

    def pick_one(imp):
        mx = jnp.max(imp, axis=0, keepdims=True)
        first = jnp.min(jnp.where(imp == mx, blk, float(nrows)), axis=0, keepdims=True)
        return jnp.where(blk == first, -jnp.inf, imp)

    curs, cands = [], []
    for u in range(CMP_UNITS):
        t = i * CMP_TQ + u * LANES + lax.broadcasted_iota(jnp.int32, (nrows, LANES), 1)
        cur = (t >> 6).astype(F32)
        forced = (blk == 0.0) | (blk == cur) | (blk == cur - 1.0)
        imp = jnp.where(forced, FORCE, imps[u])
        curs.append(cur)
        cands.append(jnp.where(blk <= cur, imp, NEG))
    cands = lax.fori_loop(0, min(SEL_TOPK, nrows), lambda _, c: tuple(pick_one(x) for x in c), tuple(cands))
    for u in range(CMP_UNITS):
        member = (cands[u] == -jnp.inf) & (blk <= curs[u])
        mm_ref[0:nrows, u * LANES:(u + 1) * LANES] = jnp.where(member, 0.0, -1.0).astype(BF16)
    if nrows < n_sel:
        mm_ref[nrows:n_sel, :] = jnp.full((n_sel - nrows, CMP_TQ), -1.0, BF16)


def _cmp_topk(qT, kc4, vc4, ov, cmp_bias):
    B, _, S = qT.shape
    nstep = S // CMP_TQ
    nch = kc4.shape[1]
    n_sel = S // SEL_BLOCK
    nq = NSA_HEADS * LANES
    nbias = cmp_bias.shape[0] // CMP_UNITS
    return pl.pallas_call(
        functools.partial(_cmp_kernel, n_sel=n_sel, nstep=nstep),
        grid=(B, nstep),
        in_specs=[pl.BlockSpec((None, NSA_HEADS * HEAD_DIM, CMP_TQ), lambda b, i: (b, A_Q_BLK, i)),
                  pl.BlockSpec((None, nch, LANES, LANES), lambda b, i: (b, 0, 0, 0)),
                  pl.BlockSpec((None, nch, VROWS, LANES), lambda b, i: (b, 0, 0, 0)),
                  pl.BlockSpec((40, LANES), lambda b, i: (0, 0)),
                  pl.BlockSpec((None, CMP_UNITS, LANES, LANES), lambda b, i: (i % nbias, 0, 0, 0))],
        out_specs=[pl.BlockSpec((None, NSA_HEADS * HEAD_DIM, CMP_TQ), lambda b, i: (b, 0, i)),
                   pl.BlockSpec((None, n_sel, CMP_TQ), lambda b, i: (b, 0, i))],
        out_shape=[jax.ShapeDtypeStruct((B, NSA_HEADS * HEAD_DIM, S), F32),
                   jax.ShapeDtypeStruct((B, n_sel, S), BF16)],
        scratch_shapes=[pltpu.VMEM((CMP_UNITS, 32 * nch + 64, nq), F32)],
        compiler_params=_cp(("parallel", "parallel")),
        name="nsa_cmp_topk",
    )(qT, kc4, vc4, ov, cmp_bias.reshape(nbias, CMP_UNITS, LANES, LANES))


SEL_GROUP = 8
SEL_TQ = 2 * LANES


def _sel_kernel(q_ref, mm_ref, ks_ref, v_ref, causal_ref, ocmp_ref, owin_ref, gate_ref, out_ref,
                qa_scr, mm_scr, s_scr):
    i = pl.program_id(1)
    nq = NSA_HEADS * SEL_TQ
    qa_scr[0:HEAD_DIM, :] = _stack_heads(q_ref, NSA_HEADS)
    qa_scr[HEAD_DIM:LANES, :] = jnp.zeros((LANES - HEAD_DIM, nq), BF16)
    mm = mm_ref[...]
    for h in range(NSA_HEADS):
        mm_scr[:, h * SEL_TQ:(h + 1) * SEL_TQ] = mm

    gkeys = SEL_GROUP * LANES

    def load_query(gi):
        rg = (gi * SEL_GROUP) // 8
        qa_scr[HEAD_DIM:HEAD_DIM + 16, :] = mm_scr[pl.ds(pl.multiple_of(rg * 16, 16), 16), :]
        return qa_scr[...]

    def chunk_scores(gi, u, qa):
        row = pl.multiple_of((gi * SEL_GROUP + u) * LANES, LANES)
        return jnp.dot(ks_ref[pl.ds(row, LANES), :], qa, preferred_element_type=F32)

    def weighted_values(gi, ps):
        vcat = jnp.concatenate([v_ref[gi * SEL_GROUP + u] for u in range(SEL_GROUP)], axis=1)
        return jnp.dot(vcat, jnp.concatenate(ps, axis=0), preferred_element_type=F32)

    def body(gi, carry):
        m, acc, mg = carry
        m_new = jnp.maximum(m, mg)
        qa = load_query(gi + 1)
        ps, mx = [], None
        for u in range(SEL_GROUP):
            rows = slice(u * LANES, (u + 1) * LANES)
            ps.append(jnp.exp(s_scr[rows, :] - m_new).astype(BF16))
            nxt = chunk_scores(gi + 1, u, qa)
            s_scr[rows, :] = nxt
            mx = nxt if mx is None else jnp.maximum(mx, nxt)
        acc = acc * jnp.exp(m - m_new) + weighted_values(gi, ps)
        return m_new, acc, jnp.max(mx, axis=0, keepdims=True)

    qa = load_query(0)
    mx = None
    for u in range(SEL_GROUP):
        s0 = chunk_scores(0, u, qa)
        s_scr[u * LANES:(u + 1) * LANES, :] = s0
        mx = s0 if mx is None else jnp.maximum(mx, s0)
    c0 = i * (SEL_TQ // LANES)
    last = c0 // SEL_GROUP
    m, acc, _ = lax.fori_loop(0, last, body, (jnp.full((1, nq), NEG, F32), jnp.zeros((VROWS, nq), F32),
                                              jnp.max(mx, axis=0, keepdims=True)))
    drow = pl.multiple_of((c0 % SEL_GROUP) * LANES, SEL_TQ)
    s_scr[pl.ds(drow, SEL_TQ), :] += causal_ref[...]
    s = s_scr[...]
    m_new = jnp.maximum(m, jnp.max(s, axis=0, keepdims=True))
    acc = acc * jnp.exp(m - m_new) + weighted_values(last, [jnp.exp(s - m_new).astype(BF16)])
    o = acc[0:HEAD_DIM] * (1.0 / acc[HEAD_DIM:HEAD_DIM + 1])

    g = gate_ref[...]
    for h in range(NSA_HEADS):
        rs = slice(HEAD_DIM * h, HEAD_DIM * (h + 1))
        out = (g[3 * h:3 * h + 1] * ocmp_ref[rs, :] + g[3 * h + 1:3 * h + 2] * o[:, h * SEL_TQ:(h + 1) * SEL_TQ]
               + g[3 * h + 2:3 * h + 3] * owin_ref[rs, :])
        out_ref[rs, :] = out.astype(BF16)


def _sel_attend(qT, mm, nk, v4, causal4, ocmp, owin, gates):
    B, _, S = qT.shape
    nstep = S // SEL_TQ
    n_sel = S // SEL_BLOCK
    nq = NSA_HEADS * SEL_TQ
    ar = NSA_HEADS * HEAD_DIM
    return pl.pallas_call(
        _sel_kernel,
        grid=(B, nstep),
        in_specs=[pl.BlockSpec((None, ar, SEL_TQ), lambda b, i: (b, A_Q_BLK, i)),
                  pl.BlockSpec((None, n_sel, SEL_TQ), lambda b, i: (b, 0, i)),
                  pl.BlockSpec((None, S, LANES), lambda b, i: (b, 0, 0)),
                  pl.BlockSpec((None, S // LANES, VROWS, LANES), lambda b, i: (b, 0, 0, 0)),
                  _const_spec((SEL_TQ, nq), lambda b, i: (0, 0)),
                  pl.BlockSpec((None, ar, SEL_TQ), lambda b, i: (b, 0, i)),
                  pl.BlockSpec((None, ar, SEL_TQ), lambda b, i: (b, 0, i)),
                  pl.BlockSpec((None, G_ROWS, SEL_TQ), lambda b, i: (b, 0, i))],
        out_specs=pl.BlockSpec((None, ar, SEL_TQ), lambda b, i: (b, 0, i)),
        out_shape=jax.ShapeDtypeStruct((B, ar, S), BF16),
        scratch_shapes=[pltpu.VMEM((LANES, nq), BF16), pltpu.VMEM((n_sel, nq), BF16),
                        pltpu.VMEM((SEL_GROUP * LANES, nq), F32)],
        compiler_params=_cp(("parallel", "parallel")),
        name="nsa_sel",
    )(qT, mm, nk, v4, causal4, ocmp, owin, gates)


def _banded_kernel(*refs, hkv, grp, nprev, span, zero_bias, dynamic, has_sink, want_lse):
    q_ref, k_ref, v_ref, bias_ref = refs[:4]
    pos = 4
    sink_ref = None
    if has_sink:
        sink_ref = refs[pos]
        pos += 1
    o_ref = refs[pos]
    lse_ref = refs[pos + 1] if want_lse else None
    j = pl.program_id(1)

    width = span * LANES
    nchunk = nprev + span

    def scores(s, g, first):
        lanes = slice(s * width, (s + 1) * width)
        qg = jnp.concatenate([q_ref[HEAD_DIM * (g * grp + u):HEAD_DIM * (g * grp + u + 1), lanes]
                              for u in range(grp)], axis=1)
        parts = []
        for ci in range(nchunk):
            if first:
                kc = s * span - nprev + ci
                if kc < 0:
                    continue
                kcc = kc
                row = kc * LANES
            else:
                kc = j * NSUB + s * span - nprev + ci
                kcc = jnp.maximum(kc, 0) if dynamic else kc
                row = pl.multiple_of(kcc * LANES, LANES)
            sc = jnp.dot(k_ref[pl.ds(row, LANES), HEAD_DIM * g:HEAD_DIM * (g + 1)], qg,
                         preferred_element_type=F32)
            if dynamic:
                sc = sc + bias_ref[jnp.where(kc >= 0, ci, nchunk)]
            elif not zero_bias[ci]:
                sc = sc + bias_ref[ci]
            parts.append((kcc, sc))
        return parts

    def finish(s, g, parts):
        lanes = slice(s * width, (s + 1) * width)
        m = None
        for _, sc in parts:
            mc = jnp.max(sc, axis=0, keepdims=True)
            m = mc if m is None else jnp.maximum(m, mc)
        if has_sink:
            sk = sink_ref[g, 0:1, :]
            m = jnp.maximum(m, sk)
        acc = None
        for kcc, sc in parts:
            t = jnp.dot(v_ref[kcc, VROWS * g:VROWS * (g + 1), :], jnp.exp(sc - m).astype(BF16),
                        preferred_element_type=F32)
            acc = t if acc is None else acc + t
        l = acc[HEAD_DIM:HEAD_DIM + 1]
        if has_sink:
            l = l + jnp.exp(sk - m)
        o = acc[0:HEAD_DIM] * (1.0 / l)
        for u in range(grp):
            hq = g * grp + u
            o_ref[HEAD_DIM * hq:HEAD_DIM * (hq + 1), lanes] = o[:, u * width:(u + 1) * width].astype(o_ref.dtype)
        if want_lse:
            lse_ref[8 * g:8 * (g + 1), lanes] = jnp.broadcast_to(m + jnp.log(l), (8, width))

    def run(first):
        pending = None
        for s in range(NSUB // span):
            for g in range(hkv):
                parts = scores(s, g, first)
                if pending is not None:
                    finish(*pending)
                pending = (s, g, parts)
        finish(*pending)

    if dynamic:
        run(False)
    else:
        pl.when(j == 0)(lambda: run(True))
        pl.when(j > 0)(lambda: run(False))


def _banded(qT, q_blk, nk, k_blk, v4, v_blk, bias_and_zero, *, hkv, grp, nprev, span, dynamic,
            out_dtype, sinks=None, want_lse=False, name):
    bias, zero_bias = bias_and_zero
    assert span == 1 or (grp == 1 and sinks is None)
    B, _, S = qT.shape
    nt = S // TILE
    qrows = hkv * grp * HEAD_DIM
    in_specs = [pl.BlockSpec((None, qrows, TILE), lambda b, j: (b, q_blk, j)),
                pl.BlockSpec((None, S, LANES), lambda b, j: (b, 0, k_blk)),
                pl.BlockSpec((None, S // LANES, hkv * VROWS, LANES), lambda b, j: (b, 0, v_blk, 0)),
                pl.BlockSpec(bias.shape, lambda b, j: (0, 0, 0))]
    args = [qT, nk, v4, bias]
    if sinks is not None:
        in_specs.append(pl.BlockSpec(sinks.shape, lambda b, j: (0, 0, 0)))
        args.append(sinks)
    out_specs = [pl.BlockSpec((None, qrows, TILE), lambda b, j: (b, 0, j))]
    out_shape = [jax.ShapeDtypeStruct((B, qrows, S), out_dtype)]
    if want_lse:
        out_specs.append(pl.BlockSpec((None, 8 * hkv, TILE), lambda b, j: (b, 0, j)))
        out_shape.append(jax.ShapeDtypeStruct((B, 8 * hkv, S), F32))
    return pl.pallas_call(
        functools.partial(_banded_kernel, hkv=hkv, grp=grp, nprev=nprev, span=span, zero_bias=zero_bias,
                          dynamic=dynamic, has_sink=sinks is not None, want_lse=want_lse),
        grid=(B, nt),
        in_specs=in_specs,
        out_specs=out_specs,
        out_shape=out_shape,
        compiler_params=_cp(("parallel", "parallel")),
        name=name,
    )(*args)


def _post_norm_residual(y, x_ref, gw_ref, o_ref, token_major=False):
    ms = jnp.mean(y * y, axis=0, keepdims=True)
    yn = y * lax.rsqrt(ms + RMS_EPS)
    gw = gw_ref[...]
    for c in range(NSUB):
        sl = slice(c * LANES, (c + 1) * LANES)
        out = x_ref[:, sl] + gw * yn[:, sl]
        if token_major:
            o_ref[sl, :] = out.T
        else:
            o_ref[:, sl] = out


def _mix_ffn_kernel(a_ref, b0_ref, b1_ref, b2_ref, l0_ref, l1_ref, l2_ref, c_ref, wo_ref, x_ref, gw1_ref,
                    a2_ref, sh2_ref, wg_ref, wu_ref, wd_ref, cv_ref, gw2_ref, o_ref,
                    mix_scr, x1_scr, h_scr, carry_scr, act_scr, *, fchunk, token_major_out):
    s = pl.program_id(1)
    cur = s % 2
    prv = 1 - cur
    d_ff = wg_ref.shape[0]

    @pl.when(s == 0)
    def _():
        carry_scr[...] = jnp.zeros(carry_scr.shape, F32)
        h_scr[1] = jnp.zeros(h_scr.shape[1:], BF16)
        x1_scr[1] = jnp.zeros(x1_scr.shape[1:], F32)

    h = h_scr[prv]
    lane = lax.broadcasted_iota(jnp.int32, (fchunk, LANES), 1)
    for c in range(d_ff // fchunk):
        rs = slice(c * fchunk, (c + 1) * fchunk)
        g = jnp.dot(wg_ref[rs, :], h, preferred_element_type=F32)
        prev = carry_scr[rs, :]
        carry_scr[rs, :] = g[:, TILE - LANES:TILE]
        g1 = pltpu.roll(g, 1, 1)
        g2 = pltpu.roll(g, 2, 1)
        fix1 = jnp.where(lane < 1, pltpu.roll(prev, 1, 1), g1[:, 0:LANES])
        fix2 = jnp.where(lane < 2, pltpu.roll(prev, 2, 1), g2[:, 0:LANES])
        g1 = jnp.concatenate([fix1, g1[:, LANES:]], axis=1)
        g2 = jnp.concatenate([fix2, g2[:, LANES:]], axis=1)
        w0 = jnp.concatenate([cv_ref[0, rs, :]] * NSUB, axis=1)
        w1 = jnp.concatenate([cv_ref[1, rs, :]] * NSUB, axis=1)
        w2 = jnp.concatenate([cv_ref[2, rs, :]] * NSUB, axis=1)
        cb = jnp.concatenate([cv_ref[3, rs, :]] * NSUB, axis=1)
        acc = g2 * w0 + g1 * w1 + g * w2 + cb
        up = jnp.dot(wu_ref[rs, :], h, preferred_element_type=F32)
        act_scr[rs, :] = (jax.nn.gelu(acc, approximate=True) * up).astype(BF16)

    b_refs = (b0_ref, b1_ref, b2_ref)
    l_refs = (l0_ref, l1_ref, l2_ref)
    for hh in range(2):
        ls = [r[8 * hh:8 * hh + 1, :] for r in l_refs]
        mx = jnp.maximum(jnp.maximum(ls[0], ls[1]), ls[2])
        es = [jnp.exp(v - mx) for v in ls]
        inv = 1.0 / (es[0] + es[1] + es[2])
        for g in range(3):
            rs = slice(HEAD_DIM * hh, HEAD_DIM * (hh + 1))
            mix_scr[128 * g + HEAD_DIM * hh:128 * g + HEAD_DIM * (hh + 1), :] = (
                b_refs[g][rs, :] * (es[g] * inv)).astype(BF16)
    y1 = jnp.dot(wo_ref[:, 0:256], a_ref[...], preferred_element_type=F32)
    y1 = y1 + jnp.dot(wo_ref[:, 256:640], mix_scr[...], preferred_element_type=F32)
    y1 = y1 + jnp.dot(wo_ref[:, 640:1024], c_ref[...], preferred_element_type=F32)

    y2 = jnp.dot(wd_ref[...], act_scr[...], preferred_element_type=F32)

    _post_norm_residual(y1, x_ref, gw1_ref, x1_scr.at[cur])
    _norm_mod_to_scratch(x1_scr.at[cur], a2_ref, sh2_ref, h_scr.at[cur])
    _post_norm_residual(y2, x1_scr.at[prv], gw2_ref, o_ref, token_major_out)


def _mix_ffn(aT, bs, lses, cT, w_outT, xT, gw1, a2, sh2, wgT, wuT, wdT, cv, gw2, layer, token_major_out):
    B, D, S = xT.shape
    nt = S // TILE
    d_ff = wgT.shape[1]
    tile = lambda rows: pl.BlockSpec((None, rows, TILE), lambda b, s: (b, 0, jnp.minimum(s, nt - 1)))
    vec = pl.BlockSpec((None, D, LANES), lambda b, s: (b, 0, 0))
    return pl.pallas_call(
        functools.partial(_mix_ffn_kernel, fchunk=256, token_major_out=token_major_out),
        grid=(B, nt + 1),
        in_specs=[tile(256), tile(128), tile(128), tile(128), tile(16), tile(16), tile(16), tile(384),
                  _const_spec((None, D, D), lambda b, s: (layer, 0, 0)),
                  tile(D), vec, vec, vec,
                  _const_spec((None, d_ff, D), lambda b, s: (layer, 0, 0)),
                  _const_spec((None, d_ff, D), lambda b, s: (layer, 0, 0)),
                  _const_spec((None, D, d_ff), lambda b, s: (layer, 0, 0)),
                  _const_spec((None, 4, d_ff, LANES), lambda b, s: (layer, 0, 0, 0)),
                  vec],
        out_specs=(pl.BlockSpec((None, TILE, D), lambda b, s: (b, jnp.maximum(s - 1, 0), 0)) if token_major_out
                   else pl.BlockSpec((None, D, TILE), lambda b, s: (b, 0, jnp.maximum(s - 1, 0)))),
        out_shape=jax.ShapeDtypeStruct((B, S, D) if token_major_out else (B, D, S), F32),
        scratch_shapes=[pltpu.VMEM((384, TILE), BF16), pltpu.VMEM((2, D, TILE), F32),
                        pltpu.VMEM((2, D, TILE), BF16), pltpu.VMEM((d_ff, LANES), F32),
                        pltpu.VMEM((d_ff, TILE), BF16)],
        compiler_params=_cp(("arbitrary", "arbitrary")),
        name="mix_ffn",
    )(aT, bs[0], bs[1], bs[2], lses[0], lses[1], lses[2], cT, w_outT, xT, gw1,
      a2, sh2, wgT, wuT, wdT, cv, gw2)


def _band_bias(dilation, old_edge, reps, nprev, span):
    kk = np.arange(LANES)[:, None]
    qq = np.arange(LANES)[None, :]
    res = ((qq - kk) % dilation) == 0
    none = np.zeros_like(res)

    def role(ci):
        if ci < 0 or ci > nprev:
            return none
        return res & (kk - qq >= old_edge) if ci == 0 else (res & (kk <= qq) if ci == nprev else res)

    tabs = [np.concatenate([role(u - a) for a in range(span)], axis=1) for u in range(nprev + span)]
    zero = tuple(bool(t.all()) for t in tabs)
    tabs.append(np.concatenate([none] * span, axis=1))
    out = np.stack([np.where(t, 0.0, NEG) for t in tabs]).astype(np.float32)
    return jnp.asarray(np.tile(out, (1, 1, reps))), zero


def _cmp_bias():
    nn = np.arange(LANES)[:, None]
    qq = np.arange(LANES)[None, :]
    tabs = [np.where(CMP_STRIDE * nn + CMP_BLOCK - 1 <= LANES * r + qq, 0.0, NEG) for r in range(16)]
    return jnp.asarray(np.stack(tabs).astype(np.float32))


def _overlap_rows():
    jj = np.arange(40)[:, None]
    nn = np.arange(LANES)[None, :]
    return jnp.asarray(((nn >= 4 * jj - 1) & (nn <= 4 * jj + 3)).astype(np.float32), dtype=BF16)


_IN_COL_ORDER = ((1804, 2188), (652, 1036), (0, 256),
                 (256, 320), (320, 384), (384, 448), (512, 576), (1036, 1420), (2188, 2316),
                 (448, 512), (576, 640), (1420, 1804), (2316, 2444),
                 (640, 652))


def _prep_w_in(w_in):
    wt = jnp.swapaxes(jnp.concatenate([w_in[:, :, a:b] for a, b in _IN_COL_ORDER], axis=2), 1, 2)
    scale = np.ones((wt.shape[1], 1), np.float32)
    scale[:Q_ROWS] = HEAD_DIM ** -0.5
    wt = wt * scale
    wt = jnp.pad(wt, ((0, 0), (0, W_ROWS - wt.shape[1]), (0, 0)))
    return wt.astype(BF16)


def _prep_compress(w_ck, w_cv, pe_k, pe_v):
    L = w_ck.shape[0]
    half = CMP_BLOCK // 2

    def big(lo):
        wk = w_ck[:, lo:lo + half]
        wv = w_cv[:, lo:lo + half]
        z = jnp.zeros_like(wk)
        top = jnp.concatenate([wk, z], axis=-1)
        bot = jnp.concatenate([z, wv], axis=-1)
        return jnp.concatenate([top, bot], axis=2).reshape(L, half * LANES, LANES)

    wbig = jnp.concatenate([big(0), big(half)], axis=-1).astype(BF16)
    pe = jnp.concatenate([pe_k, pe_v], axis=-1)
    pe2 = jnp.zeros((L, 16, half * LANES), F32)
    pe2 = pe2.at[:, 0].set(pe[:, :half].reshape(L, -1)).at[:, 8].set(pe[:, half:].reshape(L, -1))
    return wbig, pe2.astype(BF16)


def _lane_bcast(v):
    return jnp.broadcast_to(v[..., None], v.shape + (LANES,))


def kernel(x, c, positions, w_in, w_out, w_ada, b_ada, norm_w, cmp_w_k, cmp_w_v, cmp_pe_k, cmp_pe_v,
           sinks, w_gate, w_up, conv_w, conv_b, w_down):
    B, S, D = x.shape
    depth = w_in.shape[0]
    assert S % 2048 == 0 and D == 1024 and w_in.shape[2] == 2444

    inv = ROPE_THETA ** (-jnp.arange(0, HEAD_DIM, 2, dtype=F32) / HEAD_DIM)
    ang = positions.astype(F32)[:, None, :] * inv[None, :, None]
    cosT, sinT = jnp.cos(ang), jnp.sin(ang)

    c8 = jnp.pad(c, ((0, 8 - B), (0, 0)))
    ada = _adaln(c8, w_ada, b_ada)[:, :B]
    sh1, sc1, g1, sh2, sc2, g2 = [ada[:, :, k * D:(k + 1) * D] for k in range(6)]
    nw = norm_w[:, :, None, :]
    a1 = _lane_bcast(nw[:, 0] * (1 + sc1))
    gw1 = _lane_bcast(g1 * nw[:, 1])
    a2 = _lane_bcast(nw[:, 2] * (1 + sc2))
    gw2 = _lane_bcast(g2 * nw[:, 3])
    sh1b, sh2b = _lane_bcast(sh1), _lane_bcast(sh2)

    w_inT = _prep_w_in(w_in)
    w_outT = jnp.swapaxes(w_out, 1, 2).astype(BF16)
    wgT = jnp.swapaxes(w_gate, 1, 2).astype(BF16)
    wuT = jnp.swapaxes(w_up, 1, 2).astype(BF16)
    wdT = jnp.swapaxes(w_down, 1, 2).astype(BF16)
    cv = _lane_bcast(jnp.concatenate([conv_w, conv_b[:, None, :]], axis=1))
    wbig, pe2 = _prep_compress(cmp_w_k, cmp_w_v, cmp_pe_k, cmp_pe_v)
    sink_tab = jnp.broadcast_to(
        jnp.repeat(sinks.reshape(depth, SWA_KV_HEADS, SWA_HEADS // SWA_KV_HEADS), LANES, axis=-1)[:, :, None, :],
        (depth, SWA_KV_HEADS, 8, LANES * (SWA_HEADS // SWA_KV_HEADS)))

    ov = _overlap_rows()
    cmp_bias = _cmp_bias()
    kk = np.arange(SEL_TQ)[:, None]
    causal4 = jnp.asarray(np.tile(np.where(kk <= kk.T, 0.0, NEG).astype(np.float32), (1, NSA_HEADS)))
    bias_win = _band_bias(1, 1, NSA_HEADS, NSA_WINDOW // LANES, 1)
    bias_swa = _band_bias(1, 1, SWA_HEADS // SWA_KV_HEADS, SWA_WINDOW // LANES, 1)
    dil_span = [2 if d >= NSUB else 1 for _, d in DIL_PATTERNS]
    bias_dil = [_band_bias(d, 0, 1, d, sp) for (_, d), sp in zip(DIL_PATTERNS, dil_span)]

    xT = x
    for l in range(depth):
        first, last_layer = l == 0, l == depth - 1
        outs = _inproj(xT, a1[l], sh1b[l], w_inT, l, cosT, sinT, token_major=first)
        qT, kcvc, nk, v4, gates = outs[:5]
        if first:
            xT = outs[5]
        kc4, vc4 = _compress(kcvc, wbig, pe2, l)
        ocmp, mm = _cmp_topk(qT, kc4, vc4, ov, cmp_bias)
        owin, = _banded(qT, A_Q_BLK, nk, 1, v4, 1, bias_win, hkv=1, grp=NSA_HEADS, nprev=NSA_WINDOW // LANES,
                        span=1, dynamic=False, out_dtype=F32, name="nsa_win")
        aT = _sel_attend(qT, mm, nk, v4, causal4, ocmp, owin, gates)
        bs, lses = [], []
        for gi, (win, dil) in enumerate(DIL_PATTERNS):
            o, lse = _banded(qT, 3 + gi, nk, 2 + gi, v4, 1 + gi, bias_dil[gi], hkv=2, grp=1, nprev=dil,
                             span=dil_span[gi], dynamic=dil > NSUB, out_dtype=F32, want_lse=True,
                             name="dil%d" % dil)
            bs.append(o)
            lses.append(lse)
        cT, = _banded(qT, 0, nk, 5, v4, 4, bias_swa, hkv=SWA_KV_HEADS, grp=SWA_HEADS // SWA_KV_HEADS,
                      nprev=SWA_WINDOW // LANES, span=1, dynamic=False, out_dtype=BF16,
                      sinks=sink_tab[l], name="swa")
        xT = _mix_ffn(aT, bs, lses, cT, w_outT, xT, gw1[l], a2[l], sh2b[l], wgT, wuT, wdT, cv, gw2[l], l,
                      token_major_out=last_layer)
    return xT
```

```python
import functools

import numpy as np
import jax
import jax.numpy as jnp
from jax import lax
from jax.experimental import pallas as pl
from jax.experimental.pallas import tpu as pltpu

F32 = jnp.float32
BF16 = jnp.bfloat16

HEAD_DIM = 64
HALF = HEAD_DIM // 2
NSA_HEADS = 4
CMP_BLOCK = 32
CMP_STRIDE = 16
SEL_BLOCK = 64
SEL_TOPK = 16
NSA_WINDOW = 512
DIL_PATTERNS = ((128, 1), (512, 4), (2048, 16))
DIL_HEADS = 6
SWA_HEADS = 6
SWA_KV_HEADS = 2
SWA_WINDOW = 128
ROPE_THETA = 10000.0
RMS_EPS = 1e-6
NEG = -1e30
FORCE = 1e4
CONV_WIDTH = 3

LANES = 128
TILE = 512
NSUB = TILE // LANES
BAND_TILE = 1024
BAND_NSUB = BAND_TILE // LANES
VROWS = HEAD_DIM + 16
MEMBER_BIG = 2.0 ** 100
VMEM_LIMIT = 56 * 1024 * 1024

Q_ROWS = 1024
A_Q_BLK = 3
NK_ROWS = 768
NK_LANES = 768
V_ROWS = 640
G_ROWS = 16
W_ROWS = Q_ROWS + NK_ROWS + V_ROWS + G_ROWS
N_VPIECES = V_ROWS // HEAD_DIM


def _cp(sem):
    return pltpu.CompilerParams(dimension_semantics=sem, vmem_limit_bytes=VMEM_LIMIT)


def _const_spec(shape, index_map):
    return pl.BlockSpec(shape, index_map, pipeline_mode=pl.Buffered(1))


def _adaln_kernel(c_ref, w_ref, b_ref, o_ref):
    c = c_ref[...]
    cond = c * jax.nn.sigmoid(c)
    o_ref[...] = jnp.dot(cond, w_ref[...], preferred_element_type=F32,
                         precision=lax.Precision.HIGHEST) + b_ref[...]


def _adaln(c8, w_ada, b_ada):
    depth, d, six_d = w_ada.shape
    nblk = six_d // d
    return pl.pallas_call(
        _adaln_kernel,
        grid=(depth, nblk),
        in_specs=[pl.BlockSpec((8, d), lambda l, n: (0, 0)),
                  pl.BlockSpec((None, d, d), lambda l, n: (l, 0, n)),
                  pl.BlockSpec((None, 1, d), lambda l, n: (l, 0, n))],
        out_specs=pl.BlockSpec((None, 8, d), lambda l, n: (l, 0, n)),
        out_shape=jax.ShapeDtypeStruct((depth, 8, six_d), F32),
        compiler_params=_cp(("parallel", "parallel")),
        name="adaln",
    )(c8, w_ada, b_ada.reshape(depth, 1, six_d))


def _norm_mod_to_scratch(x_ref, a_ref, sh_ref, h_scr):
    for c in range(NSUB):
        sl = slice(c * LANES, (c + 1) * LANES)
        xs = x_ref[:, sl]
        ms = jnp.mean(xs * xs, axis=0, keepdims=True)
        h_scr[:, sl] = ((xs * lax.rsqrt(ms + RMS_EPS)) * a_ref[...] + sh_ref[...]).astype(BF16)


def _inproj_kernel(x_ref, a_ref, sh_ref, w_ref, cos_ref, sin_ref,
                   q_ref, kcvc_ref, nk_ref, v4_ref, gate_ref, *rest, token_major):
    j = pl.program_id(1)
    if token_major:
        xt_ref, h_scr = rest
        for c in range(NSUB):
            xt_ref[:, c * LANES:(c + 1) * LANES] = x_ref[c * LANES:(c + 1) * LANES, :].T
        x_ref = xt_ref
    else:
        h_scr, = rest
    _norm_mod_to_scratch(x_ref, a_ref, sh_ref, h_scr)
    h = h_scr[...]
    cos = cos_ref[...]
    sin = sin_ref[...]

    def proj(r0, r1):
        return jnp.dot(w_ref[r0:r1, :], h, preferred_element_type=F32)

    def rope(r, nh):
        outs = []
        for hh in range(nh):
            t1 = r[HEAD_DIM * hh:HEAD_DIM * hh + HALF]
            t2 = r[HEAD_DIM * hh + HALF:HEAD_DIM * (hh + 1)]
            outs.append(t1 * cos - t2 * sin)
            outs.append(t2 * cos + t1 * sin)
        return jnp.concatenate(outs, axis=0)

    for r0, r1 in ((0, 384), (384, 768), (768, 1024)):
        q_ref[r0:r1, :] = rope(proj(r0, r1), (r1 - r0) // HEAD_DIM).astype(BF16)

    base = Q_ROWS
    r = proj(base, base + 128)
    kcvc = jnp.concatenate([rope(r[0:64], 1), r[64:128]], axis=0)
    kcvc_ref[...] = kcvc.T.astype(BF16)

    r = proj(base + 128, base + 192)
    tok = j * TILE + lax.broadcasted_iota(jnp.int32, (HEAD_DIM, TILE), 1)
    row = lax.broadcasted_iota(jnp.int32, (HEAD_DIM, TILE), 0)
    member_cols = jnp.where(row == ((tok >> 6) & 15), MEMBER_BIG, 0.0).astype(F32)
    nk_ref[:, 0:128] = jnp.concatenate([rope(r, 1), member_cols], axis=0).T.astype(BF16)

    r = proj(base + 192, base + 256)
    nk_ref[:, 128:256] = jnp.concatenate([rope(r, 1), jnp.zeros((HEAD_DIM, TILE), F32)], axis=0).T.astype(BF16)

    r = proj(base + 256, base + 640)
    nk_ref[:, 256:640] = rope(r, 6).T.astype(BF16)

    r = proj(base + 640, base + 768)
    nk_ref[:, 640:768] = rope(r, 2).T.astype(BF16)

    base = Q_ROWS + NK_ROWS
    r = proj(base, base + V_ROWS).astype(BF16)
    ones = jnp.ones((VROWS - HEAD_DIM, LANES), BF16)
    for c in range(NSUB):
        for p in range(N_VPIECES):
            v4_ref[c, VROWS * p:VROWS * p + HEAD_DIM, :] = r[HEAD_DIM * p:HEAD_DIM * (p + 1),
                                                             c * LANES:(c + 1) * LANES]
            v4_ref[c, VROWS * p + HEAD_DIM:VROWS * (p + 1), :] = ones

    base = Q_ROWS + NK_ROWS + V_ROWS
    gate_ref[...] = jax.nn.sigmoid(proj(base, base + G_ROWS))


def _inproj(x, a1, sh1, w_inT, layer, cosT, sinT, token_major):
    if token_major:
        B, S, D = x.shape
        x_spec = pl.BlockSpec((None, TILE, D), lambda b, j: (b, j, 0))
    else:
        B, D, S = x.shape
        x_spec = pl.BlockSpec((None, D, TILE), lambda b, j: (b, 0, j))
    nt = S // TILE
    out_specs = [pl.BlockSpec((None, Q_ROWS, TILE), lambda b, j: (b, 0, j)),
                 pl.BlockSpec((None, TILE, LANES), lambda b, j: (b, j, 0)),
                 pl.BlockSpec((None, TILE, NK_LANES), lambda b, j: (b, j, 0)),
                 pl.BlockSpec((None, NSUB, N_VPIECES * VROWS, LANES), lambda b, j: (b, j, 0, 0)),
                 pl.BlockSpec((None, G_ROWS, TILE), lambda b, j: (b, 0, j))]
    out_shape = [jax.ShapeDtypeStruct((B, Q_ROWS, S), BF16),
                 jax.ShapeDtypeStruct((B, S, LANES), BF16),
                 jax.ShapeDtypeStruct((B, S, NK_LANES), BF16),
                 jax.ShapeDtypeStruct((B, S // LANES, N_VPIECES * VROWS, LANES), BF16),
                 jax.ShapeDtypeStruct((B, G_ROWS, S), F32)]
    if token_major:
        out_specs.append(pl.BlockSpec((None, D, TILE), lambda b, j: (b, 0, j)))
        out_shape.append(jax.ShapeDtypeStruct((B, D, S), F32))
    return pl.pallas_call(
        functools.partial(_inproj_kernel, token_major=token_major),
        grid=(B, nt),
        in_specs=[x_spec,
                  pl.BlockSpec((None, D, LANES), lambda b, j: (b, 0, 0)),
                  pl.BlockSpec((None, D, LANES), lambda b, j: (b, 0, 0)),
                  _const_spec((None, W_ROWS, D), lambda b, j: (layer, 0, 0)),
                  pl.BlockSpec((None, HALF, TILE), lambda b, j: (b, 0, j)),
                  pl.BlockSpec((None, HALF, TILE), lambda b, j: (b, 0, j))],
        out_specs=out_specs,
        out_shape=out_shape,
        scratch_shapes=[pltpu.VMEM((D, TILE), BF16)],
        compiler_params=_cp(("parallel", "parallel")),
        name="inproj",
    )(x, a1, sh1, w_inT, cosT, sinT)


def _compress_kernel(t_ref, w_ref, pe_ref, kc_ref, vc_ref):
    n = t_ref.shape[0]
    a = jnp.dot(t_ref[...], w_ref[...], preferred_element_type=F32)
    pc = jnp.dot(pe_ref[...], w_ref[...], preferred_element_type=F32)
    const = pc[0:1, 0:LANES] + pc[8:9, LANES:2 * LANES]
    cmp = a[:, 0:LANES] + pltpu.roll(a[:, LANES:2 * LANES], n - 1, 0) + const
    cmp_t = cmp.T
    ones = jnp.ones((VROWS - HEAD_DIM, LANES), BF16)
    for c in range(n // LANES):
        kc_ref[c] = cmp[c * LANES:(c + 1) * LANES].astype(BF16)
        vc_ref[c, 0:HEAD_DIM, :] = cmp_t[HEAD_DIM:2 * HEAD_DIM, c * LANES:(c + 1) * LANES].astype(BF16)
        vc_ref[c, HEAD_DIM:VROWS, :] = ones


def _compress(kcvc, wbig, pe2, layer):
    B, S, _ = kcvc.shape
    n = S // CMP_STRIDE
    nch = n // LANES
    tview = kcvc.reshape(B, n, CMP_STRIDE * LANES)
    return pl.pallas_call(
        _compress_kernel,
        grid=(B,),
        in_specs=[pl.BlockSpec((None, n, CMP_STRIDE * LANES), lambda b: (b, 0, 0)),
                  pl.BlockSpec((None, CMP_STRIDE * LANES, 2 * LANES), lambda b: (layer, 0, 0)),
                  pl.BlockSpec((None, 16, CMP_STRIDE * LANES), lambda b: (layer, 0, 0))],
        out_specs=[pl.BlockSpec((None, nch, LANES, LANES), lambda b: (b, 0, 0, 0)),
                   pl.BlockSpec((None, nch, VROWS, LANES), lambda b: (b, 0, 0, 0))],
        out_shape=[jax.ShapeDtypeStruct((B, nch, LANES, LANES), BF16),
                   jax.ShapeDtypeStruct((B, nch, VROWS, LANES), BF16)],
        compiler_params=_cp(("parallel",)),
        name="compress",
    )(tview, wbig, pe2)


def _stack_heads(q_ref, nh, lane_slice=slice(None)):
    return jnp.concatenate([q_ref[HEAD_DIM * h:HEAD_DIM * (h + 1), lane_slice] for h in range(nh)], axis=1)


CMP_UNITS = 2
CMP_TQ = CMP_UNITS * LANES


def _cmp_kernel(q_ref, kc_ref, vc_ref, ov_ref, bias_ref, o_ref, mm_ref, imp_scr, *, n_sel, nstep):
    i = pl.program_id(1)
    nch_all = kc_ref.shape[0]
    parts = 4 if nch_all % 4 == 0 else (2 if nch_all % 2 == 0 else 1)
    for v in range(parts):
        pl.when(i // (nstep // parts) == v)(functools.partial(
            _cmp_body, q_ref, kc_ref, vc_ref, ov_ref, bias_ref, o_ref, mm_ref, imp_scr,
            nch=nch_all * (v + 1) // parts, nrows=n_sel * (v + 1) // parts, n_sel=n_sel))


def _cmp_body(q_ref, kc_ref, vc_ref, ov_ref, bias_ref, o_ref, mm_ref, imp_scr, *, nch, nrows, n_sel):
    i = pl.program_id(1)
    cd = (i * CMP_UNITS) // 16

    def scores(u):
        q = _stack_heads(q_ref, NSA_HEADS, slice(u * LANES, (u + 1) * LANES))
        edge = bias_ref[u]
        ss = []
        for c in range(nch):
            b = jnp.where(c < cd, 0.0, jnp.where(c == cd, edge, NEG))
            s = jnp.dot(kc_ref[c, :, 0:HEAD_DIM], q, preferred_element_type=F32)
            ss.append(s + jnp.concatenate([b] * NSA_HEADS, axis=1))
        return ss

    def finish(u, ss):
        lanes = slice(u * LANES, (u + 1) * LANES)
        m = jnp.max(functools.reduce(jnp.maximum, ss), axis=0, keepdims=True)
        valid = m > 0.5 * NEG
        imp_scr[u, 0:32 * nch + 8, :] = jnp.zeros((32 * nch + 8, imp_scr.shape[2]), F32)
        acc = None
        for c in range(nch):
            p = jnp.exp(ss[c] - m).astype(BF16)
            t = jnp.dot(vc_ref[c], p, preferred_element_type=F32)
            acc = t if acc is None else acc + t
            imp_scr[u, 32 * c:32 * c + 40, :] += jnp.dot(ov_ref[...], p, preferred_element_type=F32)
        inv = jnp.where(valid, 1.0 / acc[HEAD_DIM:HEAD_DIM + 1], 0.0)
        o = acc[0:HEAD_DIM] * inv
        for h in range(NSA_HEADS):
            o_ref[HEAD_DIM * h:HEAD_DIM * (h + 1), lanes] = o[:, h * LANES:(h + 1) * LANES]
        imp = jnp.zeros((nrows, LANES), F32)
        for h in range(NSA_HEADS):
            sl = slice(h * LANES, (h + 1) * LANES)
            imp = imp + imp_scr[u, 0:nrows, sl] * inv[:, sl]
        return imp

    pending = scores(0)
    imps = []
    for u in range(1, CMP_UNITS):
        nxt = scores(u)
        imps.append(finish(u - 1, pending))
        pending = nxt
    imps.append(finish(CMP_UNITS - 1, pending))

    blk = lax.broadcasted_iota(jnp.int32, (nrows, LANES), 0).astype(F32)

    def pick_one(imp):
        mx = jnp.max(imp, axis=0, keepdims=True)
        first = jnp.min(jnp.where(imp == mx, blk, float(nrows)), axis=0, keepdims=True)
        return jnp.where(blk == first, -jnp.inf, imp)

    curs, cands = [], []
    for u in range(CMP_UNITS):
        t = i * CMP_TQ + u * LANES + lax.broadcasted_iota(jnp.int32, (nrows, LANES), 1)
        cur = (t >> 6).astype(F32)
        forced = (blk == 0.0) | (blk == cur) | (blk == cur - 1.0)
        imp = jnp.where(forced, FORCE, imps[u])
        curs.append(cur)
        cands.append(jnp.where(blk <= cur, imp, NEG))
    cands = lax.fori_loop(0, min(SEL_TOPK, nrows), lambda _, c: tuple(pick_one(x) for x in c), tuple(cands))
    for u in range(CMP_UNITS):
        member = (cands[u] == -jnp.inf) & (blk <= curs[u])
        mm_ref[0:nrows, u * LANES:(u + 1) * LANES] = jnp.where(member, 0.0, -1.0).astype(BF16)
    if nrows < n_sel:
        mm_ref[nrows:n_sel, :] = jnp.full((n_sel - nrows, CMP_TQ), -1.0, BF16)


def _cmp_topk(qT, kc4, vc4, ov, cmp_bias):
    B, _, S = qT.shape
    nstep = S // CMP_TQ
    nch = kc4.shape[1]
    n_sel = S // SEL_BLOCK
    nq = NSA_HEADS * LANES
    nbias = cmp_bias.shape[0] // CMP_UNITS
    return pl.pallas_call(
        functools.partial(_cmp_kernel, n_sel=n_sel, nstep=nstep),
        grid=(B, nstep),
        in_specs=[pl.BlockSpec((None, NSA_HEADS * HEAD_DIM, CMP_TQ), lambda b, i: (b, A_Q_BLK, i)),
                  pl.BlockSpec((None, nch, LANES, LANES), lambda b, i: (b, 0, 0, 0)),
                  pl.BlockSpec((None, nch, VROWS, LANES), lambda b, i: (b, 0, 0, 0)),
                  pl.BlockSpec((40, LANES), lambda b, i: (0, 0)),
                  pl.BlockSpec((None, CMP_UNITS, LANES, LANES), lambda b, i: (i % nbias, 0, 0, 0))],
        out_specs=[pl.BlockSpec((None, NSA_HEADS * HEAD_DIM, CMP_TQ), lambda b, i: (b, 0, i)),
                   pl.BlockSpec((None, n_sel, CMP_TQ), lambda b, i: (b, 0, i))],
        out_shape=[jax.ShapeDtypeStruct((B, NSA_HEADS * HEAD_DIM, S), F32),
                   jax.ShapeDtypeStruct((B, n_sel, S), BF16)],
        scratch_shapes=[pltpu.VMEM((CMP_UNITS, 32 * nch + 64, nq), F32)],
        compiler_params=_cp(("parallel", "parallel")),
        name="nsa_cmp_topk",
    )(qT, kc4, vc4, ov, cmp_bias.reshape(nbias, CMP_UNITS, LANES, LANES))


SEL_GROUP = 8
SEL_TQ = 2 * LANES


def _sel_kernel(q_ref, mm_ref, ks_ref, v_ref, causal_ref, ocmp_ref, owin_ref, gate_ref, out_ref,
                qa_scr, mm_scr, s_scr):
    i = pl.program_id(1)
    nq = NSA_HEADS * SEL_TQ
    qa_scr[0:HEAD_DIM, :] = _stack_heads(q_ref, NSA_HEADS)
    qa_scr[HEAD_DIM:LANES, :] = jnp.zeros((LANES - HEAD_DIM, nq), BF16)
    mm = mm_ref[...]
    for h in range(NSA_HEADS):
        mm_scr[:, h * SEL_TQ:(h + 1) * SEL_TQ] = mm

    gkeys = SEL_GROUP * LANES

    def load_query(gi):
        rg = (gi * SEL_GROUP) // 8
        qa_scr[HEAD_DIM:HEAD_DIM + 16, :] = mm_scr[pl.ds(pl.multiple_of(rg * 16, 16), 16), :]
        return qa_scr[...]

    def chunk_scores(gi, u, qa):
        row = pl.multiple_of((gi * SEL_GROUP + u) * LANES, LANES)
        return jnp.dot(ks_ref[pl.ds(row, LANES), :], qa, preferred_element_type=F32)

    def weighted_values(gi, ps):
        vcat = jnp.concatenate([v_ref[gi * SEL_GROUP + u] for u in range(SEL_GROUP)], axis=1)
        return jnp.dot(vcat, jnp.concatenate(ps, axis=0), preferred_element_type=F32)

    def body(gi, carry):
        m, acc, mg = carry
        m_new = jnp.maximum(m, mg)
        qa = load_query(gi + 1)
        ps, mx = [], None
        for u in range(SEL_GROUP):
            rows = slice(u * LANES, (u + 1) * LANES)
            ps.append(jnp.exp(s_scr[rows, :] - m_new).astype(BF16))
            nxt = chunk_scores(gi + 1, u, qa)
            s_scr[rows, :] = nxt
            mx = nxt if mx is None else jnp.maximum(mx, nxt)
        acc = acc * jnp.exp(m - m_new) + weighted_values(gi, ps)
        return m_new, acc, jnp.max(mx, axis=0, keepdims=True)

    qa = load_query(0)
    mx = None
    for u in range(SEL_GROUP):
        s0 = chunk_scores(0, u, qa)
        s_scr[u * LANES:(u + 1) * LANES, :] = s0
        mx = s0 if mx is None else jnp.maximum(mx, s0)
    c0 = i * (SEL_TQ // LANES)
    last = c0 // SEL_GROUP
    m, acc, _ = lax.fori_loop(0, last, body, (jnp.full((1, nq), NEG, F32), jnp.zeros((VROWS, nq), F32),
                                              jnp.max(mx, axis=0, keepdims=True)))
    drow = pl.multiple_of((c0 % SEL_GROUP) * LANES, SEL_TQ)
    s_scr[pl.ds(drow, SEL_TQ), :] += causal_ref[...]
    s = s_scr[...]
    m_new = jnp.maximum(m, jnp.max(s, axis=0, keepdims=True))
    acc = acc * jnp.exp(m - m_new) + weighted_values(last, [jnp.exp(s - m_new).astype(BF16)])
    o = acc[0:HEAD_DIM] * (1.0 / acc[HEAD_DIM:HEAD_DIM + 1])

    g = gate_ref[...]
    for h in range(NSA_HEADS):
        rs = slice(HEAD_DIM * h, HEAD_DIM * (h + 1))
        out = (g[3 * h:3 * h + 1] * ocmp_ref[rs, :] + g[3 * h + 1:3 * h + 2] * o[:, h * SEL_TQ:(h + 1) * SEL_TQ]
               + g[3 * h + 2:3 * h + 3] * owin_ref[rs, :])
        out_ref[rs, :] = out.astype(BF16)


def _sel_attend(qT, mm, nk, v4, causal4, ocmp, owin, gates):
    B, _, S = qT.shape
    nstep = S // SEL_TQ
    n_sel = S // SEL_BLOCK
    nq = NSA_HEADS * SEL_TQ
    ar = NSA_HEADS * HEAD_DIM
    return pl.pallas_call(
        _sel_kernel,
        grid=(B, nstep),
        in_specs=[pl.BlockSpec((None, ar, SEL_TQ), lambda b, i: (b, A_Q_BLK, i)),
                  pl.BlockSpec((None, n_sel, SEL_TQ), lambda b, i: (b, 0, i)),
                  pl.BlockSpec((None, S, LANES), lambda b, i: (b, 0, 0)),
                  pl.BlockSpec((None, S // LANES, VROWS, LANES), lambda b, i: (b, 0, 0, 0)),
                  _const_spec((SEL_TQ, nq), lambda b, i: (0, 0)),
                  pl.BlockSpec((None, ar, SEL_TQ), lambda b, i: (b, 0, i)),
                  pl.BlockSpec((None, ar, SEL_TQ), lambda b, i: (b, 0, i)),
                  pl.BlockSpec((None, G_ROWS, SEL_TQ), lambda b, i: (b, 0, i))],
        out_specs=pl.BlockSpec((None, ar, SEL_TQ), lambda b, i: (b, 0, i)),
        out_shape=jax.ShapeDtypeStruct((B, ar, S), BF16),
        scratch_shapes=[pltpu.VMEM((LANES, nq), BF16), pltpu.VMEM((n_sel, nq), BF16),
                        pltpu.VMEM((SEL_GROUP * LANES, nq), F32)],
        compiler_params=_cp(("parallel", "parallel")),
        name="nsa_sel",
    )(qT, mm, nk, v4, causal4, ocmp, owin, gates)


def _banded_kernel(*refs, hkv, grp, nprev, span, zero_bias, dynamic, has_sink, want_lse):
    q_ref, k_ref, v_ref, bias_ref = refs[:4]
    pos = 4
    sink_ref = None
    if has_sink:
        sink_ref = refs[pos]
        pos += 1
    o_ref = refs[pos]
    lse_ref = refs[pos + 1] if want_lse else None
    j = pl.program_id(1)

    width = span * LANES
    nchunk = nprev + span

    def scores(s, g, first):
        lanes = slice(s * width, (s + 1) * width)
        qg = jnp.concatenate([q_ref[HEAD_DIM * (g * grp + u):HEAD_DIM * (g * grp + u + 1), lanes]
                              for u in range(grp)], axis=1)
        parts = []
        for ci in range(nchunk):
            if first:
                kc = s * span - nprev + ci
                if kc < 0:
                    continue
                kcc = kc
                row = kc * LANES
            else:
                kc = j * BAND_NSUB + s * span - nprev + ci
                kcc = jnp.maximum(kc, 0) if dynamic else kc
                row = pl.multiple_of(kcc * LANES, LANES)
            sc = jnp.dot(k_ref[pl.ds(row, LANES), HEAD_DIM * g:HEAD_DIM * (g + 1)], qg,
                         preferred_element_type=F32)
            if dynamic:
                sc = sc + bias_ref[jnp.where(kc >= 0, ci, nchunk)]
            elif not zero_bias[ci]:
                sc = sc + bias_ref[ci]
            parts.append((kcc, sc))
        return parts

    def finish(s, g, parts):
        lanes = slice(s * width, (s + 1) * width)
        m = None
        for _, sc in parts:
            mc = jnp.max(sc, axis=0, keepdims=True)
            m = mc if m is None else jnp.maximum(m, mc)
        if has_sink:
            sk = sink_ref[g, 0:1, :]
            m = jnp.maximum(m, sk)
        acc = None
        for kcc, sc in parts:
            t = jnp.dot(v_ref[kcc, VROWS * g:VROWS * (g + 1), :], jnp.exp(sc - m).astype(BF16),
                        preferred_element_type=F32)
            acc = t if acc is None else acc + t
        l = acc[HEAD_DIM:HEAD_DIM + 1]
        if has_sink:
            l = l + jnp.exp(sk - m)
        o = acc[0:HEAD_DIM] * (1.0 / l)
        for u in range(grp):
            hq = g * grp + u
            o_ref[HEAD_DIM * hq:HEAD_DIM * (hq + 1), lanes] = o[:, u * width:(u + 1) * width].astype(o_ref.dtype)
        if want_lse:
            lse_ref[8 * g:8 * (g + 1), lanes] = jnp.broadcast_to(m + jnp.log(l), (8, width))

    def run(first):
        pending = None
        for s in range(BAND_NSUB // span):
            for g in range(hkv):
                parts = scores(s, g, first)
                if pending is not None:
                    finish(*pending)
                pending = (s, g, parts)
        finish(*pending)

    if dynamic:
        run(False)
    else:
        pl.when(j == 0)(lambda: run(True))
        pl.when(j > 0)(lambda: run(False))


def _banded(qT, q_blk, nk, k_blk, v4, v_blk, bias_and_zero, *, hkv, grp, nprev, span, dynamic,
            out_dtype, sinks=None, want_lse=False, name):
    bias, zero_bias = bias_and_zero
    assert span == 1 or (grp == 1 and sinks is None)
    B, _, S = qT.shape
    nt = S // BAND_TILE
    qrows = hkv * grp * HEAD_DIM
    in_specs = [pl.BlockSpec((None, qrows, BAND_TILE), lambda b, j: (b, q_blk, j)),
                pl.BlockSpec((None, S, LANES), lambda b, j: (b, 0, k_blk)),
                pl.BlockSpec((None, S // LANES, hkv * VROWS, LANES), lambda b, j: (b, 0, v_blk, 0)),
                pl.BlockSpec(bias.shape, lambda b, j: (0, 0, 0))]
    args = [qT, nk, v4, bias]
    if sinks is not None:
        in_specs.append(pl.BlockSpec(sinks.shape, lambda b, j: (0, 0, 0)))
        args.append(sinks)
    out_specs = [pl.BlockSpec((None, qrows, BAND_TILE), lambda b, j: (b, 0, j))]
    out_shape = [jax.ShapeDtypeStruct((B, qrows, S), out_dtype)]
    if want_lse:
        out_specs.append(pl.BlockSpec((None, 8 * hkv, BAND_TILE), lambda b, j: (b, 0, j)))
        out_shape.append(jax.ShapeDtypeStruct((B, 8 * hkv, S), F32))
    return pl.pallas_call(
        functools.partial(_banded_kernel, hkv=hkv, grp=grp, nprev=nprev, span=span, zero_bias=zero_bias,
                          dynamic=dynamic, has_sink=sinks is not None, want_lse=want_lse),
        grid=(B, nt),
        in_specs=in_specs,
        out_specs=out_specs,
        out_shape=out_shape,
        compiler_params=_cp(("parallel", "parallel")),
        name=name,
    )(*args)


def _post_norm_residual(y, x_ref, gw_ref, o_ref, token_major=False):
    ms = jnp.mean(y * y, axis=0, keepdims=True)
    yn = y * lax.rsqrt(ms + RMS_EPS)
    gw = gw_ref[...]
    for c in range(NSUB):
        sl = slice(c * LANES, (c + 1) * LANES)
        out = x_ref[:, sl] + gw * yn[:, sl]
        if token_major:
            o_ref[sl, :] = out.T
        else:
            o_ref[:, sl] = out


def _mix_ffn_kernel(a_ref, b0_ref, b1_ref, b2_ref, l0_ref, l1_ref, l2_ref, c_ref, wo_ref, x_ref, gw1_ref,
                    a2_ref, sh2_ref, wg_ref, wu_ref, wd_ref, cv_ref, gw2_ref, o_ref,
                    mix_scr, x1_scr, h_scr, carry_scr, act_scr, *, fchunk, token_major_out):
    s = pl.program_id(1)
    cur = s % 2
    prv = 1 - cur
    d_ff = wg_ref.shape[0]

    @pl.when(s == 0)
    def _():
        carry_scr[...] = jnp.zeros(carry_scr.shape, F32)
        h_scr[1] = jnp.zeros(h_scr.shape[1:], BF16)
        x1_scr[1] = jnp.zeros(x1_scr.shape[1:], F32)

    h = h_scr[prv]
    lane = lax.broadcasted_iota(jnp.int32, (fchunk, LANES), 1)
    for c in range(d_ff // fchunk):
        rs = slice(c * fchunk, (c + 1) * fchunk)
        g = jnp.dot(wg_ref[rs, :], h, preferred_element_type=F32)
        prev = carry_scr[rs, :]
        carry_scr[rs, :] = g[:, TILE - LANES:TILE]
        g1 = pltpu.roll(g, 1, 1)
        g2 = pltpu.roll(g, 2, 1)
        fix1 = jnp.where(lane < 1, pltpu.roll(prev, 1, 1), g1[:, 0:LANES])
        fix2 = jnp.where(lane < 2, pltpu.roll(prev, 2, 1), g2[:, 0:LANES])
        g1 = jnp.concatenate([fix1, g1[:, LANES:]], axis=1)
        g2 = jnp.concatenate([fix2, g2[:, LANES:]], axis=1)
        w0 = jnp.concatenate([cv_ref[0, rs, :]] * NSUB, axis=1)
        w1 = jnp.concatenate([cv_ref[1, rs, :]] * NSUB, axis=1)
        w2 = jnp.concatenate([cv_ref[2, rs, :]] * NSUB, axis=1)
        cb = jnp.concatenate([cv_ref[3, rs, :]] * NSUB, axis=1)
        acc = g2 * w0 + g1 * w1 + g * w2 + cb
        up = jnp.dot(wu_ref[rs, :], h, preferred_element_type=F32)
        act_scr[rs, :] = (jax.nn.gelu(acc, approximate=True) * up).astype(BF16)

    b_refs = (b0_ref, b1_ref, b2_ref)
    l_refs = (l0_ref, l1_ref, l2_ref)
    for hh in range(2):
        ls = [r[8 * hh:8 * hh + 1, :] for r in l_refs]
        mx = jnp.maximum(jnp.maximum(ls[0], ls[1]), ls[2])
        es = [jnp.exp(v - mx) for v in ls]
        inv = 1.0 / (es[0] + es[1] + es[2])
        for g in range(3):
            rs = slice(HEAD_DIM * hh, HEAD_DIM * (hh + 1))
            mix_scr[128 * g + HEAD_DIM * hh:128 * g + HEAD_DIM * (hh + 1), :] = (
                b_refs[g][rs, :] * (es[g] * inv)).astype(BF16)
    y1 = jnp.dot(wo_ref[:, 0:256], a_ref[...], preferred_element_type=F32)
    y1 = y1 + jnp.dot(wo_ref[:, 256:640], mix_scr[...], preferred_element_type=F32)
    y1 = y1 + jnp.dot(wo_ref[:, 640:1024], c_ref[...], preferred_element_type=F32)

    y2 = jnp.dot(wd_ref[...], act_scr[...], preferred_element_type=F32)

    _post_norm_residual(y1, x_ref, gw1_ref, x1_scr.at[cur])
    _norm_mod_to_scratch(x1_scr.at[cur], a2_ref, sh2_ref, h_scr.at[cur])
    _post_norm_residual(y2, x1_scr.at[prv], gw2_ref, o_ref, token_major_out)


def _mix_ffn(aT, bs, lses, cT, w_outT, xT, gw1, a2, sh2, wgT, wuT, wdT, cv, gw2, layer, token_major_out):
    B, D, S = xT.shape
    nt = S // TILE
    d_ff = wgT.shape[1]
    tile = lambda rows: pl.BlockSpec((None, rows, TILE), lambda b, s: (b, 0, jnp.minimum(s, nt - 1)))
    vec = pl.BlockSpec((None, D, LANES), lambda b, s: (b, 0, 0))
    return pl.pallas_call(
        functools.partial(_mix_ffn_kernel, fchunk=256, token_major_out=token_major_out),
        grid=(B, nt + 1),
        in_specs=[tile(256), tile(128), tile(128), tile(128), tile(16), tile(16), tile(16), tile(384),
                  _const_spec((None, D, D), lambda b, s: (layer, 0, 0)),
                  tile(D), vec, vec, vec,
                  _const_spec((None, d_ff, D), lambda b, s: (layer, 0, 0)),
                  _const_spec((None, d_ff, D), lambda b, s: (layer, 0, 0)),
                  _const_spec((None, D, d_ff), lambda b, s: (layer, 0, 0)),
                  _const_spec((None, 4, d_ff, LANES), lambda b, s: (layer, 0, 0, 0)),
                  vec],
        out_specs=(pl.BlockSpec((None, TILE, D), lambda b, s: (b, jnp.maximum(s - 1, 0), 0)) if token_major_out
                   else pl.BlockSpec((None, D, TILE), lambda b, s: (b, 0, jnp.maximum(s - 1, 0)))),
        out_shape=jax.ShapeDtypeStruct((B, S, D) if token_major_out else (B, D, S), F32),
        scratch_shapes=[pltpu.VMEM((384, TILE), BF16), pltpu.VMEM((2, D, TILE), F32),
                        pltpu.VMEM((2, D, TILE), BF16), pltpu.VMEM((d_ff, LANES), F32),
                        pltpu.VMEM((d_ff, TILE), BF16)],
        compiler_params=_cp(("arbitrary", "arbitrary")),
        name="mix_ffn",
    )(aT, bs[0], bs[1], bs[2], lses[0], lses[1], lses[2], cT, w_outT, xT, gw1,
      a2, sh2, wgT, wuT, wdT, cv, gw2)


def _band_bias(dilation, old_edge, reps, nprev, span):
    kk = np.arange(LANES)[:, None]
    qq = np.arange(LANES)[None, :]
    res = ((qq - kk) % dilation) == 0
    none = np.zeros_like(res)

    def role(ci):
        if ci < 0 or ci > nprev:
            return none
        return res & (kk - qq >= old_edge) if ci == 0 else (res & (kk <= qq) if ci == nprev else res)

    tabs = [np.concatenate([role(u - a) for a in range(span)], axis=1) for u in range(nprev + span)]
    zero = tuple(bool(t.all()) for t in tabs)
    tabs.append(np.concatenate([none] * span, axis=1))
    out = np.stack([np.where(t, 0.0, NEG) for t in tabs]).astype(np.float32)
    return jnp.asarray(np.tile(out, (1, 1, reps))), zero


def _cmp_bias():
    nn = np.arange(LANES)[:, None]
    qq = np.arange(LANES)[None, :]
    tabs = [np.where(CMP_STRIDE * nn + CMP_BLOCK - 1 <= LANES * r + qq, 0.0, NEG) for r in range(16)]
    return jnp.asarray(np.stack(tabs).astype(np.float32))


def _overlap_rows():
    jj = np.arange(40)[:, None]
    nn = np.arange(LANES)[None, :]
    return jnp.asarray(((nn >= 4 * jj - 1) & (nn <= 4 * jj + 3)).astype(np.float32), dtype=BF16)


_IN_COL_ORDER = ((1804, 2188), (652, 1036), (0, 256),
                 (256, 320), (320, 384), (384, 448), (512, 576), (1036, 1420), (2188, 2316),
                 (448, 512), (576, 640), (1420, 1804), (2316, 2444),
                 (640, 652))


def _prep_w_in(w_in):
    wt = jnp.swapaxes(jnp.concatenate([w_in[:, :, a:b] for a, b in _IN_COL_ORDER], axis=2), 1, 2)
    scale = np.ones((wt.shape[1], 1), np.float32)
    scale[:Q_ROWS] = HEAD_DIM ** -0.5
    wt = wt * scale
    wt = jnp.pad(wt, ((0, 0), (0, W_ROWS - wt.shape[1]), (0, 0)))
    return wt.astype(BF16)


def _prep_compress(w_ck, w_cv, pe_k, pe_v):
    L = w_ck.shape[0]
    half = CMP_BLOCK // 2

    def big(lo):
        wk = w_ck[:, lo:lo + half]
        wv = w_cv[:, lo:lo + half]
        z = jnp.zeros_like(wk)
        top = jnp.concatenate([wk, z], axis=-1)
        bot = jnp.concatenate([z, wv], axis=-1)
        return jnp.concatenate([top, bot], axis=2).reshape(L, half * LANES, LANES)

    wbig = jnp.concatenate([big(0), big(half)], axis=-1).astype(BF16)
    pe = jnp.concatenate([pe_k, pe_v], axis=-1)
    pe2 = jnp.zeros((L, 16, half * LANES), F32)
    pe2 = pe2.at[:, 0].set(pe[:, :half].reshape(L, -1)).at[:, 8].set(pe[:, half:].reshape(L, -1))
    return wbig, pe2.astype(BF16)


def _lane_bcast(v):
    return jnp.broadcast_to(v[..., None], v.shape + (LANES,))


def kernel(x, c, positions, w_in, w_out, w_ada, b_ada, norm_w, cmp_w_k, cmp_w_v, cmp_pe_k, cmp_pe_v,
           sinks, w_gate, w_up, conv_w, conv_b, w_down):
    B, S, D = x.shape
    depth = w_in.shape[0]
    assert S % 2048 == 0 and D == 1024 and w_in.shape[2] == 2444

    inv = ROPE_THETA ** (-jnp.arange(0, HEAD_DIM, 2, dtype=F32) / HEAD_DIM)
    ang = positions.astype(F32)[:, None, :] * inv[None, :, None]
    cosT, sinT = jnp.cos(ang), jnp.sin(ang)

    c8 = jnp.pad(c, ((0, 8 - B), (0, 0)))
    ada = _adaln(c8, w_ada, b_ada)[:, :B]
    sh1, sc1, g1, sh2, sc2, g2 = [ada[:, :, k * D:(k + 1) * D] for k in range(6)]
    nw = norm_w[:, :, None, :]
    a1 = _lane_bcast(nw[:, 0] * (1 + sc1))
    gw1 = _lane_bcast(g1 * nw[:, 1])
    a2 = _lane_bcast(nw[:, 2] * (1 + sc2))
    gw2 = _lane_bcast(g2 * nw[:, 3])
    sh1b, sh2b = _lane_bcast(sh1), _lane_bcast(sh2)

    w_inT = _prep_w_in(w_in)
    w_outT = jnp.swapaxes(w_out, 1, 2).astype(BF16)
    wgT = jnp.swapaxes(w_gate, 1, 2).astype(BF16)
    wuT = jnp.swapaxes(w_up, 1, 2).astype(BF16)
    wdT = jnp.swapaxes(w_down, 1, 2).astype(BF16)
    cv = _lane_bcast(jnp.concatenate([conv_w, conv_b[:, None, :]], axis=1))
    wbig, pe2 = _prep_compress(cmp_w_k, cmp_w_v, cmp_pe_k, cmp_pe_v)
    sink_tab = jnp.broadcast_to(
        jnp.repeat(sinks.reshape(depth, SWA_KV_HEADS, SWA_HEADS // SWA_KV_HEADS), LANES, axis=-1)[:, :, None, :],
        (depth, SWA_KV_HEADS, 8, LANES * (SWA_HEADS // SWA_KV_HEADS)))

    ov = _overlap_rows()
    cmp_bias = _cmp_bias()
    kk = np.arange(SEL_TQ)[:, None]
    causal4 = jnp.asarray(np.tile(np.where(kk <= kk.T, 0.0, NEG).astype(np.float32), (1, NSA_HEADS)))
    bias_win = _band_bias(1, 1, NSA_HEADS, NSA_WINDOW // LANES, 1)
    bias_swa = _band_bias(1, 1, SWA_HEADS // SWA_KV_HEADS, SWA_WINDOW // LANES, 1)
    dil_span = [2 if d >= 4 else 1 for _, d in DIL_PATTERNS]
    bias_dil = [_band_bias(d, 0, 1, d, sp) for (_, d), sp in zip(DIL_PATTERNS, dil_span)]

    xT = x
    for l in range(depth):
        first, last_layer = l == 0, l == depth - 1
        outs = _inproj(xT, a1[l], sh1b[l], w_inT, l, cosT, sinT, token_major=first)
        qT, kcvc, nk, v4, gates = outs[:5]
        if first:
            xT = outs[5]
        kc4, vc4 = _compress(kcvc, wbig, pe2, l)
        ocmp, mm = _cmp_topk(qT, kc4, vc4, ov, cmp_bias)
        owin, = _banded(qT, A_Q_BLK, nk, 1, v4, 1, bias_win, hkv=1, grp=NSA_HEADS, nprev=NSA_WINDOW // LANES,
                        span=1, dynamic=False, out_dtype=F32, name="nsa_win")
        aT = _sel_attend(qT, mm, nk, v4, causal4, ocmp, owin, gates)
        bs, lses = [], []
        for gi, (win, dil) in enumerate(DIL_PATTERNS):
            o, lse = _banded(qT, 3 + gi, nk, 2 + gi, v4, 1 + gi, bias_dil[gi], hkv=2, grp=1, nprev=dil,
                             span=dil_span[gi], dynamic=dil > BAND_NSUB, out_dtype=F32, want_lse=True,
                             name="dil%d" % dil)
            bs.append(o)
            lses.append(lse)
        cT, = _banded(qT, 0, nk, 5, v4, 4, bias_swa, hkv=SWA_KV_HEADS, grp=SWA_HEADS // SWA_KV_HEADS,
                      nprev=SWA_WINDOW // LANES, span=1, dynamic=False, out_dtype=BF16,
                      sinks=sink_tab[l], name="swa")
        xT = _mix_ffn(aT, bs, lses, cT, w_outT, xT, gw1[l], a2[l], sh2b[l], wgT, wuT, wdT, cv, gw2[l], l,
                      token_major_out=last_layer)
    return xT
```

```python
import functools

import numpy as np
import jax
import jax.numpy as jnp
from jax import lax
from jax.experimental import pallas as pl
from jax.experimental.pallas import tpu as pltpu

F32 = jnp.float32
BF16 = jnp.bfloat16

HEAD_DIM = 64
HALF = HEAD_DIM // 2
NSA_HEADS = 4
CMP_BLOCK = 32
CMP_STRIDE = 16
SEL_BLOCK = 64
SEL_TOPK = 16
NSA_WINDOW = 512
DIL_PATTERNS = ((128, 1), (512, 4), (2048, 16))
DIL_HEADS = 6
SWA_HEADS = 6
SWA_KV_HEADS = 2
SWA_WINDOW = 128
ROPE_THETA = 10000.0
RMS_EPS = 1e-6
NEG = -1e30
FORCE = 1e4
CONV_WIDTH = 3

LANES = 128
TILE = 512
NSUB = TILE // LANES
BAND_TILE = 1024
BAND_NSUB = BAND_TILE // LANES
VROWS = HEAD_DIM + 16
MEMBER_BIG = 2.0 ** 100
VMEM_LIMIT = 56 * 1024 * 1024

Q_ROWS = 1024
A_Q_BLK = 3
NK_ROWS = 768
NK_LANES = 768
V_ROWS = 640
G_ROWS = 16
W_ROWS = Q_ROWS + NK_ROWS + V_ROWS + G_ROWS
N_VPIECES = V_ROWS // HEAD_DIM
FOLD_ROWS = slice(256, 384)
FOLD_K_BLK = 4


def _cp(sem):
    return pltpu.CompilerParams(dimension_semantics=sem, vmem_limit_bytes=VMEM_LIMIT)


def _const_spec(shape, index_map):
    return pl.BlockSpec(shape, index_map, pipeline_mode=pl.Buffered(1))


def _adaln_kernel(c_ref, w_ref, b_ref, o_ref):
    c = c_ref[...]
    cond = c * jax.nn.sigmoid(c)
    o_ref[...] = jnp.dot(cond, w_ref[...], preferred_element_type=F32,
                         precision=lax.Precision.HIGHEST) + b_ref[...]


def _adaln(c8, w_ada, b_ada):
    depth, d, six_d = w_ada.shape
    nblk = six_d // d
    return pl.pallas_call(
        _adaln_kernel,
        grid=(depth, nblk),
        in_specs=[pl.BlockSpec((8, d), lambda l, n: (0, 0)),
                  pl.BlockSpec((None, d, d), lambda l, n: (l, 0, n)),
                  pl.BlockSpec((None, 1, d), lambda l, n: (l, 0, n))],
        out_specs=pl.BlockSpec((None, 8, d), lambda l, n: (l, 0, n)),
        out_shape=jax.ShapeDtypeStruct((depth, 8, six_d), F32),
        compiler_params=_cp(("parallel", "parallel")),
        name="adaln",
    )(c8, w_ada, b_ada.reshape(depth, 1, six_d))


def _norm_mod_to_scratch(x_ref, a_ref, sh_ref, h_scr):
    for c in range(NSUB):
        sl = slice(c * LANES, (c + 1) * LANES)
        xs = x_ref[:, sl]
        ms = jnp.mean(xs * xs, axis=0, keepdims=True)
        h_scr[:, sl] = ((xs * lax.rsqrt(ms + RMS_EPS)) * a_ref[...] + sh_ref[...]).astype(BF16)


def _inproj_kernel(x_ref, a_ref, sh_ref, w_ref, cos_ref, sin_ref,
                   q_ref, kcvc_ref, nk_ref, v4_ref, gate_ref, fq_ref, fv_ref, *rest, token_major):
    j = pl.program_id(1)
    if token_major:
        xt_ref, h_scr = rest
        for c in range(NSUB):
            xt_ref[:, c * LANES:(c + 1) * LANES] = x_ref[c * LANES:(c + 1) * LANES, :].T
        x_ref = xt_ref
    else:
        h_scr, = rest
    _norm_mod_to_scratch(x_ref, a_ref, sh_ref, h_scr)
    h = h_scr[...]
    cos = cos_ref[...]
    sin = sin_ref[...]

    def proj(r0, r1):
        return jnp.dot(w_ref[r0:r1, :], h, preferred_element_type=F32)

    def rope(r, nh):
        outs = []
        for hh in range(nh):
            t1 = r[HEAD_DIM * hh:HEAD_DIM * hh + HALF]
            t2 = r[HEAD_DIM * hh + HALF:HEAD_DIM * (hh + 1)]
            outs.append(t1 * cos - t2 * sin)
            outs.append(t2 * cos + t1 * sin)
        return jnp.concatenate(outs, axis=0)

    for r0, r1 in ((0, 384), (384, 768), (768, 1024)):
        r = rope(proj(r0, r1), (r1 - r0) // HEAD_DIM)
        q_ref[r0:r1, :] = r.astype(BF16)
        if r0 == 384:
            fq_ref[...] = r[FOLD_ROWS].T.astype(BF16)

    base = Q_ROWS
    r = proj(base, base + 128)
    kcvc = jnp.concatenate([rope(r[0:64], 1), r[64:128]], axis=0)
    kcvc_ref[...] = kcvc.T.astype(BF16)

    r = proj(base + 128, base + 192)
    tok = j * TILE + lax.broadcasted_iota(jnp.int32, (HEAD_DIM, TILE), 1)
    row = lax.broadcasted_iota(jnp.int32, (HEAD_DIM, TILE), 0)
    member_cols = jnp.where(row == ((tok >> 6) & 15), MEMBER_BIG, 0.0).astype(F32)
    nk_ref[:, 0:128] = jnp.concatenate([rope(r, 1), member_cols], axis=0).T.astype(BF16)

    r = proj(base + 192, base + 256)
    nk_ref[:, 128:256] = jnp.concatenate([rope(r, 1), jnp.zeros((HEAD_DIM, TILE), F32)], axis=0).T.astype(BF16)

    r = proj(base + 256, base + 640)
    nk_ref[:, 256:640] = rope(r, 6).T.astype(BF16)

    r = proj(base + 640, base + 768)
    nk_ref[:, 640:768] = rope(r, 2).T.astype(BF16)

    base = Q_ROWS + NK_ROWS
    r = proj(base, base + V_ROWS)
    fv_ref[...] = r[128 + FOLD_ROWS.start:128 + FOLD_ROWS.stop].T.astype(BF16)
    r = r.astype(BF16)
    ones = jnp.ones((VROWS - HEAD_DIM, LANES), BF16)
    for c in range(NSUB):
        for p in range(N_VPIECES):
            v4_ref[c, VROWS * p:VROWS * p + HEAD_DIM, :] = r[HEAD_DIM * p:HEAD_DIM * (p + 1),
                                                             c * LANES:(c + 1) * LANES]
            v4_ref[c, VROWS * p + HEAD_DIM:VROWS * (p + 1), :] = ones

    base = Q_ROWS + NK_ROWS + V_ROWS
    gate_ref[...] = jax.nn.sigmoid(proj(base, base + G_ROWS))


def _inproj(x, a1, sh1, w_inT, layer, cosT, sinT, token_major):
    if token_major:
        B, S, D = x.shape
        x_spec = pl.BlockSpec((None, TILE, D), lambda b, j: (b, j, 0))
    else:
        B, D, S = x.shape
        x_spec = pl.BlockSpec((None, D, TILE), lambda b, j: (b, 0, j))
    nt = S // TILE
    out_specs = [pl.BlockSpec((None, Q_ROWS, TILE), lambda b, j: (b, 0, j)),
                 pl.BlockSpec((None, TILE, LANES), lambda b, j: (b, j, 0)),
                 pl.BlockSpec((None, TILE, NK_LANES), lambda b, j: (b, j, 0)),
                 pl.BlockSpec((None, NSUB, N_VPIECES * VROWS, LANES), lambda b, j: (b, j, 0, 0)),
                 pl.BlockSpec((None, G_ROWS, TILE), lambda b, j: (b, 0, j)),
                 pl.BlockSpec((None, TILE, LANES), lambda b, j: (b, j, 0)),
                 pl.BlockSpec((None, TILE, LANES), lambda b, j: (b, j, 0))]
    out_shape = [jax.ShapeDtypeStruct((B, Q_ROWS, S), BF16),
                 jax.ShapeDtypeStruct((B, S, LANES), BF16),
                 jax.ShapeDtypeStruct((B, S, NK_LANES), BF16),
                 jax.ShapeDtypeStruct((B, S // LANES, N_VPIECES * VROWS, LANES), BF16),
                 jax.ShapeDtypeStruct((B, G_ROWS, S), F32),
                 jax.ShapeDtypeStruct((B, S, LANES), BF16),
                 jax.ShapeDtypeStruct((B, S, LANES), BF16)]
    if token_major:
        out_specs.append(pl.BlockSpec((None, D, TILE), lambda b, j: (b, 0, j)))
        out_shape.append(jax.ShapeDtypeStruct((B, D, S), F32))
    return pl.pallas_call(
        functools.partial(_inproj_kernel, token_major=token_major),
        grid=(B, nt),
        in_specs=[x_spec,
                  pl.BlockSpec((None, D, LANES), lambda b, j: (b, 0, 0)),
                  pl.BlockSpec((None, D, LANES), lambda b, j: (b, 0, 0)),
                  _const_spec((None, W_ROWS, D), lambda b, j: (layer, 0, 0)),
                  pl.BlockSpec((None, HALF, TILE), lambda b, j: (b, 0, j)),
                  pl.BlockSpec((None, HALF, TILE), lambda b, j: (b, 0, j))],
        out_specs=out_specs,
        out_shape=out_shape,
        scratch_shapes=[pltpu.VMEM((D, TILE), BF16)],
        compiler_params=_cp(("parallel", "parallel")),
        name="inproj",
    )(x, a1, sh1, w_inT, cosT, sinT)


def _compress_kernel(t_ref, w_ref, pe_ref, kc_ref, vc_ref):
    n = t_ref.shape[0]
    a = jnp.dot(t_ref[...], w_ref[...], preferred_element_type=F32)
    pc = jnp.dot(pe_ref[...], w_ref[...], preferred_element_type=F32)
    const = pc[0:1, 0:LANES] + pc[8:9, LANES:2 * LANES]
    cmp = a[:, 0:LANES] + pltpu.roll(a[:, LANES:2 * LANES], n - 1, 0) + const
    cmp_t = cmp.T
    ones = jnp.ones((VROWS - HEAD_DIM, LANES), BF16)
    for c in range(n // LANES):
        kc_ref[c] = cmp[c * LANES:(c + 1) * LANES].astype(BF16)
        vc_ref[c, 0:HEAD_DIM, :] = cmp_t[HEAD_DIM:2 * HEAD_DIM, c * LANES:(c + 1) * LANES].astype(BF16)
        vc_ref[c, HEAD_DIM:VROWS, :] = ones


def _compress(kcvc, wbig, pe2, layer):
    B, S, _ = kcvc.shape
    n = S // CMP_STRIDE
    nch = n // LANES
    tview = kcvc.reshape(B, n, CMP_STRIDE * LANES)
    return pl.pallas_call(
        _compress_kernel,
        grid=(B,),
        in_specs=[pl.BlockSpec((None, n, CMP_STRIDE * LANES), lambda b: (b, 0, 0)),
                  pl.BlockSpec((None, CMP_STRIDE * LANES, 2 * LANES), lambda b: (layer, 0, 0)),
                  pl.BlockSpec((None, 16, CMP_STRIDE * LANES), lambda b: (layer, 0, 0))],
        out_specs=[pl.BlockSpec((None, nch, LANES, LANES), lambda b: (b, 0, 0, 0)),
                   pl.BlockSpec((None, nch, VROWS, LANES), lambda b: (b, 0, 0, 0))],
        out_shape=[jax.ShapeDtypeStruct((B, nch, LANES, LANES), BF16),
                   jax.ShapeDtypeStruct((B, nch, VROWS, LANES), BF16)],
        compiler_params=_cp(("parallel",)),
        name="compress",
    )(tview, wbig, pe2)


def _stack_heads(q_ref, nh, lane_slice=slice(None)):
    return jnp.concatenate([q_ref[HEAD_DIM * h:HEAD_DIM * (h + 1), lane_slice] for h in range(nh)], axis=1)


CMP_UNITS = 2
CMP_TQ = CMP_UNITS * LANES


def _cmp_kernel(q_ref, kc_ref, vc_ref, ov_ref, bias_ref, o_ref, mm_ref, imp_scr, *, n_sel, nstep):
    i = pl.program_id(1)
    nch_all = kc_ref.shape[0]
    parts = 4 if nch_all % 4 == 0 else (2 if nch_all % 2 == 0 else 1)
    for v in range(parts):
        pl.when(i // (nstep // parts) == v)(functools.partial(
            _cmp_body, q_ref, kc_ref, vc_ref, ov_ref, bias_ref, o_ref, mm_ref, imp_scr,
            nch=nch_all * (v + 1) // parts, nrows=n_sel * (v + 1) // parts, n_sel=n_sel))


def _cmp_body(q_ref, kc_ref, vc_ref, ov_ref, bias_ref, o_ref, mm_ref, imp_scr, *, nch, nrows, n_sel):
    i = pl.program_id(1)
    cd = (i * CMP_UNITS) // 16

    def scores(u):
        q = _stack_heads(q_ref, NSA_HEADS, slice(u * LANES, (u + 1) * LANES))
        edge = bias_ref[u]
        ss = []
        for c in range(nch):
            b = jnp.where(c < cd, 0.0, jnp.where(c == cd, edge, NEG))
            s = jnp.dot(kc_ref[c, :, 0:HEAD_DIM], q, preferred_element_type=F32)
            ss.append(s + jnp.concatenate([b] * NSA_HEADS, axis=1))
        return ss

    def finish(u, ss):
        lanes = slice(u * LANES, (u + 1) * LANES)
        m = jnp.max(functools.reduce(jnp.maximum, ss), axis=0, keepdims=True)
        valid = m > 0.5 * NEG
        imp_scr[u, 0:32 * nch + 8, :] = jnp.zeros((32 * nch + 8, imp_scr.shape[2]), F32)
        acc = None
        for c in range(nch):
            p = jnp.exp(ss[c] - m).astype(BF16)
            t = jnp.dot(vc_ref[c], p, preferred_element_type=F32)
            acc = t if acc is None else acc + t
            imp_scr[u, 32 * c:32 * c + 40, :] += jnp.dot(ov_ref[...], p, preferred_element_type=F32)
        inv = jnp.where(valid, 1.0 / acc[HEAD_DIM:HEAD_DIM + 1], 0.0)
        o = acc[0:HEAD_DIM] * inv
        for h in range(NSA_HEADS):
            o_ref[HEAD_DIM * h:HEAD_DIM * (h + 1), lanes] = o[:, h * LANES:(h + 1) * LANES]
        imp = jnp.zeros((nrows, LANES), F32)
        for h in range(NSA_HEADS):
            sl = slice(h * LANES, (h + 1) * LANES)
            imp = imp + imp_scr[u, 0:nrows, sl] * inv[:, sl]
        return imp

    pending = scores(0)
    imps = []
    for u in range(1, CMP_UNITS):
        nxt = scores(u)
        imps.append(finish(u - 1, pending))
        pending = nxt
    imps.append(finish(CMP_UNITS - 1, pending))

    blk = lax.broadcasted_iota(jnp.int32, (nrows, LANES), 0).astype(F32)

    def pick_one(imp):
        mx = jnp.max(imp, axis=0, keepdims=True)
        first = jnp.min(jnp.where(imp == mx, blk, float(nrows)), axis=0, keepdims=True)
        return jnp.where(blk == first, -jnp.inf, imp)

    curs, cands = [], []
    for u in range(CMP_UNITS):
        t = i * CMP_TQ + u * LANES + lax.broadcasted_iota(jnp.int32, (nrows, LANES), 1)
        cur = (t >> 6).astype(F32)
        forced = (blk == 0.0) | (blk == cur) | (blk == cur - 1.0)
        imp = jnp.where(forced, FORCE, imps[u])
        curs.append(cur)
        cands.append(jnp.where(blk <= cur, imp, NEG))
    cands = lax.fori_loop(0, min(SEL_TOPK, nrows), lambda _, c: tuple(pick_one(x) for x in c), tuple(cands))
    for u in range(CMP_UNITS):
        member = (cands[u] == -jnp.inf) & (blk <= curs[u])
        mm_ref[0:nrows, u * LANES:(u + 1) * LANES] = jnp.where(member, 0.0, -1.0).astype(BF16)
    if nrows < n_sel:
        mm_ref[nrows:n_sel, :] = jnp.full((n_sel - nrows, CMP_TQ), -1.0, BF16)


def _cmp_topk(qT, kc4, vc4, ov, cmp_bias):
    B, _, S = qT.shape
    nstep = S // CMP_TQ
    nch = kc4.shape[1]
    n_sel = S // SEL_BLOCK
    nq = NSA_HEADS * LANES
    nbias = cmp_bias.shape[0] // CMP_UNITS
    return pl.pallas_call(
        functools.partial(_cmp_kernel, n_sel=n_sel, nstep=nstep),
        grid=(B, nstep),
        in_specs=[pl.BlockSpec((None, NSA_HEADS * HEAD_DIM, CMP_TQ), lambda b, i: (b, A_Q_BLK, i)),
                  pl.BlockSpec((None, nch, LANES, LANES), lambda b, i: (b, 0, 0, 0)),
                  pl.BlockSpec((None, nch, VROWS, LANES), lambda b, i: (b, 0, 0, 0)),
                  pl.BlockSpec((40, LANES), lambda b, i: (0, 0)),
                  pl.BlockSpec((None, CMP_UNITS, LANES, LANES), lambda b, i: (i % nbias, 0, 0, 0))],
        out_specs=[pl.BlockSpec((None, NSA_HEADS * HEAD_DIM, CMP_TQ), lambda b, i: (b, 0, i)),
                   pl.BlockSpec((None, n_sel, CMP_TQ), lambda b, i: (b, 0, i))],
        out_shape=[jax.ShapeDtypeStruct((B, NSA_HEADS * HEAD_DIM, S), F32),
                   jax.ShapeDtypeStruct((B, n_sel, S), BF16)],
        scratch_shapes=[pltpu.VMEM((CMP_UNITS, 32 * nch + 64, nq), F32)],
        compiler_params=_cp(("parallel", "parallel")),
        name="nsa_cmp_topk",
    )(qT, kc4, vc4, ov, cmp_bias.reshape(nbias, CMP_UNITS, LANES, LANES))


SEL_GROUP = 8
SEL_TQ = 2 * LANES


def _sel_kernel(q_ref, mm_ref, ks_ref, v_ref, causal_ref, ocmp_ref, owin_ref, gate_ref, out_ref,
                qa_scr, mm_scr, s_scr):
    i = pl.program_id(1)
    nq = NSA_HEADS * SEL_TQ
    qa_scr[0:HEAD_DIM, :] = _stack_heads(q_ref, NSA_HEADS)
    qa_scr[HEAD_DIM:LANES, :] = jnp.zeros((LANES - HEAD_DIM, nq), BF16)
    mm = mm_ref[...]
    for h in range(NSA_HEADS):
        mm_scr[:, h * SEL_TQ:(h + 1) * SEL_TQ] = mm

    gkeys = SEL_GROUP * LANES

    def load_query(gi):
        rg = (gi * SEL_GROUP) // 8
        qa_scr[HEAD_DIM:HEAD_DIM + 16, :] = mm_scr[pl.ds(pl.multiple_of(rg * 16, 16), 16), :]
        return qa_scr[...]

    def chunk_scores(gi, u, qa):
        row = pl.multiple_of((gi * SEL_GROUP + u) * LANES, LANES)
        return jnp.dot(ks_ref[pl.ds(row, LANES), :], qa, preferred_element_type=F32)

    def weighted_values(gi, ps):
        vcat = jnp.concatenate([v_ref[gi * SEL_GROUP + u] for u in range(SEL_GROUP)], axis=1)
        return jnp.dot(vcat, jnp.concatenate(ps, axis=0), preferred_element_type=F32)

    def body(gi, carry):
        m, acc, mg = carry
        m_new = jnp.maximum(m, mg)
        qa = load_query(gi + 1)
        ps, mx = [], None
        for u in range(SEL_GROUP):
            rows = slice(u * LANES, (u + 1) * LANES)
            ps.append(jnp.exp(s_scr[rows, :] - m_new).astype(BF16))
            nxt = chunk_scores(gi + 1, u, qa)
            s_scr[rows, :] = nxt
            mx = nxt if mx is None else jnp.maximum(mx, nxt)
        acc = acc * jnp.exp(m - m_new) + weighted_values(gi, ps)
        return m_new, acc, jnp.max(mx, axis=0, keepdims=True)

    qa = load_query(0)
    mx = None
    for u in range(SEL_GROUP):
        s0 = chunk_scores(0, u, qa)
        s_scr[u * LANES:(u + 1) * LANES, :] = s0
        mx = s0 if mx is None else jnp.maximum(mx, s0)
    c0 = i * (SEL_TQ // LANES)
    last = c0 // SEL_GROUP
    m, acc, _ = lax.fori_loop(0, last, body, (jnp.full((1, nq), NEG, F32), jnp.zeros((VROWS, nq), F32),
                                              jnp.max(mx, axis=0, keepdims=True)))
    drow = pl.multiple_of((c0 % SEL_GROUP) * LANES, SEL_TQ)
    s_scr[pl.ds(drow, SEL_TQ), :] += causal_ref[...]
    s = s_scr[...]
    m_new = jnp.maximum(m, jnp.max(s, axis=0, keepdims=True))
    acc = acc * jnp.exp(m - m_new) + weighted_values(last, [jnp.exp(s - m_new).astype(BF16)])
    o = acc[0:HEAD_DIM] * (1.0 / acc[HEAD_DIM:HEAD_DIM + 1])

    g = gate_ref[...]
    for h in range(NSA_HEADS):
        rs = slice(HEAD_DIM * h, HEAD_DIM * (h + 1))
        out = (g[3 * h:3 * h + 1] * ocmp_ref[rs, :] + g[3 * h + 1:3 * h + 2] * o[:, h * SEL_TQ:(h + 1) * SEL_TQ]
               + g[3 * h + 2:3 * h + 3] * owin_ref[rs, :])
        out_ref[rs, :] = out.astype(BF16)


def _sel_attend(qT, mm, nk, v4, causal4, ocmp, owin, gates):
    B, _, S = qT.shape
    nstep = S // SEL_TQ
    n_sel = S // SEL_BLOCK
    nq = NSA_HEADS * SEL_TQ
    ar = NSA_HEADS * HEAD_DIM
    return pl.pallas_call(
        _sel_kernel,
        grid=(B, nstep),
        in_specs=[pl.BlockSpec((None, ar, SEL_TQ), lambda b, i: (b, A_Q_BLK, i)),
                  pl.BlockSpec((None, n_sel, SEL_TQ), lambda b, i: (b, 0, i)),
                  pl.BlockSpec((None, S, LANES), lambda b, i: (b, 0, 0)),
                  pl.BlockSpec((None, S // LANES, VROWS, LANES), lambda b, i: (b, 0, 0, 0)),
                  _const_spec((SEL_TQ, nq), lambda b, i: (0, 0)),
                  pl.BlockSpec((None, ar, SEL_TQ), lambda b, i: (b, 0, i)),
                  pl.BlockSpec((None, ar, SEL_TQ), lambda b, i: (b, 0, i)),
                  pl.BlockSpec((None, G_ROWS, SEL_TQ), lambda b, i: (b, 0, i))],
        out_specs=pl.BlockSpec((None, ar, SEL_TQ), lambda b, i: (b, 0, i)),
        out_shape=jax.ShapeDtypeStruct((B, ar, S), BF16),
        scratch_shapes=[pltpu.VMEM((LANES, nq), BF16), pltpu.VMEM((n_sel, nq), BF16),
                        pltpu.VMEM((SEL_GROUP * LANES, nq), F32)],
        compiler_params=_cp(("parallel", "parallel")),
        name="nsa_sel",
    )(qT, mm, nk, v4, causal4, ocmp, owin, gates)


def _banded_kernel(*refs, hkv, grp, nprev, span, zero_bias, dynamic, has_sink, want_lse):
    q_ref, k_ref, v_ref, bias_ref = refs[:4]
    pos = 4
    sink_ref = None
    if has_sink:
        sink_ref = refs[pos]
        pos += 1
    o_ref = refs[pos]
    lse_ref = refs[pos + 1] if want_lse else None
    j = pl.program_id(1)

    width = span * LANES
    nchunk = nprev + span

    def scores(s, g, first):
        lanes = slice(s * width, (s + 1) * width)
        qg = jnp.concatenate([q_ref[HEAD_DIM * (g * grp + u):HEAD_DIM * (g * grp + u + 1), lanes]
                              for u in range(grp)], axis=1)
        parts = []
        for ci in range(nchunk):
            if first:
                kc = s * span - nprev + ci
                if kc < 0:
                    continue
                kcc = kc
                row = kc * LANES
            else:
                kc = j * BAND_NSUB + s * span - nprev + ci
                kcc = jnp.maximum(kc, 0) if dynamic else kc
                row = pl.multiple_of(kcc * LANES, LANES)
            sc = jnp.dot(k_ref[pl.ds(row, LANES), HEAD_DIM * g:HEAD_DIM * (g + 1)], qg,
                         preferred_element_type=F32)
            if dynamic:
                sc = sc + bias_ref[jnp.where(kc >= 0, ci, nchunk)]
            elif not zero_bias[ci]:
                sc = sc + bias_ref[ci]
            parts.append((kcc, sc))
        return parts

    def finish(s, g, parts):
        lanes = slice(s * width, (s + 1) * width)
        m = None
        for _, sc in parts:
            mc = jnp.max(sc, axis=0, keepdims=True)
            m = mc if m is None else jnp.maximum(m, mc)
        if has_sink:
            sk = sink_ref[g, 0:1, :]
            m = jnp.maximum(m, sk)
        acc = None
        for kcc, sc in parts:
            t = jnp.dot(v_ref[kcc, VROWS * g:VROWS * (g + 1), :], jnp.exp(sc - m).astype(BF16),
                        preferred_element_type=F32)
            acc = t if acc is None else acc + t
        l = acc[HEAD_DIM:HEAD_DIM + 1]
        if has_sink:
            l = l + jnp.exp(sk - m)
        o = acc[0:HEAD_DIM] * (1.0 / l)
        for u in range(grp):
            hq = g * grp + u
            o_ref[HEAD_DIM * hq:HEAD_DIM * (hq + 1), lanes] = o[:, u * width:(u + 1) * width].astype(o_ref.dtype)
        if want_lse:
            lse_ref[8 * g:8 * (g + 1), lanes] = jnp.broadcast_to(m + jnp.log(l), (8, width))

    def run(first):
        pending = None
        for s in range(BAND_NSUB // span):
            for g in range(hkv):
                parts = scores(s, g, first)
                if pending is not None:
                    finish(*pending)
                pending = (s, g, parts)
        finish(*pending)

    if dynamic:
        run(False)
    else:
        pl.when(j == 0)(lambda: run(True))
        pl.when(j > 0)(lambda: run(False))


def _banded(qT, q_blk, nk, k_blk, v4, v_blk, bias_and_zero, *, hkv, grp, nprev, span, dynamic,
            out_dtype, sinks=None, want_lse=False, name):
    bias, zero_bias = bias_and_zero
    assert span == 1 or (grp == 1 and sinks is None)
    B, _, S = qT.shape
    nt = S // BAND_TILE
    qrows = hkv * grp * HEAD_DIM
    in_specs = [pl.BlockSpec((None, qrows, BAND_TILE), lambda b, j: (b, q_blk, j)),
                pl.BlockSpec((None, S, LANES), lambda b, j: (b, 0, k_blk)),
                pl.BlockSpec((None, S // LANES, hkv * VROWS, LANES), lambda b, j: (b, 0, v_blk, 0)),
                pl.BlockSpec(bias.shape, lambda b, j: (0, 0, 0))]
    args = [qT, nk, v4, bias]
    if sinks is not None:
        in_specs.append(pl.BlockSpec(sinks.shape, lambda b, j: (0, 0, 0)))
        args.append(sinks)
    out_specs = [pl.BlockSpec((None, qrows, BAND_TILE), lambda b, j: (b, 0, j))]
    out_shape = [jax.ShapeDtypeStruct((B, qrows, S), out_dtype)]
    if want_lse:
        out_specs.append(pl.BlockSpec((None, 8 * hkv, BAND_TILE), lambda b, j: (b, 0, j)))
        out_shape.append(jax.ShapeDtypeStruct((B, 8 * hkv, S), F32))
    return pl.pallas_call(
        functools.partial(_banded_kernel, hkv=hkv, grp=grp, nprev=nprev, span=span, zero_bias=zero_bias,
                          dynamic=dynamic, has_sink=sinks is not None, want_lse=want_lse),
        grid=(B, nt),
        in_specs=in_specs,
        out_specs=out_specs,
        out_shape=out_shape,
        compiler_params=_cp(("parallel", "parallel")),
        name=name,
    )(*args)


def _folded_kernel(q_ref, k_ref, v_ref, bias_ref, o_ref, lse_ref):
    j = pl.program_id(2)
    nsub = q_ref.shape[0] // LANES
    ones = jnp.ones((VROWS - HEAD_DIM, LANES), BF16)
    row = lax.broadcasted_iota(jnp.int32, (LANES, LANES), 0)

    def scores(s):
        qt = q_ref[s * LANES:(s + 1) * LANES, :].astype(F32).T.astype(BF16)
        parts = []
        for ci in range(2):
            kc = j * nsub + s - 1 + ci
            kcc = jnp.maximum(kc, 0)
            rows = pl.ds(pl.multiple_of(kcc * LANES, LANES), LANES)
            bias = bias_ref[jnp.where(kc >= 0, ci, 2)]
            vt = v_ref[rows, :].astype(F32).T.astype(BF16)
            for g in range(2):
                hs = slice(HEAD_DIM * g, HEAD_DIM * (g + 1))
                sc = jnp.dot(k_ref[rows, hs], qt[hs], preferred_element_type=F32) + bias
                parts.append((g, jnp.concatenate([vt[hs], ones], axis=0), sc))
        return parts

    def finish(s, parts):
        outs, lses = [], []
        for g in range(2):
            mine = [(v, sc) for gg, v, sc in parts if gg == g]
            m = functools.reduce(jnp.maximum, [jnp.max(sc, axis=0, keepdims=True) for _, sc in mine])
            acc = None
            for v, sc in mine:
                t = jnp.dot(v, jnp.exp(sc - m).astype(BF16), preferred_element_type=F32)
                acc = t if acc is None else acc + t
            l = acc[HEAD_DIM:HEAD_DIM + 1]
            outs.append(acc[0:HEAD_DIM] * (1.0 / l))
            lses.append(m + jnp.log(l))
        rows = slice(s * LANES, (s + 1) * LANES)
        o_ref[rows, :] = jnp.concatenate(outs, axis=0).T
        lse_ref[rows, :] = jnp.where(row == 0, lses[0], jnp.where(row == 1, lses[1], 0.0)).T

    pending = None
    for s in range(nsub):
        parts = scores(s)
        if pending is not None:
            finish(*pending)
        pending = (s, parts)
    finish(*pending)


def _folded_dilated(fq, nk, fv, bias, dil):
    B, S, _ = fq.shape
    n = S // dil
    tq = min(n, BAND_TILE)
    kblocks = nk.shape[2] // LANES
    view = lambda a: a.reshape(B, n, dil * a.shape[2])
    o, lse = pl.pallas_call(
        _folded_kernel,
        grid=(B, dil, n // tq),
        in_specs=[pl.BlockSpec((None, tq, LANES), lambda b, r, j: (b, j, r)),
                  pl.BlockSpec((None, n, LANES), lambda b, r, j: (b, 0, r * kblocks + FOLD_K_BLK)),
                  pl.BlockSpec((None, n, LANES), lambda b, r, j: (b, 0, r)),
                  pl.BlockSpec(bias.shape, lambda b, r, j: (0, 0, 0))],
        out_specs=[pl.BlockSpec((None, tq, LANES), lambda b, r, j: (b, j, r)),
                   pl.BlockSpec((None, tq, LANES), lambda b, r, j: (b, j, r))],
        out_shape=[jax.ShapeDtypeStruct((B, n, dil * LANES), F32),
                   jax.ShapeDtypeStruct((B, n, dil * LANES), F32)],
        compiler_params=_cp(("parallel", "parallel", "parallel")),
        name="dil%d_folded" % dil,
    )(view(fq), view(nk), view(fv), bias)
    return o.reshape(B, S, LANES), lse.reshape(B, S, LANES)


def _post_norm_residual(y, x_ref, gw_ref, o_ref, token_major=False):
    ms = jnp.mean(y * y, axis=0, keepdims=True)
    yn = y * lax.rsqrt(ms + RMS_EPS)
    gw = gw_ref[...]
    for c in range(NSUB):
        sl = slice(c * LANES, (c + 1) * LANES)
        out = x_ref[:, sl] + gw * yn[:, sl]
        if token_major:
            o_ref[sl, :] = out.T
        else:
            o_ref[:, sl] = out


def _mix_ffn_kernel(a_ref, b0_ref, b1_ref, b2_ref, l0_ref, l1_ref, l2_ref, c_ref, wo_ref, x_ref, gw1_ref,
                    a2_ref, sh2_ref, wg_ref, wu_ref, wd_ref, cv_ref, gw2_ref, o_ref,
                    mix_scr, x1_scr, h_scr, carry_scr, act_scr, *, fchunk, token_major_out):
    s = pl.program_id(1)
    cur = s % 2
    prv = 1 - cur
    d_ff = wg_ref.shape[0]

    @pl.when(s == 0)
    def _():
        carry_scr[...] = jnp.zeros(carry_scr.shape, F32)
        h_scr[1] = jnp.zeros(h_scr.shape[1:], BF16)
        x1_scr[1] = jnp.zeros(x1_scr.shape[1:], F32)

    h = h_scr[prv]
    lane = lax.broadcasted_iota(jnp.int32, (fchunk, LANES), 1)
    for c in range(d_ff // fchunk):
        rs = slice(c * fchunk, (c + 1) * fchunk)
        g = jnp.dot(wg_ref[rs, :], h, preferred_element_type=F32)
        prev = carry_scr[rs, :]
        carry_scr[rs, :] = g[:, TILE - LANES:TILE]
        g1 = pltpu.roll(g, 1, 1)
        g2 = pltpu.roll(g, 2, 1)
        fix1 = jnp.where(lane < 1, pltpu.roll(prev, 1, 1), g1[:, 0:LANES])
        fix2 = jnp.where(lane < 2, pltpu.roll(prev, 2, 1), g2[:, 0:LANES])
        g1 = jnp.concatenate([fix1, g1[:, LANES:]], axis=1)
        g2 = jnp.concatenate([fix2, g2[:, LANES:]], axis=1)
        w0 = jnp.concatenate([cv_ref[0, rs, :]] * NSUB, axis=1)
        w1 = jnp.concatenate([cv_ref[1, rs, :]] * NSUB, axis=1)
        w2 = jnp.concatenate([cv_ref[2, rs, :]] * NSUB, axis=1)
        cb = jnp.concatenate([cv_ref[3, rs, :]] * NSUB, axis=1)
        acc = g2 * w0 + g1 * w1 + g * w2 + cb
        up = jnp.dot(wu_ref[rs, :], h, preferred_element_type=F32)
        act_scr[rs, :] = (jax.nn.gelu(acc, approximate=True) * up).astype(BF16)

    b_vals = (b0_ref[...], b1_ref[...], b2_ref[...].T)
    l2t = l2_ref[...].T
    for hh in range(2):
        ls = [l0_ref[8 * hh:8 * hh + 1, :], l1_ref[8 * hh:8 * hh + 1, :], l2t[hh:hh + 1, :]]
        mx = jnp.maximum(jnp.maximum(ls[0], ls[1]), ls[2])
        es = [jnp.exp(v - mx) for v in ls]
        inv = 1.0 / (es[0] + es[1] + es[2])
        for g in range(3):
            rs = slice(HEAD_DIM * hh, HEAD_DIM * (hh + 1))
            mix_scr[128 * g + HEAD_DIM * hh:128 * g + HEAD_DIM * (hh + 1), :] = (
                b_vals[g][rs, :] * (es[g] * inv)).astype(BF16)
    y1 = jnp.dot(wo_ref[:, 0:256], a_ref[...], preferred_element_type=F32)
    y1 = y1 + jnp.dot(wo_ref[:, 256:640], mix_scr[...], preferred_element_type=F32)
    y1 = y1 + jnp.dot(wo_ref[:, 640:1024], c_ref[...], preferred_element_type=F32)

    y2 = jnp.dot(wd_ref[...], act_scr[...], preferred_element_type=F32)

    _post_norm_residual(y1, x_ref, gw1_ref, x1_scr.at[cur])
    _norm_mod_to_scratch(x1_scr.at[cur], a2_ref, sh2_ref, h_scr.at[cur])
    _post_norm_residual(y2, x1_scr.at[prv], gw2_ref, o_ref, token_major_out)


def _mix_ffn(aT, bs, lses, cT, w_outT, xT, gw1, a2, sh2, wgT, wuT, wdT, cv, gw2, layer, token_major_out):
    B, D, S = xT.shape
    nt = S // TILE
    d_ff = wgT.shape[1]
    tile = lambda rows: pl.BlockSpec((None, rows, TILE), lambda b, s: (b, 0, jnp.minimum(s, nt - 1)))
    tok = pl.BlockSpec((None, TILE, LANES), lambda b, s: (b, jnp.minimum(s, nt - 1), 0))
    vec = pl.BlockSpec((None, D, LANES), lambda b, s: (b, 0, 0))
    return pl.pallas_call(
        functools.partial(_mix_ffn_kernel, fchunk=256, token_major_out=token_major_out),
        grid=(B, nt + 1),
        in_specs=[tile(256), tile(128), tile(128), tok, tile(16), tile(16), tok, tile(384),
                  _const_spec((None, D, D), lambda b, s: (layer, 0, 0)),
                  tile(D), vec, vec, vec,
                  _const_spec((None, d_ff, D), lambda b, s: (layer, 0, 0)),
                  _const_spec((None, d_ff, D), lambda b, s: (layer, 0, 0)),
                  _const_spec((None, D, d_ff), lambda b, s: (layer, 0, 0)),
                  _const_spec((None, 4, d_ff, LANES), lambda b, s: (layer, 0, 0, 0)),
                  vec],
        out_specs=(pl.BlockSpec((None, TILE, D), lambda b, s: (b, jnp.maximum(s - 1, 0), 0)) if token_major_out
                   else pl.BlockSpec((None, D, TILE), lambda b, s: (b, 0, jnp.maximum(s - 1, 0)))),
        out_shape=jax.ShapeDtypeStruct((B, S, D) if token_major_out else (B, D, S), F32),
        scratch_shapes=[pltpu.VMEM((384, TILE), BF16), pltpu.VMEM((2, D, TILE), F32),
                        pltpu.VMEM((2, D, TILE), BF16), pltpu.VMEM((d_ff, LANES), F32),
                        pltpu.VMEM((d_ff, TILE), BF16)],
        compiler_params=_cp(("arbitrary", "arbitrary")),
        name="mix_ffn",
    )(aT, bs[0], bs[1], bs[2], lses[0], lses[1], lses[2], cT, w_outT, xT, gw1,
      a2, sh2, wgT, wuT, wdT, cv, gw2)


def _band_bias(dilation, old_edge, reps, nprev, span):
    kk = np.arange(LANES)[:, None]
    qq = np.arange(LANES)[None, :]
    res = ((qq - kk) % dilation) == 0
    none = np.zeros_like(res)

    def role(ci):
        if ci < 0 or ci > nprev:
            return none
        return res & (kk - qq >= old_edge) if ci == 0 else (res & (kk <= qq) if ci == nprev else res)

    tabs = [np.concatenate([role(u - a) for a in range(span)], axis=1) for u in range(nprev + span)]
    zero = tuple(bool(t.all()) for t in tabs)
    tabs.append(np.concatenate([none] * span, axis=1))
    out = np.stack([np.where(t, 0.0, NEG) for t in tabs]).astype(np.float32)
    return jnp.asarray(np.tile(out, (1, 1, reps))), zero


def _cmp_bias():
    nn = np.arange(LANES)[:, None]
    qq = np.arange(LANES)[None, :]
    tabs = [np.where(CMP_STRIDE * nn + CMP_BLOCK - 1 <= LANES * r + qq, 0.0, NEG) for r in range(16)]
    return jnp.asarray(np.stack(tabs).astype(np.float32))


def _overlap_rows():
    jj = np.arange(40)[:, None]
    nn = np.arange(LANES)[None, :]
    return jnp.asarray(((nn >= 4 * jj - 1) & (nn <= 4 * jj + 3)).astype(np.float32), dtype=BF16)


_IN_COL_ORDER = ((1804, 2188), (652, 1036), (0, 256),
                 (256, 320), (320, 384), (384, 448), (512, 576), (1036, 1420), (2188, 2316),
                 (448, 512), (576, 640), (1420, 1804), (2316, 2444),
                 (640, 652))


def _prep_w_in(w_in):
    wt = jnp.swapaxes(jnp.concatenate([w_in[:, :, a:b] for a, b in _IN_COL_ORDER], axis=2), 1, 2)
    scale = np.ones((wt.shape[1], 1), np.float32)
    scale[:Q_ROWS] = HEAD_DIM ** -0.5
    wt = wt * scale
    wt = jnp.pad(wt, ((0, 0), (0, W_ROWS - wt.shape[1]), (0, 0)))
    return wt.astype(BF16)


def _prep_compress(w_ck, w_cv, pe_k, pe_v):
    L = w_ck.shape[0]
    half = CMP_BLOCK // 2

    def big(lo):
        wk = w_ck[:, lo:lo + half]
        wv = w_cv[:, lo:lo + half]
        z = jnp.zeros_like(wk)
        top = jnp.concatenate([wk, z], axis=-1)
        bot = jnp.concatenate([z, wv], axis=-1)
        return jnp.concatenate([top, bot], axis=2).reshape(L, half * LANES, LANES)

    wbig = jnp.concatenate([big(0), big(half)], axis=-1).astype(BF16)
    pe = jnp.concatenate([pe_k, pe_v], axis=-1)
    pe2 = jnp.zeros((L, 16, half * LANES), F32)
    pe2 = pe2.at[:, 0].set(pe[:, :half].reshape(L, -1)).at[:, 8].set(pe[:, half:].reshape(L, -1))
    return wbig, pe2.astype(BF16)


def _lane_bcast(v):
    return jnp.broadcast_to(v[..., None], v.shape + (LANES,))


def kernel(x, c, positions, w_in, w_out, w_ada, b_ada, norm_w, cmp_w_k, cmp_w_v, cmp_pe_k, cmp_pe_v,
           sinks, w_gate, w_up, conv_w, conv_b, w_down):
    B, S, D = x.shape
    depth = w_in.shape[0]
    assert S % 2048 == 0 and D == 1024 and w_in.shape[2] == 2444

    inv = ROPE_THETA ** (-jnp.arange(0, HEAD_DIM, 2, dtype=F32) / HEAD_DIM)
    ang = positions.astype(F32)[:, None, :] * inv[None, :, None]
    cosT, sinT = jnp.cos(ang), jnp.sin(ang)

    c8 = jnp.pad(c, ((0, 8 - B), (0, 0)))
    ada = _adaln(c8, w_ada, b_ada)[:, :B]
    sh1, sc1, g1, sh2, sc2, g2 = [ada[:, :, k * D:(k + 1) * D] for k in range(6)]
    nw = norm_w[:, :, None, :]
    a1 = _lane_bcast(nw[:, 0] * (1 + sc1))
    gw1 = _lane_bcast(g1 * nw[:, 1])
    a2 = _lane_bcast(nw[:, 2] * (1 + sc2))
    gw2 = _lane_bcast(g2 * nw[:, 3])
    sh1b, sh2b = _lane_bcast(sh1), _lane_bcast(sh2)

    w_inT = _prep_w_in(w_in)
    w_outT = jnp.swapaxes(w_out, 1, 2).astype(BF16)
    wgT = jnp.swapaxes(w_gate, 1, 2).astype(BF16)
    wuT = jnp.swapaxes(w_up, 1, 2).astype(BF16)
    wdT = jnp.swapaxes(w_down, 1, 2).astype(BF16)
    cv = _lane_bcast(jnp.concatenate([conv_w, conv_b[:, None, :]], axis=1))
    wbig, pe2 = _prep_compress(cmp_w_k, cmp_w_v, cmp_pe_k, cmp_pe_v)
    sink_tab = jnp.broadcast_to(
        jnp.repeat(sinks.reshape(depth, SWA_KV_HEADS, SWA_HEADS // SWA_KV_HEADS), LANES, axis=-1)[:, :, None, :],
        (depth, SWA_KV_HEADS, 8, LANES * (SWA_HEADS // SWA_KV_HEADS)))

    ov = _overlap_rows()
    cmp_bias = _cmp_bias()
    kk = np.arange(SEL_TQ)[:, None]
    causal4 = jnp.asarray(np.tile(np.where(kk <= kk.T, 0.0, NEG).astype(np.float32), (1, NSA_HEADS)))
    bias_win = _band_bias(1, 1, NSA_HEADS, NSA_WINDOW // LANES, 1)
    bias_swa = _band_bias(1, 1, SWA_HEADS // SWA_KV_HEADS, SWA_WINDOW // LANES, 1)
    dil_span = [2 if d >= 4 else 1 for _, d in DIL_PATTERNS[:2]]
    bias_dil = [_band_bias(d, 0, 1, d, sp) for (_, d), sp in zip(DIL_PATTERNS[:2], dil_span)]
    assert DIL_PATTERNS[0] == (LANES, 1) and DIL_PATTERNS[2][0] // DIL_PATTERNS[2][1] == LANES

    xT = x
    for l in range(depth):
        first, last_layer = l == 0, l == depth - 1
        outs = _inproj(xT, a1[l], sh1b[l], w_inT, l, cosT, sinT, token_major=first)
        qT, kcvc, nk, v4, gates, fq, fv = outs[:7]
        if first:
            xT = outs[7]
        kc4, vc4 = _compress(kcvc, wbig, pe2, l)
        ocmp, mm = _cmp_topk(qT, kc4, vc4, ov, cmp_bias)
        owin, = _banded(qT, A_Q_BLK, nk, 1, v4, 1, bias_win, hkv=1, grp=NSA_HEADS, nprev=NSA_WINDOW // LANES,
                        span=1, dynamic=False, out_dtype=F32, name="nsa_win")
        aT = _sel_attend(qT, mm, nk, v4, causal4, ocmp, owin, gates)
        bs, lses = [], []
        for gi, (win, dil) in enumerate(DIL_PATTERNS[:2]):
            o, lse = _banded(qT, 3 + gi, nk, 2 + gi, v4, 1 + gi, bias_dil[gi], hkv=2, grp=1, nprev=dil,
                             span=dil_span[gi], dynamic=dil > BAND_NSUB, out_dtype=F32, want_lse=True,
                             name="dil%d" % dil)
            bs.append(o)
            lses.append(lse)
        o, lse = _folded_dilated(fq, nk, fv, bias_dil[0][0], DIL_PATTERNS[2][1])
        bs.append(o)
        lses.append(lse)
        cT, = _banded(qT, 0, nk, 5, v4, 4, bias_swa, hkv=SWA_KV_HEADS, grp=SWA_HEADS // SWA_KV_HEADS,
                      nprev=SWA_WINDOW // LANES, span=1, dynamic=False, out_dtype=BF16,
                      sinks=sink_tab[l], name="swa")
        xT = _mix_ffn(aT, bs, lses, cT, w_outT, xT, gw1[l], a2[l], sh2b[l], wgT, wuT, wdT, cv, gw2[l], l,
                      token_major_out=last_layer)
    return xT
```

```python
import functools

import numpy as np
import jax
import jax.numpy as jnp
from jax import lax
from jax.experimental import pallas as pl
from jax.experimental.pallas import tpu as pltpu

F32 = jnp.float32
BF16 = jnp.bfloat16

HEAD_DIM = 64
HALF = HEAD_DIM // 2
NSA_HEADS = 4
CMP_BLOCK = 32
CMP_STRIDE = 16
SEL_BLOCK = 64
SEL_TOPK = 16
NSA_WINDOW = 512
DIL_PATTERNS = ((128, 1), (512, 4), (2048, 16))
DIL_HEADS = 6
SWA_HEADS = 6
SWA_KV_HEADS = 2
SWA_WINDOW = 128
ROPE_THETA = 10000.0
RMS_EPS = 1e-6
NEG = -1e30
FORCE = 1e4
CONV_WIDTH = 3

LANES = 128
TILE = 512
NSUB = TILE // LANES
BAND_TILE = 1024
BAND_NSUB = BAND_TILE // LANES
VROWS = HEAD_DIM + 16
MEMBER_BIG = 2.0 ** 100
VMEM_LIMIT = 56 * 1024 * 1024

Q_ROWS = 1024
A_Q_BLK = 3
NK_ROWS = 768
NK_LANES = 768
V_ROWS = 640
G_ROWS = 16
W_ROWS = Q_ROWS + NK_ROWS + V_ROWS + G_ROWS
N_VPIECES = V_ROWS // HEAD_DIM
FOLD_ROWS = slice(256, 384)
FOLD = DIL_PATTERNS[2][1]


def _cp(sem):
    return pltpu.CompilerParams(dimension_semantics=sem, vmem_limit_bytes=VMEM_LIMIT)


def _const_spec(shape, index_map):
    return pl.BlockSpec(shape, index_map, pipeline_mode=pl.Buffered(1))


def _adaln_kernel(c_ref, w_ref, b_ref, o_ref):
    c = c_ref[...]
    cond = c * jax.nn.sigmoid(c)
    o_ref[...] = jnp.dot(cond, w_ref[...], preferred_element_type=F32,
                         precision=lax.Precision.HIGHEST) + b_ref[...]


def _adaln(c8, w_ada, b_ada):
    depth, d, six_d = w_ada.shape
    nblk = six_d // d
    return pl.pallas_call(
        _adaln_kernel,
        grid=(depth, nblk),
        in_specs=[pl.BlockSpec((8, d), lambda l, n: (0, 0)),
                  pl.BlockSpec((None, d, d), lambda l, n: (l, 0, n)),
                  pl.BlockSpec((None, 1, d), lambda l, n: (l, 0, n))],
        out_specs=pl.BlockSpec((None, 8, d), lambda l, n: (l, 0, n)),
        out_shape=jax.ShapeDtypeStruct((depth, 8, six_d), F32),
        compiler_params=_cp(("parallel", "parallel")),
        name="adaln",
    )(c8, w_ada, b_ada.reshape(depth, 1, six_d))


def _norm_mod_to_scratch(x_ref, a_ref, sh_ref, h_scr):
    for c in range(NSUB):
        sl = slice(c * LANES, (c + 1) * LANES)
        xs = x_ref[:, sl]
        ms = jnp.mean(xs * xs, axis=0, keepdims=True)
        h_scr[:, sl] = ((xs * lax.rsqrt(ms + RMS_EPS)) * a_ref[...] + sh_ref[...]).astype(BF16)


def _inproj_kernel(x_ref, a_ref, sh_ref, w_ref, cos_ref, sin_ref,
                   q_ref, kcvc_ref, nk_ref, v4_ref, gate_ref, fq_ref, fk_ref, fv_ref, *rest, token_major):
    j = pl.program_id(1)
    if token_major:
        xt_ref, h_scr, fold_scr = rest
        for c in range(NSUB):
            xt_ref[:, c * LANES:(c + 1) * LANES] = x_ref[c * LANES:(c + 1) * LANES, :].T
        x_ref = xt_ref
    else:
        h_scr, fold_scr = rest
    _norm_mod_to_scratch(x_ref, a_ref, sh_ref, h_scr)
    h = h_scr[...]
    cos = cos_ref[...]
    sin = sin_ref[...]

    def proj(r0, r1):
        return jnp.dot(w_ref[r0:r1, :], h, preferred_element_type=F32)

    def fold_store(t, dst_ref):
        fold_scr[...] = t
        for res in range(FOLD):
            dst_ref[:, res * LANES:(res + 1) * LANES] = fold_scr[pl.ds(res, TILE // FOLD, stride=FOLD), :].astype(BF16)

    def rope(r, nh):
        outs = []
        for hh in range(nh):
            t1 = r[HEAD_DIM * hh:HEAD_DIM * hh + HALF]
            t2 = r[HEAD_DIM * hh + HALF:HEAD_DIM * (hh + 1)]
            outs.append(t1 * cos - t2 * sin)
            outs.append(t2 * cos + t1 * sin)
        return jnp.concatenate(outs, axis=0)

    for r0, r1 in ((0, 384), (384, 768), (768, 1024)):
        r = rope(proj(r0, r1), (r1 - r0) // HEAD_DIM)
        q_ref[r0:r1, :] = r.astype(BF16)
        if r0 == 384:
            fold_store(r[FOLD_ROWS].T, fq_ref)

    base = Q_ROWS
    r = proj(base, base + 128)
    kcvc = jnp.concatenate([rope(r[0:64], 1), r[64:128]], axis=0)
    kcvc_ref[...] = kcvc.T.astype(BF16)

    r = proj(base + 128, base + 192)
    tok = j * TILE + lax.broadcasted_iota(jnp.int32, (HEAD_DIM, TILE), 1)
    row = lax.broadcasted_iota(jnp.int32, (HEAD_DIM, TILE), 0)
    member_cols = jnp.where(row == ((tok >> 6) & 15), MEMBER_BIG, 0.0).astype(F32)
    nk_ref[:, 0:128] = jnp.concatenate([rope(r, 1), member_cols], axis=0).T.astype(BF16)

    r = proj(base + 192, base + 256)
    nk_ref[:, 128:256] = jnp.concatenate([rope(r, 1), jnp.zeros((HEAD_DIM, TILE), F32)], axis=0).T.astype(BF16)

    r = proj(base + 256, base + 640)
    bk = rope(r, 6).T
    nk_ref[:, 256:640] = bk.astype(BF16)
    fold_store(bk[:, FOLD_ROWS], fk_ref)

    r = proj(base + 640, base + 768)
    nk_ref[:, 640:768] = rope(r, 2).T.astype(BF16)

    base = Q_ROWS + NK_ROWS
    r = proj(base, base + V_ROWS)
    fold_store(r[128 + FOLD_ROWS.start:128 + FOLD_ROWS.stop].T, fv_ref)
    r = r.astype(BF16)
    ones = jnp.ones((VROWS - HEAD_DIM, LANES), BF16)
    for c in range(NSUB):
        for p in range(N_VPIECES):
            v4_ref[c, VROWS * p:VROWS * p + HEAD_DIM, :] = r[HEAD_DIM * p:HEAD_DIM * (p + 1),
                                                             c * LANES:(c + 1) * LANES]
            v4_ref[c, VROWS * p + HEAD_DIM:VROWS * (p + 1), :] = ones

    base = Q_ROWS + NK_ROWS + V_ROWS
    gate_ref[...] = jax.nn.sigmoid(proj(base, base + G_ROWS))


def _inproj(x, a1, sh1, w_inT, layer, cosT, sinT, token_major):
    if token_major:
        B, S, D = x.shape
        x_spec = pl.BlockSpec((None, TILE, D), lambda b, j: (b, j, 0))
    else:
        B, D, S = x.shape
        x_spec = pl.BlockSpec((None, D, TILE), lambda b, j: (b, 0, j))
    nt = S // TILE
    fold_spec = pl.BlockSpec((None, TILE // FOLD, FOLD * LANES), lambda b, j: (b, j, 0))
    out_specs = [pl.BlockSpec((None, Q_ROWS, TILE), lambda b, j: (b, 0, j)),
                 pl.BlockSpec((None, TILE, LANES), lambda b, j: (b, j, 0)),
                 pl.BlockSpec((None, TILE, NK_LANES), lambda b, j: (b, j, 0)),
                 pl.BlockSpec((None, NSUB, N_VPIECES * VROWS, LANES), lambda b, j: (b, j, 0, 0)),
                 pl.BlockSpec((None, G_ROWS, TILE), lambda b, j: (b, 0, j)),
                 fold_spec, fold_spec, fold_spec]
    out_shape = [jax.ShapeDtypeStruct((B, Q_ROWS, S), BF16),
                 jax.ShapeDtypeStruct((B, S, LANES), BF16),
                 jax.ShapeDtypeStruct((B, S, NK_LANES), BF16),
                 jax.ShapeDtypeStruct((B, S // LANES, N_VPIECES * VROWS, LANES), BF16),
                 jax.ShapeDtypeStruct((B, G_ROWS, S), F32)] + [
                     jax.ShapeDtypeStruct((B, S // FOLD, FOLD * LANES), BF16)] * 3
    if token_major:
        out_specs.append(pl.BlockSpec((None, D, TILE), lambda b, j: (b, 0, j)))
        out_shape.append(jax.ShapeDtypeStruct((B, D, S), F32))
    return pl.pallas_call(
        functools.partial(_inproj_kernel, token_major=token_major),
        grid=(B, nt),
        in_specs=[x_spec,
                  pl.BlockSpec((None, D, LANES), lambda b, j: (b, 0, 0)),
                  pl.BlockSpec((None, D, LANES), lambda b, j: (b, 0, 0)),
                  _const_spec((None, W_ROWS, D), lambda b, j: (layer, 0, 0)),
                  pl.BlockSpec((None, HALF, TILE), lambda b, j: (b, 0, j)),
                  pl.BlockSpec((None, HALF, TILE), lambda b, j: (b, 0, j))],
        out_specs=out_specs,
        out_shape=out_shape,
        scratch_shapes=[pltpu.VMEM((D, TILE), BF16), pltpu.VMEM((TILE, LANES), F32)],
        compiler_params=_cp(("parallel", "parallel")),
        name="inproj",
    )(x, a1, sh1, w_inT, cosT, sinT)


def _compress_kernel(t_ref, w_ref, pe_ref, kc_ref, vc_ref):
    n = t_ref.shape[0]
    a = jnp.dot(t_ref[...], w_ref[...], preferred_element_type=F32)
    pc = jnp.dot(pe_ref[...], w_ref[...], preferred_element_type=F32)
    const = pc[0:1, 0:LANES] + pc[8:9, LANES:2 * LANES]
    cmp = a[:, 0:LANES] + pltpu.roll(a[:, LANES:2 * LANES], n - 1, 0) + const
    cmp_t = cmp.T
    ones = jnp.ones((VROWS - HEAD_DIM, LANES), BF16)
    for c in range(n // LANES):
        kc_ref[c] = cmp[c * LANES:(c + 1) * LANES].astype(BF16)
        vc_ref[c, 0:HEAD_DIM, :] = cmp_t[HEAD_DIM:2 * HEAD_DIM, c * LANES:(c + 1) * LANES].astype(BF16)
        vc_ref[c, HEAD_DIM:VROWS, :] = ones


def _compress(kcvc, wbig, pe2, layer):
    B, S, _ = kcvc.shape
    n = S // CMP_STRIDE
    nch = n // LANES
    tview = kcvc.reshape(B, n, CMP_STRIDE * LANES)
    return pl.pallas_call(
        _compress_kernel,
        grid=(B,),
        in_specs=[pl.BlockSpec((None, n, CMP_STRIDE * LANES), lambda b: (b, 0, 0)),
                  pl.BlockSpec((None, CMP_STRIDE * LANES, 2 * LANES), lambda b: (layer, 0, 0)),
                  pl.BlockSpec((None, 16, CMP_STRIDE * LANES), lambda b: (layer, 0, 0))],
        out_specs=[pl.BlockSpec((None, nch, LANES, LANES), lambda b: (b, 0, 0, 0)),
                   pl.BlockSpec((None, nch, VROWS, LANES), lambda b: (b, 0, 0, 0))],
        out_shape=[jax.ShapeDtypeStruct((B, nch, LANES, LANES), BF16),
                   jax.ShapeDtypeStruct((B, nch, VROWS, LANES), BF16)],
        compiler_params=_cp(("parallel",)),
        name="compress",
    )(tview, wbig, pe2)


def _stack_heads(q_ref, nh, lane_slice=slice(None)):
    return jnp.concatenate([q_ref[HEAD_DIM * h:HEAD_DIM * (h + 1), lane_slice] for h in range(nh)], axis=1)


CMP_UNITS = 2
CMP_TQ = CMP_UNITS * LANES


def _cmp_kernel(q_ref, kc_ref, vc_ref, ov_ref, bias_ref, o_ref, mm_ref, imp_scr, *, n_sel, nstep):
    i = pl.program_id(1)
    nch_all = kc_ref.shape[0]
    parts = 4 if nch_all % 4 == 0 else (2 if nch_all % 2 == 0 else 1)
    for v in range(parts):
        pl.when(i // (nstep // parts) == v)(functools.partial(
            _cmp_body, q_ref, kc_ref, vc_ref, ov_ref, bias_ref, o_ref, mm_ref, imp_scr,
            nch=nch_all * (v + 1) // parts, nrows=n_sel * (v + 1) // parts, n_sel=n_sel))


def _cmp_body(q_ref, kc_ref, vc_ref, ov_ref, bias_ref, o_ref, mm_ref, imp_scr, *, nch, nrows, n_sel):
    i = pl.program_id(1)
    cd = (i * CMP_UNITS) // 16

    def scores(u):
        q = _stack_heads(q_ref, NSA_HEADS, slice(u * LANES, (u + 1) * LANES))
        edge = bias_ref[u]
        ss = []
        for c in range(nch):
            b = jnp.where(c < cd, 0.0, jnp.where(c == cd, edge, NEG))
            s = jnp.dot(kc_ref[c, :, 0:HEAD_DIM], q, preferred_element_type=F32)
            ss.append(s + jnp.concatenate([b] * NSA_HEADS, axis=1))
        return ss

    def finish(u, ss):
        lanes = slice(u * LANES, (u + 1) * LANES)
        m = jnp.max(functools.reduce(jnp.maximum, ss), axis=0, keepdims=True)
        valid = m > 0.5 * NEG
        imp_scr[u, 0:32 * nch + 8, :] = jnp.zeros((32 * nch + 8, imp_scr.shape[2]), F32)
        acc = None
        for c in range(nch):
            p = jnp.exp(ss[c] - m).astype(BF16)
            t = jnp.dot(vc_ref[c], p, preferred_element_type=F32)
            acc = t if acc is None else acc + t
            imp_scr[u, 32 * c:32 * c + 40, :] += jnp.dot(ov_ref[...], p, preferred_element_type=F32)
        inv = jnp.where(valid, 1.0 / acc[HEAD_DIM:HEAD_DIM + 1], 0.0)
        o = acc[0:HEAD_DIM] * inv
        for h in range(NSA_HEADS):
            o_ref[HEAD_DIM * h:HEAD_DIM * (h + 1), lanes] = o[:, h * LANES:(h + 1) * LANES]
        imp = jnp.zeros((nrows, LANES), F32)
        for h in range(NSA_HEADS):
            sl = slice(h * LANES, (h + 1) * LANES)
            imp = imp + imp_scr[u, 0:nrows, sl] * inv[:, sl]
        return imp

    pending = scores(0)
    imps = []
    for u in range(1, CMP_UNITS):
        nxt = scores(u)
        imps.append(finish(u - 1, pending))
        pending = nxt
    imps.append(finish(CMP_UNITS - 1, pending))

    blk = lax.broadcasted_iota(jnp.int32, (nrows, LANES), 0).astype(F32)

    def pick_one(imp):
        mx = jnp.max(imp, axis=0, keepdims=True)
        first = jnp.min(jnp.where(imp == mx, blk, float(nrows)), axis=0, keepdims=True)
        return jnp.where(blk == first, -jnp.inf, imp)

    curs, cands = [], []
    for u in range(CMP_UNITS):
        t = i * CMP_TQ + u * LANES + lax.broadcasted_iota(jnp.int32, (nrows, LANES), 1)
        cur = (t >> 6).astype(F32)
        forced = (blk == 0.0) | (blk == cur) | (blk == cur - 1.0)
        imp = jnp.where(forced, FORCE, imps[u])
        curs.append(cur)
        cands.append(jnp.where(blk <= cur, imp, NEG))
    cands = lax.fori_loop(0, min(SEL_TOPK, nrows), lambda _, c: tuple(pick_one(x) for x in c), tuple(cands))
    for u in range(CMP_UNITS):
        member = (cands[u] == -jnp.inf) & (blk <= curs[u])
        mm_ref[0:nrows, u * LANES:(u + 1) * LANES] = jnp.where(member, 0.0, -1.0).astype(BF16)
    if nrows < n_sel:
        mm_ref[nrows:n_sel, :] = jnp.full((n_sel - nrows, CMP_TQ), -1.0, BF16)


def _cmp_topk(qT, kc4, vc4, ov, cmp_bias):
    B, _, S = qT.shape
    nstep = S // CMP_TQ
    nch = kc4.shape[1]
    n_sel = S // SEL_BLOCK
    nq = NSA_HEADS * LANES
    nbias = cmp_bias.shape[0] // CMP_UNITS
    return pl.pallas_call(
        functools.partial(_cmp_kernel, n_sel=n_sel, nstep=nstep),
        grid=(B, nstep),
        in_specs=[pl.BlockSpec((None, NSA_HEADS * HEAD_DIM, CMP_TQ), lambda b, i: (b, A_Q_BLK, i)),
                  pl.BlockSpec((None, nch, LANES, LANES), lambda b, i: (b, 0, 0, 0)),
                  pl.BlockSpec((None, nch, VROWS, LANES), lambda b, i: (b, 0, 0, 0)),
                  pl.BlockSpec((40, LANES), lambda b, i: (0, 0)),
                  pl.BlockSpec((None, CMP_UNITS, LANES, LANES), lambda b, i: (i % nbias, 0, 0, 0))],
        out_specs=[pl.BlockSpec((None, NSA_HEADS * HEAD_DIM, CMP_TQ), lambda b, i: (b, 0, i)),
                   pl.BlockSpec((None, n_sel, CMP_TQ), lambda b, i: (b, 0, i))],
        out_shape=[jax.ShapeDtypeStruct((B, NSA_HEADS * HEAD_DIM, S), F32),
                   jax.ShapeDtypeStruct((B, n_sel, S), BF16)],
        scratch_shapes=[pltpu.VMEM((CMP_UNITS, 32 * nch + 64, nq), F32)],
        compiler_params=_cp(("parallel", "parallel")),
        name="nsa_cmp_topk",
    )(qT, kc4, vc4, ov, cmp_bias.reshape(nbias, CMP_UNITS, LANES, LANES))


SEL_GROUP = 8
SEL_TQ = 2 * LANES


def _sel_kernel(q_ref, mm_ref, ks_ref, v_ref, causal_ref, ocmp_ref, owin_ref, gate_ref, out_ref,
                qa_scr, mm_scr, s_scr):
    i = pl.program_id(1)
    nq = NSA_HEADS * SEL_TQ
    qa_scr[0:HEAD_DIM, :] = _stack_heads(q_ref, NSA_HEADS)
    qa_scr[HEAD_DIM:LANES, :] = jnp.zeros((LANES - HEAD_DIM, nq), BF16)
    mm = mm_ref[...]
    for h in range(NSA_HEADS):
        mm_scr[:, h * SEL_TQ:(h + 1) * SEL_TQ] = mm

    gkeys = SEL_GROUP * LANES

    def load_query(gi):
        rg = (gi * SEL_GROUP) // 8
        qa_scr[HEAD_DIM:HEAD_DIM + 16, :] = mm_scr[pl.ds(pl.multiple_of(rg * 16, 16), 16), :]
        return qa_scr[...]

    def chunk_scores(gi, u, qa):
        row = pl.multiple_of((gi * SEL_GROUP + u) * LANES, LANES)
        return jnp.dot(ks_ref[pl.ds(row, LANES), :], qa, preferred_element_type=F32)

    def weighted_values(gi, ps):
        vcat = jnp.concatenate([v_ref[gi * SEL_GROUP + u] for u in range(SEL_GROUP)], axis=1)
        return jnp.dot(vcat, jnp.concatenate(ps, axis=0), preferred_element_type=F32)

    def body(gi, carry):
        m, acc, mg = carry
        m_new = jnp.maximum(m, mg)
        qa = load_query(gi + 1)
        ps, mx = [], None
        for u in range(SEL_GROUP):
            rows = slice(u * LANES, (u + 1) * LANES)
            ps.append(jnp.exp(s_scr[rows, :] - m_new).astype(BF16))
            nxt = chunk_scores(gi + 1, u, qa)
            s_scr[rows, :] = nxt
            mx = nxt if mx is None else jnp.maximum(mx, nxt)
        acc = acc * jnp.exp(m - m_new) + weighted_values(gi, ps)
        return m_new, acc, jnp.max(mx, axis=0, keepdims=True)

    qa = load_query(0)
    mx = None
    for u in range(SEL_GROUP):
        s0 = chunk_scores(0, u, qa)
        s_scr[u * LANES:(u + 1) * LANES, :] = s0
        mx = s0 if mx is None else jnp.maximum(mx, s0)
    c0 = i * (SEL_TQ // LANES)
    last = c0 // SEL_GROUP
    m, acc, _ = lax.fori_loop(0, last, body, (jnp.full((1, nq), NEG, F32), jnp.zeros((VROWS, nq), F32),
                                              jnp.max(mx, axis=0, keepdims=True)))
    drow = pl.multiple_of((c0 % SEL_GROUP) * LANES, SEL_TQ)
    s_scr[pl.ds(drow, SEL_TQ), :] += causal_ref[...]
    s = s_scr[...]
    m_new = jnp.maximum(m, jnp.max(s, axis=0, keepdims=True))
    acc = acc * jnp.exp(m - m_new) + weighted_values(last, [jnp.exp(s - m_new).astype(BF16)])
    o = acc[0:HEAD_DIM] * (1.0 / acc[HEAD_DIM:HEAD_DIM + 1])

    g = gate_ref[...]
    for h in range(NSA_HEADS):
        rs = slice(HEAD_DIM * h, HEAD_DIM * (h + 1))
        out = (g[3 * h:3 * h + 1] * ocmp_ref[rs, :] + g[3 * h + 1:3 * h + 2] * o[:, h * SEL_TQ:(h + 1) * SEL_TQ]
               + g[3 * h + 2:3 * h + 3] * owin_ref[rs, :])
        out_ref[rs, :] = out.astype(BF16)


def _sel_attend(qT, mm, nk, v4, causal4, ocmp, owin, gates):
    B, _, S = qT.shape
    nstep = S // SEL_TQ
    n_sel = S // SEL_BLOCK
    nq = NSA_HEADS * SEL_TQ
    ar = NSA_HEADS * HEAD_DIM
    return pl.pallas_call(
        _sel_kernel,
        grid=(B, nstep),
        in_specs=[pl.BlockSpec((None, ar, SEL_TQ), lambda b, i: (b, A_Q_BLK, i)),
                  pl.BlockSpec((None, n_sel, SEL_TQ), lambda b, i: (b, 0, i)),
                  pl.BlockSpec((None, S, LANES), lambda b, i: (b, 0, 0)),
                  pl.BlockSpec((None, S // LANES, VROWS, LANES), lambda b, i: (b, 0, 0, 0)),
                  _const_spec((SEL_TQ, nq), lambda b, i: (0, 0)),
                  pl.BlockSpec((None, ar, SEL_TQ), lambda b, i: (b, 0, i)),
                  pl.BlockSpec((None, ar, SEL_TQ), lambda b, i: (b, 0, i)),
                  pl.BlockSpec((None, G_ROWS, SEL_TQ), lambda b, i: (b, 0, i))],
        out_specs=pl.BlockSpec((None, ar, SEL_TQ), lambda b, i: (b, 0, i)),
        out_shape=jax.ShapeDtypeStruct((B, ar, S), BF16),
        scratch_shapes=[pltpu.VMEM((LANES, nq), BF16), pltpu.VMEM((n_sel, nq), BF16),
                        pltpu.VMEM((SEL_GROUP * LANES, nq), F32)],
        compiler_params=_cp(("parallel", "parallel")),
        name="nsa_sel",
    )(qT, mm, nk, v4, causal4, ocmp, owin, gates)


def _banded_kernel(*refs, hkv, grp, nprev, span, zero_bias, dynamic, has_sink, want_lse):
    q_ref, k_ref, v_ref, bias_ref = refs[:4]
    pos = 4
    sink_ref = None
    if has_sink:
        sink_ref = refs[pos]
        pos += 1
    o_ref = refs[pos]
    lse_ref = refs[pos + 1] if want_lse else None
    j = pl.program_id(1)

    width = span * LANES
    nchunk = nprev + span

    def scores(s, g, first):
        lanes = slice(s * width, (s + 1) * width)
        qg = jnp.concatenate([q_ref[HEAD_DIM * (g * grp + u):HEAD_DIM * (g * grp + u + 1), lanes]
                              for u in range(grp)], axis=1)
        parts = []
        for ci in range(nchunk):
            if first:
                kc = s * span - nprev + ci
                if kc < 0:
                    continue
                kcc = kc
                row = kc * LANES
            else:
                kc = j * BAND_NSUB + s * span - nprev + ci
                kcc = jnp.maximum(kc, 0) if dynamic else kc
                row = pl.multiple_of(kcc * LANES, LANES)
            sc = jnp.dot(k_ref[pl.ds(row, LANES), HEAD_DIM * g:HEAD_DIM * (g + 1)], qg,
                         preferred_element_type=F32)
            if dynamic:
                sc = sc + bias_ref[jnp.where(kc >= 0, ci, nchunk)]
            elif not zero_bias[ci]:
                sc = sc + bias_ref[ci]
            parts.append((kcc, sc))
        return parts

    def finish(s, g, parts):
        lanes = slice(s * width, (s + 1) * width)
        m = None
        for _, sc in parts:
            mc = jnp.max(sc, axis=0, keepdims=True)
            m = mc if m is None else jnp.maximum(m, mc)
        if has_sink:
            sk = sink_ref[g, 0:1, :]
            m = jnp.maximum(m, sk)
        acc = None
        for kcc, sc in parts:
            t = jnp.dot(v_ref[kcc, VROWS * g:VROWS * (g + 1), :], jnp.exp(sc - m).astype(BF16),
                        preferred_element_type=F32)
            acc = t if acc is None else acc + t
        l = acc[HEAD_DIM:HEAD_DIM + 1]
        if has_sink:
            l = l + jnp.exp(sk - m)
        o = acc[0:HEAD_DIM] * (1.0 / l)
        for u in range(grp):
            hq = g * grp + u
            o_ref[HEAD_DIM * hq:HEAD_DIM * (hq + 1), lanes] = o[:, u * width:(u + 1) * width].astype(o_ref.dtype)
        if want_lse:
            lse_ref[8 * g:8 * (g + 1), lanes] = jnp.broadcast_to(m + jnp.log(l), (8, width))

    def run(first):
        pending = None
        for s in range(BAND_NSUB // span):
            for g in range(hkv):
                parts = scores(s, g, first)
                if pending is not None:
                    finish(*pending)
                pending = (s, g, parts)
        finish(*pending)

    if dynamic:
        run(False)
    else:
        pl.when(j == 0)(lambda: run(True))
        pl.when(j > 0)(lambda: run(False))


def _banded(qT, q_blk, nk, k_blk, v4, v_blk, bias_and_zero, *, hkv, grp, nprev, span, dynamic,
            out_dtype, sinks=None, want_lse=False, name):
    bias, zero_bias = bias_and_zero
    assert span == 1 or (grp == 1 and sinks is None)
    B, _, S = qT.shape
    nt = S // BAND_TILE
    qrows = hkv * grp * HEAD_DIM
    in_specs = [pl.BlockSpec((None, qrows, BAND_TILE), lambda b, j: (b, q_blk, j)),
                pl.BlockSpec((None, S, LANES), lambda b, j: (b, 0, k_blk)),
                pl.BlockSpec((None, S // LANES, hkv * VROWS, LANES), lambda b, j: (b, 0, v_blk, 0)),
                pl.BlockSpec(bias.shape, lambda b, j: (0, 0, 0))]
    args = [qT, nk, v4, bias]
    if sinks is not None:
        in_specs.append(pl.BlockSpec(sinks.shape, lambda b, j: (0, 0, 0)))
        args.append(sinks)
    out_specs = [pl.BlockSpec((None, qrows, BAND_TILE), lambda b, j: (b, 0, j))]
    out_shape = [jax.ShapeDtypeStruct((B, qrows, S), out_dtype)]
    if want_lse:
        out_specs.append(pl.BlockSpec((None, 8 * hkv, BAND_TILE), lambda b, j: (b, 0, j)))
        out_shape.append(jax.ShapeDtypeStruct((B, 8 * hkv, S), F32))
    return pl.pallas_call(
        functools.partial(_banded_kernel, hkv=hkv, grp=grp, nprev=nprev, span=span, zero_bias=zero_bias,
                          dynamic=dynamic, has_sink=sinks is not None, want_lse=want_lse),
        grid=(B, nt),
        in_specs=in_specs,
        out_specs=out_specs,
        out_shape=out_shape,
        compiler_params=_cp(("parallel", "parallel")),
        name=name,
    )(*args)


def _folded_kernel(q_ref, k_ref, v_ref, bias_ref, o_ref, lse_ref):
    j = pl.program_id(2)
    nsub = q_ref.shape[0] // LANES
    ones = jnp.ones((VROWS - HEAD_DIM, LANES), BF16)
    row = lax.broadcasted_iota(jnp.int32, (LANES, LANES), 0)

    def scores(s):
        qt = q_ref[s * LANES:(s + 1) * LANES, :].astype(F32).T.astype(BF16)
        parts = []
        for ci in range(2):
            kc = j * nsub + s - 1 + ci
            kcc = jnp.maximum(kc, 0)
            rows = pl.ds(pl.multiple_of(kcc * LANES, LANES), LANES)
            bias = bias_ref[jnp.where(kc >= 0, ci, 2)]
            vt = v_ref[rows, :].astype(F32).T.astype(BF16)
            for g in range(2):
                hs = slice(HEAD_DIM * g, HEAD_DIM * (g + 1))
                sc = jnp.dot(k_ref[rows, hs], qt[hs], preferred_element_type=F32) + bias
                parts.append((g, jnp.concatenate([vt[hs], ones], axis=0), sc))
        return parts

    def finish(s, parts):
        outs, lses = [], []
        for g in range(2):
            mine = [(v, sc) for gg, v, sc in parts if gg == g]
            m = functools.reduce(jnp.maximum, [jnp.max(sc, axis=0, keepdims=True) for _, sc in mine])
            acc = None
            for v, sc in mine:
                t = jnp.dot(v, jnp.exp(sc - m).astype(BF16), preferred_element_type=F32)
                acc = t if acc is None else acc + t
            l = acc[HEAD_DIM:HEAD_DIM + 1]
            outs.append(acc[0:HEAD_DIM] * (1.0 / l))
            lses.append(m + jnp.log(l))
        rows = slice(s * LANES, (s + 1) * LANES)
        o_ref[rows, :] = jnp.concatenate(outs, axis=0).T
        lse_ref[rows, :] = jnp.where(row == 0, lses[0], jnp.where(row == 1, lses[1], 0.0)).T

    pending = None
    for s in range(nsub):
        parts = scores(s)
        if pending is not None:
            finish(*pending)
        pending = (s, parts)
    finish(*pending)


def _folded_dilated(fq, fk, fv, bias, dil):
    B, n, _ = fq.shape
    tq = min(n, BAND_TILE)
    return pl.pallas_call(
        _folded_kernel,
        grid=(B, dil, n // tq),
        in_specs=[pl.BlockSpec((None, tq, LANES), lambda b, r, j: (b, j, r)),
                  pl.BlockSpec((None, n, LANES), lambda b, r, j: (b, 0, r)),
                  pl.BlockSpec((None, n, LANES), lambda b, r, j: (b, 0, r)),
                  pl.BlockSpec(bias.shape, lambda b, r, j: (0, 0, 0))],
        out_specs=[pl.BlockSpec((None, tq, LANES), lambda b, r, j: (b, j, r)),
                   pl.BlockSpec((None, tq, LANES), lambda b, r, j: (b, j, r))],
        out_shape=[jax.ShapeDtypeStruct((B, n, dil * LANES), F32),
                   jax.ShapeDtypeStruct((B, n, dil * LANES), F32)],
        compiler_params=_cp(("parallel", "parallel", "parallel")),
        name="dil%d_folded" % dil,
    )(fq, fk, fv, bias)


def _post_norm_residual(y, x_ref, gw_ref, o_ref, token_major=False):
    ms = jnp.mean(y * y, axis=0, keepdims=True)
    yn = y * lax.rsqrt(ms + RMS_EPS)
    gw = gw_ref[...]
    for c in range(NSUB):
        sl = slice(c * LANES, (c + 1) * LANES)
        out = x_ref[:, sl] + gw * yn[:, sl]
        if token_major:
            o_ref[sl, :] = out.T
        else:
            o_ref[:, sl] = out


def _mix_ffn_kernel(a_ref, b0_ref, b1_ref, b2_ref, l0_ref, l1_ref, l2_ref, c_ref, wo_ref, x_ref, gw1_ref,
                    a2_ref, sh2_ref, wg_ref, wu_ref, wd_ref, cv_ref, gw2_ref, o_ref,
                    mix_scr, x1_scr, h_scr, carry_scr, act_scr, unf_scr, *, fchunk, token_major_out):
    s = pl.program_id(1)
    cur = s % 2
    prv = 1 - cur
    d_ff = wg_ref.shape[0]

    @pl.when(s == 0)
    def _():
        carry_scr[...] = jnp.zeros(carry_scr.shape, F32)
        h_scr[1] = jnp.zeros(h_scr.shape[1:], BF16)
        x1_scr[1] = jnp.zeros(x1_scr.shape[1:], F32)

    h = h_scr[prv]
    lane = lax.broadcasted_iota(jnp.int32, (fchunk, LANES), 1)
    for c in range(d_ff // fchunk):
        rs = slice(c * fchunk, (c + 1) * fchunk)
        g = jnp.dot(wg_ref[rs, :], h, preferred_element_type=F32)
        prev = carry_scr[rs, :]
        carry_scr[rs, :] = g[:, TILE - LANES:TILE]
        g1 = pltpu.roll(g, 1, 1)
        g2 = pltpu.roll(g, 2, 1)
        fix1 = jnp.where(lane < 1, pltpu.roll(prev, 1, 1), g1[:, 0:LANES])
        fix2 = jnp.where(lane < 2, pltpu.roll(prev, 2, 1), g2[:, 0:LANES])
        g1 = jnp.concatenate([fix1, g1[:, LANES:]], axis=1)
        g2 = jnp.concatenate([fix2, g2[:, LANES:]], axis=1)
        w0 = jnp.concatenate([cv_ref[0, rs, :]] * NSUB, axis=1)
        w1 = jnp.concatenate([cv_ref[1, rs, :]] * NSUB, axis=1)
        w2 = jnp.concatenate([cv_ref[2, rs, :]] * NSUB, axis=1)
        cb = jnp.concatenate([cv_ref[3, rs, :]] * NSUB, axis=1)
        acc = g2 * w0 + g1 * w1 + g * w2 + cb
        up = jnp.dot(wu_ref[rs, :], h, preferred_element_type=F32)
        act_scr[rs, :] = (jax.nn.gelu(acc, approximate=True) * up).astype(BF16)

    def unfold(src_ref):
        for res in range(FOLD):
            unf_scr[pl.ds(res, TILE // FOLD, stride=FOLD), :] = src_ref[:, res * LANES:(res + 1) * LANES]
        return unf_scr[...].T

    b_vals = (b0_ref[...], b1_ref[...], unfold(b2_ref))
    l2t = unfold(l2_ref)
    for hh in range(2):
        ls = [l0_ref[8 * hh:8 * hh + 1, :], l1_ref[8 * hh:8 * hh + 1, :], l2t[hh:hh + 1, :]]
        mx = jnp.maximum(jnp.maximum(ls[0], ls[1]), ls[2])
        es = [jnp.exp(v - mx) for v in ls]
        inv = 1.0 / (es[0] + es[1] + es[2])
        for g in range(3):
            rs = slice(HEAD_DIM * hh, HEAD_DIM * (hh + 1))
            mix_scr[128 * g + HEAD_DIM * hh:128 * g + HEAD_DIM * (hh + 1), :] = (
                b_vals[g][rs, :] * (es[g] * inv)).astype(BF16)
    y1 = jnp.dot(wo_ref[:, 0:256], a_ref[...], preferred_element_type=F32)
    y1 = y1 + jnp.dot(wo_ref[:, 256:640], mix_scr[...], preferred_element_type=F32)
    y1 = y1 + jnp.dot(wo_ref[:, 640:1024], c_ref[...], preferred_element_type=F32)

    y2 = jnp.dot(wd_ref[...], act_scr[...], preferred_element_type=F32)

    _post_norm_residual(y1, x_ref, gw1_ref, x1_scr.at[cur])
    _norm_mod_to_scratch(x1_scr.at[cur], a2_ref, sh2_ref, h_scr.at[cur])
    _post_norm_residual(y2, x1_scr.at[prv], gw2_ref, o_ref, token_major_out)


def _mix_ffn(aT, bs, lses, cT, w_outT, xT, gw1, a2, sh2, wgT, wuT, wdT, cv, gw2, layer, token_major_out):
    B, D, S = xT.shape
    nt = S // TILE
    d_ff = wgT.shape[1]
    tile = lambda rows: pl.BlockSpec((None, rows, TILE), lambda b, s: (b, 0, jnp.minimum(s, nt - 1)))
    tok = pl.BlockSpec((None, TILE // FOLD, FOLD * LANES), lambda b, s: (b, jnp.minimum(s, nt - 1), 0))
    vec = pl.BlockSpec((None, D, LANES), lambda b, s: (b, 0, 0))
    return pl.pallas_call(
        functools.partial(_mix_ffn_kernel, fchunk=256, token_major_out=token_major_out),
        grid=(B, nt + 1),
        in_specs=[tile(256), tile(128), tile(128), tok, tile(16), tile(16), tok, tile(384),
                  _const_spec((None, D, D), lambda b, s: (layer, 0, 0)),
                  tile(D), vec, vec, vec,
                  _const_spec((None, d_ff, D), lambda b, s: (layer, 0, 0)),
                  _const_spec((None, d_ff, D), lambda b, s: (layer, 0, 0)),
                  _const_spec((None, D, d_ff), lambda b, s: (layer, 0, 0)),
                  _const_spec((None, 4, d_ff, LANES), lambda b, s: (layer, 0, 0, 0)),
                  vec],
        out_specs=(pl.BlockSpec((None, TILE, D), lambda b, s: (b, jnp.maximum(s - 1, 0), 0)) if token_major_out
                   else pl.BlockSpec((None, D, TILE), lambda b, s: (b, 0, jnp.maximum(s - 1, 0)))),
        out_shape=jax.ShapeDtypeStruct((B, S, D) if token_major_out else (B, D, S), F32),
        scratch_shapes=[pltpu.VMEM((384, TILE), BF16), pltpu.VMEM((2, D, TILE), F32),
                        pltpu.VMEM((2, D, TILE), BF16), pltpu.VMEM((d_ff, LANES), F32),
                        pltpu.VMEM((d_ff, TILE), BF16), pltpu.VMEM((TILE, LANES), F32)],
        compiler_params=_cp(("arbitrary", "arbitrary")),
        name="mix_ffn",
    )(aT, bs[0], bs[1], bs[2], lses[0], lses[1], lses[2], cT, w_outT, xT, gw1,
      a2, sh2, wgT, wuT, wdT, cv, gw2)


def _band_bias(dilation, old_edge, reps, nprev, span):
    kk = np.arange(LANES)[:, None]
    qq = np.arange(LANES)[None, :]
    res = ((qq - kk) % dilation) == 0
    none = np.zeros_like(res)

    def role(ci):
        if ci < 0 or ci > nprev:
            return none
        return res & (kk - qq >= old_edge) if ci == 0 else (res & (kk <= qq) if ci == nprev else res)

    tabs = [np.concatenate([role(u - a) for a in range(span)], axis=1) for u in range(nprev + span)]
    zero = tuple(bool(t.all()) for t in tabs)
    tabs.append(np.concatenate([none] * span, axis=1))
    out = np.stack([np.where(t, 0.0, NEG) for t in tabs]).astype(np.float32)
    return jnp.asarray(np.tile(out, (1, 1, reps))), zero


def _cmp_bias():
    nn = np.arange(LANES)[:, None]
    qq = np.arange(LANES)[None, :]
    tabs = [np.where(CMP_STRIDE * nn + CMP_BLOCK - 1 <= LANES * r + qq, 0.0, NEG) for r in range(16)]
    return jnp.asarray(np.stack(tabs).astype(np.float32))


def _overlap_rows():
    jj = np.arange(40)[:, None]
    nn = np.arange(LANES)[None, :]
    return jnp.asarray(((nn >= 4 * jj - 1) & (nn <= 4 * jj + 3)).astype(np.float32), dtype=BF16)


_IN_COL_ORDER = ((1804, 2188), (652, 1036), (0, 256),
                 (256, 320), (320, 384), (384, 448), (512, 576), (1036, 1420), (2188, 2316),
                 (448, 512), (576, 640), (1420, 1804), (2316, 2444),
                 (640, 652))


def _prep_w_in(w_in):
    wt = jnp.swapaxes(jnp.concatenate([w_in[:, :, a:b] for a, b in _IN_COL_ORDER], axis=2), 1, 2)
    scale = np.ones((wt.shape[1], 1), np.float32)
    scale[:Q_ROWS] = HEAD_DIM ** -0.5
    wt = wt * scale
    wt = jnp.pad(wt, ((0, 0), (0, W_ROWS - wt.shape[1]), (0, 0)))
    return wt.astype(BF16)


def _prep_compress(w_ck, w_cv, pe_k, pe_v):
    L = w_ck.shape[0]
    half = CMP_BLOCK // 2

    def big(lo):
        wk = w_ck[:, lo:lo + half]
        wv = w_cv[:, lo:lo + half]
        z = jnp.zeros_like(wk)
        top = jnp.concatenate([wk, z], axis=-1)
        bot = jnp.concatenate([z, wv], axis=-1)
        return jnp.concatenate([top, bot], axis=2).reshape(L, half * LANES, LANES)

    wbig = jnp.concatenate([big(0), big(half)], axis=-1).astype(BF16)
    pe = jnp.concatenate([pe_k, pe_v], axis=-1)
    pe2 = jnp.zeros((L, 16, half * LANES), F32)
    pe2 = pe2.at[:, 0].set(pe[:, :half].reshape(L, -1)).at[:, 8].set(pe[:, half:].reshape(L, -1))
    return wbig, pe2.astype(BF16)


def _lane_bcast(v):
    return jnp.broadcast_to(v[..., None], v.shape + (LANES,))


def kernel(x, c, positions, w_in, w_out, w_ada, b_ada, norm_w, cmp_w_k, cmp_w_v, cmp_pe_k, cmp_pe_v,
           sinks, w_gate, w_up, conv_w, conv_b, w_down):
    B, S, D = x.shape
    depth = w_in.shape[0]
    assert S % 2048 == 0 and D == 1024 and w_in.shape[2] == 2444

    inv = ROPE_THETA ** (-jnp.arange(0, HEAD_DIM, 2, dtype=F32) / HEAD_DIM)
    ang = positions.astype(F32)[:, None, :] * inv[None, :, None]
    cosT, sinT = jnp.cos(ang), jnp.sin(ang)

    c8 = jnp.pad(c, ((0, 8 - B), (0, 0)))
    ada = _adaln(c8, w_ada, b_ada)[:, :B]
    sh1, sc1, g1, sh2, sc2, g2 = [ada[:, :, k * D:(k + 1) * D] for k in range(6)]
    nw = norm_w[:, :, None, :]
    a1 = _lane_bcast(nw[:, 0] * (1 + sc1))
    gw1 = _lane_bcast(g1 * nw[:, 1])
    a2 = _lane_bcast(nw[:, 2] * (1 + sc2))
    gw2 = _lane_bcast(g2 * nw[:, 3])
    sh1b, sh2b = _lane_bcast(sh1), _lane_bcast(sh2)

    w_inT = _prep_w_in(w_in)
    w_outT = jnp.swapaxes(w_out, 1, 2).astype(BF16)
    wgT = jnp.swapaxes(w_gate, 1, 2).astype(BF16)
    wuT = jnp.swapaxes(w_up, 1, 2).astype(BF16)
    wdT = jnp.swapaxes(w_down, 1, 2).astype(BF16)
    cv = _lane_bcast(jnp.concatenate([conv_w, conv_b[:, None, :]], axis=1))
    wbig, pe2 = _prep_compress(cmp_w_k, cmp_w_v, cmp_pe_k, cmp_pe_v)
    sink_tab = jnp.broadcast_to(
        jnp.repeat(sinks.reshape(depth, SWA_KV_HEADS, SWA_HEADS // SWA_KV_HEADS), LANES, axis=-1)[:, :, None, :],
        (depth, SWA_KV_HEADS, 8, LANES * (SWA_HEADS // SWA_KV_HEADS)))

    ov = _overlap_rows()
    cmp_bias = _cmp_bias()
    kk = np.arange(SEL_TQ)[:, None]
    causal4 = jnp.asarray(np.tile(np.where(kk <= kk.T, 0.0, NEG).astype(np.float32), (1, NSA_HEADS)))
    bias_win = _band_bias(1, 1, NSA_HEADS, NSA_WINDOW // LANES, 1)
    bias_swa = _band_bias(1, 1, SWA_HEADS // SWA_KV_HEADS, SWA_WINDOW // LANES, 1)
    dil_span = [2 if d >= 4 else 1 for _, d in DIL_PATTERNS[:2]]
    bias_dil = [_band_bias(d, 0, 1, d, sp) for (_, d), sp in zip(DIL_PATTERNS[:2], dil_span)]
    assert DIL_PATTERNS[0] == (LANES, 1) and DIL_PATTERNS[2][0] // DIL_PATTERNS[2][1] == LANES

    xT = x
    for l in range(depth):
        first, last_layer = l == 0, l == depth - 1
        outs = _inproj(xT, a1[l], sh1b[l], w_inT, l, cosT, sinT, token_major=first)
        qT, kcvc, nk, v4, gates, fq, fk, fv = outs[:8]
        if first:
            xT = outs[8]
        kc4, vc4 = _compress(kcvc, wbig, pe2, l)
        ocmp, mm = _cmp_topk(qT, kc4, vc4, ov, cmp_bias)
        owin, = _banded(qT, A_Q_BLK, nk, 1, v4, 1, bias_win, hkv=1, grp=NSA_HEADS, nprev=NSA_WINDOW // LANES,
                        span=1, dynamic=False, out_dtype=F32, name="nsa_win")
        aT = _sel_attend(qT, mm, nk, v4, causal4, ocmp, owin, gates)
        bs, lses = [], []
        for gi, (win, dil) in enumerate(DIL_PATTERNS[:2]):
            o, lse = _banded(qT, 3 + gi, nk, 2 + gi, v4, 1 + gi, bias_dil[gi], hkv=2, grp=1, nprev=dil,
                             span=dil_span[gi], dynamic=dil > BAND_NSUB, out_dtype=F32, want_lse=True,
                             name="dil%d" % dil)
            bs.append(o)
            lses.append(lse)
        o, lse = _folded_dilated(fq, fk, fv, bias_dil[0][0], DIL_PATTERNS[2][1])
        bs.append(o)
        lses.append(lse)
        cT, = _banded(qT, 0, nk, 5, v4, 4, bias_swa, hkv=SWA_KV_HEADS, grp=SWA_HEADS // SWA_KV_HEADS,
                      nprev=SWA_WINDOW // LANES, span=1, dynamic=False, out_dtype=BF16,
                      sinks=sink_tab[l], name="swa")
        xT = _mix_ffn(aT, bs, lses, cT, w_outT, xT, gw1[l], a2[l], sh2b[l], wgT, wuT, wdT, cv, gw2[l], l,
                      token_major_out=last_layer)
    return xT
```

```python
import functools

import numpy as np
import jax
import jax.numpy as jnp
from jax import lax
from jax.experimental import pallas as pl
from jax.experimental.pallas import tpu as pltpu

F32 = jnp.float32
BF16 = jnp.bfloat16

HEAD_DIM = 64
HALF = HEAD_DIM // 2
NSA_HEADS = 4
CMP_BLOCK = 32
CMP_STRIDE = 16
SEL_BLOCK = 64
SEL_TOPK = 16
NSA_WINDOW = 512
DIL_PATTERNS = ((128, 1), (512, 4), (2048, 16))
DIL_HEADS = 6
SWA_HEADS = 6
SWA_KV_HEADS = 2
SWA_WINDOW = 128
ROPE_THETA = 10000.0
RMS_EPS = 1e-6
NEG = -1e30
FORCE = 1e4
CONV_WIDTH = 3

LANES = 128
TILE = 512
NSUB = TILE // LANES
BAND_TILE = 1024
BAND_NSUB = BAND_TILE // LANES
VROWS = HEAD_DIM + 16
MEMBER_BIG = 2.0 ** 100
VMEM_LIMIT = 56 * 1024 * 1024

Q_ROWS = 1024
A_Q_BLK = 3
NK_ROWS = 768
NK_LANES = 768
V_ROWS = 640
G_ROWS = 16
W_ROWS = Q_ROWS + NK_ROWS + V_ROWS + G_ROWS
N_VPIECES = V_ROWS // HEAD_DIM
FOLD_ROWS = slice(256, 384)
FOLD = DIL_PATTERNS[2][1]
assert FOLD == CMP_STRIDE


def _cp(sem):
    return pltpu.CompilerParams(dimension_semantics=sem, vmem_limit_bytes=VMEM_LIMIT)


def _const_spec(shape, index_map):
    return pl.BlockSpec(shape, index_map, pipeline_mode=pl.Buffered(1))


def _adaln_kernel(c_ref, w_ref, b_ref, o_ref):
    c = c_ref[...]
    cond = c * jax.nn.sigmoid(c)
    o_ref[...] = jnp.dot(cond, w_ref[...], preferred_element_type=F32,
                         precision=lax.Precision.HIGHEST) + b_ref[...]


def _adaln(c8, w_ada, b_ada):
    depth, d, six_d = w_ada.shape
    nblk = six_d // d
    return pl.pallas_call(
        _adaln_kernel,
        grid=(depth, nblk),
        in_specs=[pl.BlockSpec((8, d), lambda l, n: (0, 0)),
                  pl.BlockSpec((None, d, d), lambda l, n: (l, 0, n)),
                  pl.BlockSpec((None, 1, d), lambda l, n: (l, 0, n))],
        out_specs=pl.BlockSpec((None, 8, d), lambda l, n: (l, 0, n)),
        out_shape=jax.ShapeDtypeStruct((depth, 8, six_d), F32),
        compiler_params=_cp(("parallel", "parallel")),
        name="adaln",
    )(c8, w_ada, b_ada.reshape(depth, 1, six_d))


def _norm_mod_to_scratch(x_ref, a_ref, sh_ref, h_scr):
    for c in range(NSUB):
        sl = slice(c * LANES, (c + 1) * LANES)
        xs = x_ref[:, sl]
        ms = jnp.mean(xs * xs, axis=0, keepdims=True)
        h_scr[:, sl] = ((xs * lax.rsqrt(ms + RMS_EPS)) * a_ref[...] + sh_ref[...]).astype(BF16)


def _inproj_kernel(x_ref, a_ref, sh_ref, w_ref, cos_ref, sin_ref,
                   q_ref, kcvc_ref, nk_ref, v4_ref, gate_ref, fq_ref, fk_ref, fv_ref, *rest, token_major):
    j = pl.program_id(1)
    if token_major:
        xt_ref, h_scr, fold_scr = rest
        for c in range(NSUB):
            xt_ref[:, c * LANES:(c + 1) * LANES] = x_ref[c * LANES:(c + 1) * LANES, :].T
        x_ref = xt_ref
    else:
        h_scr, fold_scr = rest
    _norm_mod_to_scratch(x_ref, a_ref, sh_ref, h_scr)
    h = h_scr[...]
    cos = cos_ref[...]
    sin = sin_ref[...]

    def proj(r0, r1):
        return jnp.dot(w_ref[r0:r1, :], h, preferred_element_type=F32)

    def fold_store(t, dst_ref):
        fold_scr[...] = t
        for res in range(FOLD):
            dst_ref[:, res * LANES:(res + 1) * LANES] = fold_scr[pl.ds(res, TILE // FOLD, stride=FOLD), :].astype(BF16)

    def rope(r, nh):
        outs = []
        for hh in range(nh):
            t1 = r[HEAD_DIM * hh:HEAD_DIM * hh + HALF]
            t2 = r[HEAD_DIM * hh + HALF:HEAD_DIM * (hh + 1)]
            outs.append(t1 * cos - t2 * sin)
            outs.append(t2 * cos + t1 * sin)
        return jnp.concatenate(outs, axis=0)

    for r0, r1 in ((0, 384), (384, 768), (768, 1024)):
        r = rope(proj(r0, r1), (r1 - r0) // HEAD_DIM)
        q_ref[r0:r1, :] = r.astype(BF16)
        if r0 == 384:
            fold_store(r[FOLD_ROWS].T, fq_ref)

    base = Q_ROWS
    r = proj(base, base + 128)
    kcvc = jnp.concatenate([rope(r[0:64], 1), r[64:128]], axis=0)
    fold_store(kcvc.T, kcvc_ref)

    r = proj(base + 128, base + 192)
    tok = j * TILE + lax.broadcasted_iota(jnp.int32, (HEAD_DIM, TILE), 1)
    row = lax.broadcasted_iota(jnp.int32, (HEAD_DIM, TILE), 0)
    member_cols = jnp.where(row == ((tok >> 6) & 15), MEMBER_BIG, 0.0).astype(F32)
    nk_ref[:, 0:128] = jnp.concatenate([rope(r, 1), member_cols], axis=0).T.astype(BF16)

    r = proj(base + 192, base + 256)
    nk_ref[:, 128:256] = jnp.concatenate([rope(r, 1), jnp.zeros((HEAD_DIM, TILE), F32)], axis=0).T.astype(BF16)

    r = proj(base + 256, base + 640)
    bk = rope(r, 6).T
    nk_ref[:, 256:640] = bk.astype(BF16)
    fold_store(bk[:, FOLD_ROWS], fk_ref)

    r = proj(base + 640, base + 768)
    nk_ref[:, 640:768] = rope(r, 2).T.astype(BF16)

    base = Q_ROWS + NK_ROWS
    r = proj(base, base + V_ROWS)
    fold_store(r[128 + FOLD_ROWS.start:128 + FOLD_ROWS.stop].T, fv_ref)
    r = r.astype(BF16)
    ones = jnp.ones((VROWS - HEAD_DIM, LANES), BF16)
    for c in range(NSUB):
        for p in range(N_VPIECES):
            v4_ref[c, VROWS * p:VROWS * p + HEAD_DIM, :] = r[HEAD_DIM * p:HEAD_DIM * (p + 1),
                                                             c * LANES:(c + 1) * LANES]
            v4_ref[c, VROWS * p + HEAD_DIM:VROWS * (p + 1), :] = ones

    base = Q_ROWS + NK_ROWS + V_ROWS
    gate_ref[...] = jax.nn.sigmoid(proj(base, base + G_ROWS))


def _inproj(x, a1, sh1, w_inT, layer, cosT, sinT, token_major):
    if token_major:
        B, S, D = x.shape
        x_spec = pl.BlockSpec((None, TILE, D), lambda b, j: (b, j, 0))
    else:
        B, D, S = x.shape
        x_spec = pl.BlockSpec((None, D, TILE), lambda b, j: (b, 0, j))
    nt = S // TILE
    fold_spec = pl.BlockSpec((None, TILE // FOLD, FOLD * LANES), lambda b, j: (b, j, 0))
    out_specs = [pl.BlockSpec((None, Q_ROWS, TILE), lambda b, j: (b, 0, j)),
                 fold_spec,
                 pl.BlockSpec((None, TILE, NK_LANES), lambda b, j: (b, j, 0)),
                 pl.BlockSpec((None, NSUB, N_VPIECES * VROWS, LANES), lambda b, j: (b, j, 0, 0)),
                 pl.BlockSpec((None, G_ROWS, TILE), lambda b, j: (b, 0, j)),
                 fold_spec, fold_spec, fold_spec]
    out_shape = [jax.ShapeDtypeStruct((B, Q_ROWS, S), BF16),
                 jax.ShapeDtypeStruct((B, S // FOLD, FOLD * LANES), BF16),
                 jax.ShapeDtypeStruct((B, S, NK_LANES), BF16),
                 jax.ShapeDtypeStruct((B, S // LANES, N_VPIECES * VROWS, LANES), BF16),
                 jax.ShapeDtypeStruct((B, G_ROWS, S), F32)] + [
                     jax.ShapeDtypeStruct((B, S // FOLD, FOLD * LANES), BF16)] * 3
    if token_major:
        out_specs.append(pl.BlockSpec((None, D, TILE), lambda b, j: (b, 0, j)))
        out_shape.append(jax.ShapeDtypeStruct((B, D, S), F32))
    return pl.pallas_call(
        functools.partial(_inproj_kernel, token_major=token_major),
        grid=(B, nt),
        in_specs=[x_spec,
                  pl.BlockSpec((None, D, LANES), lambda b, j: (b, 0, 0)),
                  pl.BlockSpec((None, D, LANES), lambda b, j: (b, 0, 0)),
                  _const_spec((None, W_ROWS, D), lambda b, j: (layer, 0, 0)),
                  pl.BlockSpec((None, HALF, TILE), lambda b, j: (b, 0, j)),
                  pl.BlockSpec((None, HALF, TILE), lambda b, j: (b, 0, j))],
        out_specs=out_specs,
        out_shape=out_shape,
        scratch_shapes=[pltpu.VMEM((D, TILE), BF16), pltpu.VMEM((TILE, LANES), F32)],
        compiler_params=_cp(("parallel", "parallel")),
        name="inproj",
    )(x, a1, sh1, w_inT, cosT, sinT)


def _compress_kernel(t_ref, w_ref, pe_ref, kc_ref, vc_ref):
    n = t_ref.shape[0]
    a = jnp.dot(t_ref[...], w_ref[...], preferred_element_type=F32)
    pc = jnp.dot(pe_ref[...], w_ref[...], preferred_element_type=F32)
    const = pc[0:1, 0:LANES] + pc[8:9, LANES:2 * LANES]
    cmp = a[:, 0:LANES] + pltpu.roll(a[:, LANES:2 * LANES], n - 1, 0) + const
    cmp_t = cmp.T
    ones = jnp.ones((VROWS - HEAD_DIM, LANES), BF16)
    for c in range(n // LANES):
        kc_ref[c] = cmp[c * LANES:(c + 1) * LANES].astype(BF16)
        vc_ref[c, 0:HEAD_DIM, :] = cmp_t[HEAD_DIM:2 * HEAD_DIM, c * LANES:(c + 1) * LANES].astype(BF16)
        vc_ref[c, HEAD_DIM:VROWS, :] = ones


def _compress(tview, wbig, pe2, layer):
    B, n, _ = tview.shape
    nch = n // LANES
    return pl.pallas_call(
        _compress_kernel,
        grid=(B,),
        in_specs=[pl.BlockSpec((None, n, CMP_STRIDE * LANES), lambda b: (b, 0, 0)),
                  pl.BlockSpec((None, CMP_STRIDE * LANES, 2 * LANES), lambda b: (layer, 0, 0)),
                  pl.BlockSpec((None, 16, CMP_STRIDE * LANES), lambda b: (layer, 0, 0))],
        out_specs=[pl.BlockSpec((None, nch, LANES, LANES), lambda b: (b, 0, 0, 0)),
                   pl.BlockSpec((None, nch, VROWS, LANES), lambda b: (b, 0, 0, 0))],
        out_shape=[jax.ShapeDtypeStruct((B, nch, LANES, LANES), BF16),
                   jax.ShapeDtypeStruct((B, nch, VROWS, LANES), BF16)],
        compiler_params=_cp(("parallel",)),
        name="compress",
    )(tview, wbig, pe2)


def _stack_heads(q_ref, nh, lane_slice=slice(None)):
    return jnp.concatenate([q_ref[HEAD_DIM * h:HEAD_DIM * (h + 1), lane_slice] for h in range(nh)], axis=1)


CMP_UNITS = 2
CMP_TQ = CMP_UNITS * LANES


def _cmp_kernel(q_ref, kc_ref, vc_ref, ov_ref, bias_ref, o_ref, mm_ref, imp_scr, *, n_sel, nstep):
    i = pl.program_id(1)
    nch_all = kc_ref.shape[0]
    parts = 4 if nch_all % 4 == 0 else (2 if nch_all % 2 == 0 else 1)
    for v in range(parts):
        pl.when(i // (nstep // parts) == v)(functools.partial(
            _cmp_body, q_ref, kc_ref, vc_ref, ov_ref, bias_ref, o_ref, mm_ref, imp_scr,
            nch=nch_all * (v + 1) // parts, nrows=n_sel * (v + 1) // parts, n_sel=n_sel))


def _cmp_body(q_ref, kc_ref, vc_ref, ov_ref, bias_ref, o_ref, mm_ref, imp_scr, *, nch, nrows, n_sel):
    i = pl.program_id(1)
    cd = (i * CMP_UNITS) // 16

    def scores(u):
        q = _stack_heads(q_ref, NSA_HEADS, slice(u * LANES, (u + 1) * LANES))
        edge = bias_ref[u]
        ss = []
        for c in range(nch):
            b = jnp.where(c < cd, 0.0, jnp.where(c == cd, edge, NEG))
            s = jnp.dot(kc_ref[c, :, 0:HEAD_DIM], q, preferred_element_type=F32)
            ss.append(s + jnp.concatenate([b] * NSA_HEADS, axis=1))
        return ss

    def finish(u, ss):
        lanes = slice(u * LANES, (u + 1) * LANES)
        m = jnp.max(functools.reduce(jnp.maximum, ss), axis=0, keepdims=True)
        valid = m > 0.5 * NEG
        imp_scr[u, 0:32 * nch + 8, :] = jnp.zeros((32 * nch + 8, imp_scr.shape[2]), F32)
        acc = None
        for c in range(nch):
            p = jnp.exp(ss[c] - m).astype(BF16)
            t = jnp.dot(vc_ref[c], p, preferred_element_type=F32)
            acc = t if acc is None else acc + t
            imp_scr[u, 32 * c:32 * c + 40, :] += jnp.dot(ov_ref[...], p, preferred_element_type=F32)
        inv = jnp.where(valid, 1.0 / acc[HEAD_DIM:HEAD_DIM + 1], 0.0)
        o = acc[0:HEAD_DIM] * inv
        for h in range(NSA_HEADS):
            o_ref[HEAD_DIM * h:HEAD_DIM * (h + 1), lanes] = o[:, h * LANES:(h + 1) * LANES]
        imp = jnp.zeros((nrows, LANES), F32)
        for h in range(NSA_HEADS):
            sl = slice(h * LANES, (h + 1) * LANES)
            imp = imp + imp_scr[u, 0:nrows, sl] * inv[:, sl]
        return imp

    pending = scores(0)
    imps = []
    for u in range(1, CMP_UNITS):
        nxt = scores(u)
        imps.append(finish(u - 1, pending))
        pending = nxt
    imps.append(finish(CMP_UNITS - 1, pending))

    blk = lax.broadcasted_iota(jnp.int32, (nrows, LANES), 0).astype(F32)

    def pick_one(imp):
        mx = jnp.max(imp, axis=0, keepdims=True)
        first = jnp.min(jnp.where(imp == mx, blk, float(nrows)), axis=0, keepdims=True)
        return jnp.where(blk == first, -jnp.inf, imp)

    curs, cands = [], []
    for u in range(CMP_UNITS):
        t = i * CMP_TQ + u * LANES + lax.broadcasted_iota(jnp.int32, (nrows, LANES), 1)
        cur = (t >> 6).astype(F32)
        forced = (blk == 0.0) | (blk == cur) | (blk == cur - 1.0)
        imp = jnp.where(forced, FORCE, imps[u])
        curs.append(cur)
        cands.append(jnp.where(blk <= cur, imp, NEG))
    cands = lax.fori_loop(0, min(SEL_TOPK, nrows), lambda _, c: tuple(pick_one(x) for x in c), tuple(cands))
    for u in range(CMP_UNITS):
        member = (cands[u] == -jnp.inf) & (blk <= curs[u])
        mm_ref[0:nrows, u * LANES:(u + 1) * LANES] = jnp.where(member, 0.0, -1.0).astype(BF16)
    if nrows < n_sel:
        mm_ref[nrows:n_sel, :] = jnp.full((n_sel - nrows, CMP_TQ), -1.0, BF16)


def _cmp_topk(qT, kc4, vc4, ov, cmp_bias):
    B, _, S = qT.shape
    nstep = S // CMP_TQ
    nch = kc4.shape[1]
    n_sel = S // SEL_BLOCK
    nq = NSA_HEADS * LANES
    nbias = cmp_bias.shape[0] // CMP_UNITS
    return pl.pallas_call(
        functools.partial(_cmp_kernel, n_sel=n_sel, nstep=nstep),
        grid=(B, nstep),
        in_specs=[pl.BlockSpec((None, NSA_HEADS * HEAD_DIM, CMP_TQ), lambda b, i: (b, A_Q_BLK, i)),
                  pl.BlockSpec((None, nch, LANES, LANES), lambda b, i: (b, 0, 0, 0)),
                  pl.BlockSpec((None, nch, VROWS, LANES), lambda b, i: (b, 0, 0, 0)),
                  pl.BlockSpec((40, LANES), lambda b, i: (0, 0)),
                  pl.BlockSpec((None, CMP_UNITS, LANES, LANES), lambda b, i: (i % nbias, 0, 0, 0))],
        out_specs=[pl.BlockSpec((None, NSA_HEADS * HEAD_DIM, CMP_TQ), lambda b, i: (b, 0, i)),
                   pl.BlockSpec((None, n_sel, CMP_TQ), lambda b, i: (b, 0, i))],
        out_shape=[jax.ShapeDtypeStruct((B, NSA_HEADS * HEAD_DIM, S), F32),
                   jax.ShapeDtypeStruct((B, n_sel, S), BF16)],
        scratch_shapes=[pltpu.VMEM((CMP_UNITS, 32 * nch + 64, nq), F32)],
        compiler_params=_cp(("parallel", "parallel")),
        name="nsa_cmp_topk",
    )(qT, kc4, vc4, ov, cmp_bias.reshape(nbias, CMP_UNITS, LANES, LANES))


SEL_GROUP = 8
SEL_TQ = 2 * LANES


def _sel_kernel(q_ref, mm_ref, ks_ref, v_ref, causal_ref, ocmp_ref, owin_ref, gate_ref, out_ref,
                qa_scr, mm_scr, s_scr):
    i = pl.program_id(1)
    nq = NSA_HEADS * SEL_TQ
    qa_scr[0:HEAD_DIM, :] = _stack_heads(q_ref, NSA_HEADS)
    qa_scr[HEAD_DIM:LANES, :] = jnp.zeros((LANES - HEAD_DIM, nq), BF16)
    mm = mm_ref[...]
    for h in range(NSA_HEADS):
        mm_scr[:, h * SEL_TQ:(h + 1) * SEL_TQ] = mm

    gkeys = SEL_GROUP * LANES

    def load_query(gi):
        rg = (gi * SEL_GROUP) // 8
        qa_scr[HEAD_DIM:HEAD_DIM + 16, :] = mm_scr[pl.ds(pl.multiple_of(rg * 16, 16), 16), :]
        return qa_scr[...]

    def chunk_scores(gi, u, qa):
        row = pl.multiple_of((gi * SEL_GROUP + u) * LANES, LANES)
        return jnp.dot(ks_ref[pl.ds(row, LANES), :], qa, preferred_element_type=F32)

    def weighted_values(gi, ps):
        vcat = jnp.concatenate([v_ref[gi * SEL_GROUP + u] for u in range(SEL_GROUP)], axis=1)
        return jnp.dot(vcat, jnp.concatenate(ps, axis=0), preferred_element_type=F32)

    def body(gi, carry):
        m, acc, mg = carry
        m_new = jnp.maximum(m, mg)
        qa = load_query(gi + 1)
        ps, mx = [], None
        for u in range(SEL_GROUP):
            rows = slice(u * LANES, (u + 1) * LANES)
            ps.append(jnp.exp(s_scr[rows, :] - m_new).astype(BF16))
            nxt = chunk_scores(gi + 1, u, qa)
            s_scr[rows, :] = nxt
            mx = nxt if mx is None else jnp.maximum(mx, nxt)
        acc = acc * jnp.exp(m - m_new) + weighted_values(gi, ps)
        return m_new, acc, jnp.max(mx, axis=0, keepdims=True)

    qa = load_query(0)
    mx = None
    for u in range(SEL_GROUP):
        s0 = chunk_scores(0, u, qa)
        s_scr[u * LANES:(u + 1) * LANES, :] = s0
        mx = s0 if mx is None else jnp.maximum(mx, s0)
    c0 = i * (SEL_TQ // LANES)
    last = c0 // SEL_GROUP
    m, acc, _ = lax.fori_loop(0, last, body, (jnp.full((1, nq), NEG, F32), jnp.zeros((VROWS, nq), F32),
                                              jnp.max(mx, axis=0, keepdims=True)))
    drow = pl.multiple_of((c0 % SEL_GROUP) * LANES, SEL_TQ)
    s_scr[pl.ds(drow, SEL_TQ), :] += causal_ref[...]
    s = s_scr[...]
    m_new = jnp.maximum(m, jnp.max(s, axis=0, keepdims=True))
    acc = acc * jnp.exp(m - m_new) + weighted_values(last, [jnp.exp(s - m_new).astype(BF16)])
    o = acc[0:HEAD_DIM] * (1.0 / acc[HEAD_DIM:HEAD_DIM + 1])

    g = gate_ref[...]
    for h in range(NSA_HEADS):
        rs = slice(HEAD_DIM * h, HEAD_DIM * (h + 1))
        out = (g[3 * h:3 * h + 1] * ocmp_ref[rs, :] + g[3 * h + 1:3 * h + 2] * o[:, h * SEL_TQ:(h + 1) * SEL_TQ]
               + g[3 * h + 2:3 * h + 3] * owin_ref[rs, :])
        out_ref[rs, :] = out.astype(BF16)


def _sel_attend(qT, mm, nk, v4, causal4, ocmp, owin, gates):
    B, _, S = qT.shape
    nstep = S // SEL_TQ
    n_sel = S // SEL_BLOCK
    nq = NSA_HEADS * SEL_TQ
    ar = NSA_HEADS * HEAD_DIM
    return pl.pallas_call(
        _sel_kernel,
        grid=(B, nstep),
        in_specs=[pl.BlockSpec((None, ar, SEL_TQ), lambda b, i: (b, A_Q_BLK, i)),
                  pl.BlockSpec((None, n_sel, SEL_TQ), lambda b, i: (b, 0, i)),
                  pl.BlockSpec((None, S, LANES), lambda b, i: (b, 0, 0)),
                  pl.BlockSpec((None, S // LANES, VROWS, LANES), lambda b, i: (b, 0, 0, 0)),
                  _const_spec((SEL_TQ, nq), lambda b, i: (0, 0)),
                  pl.BlockSpec((None, ar, SEL_TQ), lambda b, i: (b, 0, i)),
                  pl.BlockSpec((None, ar, SEL_TQ), lambda b, i: (b, 0, i)),
                  pl.BlockSpec((None, G_ROWS, SEL_TQ), lambda b, i: (b, 0, i))],
        out_specs=pl.BlockSpec((None, ar, SEL_TQ), lambda b, i: (b, 0, i)),
        out_shape=jax.ShapeDtypeStruct((B, ar, S), BF16),
        scratch_shapes=[pltpu.VMEM((LANES, nq), BF16), pltpu.VMEM((n_sel, nq), BF16),
                        pltpu.VMEM((SEL_GROUP * LANES, nq), F32)],
        compiler_params=_cp(("parallel", "parallel")),
        name="nsa_sel",
    )(qT, mm, nk, v4, causal4, ocmp, owin, gates)


def _banded_kernel(*refs, hkv, grp, nprev, span, zero_bias, dynamic, has_sink, want_lse):
    q_ref, k_ref, v_ref, bias_ref = refs[:4]
    pos = 4
    sink_ref = None
    if has_sink:
        sink_ref = refs[pos]
        pos += 1
    o_ref = refs[pos]
    lse_ref = refs[pos + 1] if want_lse else None
    j = pl.program_id(1)

    width = span * LANES
    nchunk = nprev + span

    def scores(s, g, first):
        lanes = slice(s * width, (s + 1) * width)
        qg = jnp.concatenate([q_ref[HEAD_DIM * (g * grp + u):HEAD_DIM * (g * grp + u + 1), lanes]
                              for u in range(grp)], axis=1)
        parts = []
        for ci in range(nchunk):
            if first:
                kc = s * span - nprev + ci
                if kc < 0:
                    continue
                kcc = kc
                row = kc * LANES
            else:
                kc = j * BAND_NSUB + s * span - nprev + ci
                kcc = jnp.maximum(kc, 0) if dynamic else kc
                row = pl.multiple_of(kcc * LANES, LANES)
            sc = jnp.dot(k_ref[pl.ds(row, LANES), HEAD_DIM * g:HEAD_DIM * (g + 1)], qg,
                         preferred_element_type=F32)
            if dynamic:
                sc = sc + bias_ref[jnp.where(kc >= 0, ci, nchunk)]
            elif not zero_bias[ci]:
                sc = sc + bias_ref[ci]
            parts.append((kcc, sc))
        return parts

    def finish(s, g, parts):
        lanes = slice(s * width, (s + 1) * width)
        m = None
        for _, sc in parts:
            mc = jnp.max(sc, axis=0, keepdims=True)
            m = mc if m is None else jnp.maximum(m, mc)
        if has_sink:
            sk = sink_ref[g, 0:1, :]
            m = jnp.maximum(m, sk)
        acc = None
        for kcc, sc in parts:
            t = jnp.dot(v_ref[kcc, VROWS * g:VROWS * (g + 1), :], jnp.exp(sc - m).astype(BF16),
                        preferred_element_type=F32)
            acc = t if acc is None else acc + t
        l = acc[HEAD_DIM:HEAD_DIM + 1]
        if has_sink:
            l = l + jnp.exp(sk - m)
        o = acc[0:HEAD_DIM] * (1.0 / l)
        for u in range(grp):
            hq = g * grp + u
            o_ref[HEAD_DIM * hq:HEAD_DIM * (hq + 1), lanes] = o[:, u * width:(u + 1) * width].astype(o_ref.dtype)
        if want_lse:
            lse_ref[8 * g:8 * (g + 1), lanes] = jnp.broadcast_to(m + jnp.log(l), (8, width))

    def run(first):
        pending = None
        for s in range(BAND_NSUB // span):
            for g in range(hkv):
                parts = scores(s, g, first)
                if pending is not None:
                    finish(*pending)
                pending = (s, g, parts)
        finish(*pending)

    if dynamic:
        run(False)
    else:
        pl.when(j == 0)(lambda: run(True))
        pl.when(j > 0)(lambda: run(False))


def _banded(qT, q_blk, nk, k_blk, v4, v_blk, bias_and_zero, *, hkv, grp, nprev, span, dynamic,
            out_dtype, sinks=None, want_lse=False, name):
    bias, zero_bias = bias_and_zero
    assert span == 1 or (grp == 1 and sinks is None)
    B, _, S = qT.shape
    nt = S // BAND_TILE
    qrows = hkv * grp * HEAD_DIM
    in_specs = [pl.BlockSpec((None, qrows, BAND_TILE), lambda b, j: (b, q_blk, j)),
                pl.BlockSpec((None, S, LANES), lambda b, j: (b, 0, k_blk)),
                pl.BlockSpec((None, S // LANES, hkv * VROWS, LANES), lambda b, j: (b, 0, v_blk, 0)),
                pl.BlockSpec(bias.shape, lambda b, j: (0, 0, 0))]
    args = [qT, nk, v4, bias]
    if sinks is not None:
        in_specs.append(pl.BlockSpec(sinks.shape, lambda b, j: (0, 0, 0)))
        args.append(sinks)
    out_specs = [pl.BlockSpec((None, qrows, BAND_TILE), lambda b, j: (b, 0, j))]
    out_shape = [jax.ShapeDtypeStruct((B, qrows, S), out_dtype)]
    if want_lse:
        out_specs.append(pl.BlockSpec((None, 8 * hkv, BAND_TILE), lambda b, j: (b, 0, j)))
        out_shape.append(jax.ShapeDtypeStruct((B, 8 * hkv, S), F32))
    return pl.pallas_call(
        functools.partial(_banded_kernel, hkv=hkv, grp=grp, nprev=nprev, span=span, zero_bias=zero_bias,
                          dynamic=dynamic, has_sink=sinks is not None, want_lse=want_lse),
        grid=(B, nt),
        in_specs=in_specs,
        out_specs=out_specs,
        out_shape=out_shape,
        compiler_params=_cp(("parallel", "parallel")),
        name=name,
    )(*args)


def _folded_kernel(q_ref, k_ref, v_ref, bias_ref, o_ref, lse_ref):
    j = pl.program_id(2)
    nsub = q_ref.shape[0] // LANES
    ones = jnp.ones((VROWS - HEAD_DIM, LANES), BF16)
    row = lax.broadcasted_iota(jnp.int32, (LANES, LANES), 0)

    def scores(s):
        qt = q_ref[s * LANES:(s + 1) * LANES, :].astype(F32).T.astype(BF16)
        parts = []
        for ci in range(2):
            kc = j * nsub + s - 1 + ci
            kcc = jnp.maximum(kc, 0)
            rows = pl.ds(pl.multiple_of(kcc * LANES, LANES), LANES)
            bias = bias_ref[jnp.where(kc >= 0, ci, 2)]
            vt = v_ref[rows, :].astype(F32).T.astype(BF16)
            for g in range(2):
                hs = slice(HEAD_DIM * g, HEAD_DIM * (g + 1))
                sc = jnp.dot(k_ref[rows, hs], qt[hs], preferred_element_type=F32) + bias
                parts.append((g, jnp.concatenate([vt[hs], ones], axis=0), sc))
        return parts

    def finish(s, parts):
        outs, lses = [], []
        for g in range(2):
            mine = [(v, sc) for gg, v, sc in parts if gg == g]
            m = functools.reduce(jnp.maximum, [jnp.max(sc, axis=0, keepdims=True) for _, sc in mine])
            acc = None
            for v, sc in mine:
                t = jnp.dot(v, jnp.exp(sc - m).astype(BF16), preferred_element_type=F32)
                acc = t if acc is None else acc + t
            l = acc[HEAD_DIM:HEAD_DIM + 1]
            outs.append(acc[0:HEAD_DIM] * (1.0 / l))
            lses.append(m + jnp.log(l))
        rows = slice(s * LANES, (s + 1) * LANES)
        o_ref[rows, :] = jnp.concatenate(outs, axis=0).T
        lse_ref[rows, :] = jnp.where(row == 0, lses[0], jnp.where(row == 1, lses[1], 0.0)).T

    pending = None
    for s in range(nsub):
        parts = scores(s)
        if pending is not None:
            finish(*pending)
        pending = (s, parts)
    finish(*pending)


def _folded_dilated(fq, fk, fv, bias, dil):
    B, n, _ = fq.shape
    tq = min(n, BAND_TILE)
    return pl.pallas_call(
        _folded_kernel,
        grid=(B, dil, n // tq),
        in_specs=[pl.BlockSpec((None, tq, LANES), lambda b, r, j: (b, j, r)),
                  pl.BlockSpec((None, n, LANES), lambda b, r, j: (b, 0, r)),
                  pl.BlockSpec((None, n, LANES), lambda b, r, j: (b, 0, r)),
                  pl.BlockSpec(bias.shape, lambda b, r, j: (0, 0, 0))],
        out_specs=[pl.BlockSpec((None, tq, LANES), lambda b, r, j: (b, j, r)),
                   pl.BlockSpec((None, tq, LANES), lambda b, r, j: (b, j, r))],
        out_shape=[jax.ShapeDtypeStruct((B, n, dil * LANES), F32),
                   jax.ShapeDtypeStruct((B, n, dil * LANES), F32)],
        compiler_params=_cp(("parallel", "parallel", "parallel")),
        name="dil%d_folded" % dil,
    )(fq, fk, fv, bias)


def _post_norm_residual(y, x_ref, gw_ref, o_ref, token_major=False):
    ms = jnp.mean(y * y, axis=0, keepdims=True)
    yn = y * lax.rsqrt(ms + RMS_EPS)
    gw = gw_ref[...]
    for c in range(NSUB):
        sl = slice(c * LANES, (c + 1) * LANES)
        out = x_ref[:, sl] + gw * yn[:, sl]
        if token_major:
            o_ref[sl, :] = out.T
        else:
            o_ref[:, sl] = out


def _mix_ffn_kernel(a_ref, b0_ref, b1_ref, b2_ref, l0_ref, l1_ref, l2_ref, c_ref, wo_ref, x_ref, gw1_ref,
                    a2_ref, sh2_ref, wg_ref, wu_ref, wd_ref, cv_ref, gw2_ref, o_ref,
                    mix_scr, x1_scr, h_scr, carry_scr, act_scr, unf_scr, *, fchunk, token_major_out):
    s = pl.program_id(1)
    cur = s % 2
    prv = 1 - cur
    d_ff = wg_ref.shape[0]

    @pl.when(s == 0)
    def _():
        carry_scr[...] = jnp.zeros(carry_scr.shape, F32)
        h_scr[1] = jnp.zeros(h_scr.shape[1:], BF16)
        x1_scr[1] = jnp.zeros(x1_scr.shape[1:], F32)

    h = h_scr[prv]
    lane = lax.broadcasted_iota(jnp.int32, (fchunk, LANES), 1)
    for c in range(d_ff // fchunk):
        rs = slice(c * fchunk, (c + 1) * fchunk)
        g = jnp.dot(wg_ref[rs, :], h, preferred_element_type=F32)
        prev = carry_scr[rs, :]
        carry_scr[rs, :] = g[:, TILE - LANES:TILE]
        g1 = pltpu.roll(g, 1, 1)
        g2 = pltpu.roll(g, 2, 1)
        fix1 = jnp.where(lane < 1, pltpu.roll(prev, 1, 1), g1[:, 0:LANES])
        fix2 = jnp.where(lane < 2, pltpu.roll(prev, 2, 1), g2[:, 0:LANES])
        g1 = jnp.concatenate([fix1, g1[:, LANES:]], axis=1)
        g2 = jnp.concatenate([fix2, g2[:, LANES:]], axis=1)
        w0 = jnp.concatenate([cv_ref[0, rs, :]] * NSUB, axis=1)
        w1 = jnp.concatenate([cv_ref[1, rs, :]] * NSUB, axis=1)
        w2 = jnp.concatenate([cv_ref[2, rs, :]] * NSUB, axis=1)
        cb = jnp.concatenate([cv_ref[3, rs, :]] * NSUB, axis=1)
        acc = g2 * w0 + g1 * w1 + g * w2 + cb
        up = jnp.dot(wu_ref[rs, :], h, preferred_element_type=F32)
        act_scr[rs, :] = (jax.nn.gelu(acc, approximate=True) * up).astype(BF16)

    def unfold(src_ref):
        for res in range(FOLD):
            unf_scr[pl.ds(res, TILE // FOLD, stride=FOLD), :] = src_ref[:, res * LANES:(res + 1) * LANES]
        return unf_scr[...].T

    b_vals = (b0_ref[...], b1_ref[...], unfold(b2_ref))
    l2t = unfold(l2_ref)
    for hh in range(2):
        ls = [l0_ref[8 * hh:8 * hh + 1, :], l1_ref[8 * hh:8 * hh + 1, :], l2t[hh:hh + 1, :]]
        mx = jnp.maximum(jnp.maximum(ls[0], ls[1]), ls[2])
        es = [jnp.exp(v - mx) for v in ls]
        inv = 1.0 / (es[0] + es[1] + es[2])
        for g in range(3):
            rs = slice(HEAD_DIM * hh, HEAD_DIM * (hh + 1))
            mix_scr[128 * g + HEAD_DIM * hh:128 * g + HEAD_DIM * (hh + 1), :] = (
                b_vals[g][rs, :] * (es[g] * inv)).astype(BF16)
    y1 = jnp.dot(wo_ref[:, 0:256], a_ref[...], preferred_element_type=F32)
    y1 = y1 + jnp.dot(wo_ref[:, 256:640], mix_scr[...], preferred_element_type=F32)
    y1 = y1 + jnp.dot(wo_ref[:, 640:1024], c_ref[...], preferred_element_type=F32)

    y2 = jnp.dot(wd_ref[...], act_scr[...], preferred_element_type=F32)

    _post_norm_residual(y1, x_ref, gw1_ref, x1_scr.at[cur])
    _norm_mod_to_scratch(x1_scr.at[cur], a2_ref, sh2_ref, h_scr.at[cur])
    _post_norm_residual(y2, x1_scr.at[prv], gw2_ref, o_ref, token_major_out)


def _mix_ffn(aT, bs, lses, cT, w_outT, xT, gw1, a2, sh2, wgT, wuT, wdT, cv, gw2, layer, token_major_out):
    B, D, S = xT.shape
    nt = S // TILE
    d_ff = wgT.shape[1]
    tile = lambda rows: pl.BlockSpec((None, rows, TILE), lambda b, s: (b, 0, jnp.minimum(s, nt - 1)))
    tok = pl.BlockSpec((None, TILE // FOLD, FOLD * LANES), lambda b, s: (b, jnp.minimum(s, nt - 1), 0))
    vec = pl.BlockSpec((None, D, LANES), lambda b, s: (b, 0, 0))
    return pl.pallas_call(
        functools.partial(_mix_ffn_kernel, fchunk=256, token_major_out=token_major_out),
        grid=(B, nt + 1),
        in_specs=[tile(256), tile(128), tile(128), tok, tile(16), tile(16), tok, tile(384),
                  _const_spec((None, D, D), lambda b, s: (layer, 0, 0)),
                  tile(D), vec, vec, vec,
                  _const_spec((None, d_ff, D), lambda b, s: (layer, 0, 0)),
                  _const_spec((None, d_ff, D), lambda b, s: (layer, 0, 0)),
                  _const_spec((None, D, d_ff), lambda b, s: (layer, 0, 0)),
                  _const_spec((None, 4, d_ff, LANES), lambda b, s: (layer, 0, 0, 0)),
                  vec],
        out_specs=(pl.BlockSpec((None, TILE, D), lambda b, s: (b, jnp.maximum(s - 1, 0), 0)) if token_major_out
                   else pl.BlockSpec((None, D, TILE), lambda b, s: (b, 0, jnp.maximum(s - 1, 0)))),
        out_shape=jax.ShapeDtypeStruct((B, S, D) if token_major_out else (B, D, S), F32),
        scratch_shapes=[pltpu.VMEM((384, TILE), BF16), pltpu.VMEM((2, D, TILE), F32),
                        pltpu.VMEM((2, D, TILE), BF16), pltpu.VMEM((d_ff, LANES), F32),
                        pltpu.VMEM((d_ff, TILE), BF16), pltpu.VMEM((TILE, LANES), F32)],
        compiler_params=_cp(("arbitrary", "arbitrary")),
        name="mix_ffn",
    )(aT, bs[0], bs[1], bs[2], lses[0], lses[1], lses[2], cT, w_outT, xT, gw1,
      a2, sh2, wgT, wuT, wdT, cv, gw2)


def _band_bias(dilation, old_edge, reps, nprev, span):
    kk = np.arange(LANES)[:, None]
    qq = np.arange(LANES)[None, :]
    res = ((qq - kk) % dilation) == 0
    none = np.zeros_like(res)

    def role(ci):
        if ci < 0 or ci > nprev:
            return none
        return res & (kk - qq >= old_edge) if ci == 0 else (res & (kk <= qq) if ci == nprev else res)

    tabs = [np.concatenate([role(u - a) for a in range(span)], axis=1) for u in range(nprev + span)]
    zero = tuple(bool(t.all()) for t in tabs)
    tabs.append(np.concatenate([none] * span, axis=1))
    out = np.stack([np.where(t, 0.0, NEG) for t in tabs]).astype(np.float32)
    return jnp.asarray(np.tile(out, (1, 1, reps))), zero


def _cmp_bias():
    nn = np.arange(LANES)[:, None]
    qq = np.arange(LANES)[None, :]
    tabs = [np.where(CMP_STRIDE * nn + CMP_BLOCK - 1 <= LANES * r + qq, 0.0, NEG) for r in range(16)]
    return jnp.asarray(np.stack(tabs).astype(np.float32))


def _overlap_rows():
    jj = np.arange(40)[:, None]
    nn = np.arange(LANES)[None, :]
    return jnp.asarray(((nn >= 4 * jj - 1) & (nn <= 4 * jj + 3)).astype(np.float32), dtype=BF16)


_IN_COL_ORDER = ((1804, 2188), (652, 1036), (0, 256),
                 (256, 320), (320, 384), (384, 448), (512, 576), (1036, 1420), (2188, 2316),
                 (448, 512), (576, 640), (1420, 1804), (2316, 2444),
                 (640, 652))


def _prep_w_in(w_in):
    wt = jnp.swapaxes(jnp.concatenate([w_in[:, :, a:b] for a, b in _IN_COL_ORDER], axis=2), 1, 2)
    scale = np.ones((wt.shape[1], 1), np.float32)
    scale[:Q_ROWS] = HEAD_DIM ** -0.5
    wt = wt * scale
    wt = jnp.pad(wt, ((0, 0), (0, W_ROWS - wt.shape[1]), (0, 0)))
    return wt.astype(BF16)


def _prep_compress(w_ck, w_cv, pe_k, pe_v):
    L = w_ck.shape[0]
    half = CMP_BLOCK // 2

    def big(lo):
        wk = w_ck[:, lo:lo + half]
        wv = w_cv[:, lo:lo + half]
        z = jnp.zeros_like(wk)
        top = jnp.concatenate([wk, z], axis=-1)
        bot = jnp.concatenate([z, wv], axis=-1)
        return jnp.concatenate([top, bot], axis=2).reshape(L, half * LANES, LANES)

    wbig = jnp.concatenate([big(0), big(half)], axis=-1).astype(BF16)
    pe = jnp.concatenate([pe_k, pe_v], axis=-1)
    pe2 = jnp.zeros((L, 16, half * LANES), F32)
    pe2 = pe2.at[:, 0].set(pe[:, :half].reshape(L, -1)).at[:, 8].set(pe[:, half:].reshape(L, -1))
    return wbig, pe2.astype(BF16)


def _lane_bcast(v):
    return jnp.broadcast_to(v[..., None], v.shape + (LANES,))


def kernel(x, c, positions, w_in, w_out, w_ada, b_ada, norm_w, cmp_w_k, cmp_w_v, cmp_pe_k, cmp_pe_v,
           sinks, w_gate, w_up, conv_w, conv_b, w_down):
    B, S, D = x.shape
    depth = w_in.shape[0]
    assert S % 2048 == 0 and D == 1024 and w_in.shape[2] == 2444

    inv = ROPE_THETA ** (-jnp.arange(0, HEAD_DIM, 2, dtype=F32) / HEAD_DIM)
    ang = positions.astype(F32)[:, None, :] * inv[None, :, None]
    cosT, sinT = jnp.cos(ang), jnp.sin(ang)

    c8 = jnp.pad(c, ((0, 8 - B), (0, 0)))
    ada = _adaln(c8, w_ada, b_ada)[:, :B]
    sh1, sc1, g1, sh2, sc2, g2 = [ada[:, :, k * D:(k + 1) * D] for k in range(6)]
    nw = norm_w[:, :, None, :]
    a1 = _lane_bcast(nw[:, 0] * (1 + sc1))
    gw1 = _lane_bcast(g1 * nw[:, 1])
    a2 = _lane_bcast(nw[:, 2] * (1 + sc2))
    gw2 = _lane_bcast(g2 * nw[:, 3])
    sh1b, sh2b = _lane_bcast(sh1), _lane_bcast(sh2)

    w_inT = _prep_w_in(w_in)
    w_outT = jnp.swapaxes(w_out, 1, 2).astype(BF16)
    wgT = jnp.swapaxes(w_gate, 1, 2).astype(BF16)
    wuT = jnp.swapaxes(w_up, 1, 2).astype(BF16)
    wdT = jnp.swapaxes(w_down, 1, 2).astype(BF16)
    cv = _lane_bcast(jnp.concatenate([conv_w, conv_b[:, None, :]], axis=1))
    wbig, pe2 = _prep_compress(cmp_w_k, cmp_w_v, cmp_pe_k, cmp_pe_v)
    sink_tab = jnp.broadcast_to(
        jnp.repeat(sinks.reshape(depth, SWA_KV_HEADS, SWA_HEADS // SWA_KV_HEADS), LANES, axis=-1)[:, :, None, :],
        (depth, SWA_KV_HEADS, 8, LANES * (SWA_HEADS // SWA_KV_HEADS)))

    ov = _overlap_rows()
    cmp_bias = _cmp_bias()
    kk = np.arange(SEL_TQ)[:, None]
    causal4 = jnp.asarray(np.tile(np.where(kk <= kk.T, 0.0, NEG).astype(np.float32), (1, NSA_HEADS)))
    bias_win = _band_bias(1, 1, NSA_HEADS, NSA_WINDOW // LANES, 1)
    bias_swa = _band_bias(1, 1, SWA_HEADS // SWA_KV_HEADS, SWA_WINDOW // LANES, 1)
    dil_span = [2 if d >= 4 else 1 for _, d in DIL_PATTERNS[:2]]
    bias_dil = [_band_bias(d, 0, 1, d, sp) for (_, d), sp in zip(DIL_PATTERNS[:2], dil_span)]
    assert DIL_PATTERNS[0] == (LANES, 1) and DIL_PATTERNS[2][0] // DIL_PATTERNS[2][1] == LANES

    xT = x
    for l in range(depth):
        first, last_layer = l == 0, l == depth - 1
        outs = _inproj(xT, a1[l], sh1b[l], w_inT, l, cosT, sinT, token_major=first)
        qT, kcvc, nk, v4, gates, fq, fk, fv = outs[:8]
        if first:
            xT = outs[8]
        kc4, vc4 = _compress(kcvc, wbig, pe2, l)
        ocmp, mm = _cmp_topk(qT, kc4, vc4, ov, cmp_bias)
        owin, = _banded(qT, A_Q_BLK, nk, 1, v4, 1, bias_win, hkv=1, grp=NSA_HEADS, nprev=NSA_WINDOW // LANES,
                        span=1, dynamic=False, out_dtype=F32, name="nsa_win")
        aT = _sel_attend(qT, mm, nk, v4, causal4, ocmp, owin, gates)
        bs, lses = [], []
        for gi, (win, dil) in enumerate(DIL_PATTERNS[:2]):
            o, lse = _banded(qT, 3 + gi, nk, 2 + gi, v4, 1 + gi, bias_dil[gi], hkv=2, grp=1, nprev=dil,
                             span=dil_span[gi], dynamic=dil > BAND_NSUB, out_dtype=F32, want_lse=True,
                             name="dil%d" % dil)
            bs.append(o)
            lses.append(lse)
        o, lse = _folded_dilated(fq, fk, fv, bias_dil[0][0], DIL_PATTERNS[2][1])
        bs.append(o)
        lses.append(lse)
        cT, = _banded(qT, 0, nk, 5, v4, 4, bias_swa, hkv=SWA_KV_HEADS, grp=SWA_HEADS // SWA_KV_HEADS,
                      nprev=SWA_WINDOW // LANES, span=1, dynamic=False, out_dtype=BF16,
                      sinks=sink_tab[l], name="swa")
        xT = _mix_ffn(aT, bs, lses, cT, w_outT, xT, gw1[l], a2[l], sh2b[l], wgT, wuT, wdT, cv, gw2[l], l,
                      token_major_out=last_layer)
    return xT
```

```python
import functools

import numpy as np
import jax
import jax.numpy as jnp
from jax import lax
from jax.experimental import pallas as pl
from jax.experimental.pallas import tpu as pltpu

F32 = jnp.float32
BF16 = jnp.bfloat16

HEAD_DIM = 64
HALF = HEAD_DIM // 2
NSA_HEADS = 4
CMP_BLOCK = 32
CMP_STRIDE = 16
SEL_BLOCK = 64
SEL_TOPK = 16
NSA_WINDOW = 512
DIL_PATTERNS = ((128, 1), (512, 4), (2048, 16))
DIL_HEADS = 6
SWA_HEADS = 6
SWA_KV_HEADS = 2
SWA_WINDOW = 128
ROPE_THETA = 10000.0
RMS_EPS = 1e-6
NEG = -1e30
FORCE = 1e4
CONV_WIDTH = 3

LANES = 128
TILE = 512
NSUB = TILE // LANES
BAND_TILE = 1024
BAND_NSUB = BAND_TILE // LANES
VROWS = HEAD_DIM + 16
MEMBER_BIG = 2.0 ** 100
VMEM_LIMIT = 56 * 1024 * 1024

Q_ROWS = 1024
A_Q_BLK = 3
NK_ROWS = 768
NK_LANES = 768
V_ROWS = 640
G_ROWS = 16
W_ROWS = Q_ROWS + NK_ROWS + V_ROWS + G_ROWS
N_VPIECES = V_ROWS // HEAD_DIM
FOLDED = ((1, slice(128, 256)), (2, slice(256, 384)))
FOLD = CMP_STRIDE
assert DIL_PATTERNS[2][1] == FOLD


def _cp(sem):
    return pltpu.CompilerParams(dimension_semantics=sem, vmem_limit_bytes=VMEM_LIMIT)


def _const_spec(shape, index_map):
    return pl.BlockSpec(shape, index_map, pipeline_mode=pl.Buffered(1))


def _adaln_kernel(c_ref, w_ref, b_ref, o_ref):
    c = c_ref[...]
    cond = c * jax.nn.sigmoid(c)
    o_ref[...] = jnp.dot(cond, w_ref[...], preferred_element_type=F32,
                         precision=lax.Precision.HIGHEST) + b_ref[...]


def _adaln(c8, w_ada, b_ada):
    depth, d, six_d = w_ada.shape
    nblk = six_d // d
    return pl.pallas_call(
        _adaln_kernel,
        grid=(depth, nblk),
        in_specs=[pl.BlockSpec((8, d), lambda l, n: (0, 0)),
                  pl.BlockSpec((None, d, d), lambda l, n: (l, 0, n)),
                  pl.BlockSpec((None, 1, d), lambda l, n: (l, 0, n))],
        out_specs=pl.BlockSpec((None, 8, d), lambda l, n: (l, 0, n)),
        out_shape=jax.ShapeDtypeStruct((depth, 8, six_d), F32),
        compiler_params=_cp(("parallel", "parallel")),
        name="adaln",
    )(c8, w_ada, b_ada.reshape(depth, 1, six_d))


def _norm_mod_to_scratch(x_ref, a_ref, sh_ref, h_scr):
    for c in range(NSUB):
        sl = slice(c * LANES, (c + 1) * LANES)
        xs = x_ref[:, sl]
        ms = jnp.mean(xs * xs, axis=0, keepdims=True)
        h_scr[:, sl] = ((xs * lax.rsqrt(ms + RMS_EPS)) * a_ref[...] + sh_ref[...]).astype(BF16)


def _inproj_kernel(x_ref, a_ref, sh_ref, w_ref, cos_ref, sin_ref,
                   q_ref, kcvc_ref, nk_ref, v4_ref, gate_ref, fq1_ref, fk1_ref, fv1_ref, fq2_ref, fk2_ref, fv2_ref,
                   *rest, token_major):
    j = pl.program_id(1)
    if token_major:
        xt_ref, h_scr, fold_scr = rest
        for c in range(NSUB):
            xt_ref[:, c * LANES:(c + 1) * LANES] = x_ref[c * LANES:(c + 1) * LANES, :].T
        x_ref = xt_ref
    else:
        h_scr, fold_scr = rest
    _norm_mod_to_scratch(x_ref, a_ref, sh_ref, h_scr)
    h = h_scr[...]
    cos = cos_ref[...]
    sin = sin_ref[...]

    def proj(r0, r1):
        return jnp.dot(w_ref[r0:r1, :], h, preferred_element_type=F32)

    def fold_store(t, dst_ref, fold=FOLD):
        fold_scr[...] = t
        for res in range(fold):
            dst_ref[:, res * LANES:(res + 1) * LANES] = fold_scr[pl.ds(res, TILE // fold, stride=fold), :].astype(BF16)

    folded_refs = {1: (fq1_ref, fk1_ref, fv1_ref), 2: (fq2_ref, fk2_ref, fv2_ref)}

    def rope(r, nh):
        outs = []
        for hh in range(nh):
            t1 = r[HEAD_DIM * hh:HEAD_DIM * hh + HALF]
            t2 = r[HEAD_DIM * hh + HALF:HEAD_DIM * (hh + 1)]
            outs.append(t1 * cos - t2 * sin)
            outs.append(t2 * cos + t1 * sin)
        return jnp.concatenate(outs, axis=0)

    for r0, r1 in ((0, 384), (384, 768), (768, 1024)):
        r = rope(proj(r0, r1), (r1 - r0) // HEAD_DIM)
        q_ref[r0:r1, :] = r.astype(BF16)
        if r0 == 384:
            for grp_i, rows in FOLDED:
                fold_store(r[rows].T, folded_refs[grp_i][0], DIL_PATTERNS[grp_i][1])

    base = Q_ROWS
    r = proj(base, base + 128)
    kcvc = jnp.concatenate([rope(r[0:64], 1), r[64:128]], axis=0)
    fold_store(kcvc.T, kcvc_ref)

    r = proj(base + 128, base + 192)
    tok = j * TILE + lax.broadcasted_iota(jnp.int32, (HEAD_DIM, TILE), 1)
    row = lax.broadcasted_iota(jnp.int32, (HEAD_DIM, TILE), 0)
    member_cols = jnp.where(row == ((tok >> 6) & 15), MEMBER_BIG, 0.0).astype(F32)
    nk_ref[:, 0:128] = jnp.concatenate([rope(r, 1), member_cols], axis=0).T.astype(BF16)

    r = proj(base + 192, base + 256)
    nk_ref[:, 128:256] = jnp.concatenate([rope(r, 1), jnp.zeros((HEAD_DIM, TILE), F32)], axis=0).T.astype(BF16)

    r = proj(base + 256, base + 640)
    bk = rope(r, 6).T
    nk_ref[:, 256:640] = bk.astype(BF16)
    for grp_i, rows in FOLDED:
        fold_store(bk[:, rows], folded_refs[grp_i][1], DIL_PATTERNS[grp_i][1])

    r = proj(base + 640, base + 768)
    nk_ref[:, 640:768] = rope(r, 2).T.astype(BF16)

    base = Q_ROWS + NK_ROWS
    r = proj(base, base + V_ROWS)
    for grp_i, rows in FOLDED:
        fold_store(r[128 + rows.start:128 + rows.stop].T, folded_refs[grp_i][2], DIL_PATTERNS[grp_i][1])
    r = r.astype(BF16)
    ones = jnp.ones((VROWS - HEAD_DIM, LANES), BF16)
    for c in range(NSUB):
        for p in range(N_VPIECES):
            v4_ref[c, VROWS * p:VROWS * p + HEAD_DIM, :] = r[HEAD_DIM * p:HEAD_DIM * (p + 1),
                                                             c * LANES:(c + 1) * LANES]
            v4_ref[c, VROWS * p + HEAD_DIM:VROWS * (p + 1), :] = ones

    base = Q_ROWS + NK_ROWS + V_ROWS
    gate_ref[...] = jax.nn.sigmoid(proj(base, base + G_ROWS))


def _inproj(x, a1, sh1, w_inT, layer, cosT, sinT, token_major):
    if token_major:
        B, S, D = x.shape
        x_spec = pl.BlockSpec((None, TILE, D), lambda b, j: (b, j, 0))
    else:
        B, D, S = x.shape
        x_spec = pl.BlockSpec((None, D, TILE), lambda b, j: (b, 0, j))
    nt = S // TILE
    fold_specs = {d: pl.BlockSpec((None, TILE // d, d * LANES), lambda b, j: (b, j, 0)) for _, d in DIL_PATTERNS}
    fold_spec = fold_specs[FOLD]
    out_specs = [pl.BlockSpec((None, Q_ROWS, TILE), lambda b, j: (b, 0, j)),
                 fold_spec,
                 pl.BlockSpec((None, TILE, NK_LANES), lambda b, j: (b, j, 0)),
                 pl.BlockSpec((None, NSUB, N_VPIECES * VROWS, LANES), lambda b, j: (b, j, 0, 0)),
                 pl.BlockSpec((None, G_ROWS, TILE), lambda b, j: (b, 0, j)),
                 ] + [fold_specs[DIL_PATTERNS[g][1]] for g, _ in FOLDED for _ in range(3)]
    out_shape = [jax.ShapeDtypeStruct((B, Q_ROWS, S), BF16),
                 jax.ShapeDtypeStruct((B, S // FOLD, FOLD * LANES), BF16),
                 jax.ShapeDtypeStruct((B, S, NK_LANES), BF16),
                 jax.ShapeDtypeStruct((B, S // LANES, N_VPIECES * VROWS, LANES), BF16),
                 jax.ShapeDtypeStruct((B, G_ROWS, S), F32)] + [
                     jax.ShapeDtypeStruct((B, S // DIL_PATTERNS[g][1], DIL_PATTERNS[g][1] * LANES), BF16)
                     for g, _ in FOLDED for _ in range(3)]
    if token_major:
        out_specs.append(pl.BlockSpec((None, D, TILE), lambda b, j: (b, 0, j)))
        out_shape.append(jax.ShapeDtypeStruct((B, D, S), F32))
    return pl.pallas_call(
        functools.partial(_inproj_kernel, token_major=token_major),
        grid=(B, nt),
        in_specs=[x_spec,
                  pl.BlockSpec((None, D, LANES), lambda b, j: (b, 0, 0)),
                  pl.BlockSpec((None, D, LANES), lambda b, j: (b, 0, 0)),
                  _const_spec((None, W_ROWS, D), lambda b, j: (layer, 0, 0)),
                  pl.BlockSpec((None, HALF, TILE), lambda b, j: (b, 0, j)),
                  pl.BlockSpec((None, HALF, TILE), lambda b, j: (b, 0, j))],
        out_specs=out_specs,
        out_shape=out_shape,
        scratch_shapes=[pltpu.VMEM((D, TILE), BF16), pltpu.VMEM((TILE, LANES), F32)],
        compiler_params=_cp(("parallel", "parallel")),
        name="inproj",
    )(x, a1, sh1, w_inT, cosT, sinT)


def _compress_kernel(t_ref, w_ref, pe_ref, kc_ref, vc_ref):
    n = t_ref.shape[0]
    a = jnp.dot(t_ref[...], w_ref[...], preferred_element_type=F32)
    pc = jnp.dot(pe_ref[...], w_ref[...], preferred_element_type=F32)
    const = pc[0:1, 0:LANES] + pc[8:9, LANES:2 * LANES]
    cmp = a[:, 0:LANES] + pltpu.roll(a[:, LANES:2 * LANES], n - 1, 0) + const
    cmp_t = cmp.T
    ones = jnp.ones((VROWS - HEAD_DIM, LANES), BF16)
    for c in range(n // LANES):
        kc_ref[c] = cmp[c * LANES:(c + 1) * LANES].astype(BF16)
        vc_ref[c, 0:HEAD_DIM, :] = cmp_t[HEAD_DIM:2 * HEAD_DIM, c * LANES:(c + 1) * LANES].astype(BF16)
        vc_ref[c, HEAD_DIM:VROWS, :] = ones


def _compress(tview, wbig, pe2, layer):
    B, n, _ = tview.shape
    nch = n // LANES
    return pl.pallas_call(
        _compress_kernel,
        grid=(B,),
        in_specs=[pl.BlockSpec((None, n, CMP_STRIDE * LANES), lambda b: (b, 0, 0)),
                  pl.BlockSpec((None, CMP_STRIDE * LANES, 2 * LANES), lambda b: (layer, 0, 0)),
                  pl.BlockSpec((None, 16, CMP_STRIDE * LANES), lambda b: (layer, 0, 0))],
        out_specs=[pl.BlockSpec((None, nch, LANES, LANES), lambda b: (b, 0, 0, 0)),
                   pl.BlockSpec((None, nch, VROWS, LANES), lambda b: (b, 0, 0, 0))],
        out_shape=[jax.ShapeDtypeStruct((B, nch, LANES, LANES), BF16),
                   jax.ShapeDtypeStruct((B, nch, VROWS, LANES), BF16)],
        compiler_params=_cp(("parallel",)),
        name="compress",
    )(tview, wbig, pe2)


def _stack_heads(q_ref, nh, lane_slice=slice(None)):
    return jnp.concatenate([q_ref[HEAD_DIM * h:HEAD_DIM * (h + 1), lane_slice] for h in range(nh)], axis=1)


CMP_UNITS = 2
CMP_TQ = CMP_UNITS * LANES


def _cmp_kernel(q_ref, kc_ref, vc_ref, ov_ref, bias_ref, o_ref, mm_ref, imp_scr, *, n_sel, nstep):
    i = pl.program_id(1)
    nch_all = kc_ref.shape[0]
    parts = 4 if nch_all % 4 == 0 else (2 if nch_all % 2 == 0 else 1)
    for v in range(parts):
        pl.when(i // (nstep // parts) == v)(functools.partial(
            _cmp_body, q_ref, kc_ref, vc_ref, ov_ref, bias_ref, o_ref, mm_ref, imp_scr,
            nch=nch_all * (v + 1) // parts, nrows=n_sel * (v + 1) // parts, n_sel=n_sel))


def _cmp_body(q_ref, kc_ref, vc_ref, ov_ref, bias_ref, o_ref, mm_ref, imp_scr, *, nch, nrows, n_sel):
    i = pl.program_id(1)
    cd = (i * CMP_UNITS) // 16

    def scores(u):
        q = _stack_heads(q_ref, NSA_HEADS, slice(u * LANES, (u + 1) * LANES))
        edge = bias_ref[u]
        ss = []
        for c in range(nch):
            b = jnp.where(c < cd, 0.0, jnp.where(c == cd, edge, NEG))
            s = jnp.dot(kc_ref[c, :, 0:HEAD_DIM], q, preferred_element_type=F32)
            ss.append(s + jnp.concatenate([b] * NSA_HEADS, axis=1))
        return ss

    def finish(u, ss):
        lanes = slice(u * LANES, (u + 1) * LANES)
        m = jnp.max(functools.reduce(jnp.maximum, ss), axis=0, keepdims=True)
        valid = m > 0.5 * NEG
        imp_scr[u, 0:32 * nch + 8, :] = jnp.zeros((32 * nch + 8, imp_scr.shape[2]), F32)
        acc = None
        for c in range(nch):
            p = jnp.exp(ss[c] - m).astype(BF16)
            t = jnp.dot(vc_ref[c], p, preferred_element_type=F32)
            acc = t if acc is None else acc + t
            imp_scr[u, 32 * c:32 * c + 40, :] += jnp.dot(ov_ref[...], p, preferred_element_type=F32)
        inv = jnp.where(valid, 1.0 / acc[HEAD_DIM:HEAD_DIM + 1], 0.0)
        o = acc[0:HEAD_DIM] * inv
        for h in range(NSA_HEADS):
            o_ref[HEAD_DIM * h:HEAD_DIM * (h + 1), lanes] = o[:, h * LANES:(h + 1) * LANES]
        imp = jnp.zeros((nrows, LANES), F32)
        for h in range(NSA_HEADS):
            sl = slice(h * LANES, (h + 1) * LANES)
            imp = imp + imp_scr[u, 0:nrows, sl] * inv[:, sl]
        return imp

    pending = scores(0)
    imps = []
    for u in range(1, CMP_UNITS):
        nxt = scores(u)
        imps.append(finish(u - 1, pending))
        pending = nxt
    imps.append(finish(CMP_UNITS - 1, pending))

    blk = lax.broadcasted_iota(jnp.int32, (nrows, LANES), 0).astype(F32)

    def pick_one(imp):
        mx = jnp.max(imp, axis=0, keepdims=True)
        first = jnp.min(jnp.where(imp == mx, blk, float(nrows)), axis=0, keepdims=True)
        return jnp.where(blk == first, -jnp.inf, imp)

    curs, cands = [], []
    for u in range(CMP_UNITS):
        t = i * CMP_TQ + u * LANES + lax.broadcasted_iota(jnp.int32, (nrows, LANES), 1)
        cur = (t >> 6).astype(F32)
        forced = (blk == 0.0) | (blk == cur) | (blk == cur - 1.0)
        imp = jnp.where(forced, FORCE, imps[u])
        curs.append(cur)
        cands.append(jnp.where(blk <= cur, imp, NEG))
    cands = lax.fori_loop(0, min(SEL_TOPK, nrows), lambda _, c: tuple(pick_one(x) for x in c), tuple(cands))
    for u in range(CMP_UNITS):
        member = (cands[u] == -jnp.inf) & (blk <= curs[u])
        mm_ref[0:nrows, u * LANES:(u + 1) * LANES] = jnp.where(member, 0.0, -1.0).astype(BF16)
    if nrows < n_sel:
        mm_ref[nrows:n_sel, :] = jnp.full((n_sel - nrows, CMP_TQ), -1.0, BF16)


def _cmp_topk(qT, kc4, vc4, ov, cmp_bias):
    B, _, S = qT.shape
    nstep = S // CMP_TQ
    nch = kc4.shape[1]
    n_sel = S // SEL_BLOCK
    nq = NSA_HEADS * LANES
    nbias = cmp_bias.shape[0] // CMP_UNITS
    return pl.pallas_call(
        functools.partial(_cmp_kernel, n_sel=n_sel, nstep=nstep),
        grid=(B, nstep),
        in_specs=[pl.BlockSpec((None, NSA_HEADS * HEAD_DIM, CMP_TQ), lambda b, i: (b, A_Q_BLK, i)),
                  pl.BlockSpec((None, nch, LANES, LANES), lambda b, i: (b, 0, 0, 0)),
                  pl.BlockSpec((None, nch, VROWS, LANES), lambda b, i: (b, 0, 0, 0)),
                  pl.BlockSpec((40, LANES), lambda b, i: (0, 0)),
                  pl.BlockSpec((None, CMP_UNITS, LANES, LANES), lambda b, i: (i % nbias, 0, 0, 0))],
        out_specs=[pl.BlockSpec((None, NSA_HEADS * HEAD_DIM, CMP_TQ), lambda b, i: (b, 0, i)),
                   pl.BlockSpec((None, n_sel, CMP_TQ), lambda b, i: (b, 0, i))],
        out_shape=[jax.ShapeDtypeStruct((B, NSA_HEADS * HEAD_DIM, S), F32),
                   jax.ShapeDtypeStruct((B, n_sel, S), BF16)],
        scratch_shapes=[pltpu.VMEM((CMP_UNITS, 32 * nch + 64, nq), F32)],
        compiler_params=_cp(("parallel", "parallel")),
        name="nsa_cmp_topk",
    )(qT, kc4, vc4, ov, cmp_bias.reshape(nbias, CMP_UNITS, LANES, LANES))


SEL_GROUP = 8
SEL_TQ = 2 * LANES


def _sel_kernel(q_ref, mm_ref, ks_ref, v_ref, causal_ref, ocmp_ref, owin_ref, gate_ref, out_ref,
                qa_scr, mm_scr, s_scr):
    i = pl.program_id(1)
    nq = NSA_HEADS * SEL_TQ
    qa_scr[0:HEAD_DIM, :] = _stack_heads(q_ref, NSA_HEADS)
    qa_scr[HEAD_DIM:LANES, :] = jnp.zeros((LANES - HEAD_DIM, nq), BF16)
    mm = mm_ref[...]
    for h in range(NSA_HEADS):
        mm_scr[:, h * SEL_TQ:(h + 1) * SEL_TQ] = mm

    gkeys = SEL_GROUP * LANES

    def load_query(gi):
        rg = (gi * SEL_GROUP) // 8
        qa_scr[HEAD_DIM:HEAD_DIM + 16, :] = mm_scr[pl.ds(pl.multiple_of(rg * 16, 16), 16), :]
        return qa_scr[...]

    def chunk_scores(gi, u, qa):
        row = pl.multiple_of((gi * SEL_GROUP + u) * LANES, LANES)
        return jnp.dot(ks_ref[pl.ds(row, LANES), :], qa, preferred_element_type=F32)

    def weighted_values(gi, ps):
        vcat = jnp.concatenate([v_ref[gi * SEL_GROUP + u] for u in range(SEL_GROUP)], axis=1)
        return jnp.dot(vcat, jnp.concatenate(ps, axis=0), preferred_element_type=F32)

    def body(gi, carry):
        m, acc, mg = carry
        m_new = jnp.maximum(m, mg)
        qa = load_query(gi + 1)
        ps, mx = [], None
        for u in range(SEL_GROUP):
            rows = slice(u * LANES, (u + 1) * LANES)
            ps.append(jnp.exp(s_scr[rows, :] - m_new).astype(BF16))
            nxt = chunk_scores(gi + 1, u, qa)
            s_scr[rows, :] = nxt
            mx = nxt if mx is None else jnp.maximum(mx, nxt)
        acc = acc * jnp.exp(m - m_new) + weighted_values(gi, ps)
        return m_new, acc, jnp.max(mx, axis=0, keepdims=True)

    qa = load_query(0)
    mx = None
    for u in range(SEL_GROUP):
        s0 = chunk_scores(0, u, qa)
        s_scr[u * LANES:(u + 1) * LANES, :] = s0
        mx = s0 if mx is None else jnp.maximum(mx, s0)
    c0 = i * (SEL_TQ // LANES)
    last = c0 // SEL_GROUP
    m, acc, _ = lax.fori_loop(0, last, body, (jnp.full((1, nq), NEG, F32), jnp.zeros((VROWS, nq), F32),
                                              jnp.max(mx, axis=0, keepdims=True)))
    drow = pl.multiple_of((c0 % SEL_GROUP) * LANES, SEL_TQ)
    s_scr[pl.ds(drow, SEL_TQ), :] += causal_ref[...]
    s = s_scr[...]
    m_new = jnp.maximum(m, jnp.max(s, axis=0, keepdims=True))
    acc = acc * jnp.exp(m - m_new) + weighted_values(last, [jnp.exp(s - m_new).astype(BF16)])
    o = acc[0:HEAD_DIM] * (1.0 / acc[HEAD_DIM:HEAD_DIM + 1])

    g = gate_ref[...]
    for h in range(NSA_HEADS):
        rs = slice(HEAD_DIM * h, HEAD_DIM * (h + 1))
        out = (g[3 * h:3 * h + 1] * ocmp_ref[rs, :] + g[3 * h + 1:3 * h + 2] * o[:, h * SEL_TQ:(h + 1) * SEL_TQ]
               + g[3 * h + 2:3 * h + 3] * owin_ref[rs, :])
        out_ref[rs, :] = out.astype(BF16)


def _sel_attend(qT, mm, nk, v4, causal4, ocmp, owin, gates):
    B, _, S = qT.shape
    nstep = S // SEL_TQ
    n_sel = S // SEL_BLOCK
    nq = NSA_HEADS * SEL_TQ
    ar = NSA_HEADS * HEAD_DIM
    return pl.pallas_call(
        _sel_kernel,
        grid=(B, nstep),
        in_specs=[pl.BlockSpec((None, ar, SEL_TQ), lambda b, i: (b, A_Q_BLK, i)),
                  pl.BlockSpec((None, n_sel, SEL_TQ), lambda b, i: (b, 0, i)),
                  pl.BlockSpec((None, S, LANES), lambda b, i: (b, 0, 0)),
                  pl.BlockSpec((None, S // LANES, VROWS, LANES), lambda b, i: (b, 0, 0, 0)),
                  _const_spec((SEL_TQ, nq), lambda b, i: (0, 0)),
                  pl.BlockSpec((None, ar, SEL_TQ), lambda b, i: (b, 0, i)),
                  pl.BlockSpec((None, ar, SEL_TQ), lambda b, i: (b, 0, i)),
                  pl.BlockSpec((None, G_ROWS, SEL_TQ), lambda b, i: (b, 0, i))],
        out_specs=pl.BlockSpec((None, ar, SEL_TQ), lambda b, i: (b, 0, i)),
        out_shape=jax.ShapeDtypeStruct((B, ar, S), BF16),
        scratch_shapes=[pltpu.VMEM((LANES, nq), BF16), pltpu.VMEM((n_sel, nq), BF16),
                        pltpu.VMEM((SEL_GROUP * LANES, nq), F32)],
        compiler_params=_cp(("parallel", "parallel")),
        name="nsa_sel",
    )(qT, mm, nk, v4, causal4, ocmp, owin, gates)


def _banded_kernel(*refs, hkv, grp, nprev, span, zero_bias, dynamic, has_sink, want_lse):
    q_ref, k_ref, v_ref, bias_ref = refs[:4]
    pos = 4
    sink_ref = None
    if has_sink:
        sink_ref = refs[pos]
        pos += 1
    o_ref = refs[pos]
    lse_ref = refs[pos + 1] if want_lse else None
    j = pl.program_id(1)

    width = span * LANES
    nchunk = nprev + span

    def scores(s, g, first):
        lanes = slice(s * width, (s + 1) * width)
        qg = jnp.concatenate([q_ref[HEAD_DIM * (g * grp + u):HEAD_DIM * (g * grp + u + 1), lanes]
                              for u in range(grp)], axis=1)
        parts = []
        for ci in range(nchunk):
            if first:
                kc = s * span - nprev + ci
                if kc < 0:
                    continue
                kcc = kc
                row = kc * LANES
            else:
                kc = j * BAND_NSUB + s * span - nprev + ci
                kcc = jnp.maximum(kc, 0) if dynamic else kc
                row = pl.multiple_of(kcc * LANES, LANES)
            sc = jnp.dot(k_ref[pl.ds(row, LANES), HEAD_DIM * g:HEAD_DIM * (g + 1)], qg,
                         preferred_element_type=F32)
            if dynamic:
                sc = sc + bias_ref[jnp.where(kc >= 0, ci, nchunk)]
            elif not zero_bias[ci]:
                sc = sc + bias_ref[ci]
            parts.append((kcc, sc))
        return parts

    def finish(s, g, parts):
        lanes = slice(s * width, (s + 1) * width)
        m = None
        for _, sc in parts:
            mc = jnp.max(sc, axis=0, keepdims=True)
            m = mc if m is None else jnp.maximum(m, mc)
        if has_sink:
            sk = sink_ref[g, 0:1, :]
            m = jnp.maximum(m, sk)
        acc = None
        for kcc, sc in parts:
            t = jnp.dot(v_ref[kcc, VROWS * g:VROWS * (g + 1), :], jnp.exp(sc - m).astype(BF16),
                        preferred_element_type=F32)
            acc = t if acc is None else acc + t
        l = acc[HEAD_DIM:HEAD_DIM + 1]
        if has_sink:
            l = l + jnp.exp(sk - m)
        o = acc[0:HEAD_DIM] * (1.0 / l)
        for u in range(grp):
            hq = g * grp + u
            o_ref[HEAD_DIM * hq:HEAD_DIM * (hq + 1), lanes] = o[:, u * width:(u + 1) * width].astype(o_ref.dtype)
        if want_lse:
            lse_ref[8 * g:8 * (g + 1), lanes] = jnp.broadcast_to(m + jnp.log(l), (8, width))

    def run(first):
        pending = None
        for s in range(BAND_NSUB // span):
            for g in range(hkv):
                parts = scores(s, g, first)
                if pending is not None:
                    finish(*pending)
                pending = (s, g, parts)
        finish(*pending)

    if dynamic:
        run(False)
    else:
        pl.when(j == 0)(lambda: run(True))
        pl.when(j > 0)(lambda: run(False))


def _banded(qT, q_blk, nk, k_blk, v4, v_blk, bias_and_zero, *, hkv, grp, nprev, span, dynamic,
            out_dtype, sinks=None, want_lse=False, name):
    bias, zero_bias = bias_and_zero
    assert span == 1 or (grp == 1 and sinks is None)
    B, _, S = qT.shape
    nt = S // BAND_TILE
    qrows = hkv * grp * HEAD_DIM
    in_specs = [pl.BlockSpec((None, qrows, BAND_TILE), lambda b, j: (b, q_blk, j)),
                pl.BlockSpec((None, S, LANES), lambda b, j: (b, 0, k_blk)),
                pl.BlockSpec((None, S // LANES, hkv * VROWS, LANES), lambda b, j: (b, 0, v_blk, 0)),
                pl.BlockSpec(bias.shape, lambda b, j: (0, 0, 0))]
    args = [qT, nk, v4, bias]
    if sinks is not None:
        in_specs.append(pl.BlockSpec(sinks.shape, lambda b, j: (0, 0, 0)))
        args.append(sinks)
    out_specs = [pl.BlockSpec((None, qrows, BAND_TILE), lambda b, j: (b, 0, j))]
    out_shape = [jax.ShapeDtypeStruct((B, qrows, S), out_dtype)]
    if want_lse:
        out_specs.append(pl.BlockSpec((None, 8 * hkv, BAND_TILE), lambda b, j: (b, 0, j)))
        out_shape.append(jax.ShapeDtypeStruct((B, 8 * hkv, S), F32))
    return pl.pallas_call(
        functools.partial(_banded_kernel, hkv=hkv, grp=grp, nprev=nprev, span=span, zero_bias=zero_bias,
                          dynamic=dynamic, has_sink=sinks is not None, want_lse=want_lse),
        grid=(B, nt),
        in_specs=in_specs,
        out_specs=out_specs,
        out_shape=out_shape,
        compiler_params=_cp(("parallel", "parallel")),
        name=name,
    )(*args)


def _folded_kernel(q_ref, k_ref, v_ref, bias_ref, o_ref, lse_ref):
    j = pl.program_id(2)
    nsub = q_ref.shape[0] // LANES
    ones = jnp.ones((VROWS - HEAD_DIM, LANES), BF16)
    row = lax.broadcasted_iota(jnp.int32, (LANES, LANES), 0)

    def scores(s):
        qt = q_ref[s * LANES:(s + 1) * LANES, :].astype(F32).T.astype(BF16)
        parts = []
        for ci in range(2):
            kc = j * nsub + s - 1 + ci
            kcc = jnp.maximum(kc, 0)
            rows = pl.ds(pl.multiple_of(kcc * LANES, LANES), LANES)
            bias = bias_ref[jnp.where(kc >= 0, ci, 2)]
            vt = v_ref[rows, :].astype(F32).T.astype(BF16)
            for g in range(2):
                hs = slice(HEAD_DIM * g, HEAD_DIM * (g + 1))
                sc = jnp.dot(k_ref[rows, hs], qt[hs], preferred_element_type=F32) + bias
                parts.append((g, jnp.concatenate([vt[hs], ones], axis=0), sc))
        return parts

    def finish(s, parts):
        outs, lses = [], []
        for g in range(2):
            mine = [(v, sc) for gg, v, sc in parts if gg == g]
            m = functools.reduce(jnp.maximum, [jnp.max(sc, axis=0, keepdims=True) for _, sc in mine])
            acc = None
            for v, sc in mine:
                t = jnp.dot(v, jnp.exp(sc - m).astype(BF16), preferred_element_type=F32)
                acc = t if acc is None else acc + t
            l = acc[HEAD_DIM:HEAD_DIM + 1]
            outs.append(acc[0:HEAD_DIM] * (1.0 / l))
            lses.append(m + jnp.log(l))
        rows = slice(s * LANES, (s + 1) * LANES)
        o_ref[rows, :] = jnp.concatenate(outs, axis=0).T
        lse_ref[rows, :] = jnp.where(row == 0, lses[0], jnp.where(row == 1, lses[1], 0.0)).T

    pending = None
    for s in range(nsub):
        parts = scores(s)
        if pending is not None:
            finish(*pending)
        pending = (s, parts)
    finish(*pending)


def _folded_dilated(fq, fk, fv, bias, dil):
    B, n, _ = fq.shape
    tq = min(n, BAND_TILE)
    return pl.pallas_call(
        _folded_kernel,
        grid=(B, dil, n // tq),
        in_specs=[pl.BlockSpec((None, tq, LANES), lambda b, r, j: (b, j, r)),
                  pl.BlockSpec((None, n, LANES), lambda b, r, j: (b, 0, r)),
                  pl.BlockSpec((None, n, LANES), lambda b, r, j: (b, 0, r)),
                  pl.BlockSpec(bias.shape, lambda b, r, j: (0, 0, 0))],
        out_specs=[pl.BlockSpec((None, tq, LANES), lambda b, r, j: (b, j, r)),
                   pl.BlockSpec((None, tq, LANES), lambda b, r, j: (b, j, r))],
        out_shape=[jax.ShapeDtypeStruct((B, n, dil * LANES), F32),
                   jax.ShapeDtypeStruct((B, n, dil * LANES), F32)],
        compiler_params=_cp(("parallel", "parallel", "parallel")),
        name="dil%d_folded" % dil,
    )(fq, fk, fv, bias)


def _post_norm_residual(y, x_ref, gw_ref, o_ref, token_major=False):
    ms = jnp.mean(y * y, axis=0, keepdims=True)
    yn = y * lax.rsqrt(ms + RMS_EPS)
    gw = gw_ref[...]
    for c in range(NSUB):
        sl = slice(c * LANES, (c + 1) * LANES)
        out = x_ref[:, sl] + gw * yn[:, sl]
        if token_major:
            o_ref[sl, :] = out.T
        else:
            o_ref[:, sl] = out


def _mix_ffn_kernel(a_ref, b0_ref, b1_ref, b2_ref, l0_ref, l1_ref, l2_ref, c_ref, wo_ref, x_ref, gw1_ref,
                    a2_ref, sh2_ref, wg_ref, wu_ref, wd_ref, cv_ref, gw2_ref, o_ref,
                    mix_scr, x1_scr, h_scr, carry_scr, act_scr, unf_scr, *, fchunk, token_major_out):
    s = pl.program_id(1)
    cur = s % 2
    prv = 1 - cur
    d_ff = wg_ref.shape[0]

    @pl.when(s == 0)
    def _():
        carry_scr[...] = jnp.zeros(carry_scr.shape, F32)
        h_scr[1] = jnp.zeros(h_scr.shape[1:], BF16)
        x1_scr[1] = jnp.zeros(x1_scr.shape[1:], F32)

    h = h_scr[prv]
    lane = lax.broadcasted_iota(jnp.int32, (fchunk, LANES), 1)
    for c in range(d_ff // fchunk):
        rs = slice(c * fchunk, (c + 1) * fchunk)
        g = jnp.dot(wg_ref[rs, :], h, preferred_element_type=F32)
        prev = carry_scr[rs, :]
        carry_scr[rs, :] = g[:, TILE - LANES:TILE]
        g1 = pltpu.roll(g, 1, 1)
        g2 = pltpu.roll(g, 2, 1)
        fix1 = jnp.where(lane < 1, pltpu.roll(prev, 1, 1), g1[:, 0:LANES])
        fix2 = jnp.where(lane < 2, pltpu.roll(prev, 2, 1), g2[:, 0:LANES])
        g1 = jnp.concatenate([fix1, g1[:, LANES:]], axis=1)
        g2 = jnp.concatenate([fix2, g2[:, LANES:]], axis=1)
        w0 = jnp.concatenate([cv_ref[0, rs, :]] * NSUB, axis=1)
        w1 = jnp.concatenate([cv_ref[1, rs, :]] * NSUB, axis=1)
        w2 = jnp.concatenate([cv_ref[2, rs, :]] * NSUB, axis=1)
        cb = jnp.concatenate([cv_ref[3, rs, :]] * NSUB, axis=1)
        acc = g2 * w0 + g1 * w1 + g * w2 + cb
        up = jnp.dot(wu_ref[rs, :], h, preferred_element_type=F32)
        act_scr[rs, :] = (jax.nn.gelu(acc, approximate=True) * up).astype(BF16)

    def unfold(src_ref, fold):
        for res in range(fold):
            unf_scr[pl.ds(res, TILE // fold, stride=fold), :] = src_ref[:, res * LANES:(res + 1) * LANES]
        return unf_scr[...].T

    d1, d2 = DIL_PATTERNS[1][1], DIL_PATTERNS[2][1]
    b_vals = (b0_ref[...], unfold(b1_ref, d1), unfold(b2_ref, d2))
    l1t, l2t = unfold(l1_ref, d1), unfold(l2_ref, d2)
    for hh in range(2):
        ls = [l0_ref[8 * hh:8 * hh + 1, :], l1t[hh:hh + 1, :], l2t[hh:hh + 1, :]]
        mx = jnp.maximum(jnp.maximum(ls[0], ls[1]), ls[2])
        es = [jnp.exp(v - mx) for v in ls]
        inv = 1.0 / (es[0] + es[1] + es[2])
        for g in range(3):
            rs = slice(HEAD_DIM * hh, HEAD_DIM * (hh + 1))
            mix_scr[128 * g + HEAD_DIM * hh:128 * g + HEAD_DIM * (hh + 1), :] = (
                b_vals[g][rs, :] * (es[g] * inv)).astype(BF16)
    y1 = jnp.dot(wo_ref[:, 0:256], a_ref[...], preferred_element_type=F32)
    y1 = y1 + jnp.dot(wo_ref[:, 256:640], mix_scr[...], preferred_element_type=F32)
    y1 = y1 + jnp.dot(wo_ref[:, 640:1024], c_ref[...], preferred_element_type=F32)

    y2 = jnp.dot(wd_ref[...], act_scr[...], preferred_element_type=F32)

    _post_norm_residual(y1, x_ref, gw1_ref, x1_scr.at[cur])
    _norm_mod_to_scratch(x1_scr.at[cur], a2_ref, sh2_ref, h_scr.at[cur])
    _post_norm_residual(y2, x1_scr.at[prv], gw2_ref, o_ref, token_major_out)


def _mix_ffn(aT, bs, lses, cT, w_outT, xT, gw1, a2, sh2, wgT, wuT, wdT, cv, gw2, layer, token_major_out):
    B, D, S = xT.shape
    nt = S // TILE
    d_ff = wgT.shape[1]
    tile = lambda rows: pl.BlockSpec((None, rows, TILE), lambda b, s: (b, 0, jnp.minimum(s, nt - 1)))
    tok = lambda d: pl.BlockSpec((None, TILE // d, d * LANES), lambda b, s: (b, jnp.minimum(s, nt - 1), 0))
    t1, t2 = tok(DIL_PATTERNS[1][1]), tok(DIL_PATTERNS[2][1])
    vec = pl.BlockSpec((None, D, LANES), lambda b, s: (b, 0, 0))
    return pl.pallas_call(
        functools.partial(_mix_ffn_kernel, fchunk=256, token_major_out=token_major_out),
        grid=(B, nt + 1),
        in_specs=[tile(256), tile(128), t1, t2, tile(16), t1, t2, tile(384),
                  _const_spec((None, D, D), lambda b, s: (layer, 0, 0)),
                  tile(D), vec, vec, vec,
                  _const_spec((None, d_ff, D), lambda b, s: (layer, 0, 0)),
                  _const_spec((None, d_ff, D), lambda b, s: (layer, 0, 0)),
                  _const_spec((None, D, d_ff), lambda b, s: (layer, 0, 0)),
                  _const_spec((None, 4, d_ff, LANES), lambda b, s: (layer, 0, 0, 0)),
                  vec],
        out_specs=(pl.BlockSpec((None, TILE, D), lambda b, s: (b, jnp.maximum(s - 1, 0), 0)) if token_major_out
                   else pl.BlockSpec((None, D, TILE), lambda b, s: (b, 0, jnp.maximum(s - 1, 0)))),
        out_shape=jax.ShapeDtypeStruct((B, S, D) if token_major_out else (B, D, S), F32),
        scratch_shapes=[pltpu.VMEM((384, TILE), BF16), pltpu.VMEM((2, D, TILE), F32),
                        pltpu.VMEM((2, D, TILE), BF16), pltpu.VMEM((d_ff, LANES), F32),
                        pltpu.VMEM((d_ff, TILE), BF16), pltpu.VMEM((TILE, LANES), F32)],
        compiler_params=_cp(("arbitrary", "arbitrary")),
        name="mix_ffn",
    )(aT, bs[0], bs[1], bs[2], lses[0], lses[1], lses[2], cT, w_outT, xT, gw1,
      a2, sh2, wgT, wuT, wdT, cv, gw2)


def _band_bias(dilation, old_edge, reps, nprev, span):
    kk = np.arange(LANES)[:, None]
    qq = np.arange(LANES)[None, :]
    res = ((qq - kk) % dilation) == 0
    none = np.zeros_like(res)

    def role(ci):
        if ci < 0 or ci > nprev:
            return none
        return res & (kk - qq >= old_edge) if ci == 0 else (res & (kk <= qq) if ci == nprev else res)

    tabs = [np.concatenate([role(u - a) for a in range(span)], axis=1) for u in range(nprev + span)]
    zero = tuple(bool(t.all()) for t in tabs)
    tabs.append(np.concatenate([none] * span, axis=1))
    out = np.stack([np.where(t, 0.0, NEG) for t in tabs]).astype(np.float32)
    return jnp.asarray(np.tile(out, (1, 1, reps))), zero


def _cmp_bias():
    nn = np.arange(LANES)[:, None]
    qq = np.arange(LANES)[None, :]
    tabs = [np.where(CMP_STRIDE * nn + CMP_BLOCK - 1 <= LANES * r + qq, 0.0, NEG) for r in range(16)]
    return jnp.asarray(np.stack(tabs).astype(np.float32))


def _overlap_rows():
    jj = np.arange(40)[:, None]
    nn = np.arange(LANES)[None, :]
    return jnp.asarray(((nn >= 4 * jj - 1) & (nn <= 4 * jj + 3)).astype(np.float32), dtype=BF16)


_IN_COL_ORDER = ((1804, 2188), (652, 1036), (0, 256),
                 (256, 320), (320, 384), (384, 448), (512, 576), (1036, 1420), (2188, 2316),
                 (448, 512), (576, 640), (1420, 1804), (2316, 2444),
                 (640, 652))


def _prep_w_in(w_in):
    wt = jnp.swapaxes(jnp.concatenate([w_in[:, :, a:b] for a, b in _IN_COL_ORDER], axis=2), 1, 2)
    scale = np.ones((wt.shape[1], 1), np.float32)
    scale[:Q_ROWS] = HEAD_DIM ** -0.5
    wt = wt * scale
    wt = jnp.pad(wt, ((0, 0), (0, W_ROWS - wt.shape[1]), (0, 0)))
    return wt.astype(BF16)


def _prep_compress(w_ck, w_cv, pe_k, pe_v):
    L = w_ck.shape[0]
    half = CMP_BLOCK // 2

    def big(lo):
        wk = w_ck[:, lo:lo + half]
        wv = w_cv[:, lo:lo + half]
        z = jnp.zeros_like(wk)
        top = jnp.concatenate([wk, z], axis=-1)
        bot = jnp.concatenate([z, wv], axis=-1)
        return jnp.concatenate([top, bot], axis=2).reshape(L, half * LANES, LANES)

    wbig = jnp.concatenate([big(0), big(half)], axis=-1).astype(BF16)
    pe = jnp.concatenate([pe_k, pe_v], axis=-1)
    pe2 = jnp.zeros((L, 16, half * LANES), F32)
    pe2 = pe2.at[:, 0].set(pe[:, :half].reshape(L, -1)).at[:, 8].set(pe[:, half:].reshape(L, -1))
    return wbig, pe2.astype(BF16)


def _lane_bcast(v):
    return jnp.broadcast_to(v[..., None], v.shape + (LANES,))


def kernel(x, c, positions, w_in, w_out, w_ada, b_ada, norm_w, cmp_w_k, cmp_w_v, cmp_pe_k, cmp_pe_v,
           sinks, w_gate, w_up, conv_w, conv_b, w_down):
    B, S, D = x.shape
    depth = w_in.shape[0]
    assert S % 2048 == 0 and D == 1024 and w_in.shape[2] == 2444

    inv = ROPE_THETA ** (-jnp.arange(0, HEAD_DIM, 2, dtype=F32) / HEAD_DIM)
    ang = positions.astype(F32)[:, None, :] * inv[None, :, None]
    cosT, sinT = jnp.cos(ang), jnp.sin(ang)

    c8 = jnp.pad(c, ((0, 8 - B), (0, 0)))
    ada = _adaln(c8, w_ada, b_ada)[:, :B]
    sh1, sc1, g1, sh2, sc2, g2 = [ada[:, :, k * D:(k + 1) * D] for k in range(6)]
    nw = norm_w[:, :, None, :]
    a1 = _lane_bcast(nw[:, 0] * (1 + sc1))
    gw1 = _lane_bcast(g1 * nw[:, 1])
    a2 = _lane_bcast(nw[:, 2] * (1 + sc2))
    gw2 = _lane_bcast(g2 * nw[:, 3])
    sh1b, sh2b = _lane_bcast(sh1), _lane_bcast(sh2)

    w_inT = _prep_w_in(w_in)
    w_outT = jnp.swapaxes(w_out, 1, 2).astype(BF16)
    wgT = jnp.swapaxes(w_gate, 1, 2).astype(BF16)
    wuT = jnp.swapaxes(w_up, 1, 2).astype(BF16)
    wdT = jnp.swapaxes(w_down, 1, 2).astype(BF16)
    cv = _lane_bcast(jnp.concatenate([conv_w, conv_b[:, None, :]], axis=1))
    wbig, pe2 = _prep_compress(cmp_w_k, cmp_w_v, cmp_pe_k, cmp_pe_v)
    sink_tab = jnp.broadcast_to(
        jnp.repeat(sinks.reshape(depth, SWA_KV_HEADS, SWA_HEADS // SWA_KV_HEADS), LANES, axis=-1)[:, :, None, :],
        (depth, SWA_KV_HEADS, 8, LANES * (SWA_HEADS // SWA_KV_HEADS)))

    ov = _overlap_rows()
    cmp_bias = _cmp_bias()
    kk = np.arange(SEL_TQ)[:, None]
    causal4 = jnp.asarray(np.tile(np.where(kk <= kk.T, 0.0, NEG).astype(np.float32), (1, NSA_HEADS)))
    bias_win = _band_bias(1, 1, NSA_HEADS, NSA_WINDOW // LANES, 1)
    bias_swa = _band_bias(1, 1, SWA_HEADS // SWA_KV_HEADS, SWA_WINDOW // LANES, 1)
    bias_dil = _band_bias(1, 0, 1, 1, 1)
    assert all(w // d == LANES for w, d in DIL_PATTERNS)

    xT = x
    for l in range(depth):
        first, last_layer = l == 0, l == depth - 1
        outs = _inproj(xT, a1[l], sh1b[l], w_inT, l, cosT, sinT, token_major=first)
        qT, kcvc, nk, v4, gates = outs[:5]
        folded = {g: outs[5 + 3 * k:8 + 3 * k] for k, (g, _) in enumerate(FOLDED)}
        if first:
            xT = outs[5 + 3 * len(FOLDED)]
        kc4, vc4 = _compress(kcvc, wbig, pe2, l)
        ocmp, mm = _cmp_topk(qT, kc4, vc4, ov, cmp_bias)
        owin, = _banded(qT, A_Q_BLK, nk, 1, v4, 1, bias_win, hkv=1, grp=NSA_HEADS, nprev=NSA_WINDOW // LANES,
                        span=1, dynamic=False, out_dtype=F32, name="nsa_win")
        aT = _sel_attend(qT, mm, nk, v4, causal4, ocmp, owin, gates)
        o, lse = _banded(qT, 3, nk, 2, v4, 1, bias_dil, hkv=2, grp=1, nprev=1, span=1, dynamic=False,
                         out_dtype=F32, want_lse=True, name="dil1")
        bs, lses = [o], [lse]
        for g, _ in FOLDED:
            o, lse = _folded_dilated(*folded[g], bias_dil[0], DIL_PATTERNS[g][1])
            bs.append(o)
            lses.append(lse)
        cT, = _banded(qT, 0, nk, 5, v4, 4, bias_swa, hkv=SWA_KV_HEADS, grp=SWA_HEADS // SWA_KV_HEADS,
                      nprev=SWA_WINDOW // LANES, span=1, dynamic=False, out_dtype=BF16,
                      sinks=sink_tab[l], name="swa")
        xT = _mix_ffn(aT, bs, lses, cT, w_outT, xT, gw1[l], a2[l], sh2b[l], wgT, wuT, wdT, cv, gw2[l], l,
                      token_major_out=last_layer)
    return xT
```

```python
import functools

import numpy as np
import jax
import jax.numpy as jnp
from jax import lax
from jax.experimental import pallas as pl
from jax.experimental.pallas import tpu as pltpu

F32 = jnp.float32
BF16 = jnp.bfloat16

HEAD_DIM = 64
HALF = HEAD_DIM // 2
NSA_HEADS = 4
CMP_BLOCK = 32
CMP_STRIDE = 16
SEL_BLOCK = 64
SEL_TOPK = 16
NSA_WINDOW = 512
DIL_PATTERNS = ((128, 1), (512, 4), (2048, 16))
SWA_HEADS = 6
SWA_KV_HEADS = 2
SWA_WINDOW = 128
ROPE_THETA = 10000.0
RMS_EPS = 1e-6
NEG = -1e30
FORCE = 1e4

LANES = 128
TILE = 512
NSUB = TILE // LANES
BAND_TILE = 1024
BAND_NSUB = BAND_TILE // LANES
VROWS = HEAD_DIM + 16
MEMBER_BIG = 2.0 ** 100
VMEM_LIMIT = 56 * 1024 * 1024

Q_ROWS = 1024
A_Q_BLK = 3
NK_ROWS = 768
NK_LANES = 768
V_ROWS = 640
G_ROWS = 16
W_ROWS = Q_ROWS + NK_ROWS + V_ROWS + G_ROWS
N_VPIECES = V_ROWS // HEAD_DIM
FOLDED = ((1, slice(128, 256)), (2, slice(256, 384)))
FOLD = CMP_STRIDE
assert DIL_PATTERNS[2][1] == FOLD


def _cp(sem):
    return pltpu.CompilerParams(dimension_semantics=sem, vmem_limit_bytes=VMEM_LIMIT)


def _const_spec(shape, index_map):
    return pl.BlockSpec(shape, index_map, pipeline_mode=pl.Buffered(1))


def _adaln_kernel(c_ref, w_ref, b_ref, o_ref):
    c = c_ref[...]
    cond = c * jax.nn.sigmoid(c)
    o_ref[...] = jnp.dot(cond, w_ref[...], preferred_element_type=F32,
                         precision=lax.Precision.HIGHEST) + b_ref[...]


def _adaln(c8, w_ada, b_ada):
    depth, d, six_d = w_ada.shape
    nblk = six_d // d
    return pl.pallas_call(
        _adaln_kernel,
        grid=(depth, nblk),
        in_specs=[pl.BlockSpec((8, d), lambda l, n: (0, 0)),
                  pl.BlockSpec((None, d, d), lambda l, n: (l, 0, n)),
                  pl.BlockSpec((None, 1, d), lambda l, n: (l, 0, n))],
        out_specs=pl.BlockSpec((None, 8, d), lambda l, n: (l, 0, n)),
        out_shape=jax.ShapeDtypeStruct((depth, 8, six_d), F32),
        compiler_params=_cp(("parallel", "parallel")),
        name="adaln",
    )(c8, w_ada, b_ada.reshape(depth, 1, six_d))


def _norm_mod_to_scratch(x_ref, a_ref, sh_ref, h_scr):
    for c in range(NSUB):
        sl = slice(c * LANES, (c + 1) * LANES)
        xs = x_ref[:, sl]
        ms = jnp.mean(xs * xs, axis=0, keepdims=True)
        h_scr[:, sl] = ((xs * lax.rsqrt(ms + RMS_EPS)) * a_ref[...] + sh_ref[...]).astype(BF16)


def _inproj_kernel(x_ref, a_ref, sh_ref, w_ref, cos_ref, sin_ref,
                   q_ref, kcvc_ref, nk_ref, v4_ref, gate_ref, fq1_ref, fk1_ref, fv1_ref, fq2_ref, fk2_ref, fv2_ref,
                   *rest, token_major):
    j = pl.program_id(1)
    if token_major:
        xt_ref, h_scr, fold_scr = rest
        for c in range(NSUB):
            xt_ref[:, c * LANES:(c + 1) * LANES] = x_ref[c * LANES:(c + 1) * LANES, :].T
        x_ref = xt_ref
    else:
        h_scr, fold_scr = rest
    _norm_mod_to_scratch(x_ref, a_ref, sh_ref, h_scr)
    h = h_scr[...]
    cos = cos_ref[...]
    sin = sin_ref[...]

    def proj(r0, r1):
        return jnp.dot(w_ref[r0:r1, :], h, preferred_element_type=F32)

    def fold_store(t, dst_ref, fold=FOLD):
        fold_scr[...] = t
        for res in range(fold):
            dst_ref[:, res * LANES:(res + 1) * LANES] = fold_scr[pl.ds(res, TILE // fold, stride=fold), :].astype(BF16)

    folded_refs = {1: (fq1_ref, fk1_ref, fv1_ref), 2: (fq2_ref, fk2_ref, fv2_ref)}

    def rope(r, nh):
        outs = []
        for hh in range(nh):
            t1 = r[HEAD_DIM * hh:HEAD_DIM * hh + HALF]
            t2 = r[HEAD_DIM * hh + HALF:HEAD_DIM * (hh + 1)]
            outs.append(t1 * cos - t2 * sin)
            outs.append(t2 * cos + t1 * sin)
        return jnp.concatenate(outs, axis=0)

    for r0, r1 in ((0, 384), (384, 768), (768, 1024)):
        r = rope(proj(r0, r1), (r1 - r0) // HEAD_DIM)
        q_ref[r0:r1, :] = r.astype(BF16)
        if r0 == 384:
            for grp_i, rows in FOLDED:
                fold_store(r[rows].T, folded_refs[grp_i][0], DIL_PATTERNS[grp_i][1])

    base = Q_ROWS
    r = proj(base, base + 128)
    kcvc = jnp.concatenate([rope(r[0:64], 1), r[64:128]], axis=0)
    fold_store(kcvc.T, kcvc_ref)

    r = proj(base + 128, base + 192)
    tok = j * TILE + lax.broadcasted_iota(jnp.int32, (HEAD_DIM, TILE), 1)
    row = lax.broadcasted_iota(jnp.int32, (HEAD_DIM, TILE), 0)
    member_cols = jnp.where(row == ((tok >> 6) & 15), MEMBER_BIG, 0.0).astype(F32)
    nk_ref[:, 0:128] = jnp.concatenate([rope(r, 1), member_cols], axis=0).T.astype(BF16)

    r = proj(base + 192, base + 256)
    nk_ref[:, 128:256] = jnp.concatenate([rope(r, 1), jnp.zeros((HEAD_DIM, TILE), F32)], axis=0).T.astype(BF16)

    r = proj(base + 256, base + 640)
    bk = rope(r, 6).T
    nk_ref[:, 256:640] = bk.astype(BF16)
    for grp_i, rows in FOLDED:
        fold_store(bk[:, rows], folded_refs[grp_i][1], DIL_PATTERNS[grp_i][1])

    r = proj(base + 640, base + 768)
    nk_ref[:, 640:768] = rope(r, 2).T.astype(BF16)

    base = Q_ROWS + NK_ROWS
    r = proj(base, base + V_ROWS)
    for grp_i, rows in FOLDED:
        fold_store(r[128 + rows.start:128 + rows.stop].T, folded_refs[grp_i][2], DIL_PATTERNS[grp_i][1])
    r = r.astype(BF16)
    ones = jnp.ones((VROWS - HEAD_DIM, LANES), BF16)
    for c in range(NSUB):
        for p in range(N_VPIECES):
            v4_ref[c, VROWS * p:VROWS * p + HEAD_DIM, :] = r[HEAD_DIM * p:HEAD_DIM * (p + 1),
                                                             c * LANES:(c + 1) * LANES]
            v4_ref[c, VROWS * p + HEAD_DIM:VROWS * (p + 1), :] = ones

    base = Q_ROWS + NK_ROWS + V_ROWS
    gate_ref[...] = jax.nn.sigmoid(proj(base, base + G_ROWS))


def _inproj(x, a1, sh1, w_inT, layer, cosT, sinT, token_major):
    if token_major:
        B, S, D = x.shape
        x_spec = pl.BlockSpec((None, TILE, D), lambda b, j: (b, j, 0))
    else:
        B, D, S = x.shape
        x_spec = pl.BlockSpec((None, D, TILE), lambda b, j: (b, 0, j))
    nt = S // TILE
    fold_specs = {d: pl.BlockSpec((None, TILE // d, d * LANES), lambda b, j: (b, j, 0)) for _, d in DIL_PATTERNS}
    fold_spec = fold_specs[FOLD]
    out_specs = [pl.BlockSpec((None, Q_ROWS, TILE), lambda b, j: (b, 0, j)),
                 fold_spec,
                 pl.BlockSpec((None, TILE, NK_LANES), lambda b, j: (b, j, 0)),
                 pl.BlockSpec((None, NSUB, N_VPIECES * VROWS, LANES), lambda b, j: (b, j, 0, 0)),
                 pl.BlockSpec((None, G_ROWS, TILE), lambda b, j: (b, 0, j)),
                 ] + [fold_specs[DIL_PATTERNS[g][1]] for g, _ in FOLDED for _ in range(3)]
    out_shape = [jax.ShapeDtypeStruct((B, Q_ROWS, S), BF16),
                 jax.ShapeDtypeStruct((B, S // FOLD, FOLD * LANES), BF16),
                 jax.ShapeDtypeStruct((B, S, NK_LANES), BF16),
                 jax.ShapeDtypeStruct((B, S // LANES, N_VPIECES * VROWS, LANES), BF16),
                 jax.ShapeDtypeStruct((B, G_ROWS, S), F32)] + [
                     jax.ShapeDtypeStruct((B, S // DIL_PATTERNS[g][1], DIL_PATTERNS[g][1] * LANES), BF16)
                     for g, _ in FOLDED for _ in range(3)]
    if token_major:
        out_specs.append(pl.BlockSpec((None, D, TILE), lambda b, j: (b, 0, j)))
        out_shape.append(jax.ShapeDtypeStruct((B, D, S), F32))
    return pl.pallas_call(
        functools.partial(_inproj_kernel, token_major=token_major),
        grid=(B, nt),
        in_specs=[x_spec,
                  pl.BlockSpec((None, D, LANES), lambda b, j: (b, 0, 0)),
                  pl.BlockSpec((None, D, LANES), lambda b, j: (b, 0, 0)),
                  _const_spec((None, W_ROWS, D), lambda b, j: (layer, 0, 0)),
                  pl.BlockSpec((None, HALF, TILE), lambda b, j: (b, 0, j)),
                  pl.BlockSpec((None, HALF, TILE), lambda b, j: (b, 0, j))],
        out_specs=out_specs,
        out_shape=out_shape,
        scratch_shapes=[pltpu.VMEM((D, TILE), BF16), pltpu.VMEM((TILE, LANES), F32)],
        compiler_params=_cp(("parallel", "parallel")),
        name="inproj",
    )(x, a1, sh1, w_inT, cosT, sinT)


def _compress_kernel(t_ref, w_ref, pe_ref, kc_ref, vc_ref):
    n = t_ref.shape[0]
    a = jnp.dot(t_ref[...], w_ref[...], preferred_element_type=F32)
    pc = jnp.dot(pe_ref[...], w_ref[...], preferred_element_type=F32)
    const = pc[0:1, 0:LANES] + pc[8:9, LANES:2 * LANES]
    cmp = a[:, 0:LANES] + pltpu.roll(a[:, LANES:2 * LANES], n - 1, 0) + const
    cmp_t = cmp.T
    ones = jnp.ones((VROWS - HEAD_DIM, LANES), BF16)
    for c in range(n // LANES):
        kc_ref[c] = cmp[c * LANES:(c + 1) * LANES].astype(BF16)
        vc_ref[c, 0:HEAD_DIM, :] = cmp_t[HEAD_DIM:2 * HEAD_DIM, c * LANES:(c + 1) * LANES].astype(BF16)
        vc_ref[c, HEAD_DIM:VROWS, :] = ones


def _compress(tview, wbig, pe2, layer):
    B, n, _ = tview.shape
    nch = n // LANES
    return pl.pallas_call(
        _compress_kernel,
        grid=(B,),
        in_specs=[pl.BlockSpec((None, n, CMP_STRIDE * LANES), lambda b: (b, 0, 0)),
                  pl.BlockSpec((None, CMP_STRIDE * LANES, 2 * LANES), lambda b: (layer, 0, 0)),
                  pl.BlockSpec((None, 16, CMP_STRIDE * LANES), lambda b: (layer, 0, 0))],
        out_specs=[pl.BlockSpec((None, nch, LANES, LANES), lambda b: (b, 0, 0, 0)),
                   pl.BlockSpec((None, nch, VROWS, LANES), lambda b: (b, 0, 0, 0))],
        out_shape=[jax.ShapeDtypeStruct((B, nch, LANES, LANES), BF16),
                   jax.ShapeDtypeStruct((B, nch, VROWS, LANES), BF16)],
        compiler_params=_cp(("parallel",)),
        name="compress",
    )(tview, wbig, pe2)


def _stack_heads(q_ref, nh, lane_slice=slice(None)):
    return jnp.concatenate([q_ref[HEAD_DIM * h:HEAD_DIM * (h + 1), lane_slice] for h in range(nh)], axis=1)


CMP_UNITS = 2
CMP_TQ = CMP_UNITS * LANES


def _cmp_kernel(q_ref, kc_ref, vc_ref, ov_ref, bias_ref, o_ref, mm_ref, imp_scr, *, n_sel, nstep):
    i = pl.program_id(1)
    nch_all = kc_ref.shape[0]
    parts = 4 if nch_all % 4 == 0 else (2 if nch_all % 2 == 0 else 1)
    for v in range(parts):
        pl.when(i // (nstep // parts) == v)(functools.partial(
            _cmp_body, q_ref, kc_ref, vc_ref, ov_ref, bias_ref, o_ref, mm_ref, imp_scr,
            nch=nch_all * (v + 1) // parts, nrows=n_sel * (v + 1) // parts, n_sel=n_sel))


def _cmp_body(q_ref, kc_ref, vc_ref, ov_ref, bias_ref, o_ref, mm_ref, imp_scr, *, nch, nrows, n_sel):
    i = pl.program_id(1)
    cd = (i * CMP_UNITS) // 16

    def scores(u):
        q = _stack_heads(q_ref, NSA_HEADS, slice(u * LANES, (u + 1) * LANES))
        edge = bias_ref[u]
        ss = []
        for c in range(nch):
            b = jnp.where(c < cd, 0.0, jnp.where(c == cd, edge, NEG))
            s = jnp.dot(kc_ref[c, :, 0:HEAD_DIM], q, preferred_element_type=F32)
            ss.append(s + jnp.concatenate([b] * NSA_HEADS, axis=1))
        return ss

    def finish(u, ss):
        lanes = slice(u * LANES, (u + 1) * LANES)
        m = jnp.max(functools.reduce(jnp.maximum, ss), axis=0, keepdims=True)
        valid = m > 0.5 * NEG
        imp_scr[u, 0:32 * nch + 8, :] = jnp.zeros((32 * nch + 8, imp_scr.shape[2]), F32)
        acc = None
        for c in range(nch):
            p = jnp.exp(ss[c] - m).astype(BF16)
            t = jnp.dot(vc_ref[c], p, preferred_element_type=F32)
            acc = t if acc is None else acc + t
            imp_scr[u, 32 * c:32 * c + 40, :] += jnp.dot(ov_ref[...], p, preferred_element_type=F32)
        inv = jnp.where(valid, 1.0 / acc[HEAD_DIM:HEAD_DIM + 1], 0.0)
        o = acc[0:HEAD_DIM] * inv
        for h in range(NSA_HEADS):
            o_ref[HEAD_DIM * h:HEAD_DIM * (h + 1), lanes] = o[:, h * LANES:(h + 1) * LANES]
        imp = jnp.zeros((nrows, LANES), F32)
        for h in range(NSA_HEADS):
            sl = slice(h * LANES, (h + 1) * LANES)
            imp = imp + imp_scr[u, 0:nrows, sl] * inv[:, sl]
        return imp

    pending = scores(0)
    imps = []
    for u in range(1, CMP_UNITS):
        nxt = scores(u)
        imps.append(finish(u - 1, pending))
        pending = nxt
    imps.append(finish(CMP_UNITS - 1, pending))

    blk = lax.broadcasted_iota(jnp.int32, (nrows, LANES), 0).astype(F32)

    def pick_one(imp):
        mx = jnp.max(imp, axis=0, keepdims=True)
        first = jnp.min(jnp.where(imp == mx, blk, float(nrows)), axis=0, keepdims=True)
        return jnp.where(blk == first, -jnp.inf, imp)

    curs, cands = [], []
    for u in range(CMP_UNITS):
        t = i * CMP_TQ + u * LANES + lax.broadcasted_iota(jnp.int32, (nrows, LANES), 1)
        cur = (t >> 6).astype(F32)
        forced = (blk == 0.0) | (blk == cur) | (blk == cur - 1.0)
        imp = jnp.where(forced, FORCE, imps[u])
        curs.append(cur)
        cands.append(jnp.where(blk <= cur, imp, NEG))
    cands = lax.fori_loop(0, min(SEL_TOPK, nrows), lambda _, c: tuple(pick_one(x) for x in c), tuple(cands))
    for u in range(CMP_UNITS):
        member = (cands[u] == -jnp.inf) & (blk <= curs[u])
        mm_ref[0:nrows, u * LANES:(u + 1) * LANES] = jnp.where(member, 0.0, -1.0).astype(BF16)
    if nrows < n_sel:
        mm_ref[nrows:n_sel, :] = jnp.full((n_sel - nrows, CMP_TQ), -1.0, BF16)


def _cmp_topk(qT, kc4, vc4, ov, cmp_bias):
    B, _, S = qT.shape
    nstep = S // CMP_TQ
    nch = kc4.shape[1]
    n_sel = S // SEL_BLOCK
    nq = NSA_HEADS * LANES
    nbias = cmp_bias.shape[0] // CMP_UNITS
    return pl.pallas_call(
        functools.partial(_cmp_kernel, n_sel=n_sel, nstep=nstep),
        grid=(B, nstep),
        in_specs=[pl.BlockSpec((None, NSA_HEADS * HEAD_DIM, CMP_TQ), lambda b, i: (b, A_Q_BLK, i)),
                  pl.BlockSpec((None, nch, LANES, LANES), lambda b, i: (b, 0, 0, 0)),
                  pl.BlockSpec((None, nch, VROWS, LANES), lambda b, i: (b, 0, 0, 0)),
                  pl.BlockSpec((40, LANES), lambda b, i: (0, 0)),
                  pl.BlockSpec((None, CMP_UNITS, LANES, LANES), lambda b, i: (i % nbias, 0, 0, 0))],
        out_specs=[pl.BlockSpec((None, NSA_HEADS * HEAD_DIM, CMP_TQ), lambda b, i: (b, 0, i)),
                   pl.BlockSpec((None, n_sel, CMP_TQ), lambda b, i: (b, 0, i))],
        out_shape=[jax.ShapeDtypeStruct((B, NSA_HEADS * HEAD_DIM, S), F32),
                   jax.ShapeDtypeStruct((B, n_sel, S), BF16)],
        scratch_shapes=[pltpu.VMEM((CMP_UNITS, 32 * nch + 64, nq), F32)],
        compiler_params=_cp(("parallel", "parallel")),
        name="nsa_cmp_topk",
    )(qT, kc4, vc4, ov, cmp_bias.reshape(nbias, CMP_UNITS, LANES, LANES))


SEL_GROUP = 8
SEL_TQ = 2 * LANES


def _sel_kernel(q_ref, mm_ref, ks_ref, v_ref, causal_ref, ocmp_ref, owin_ref, gate_ref, out_ref,
                qa_scr, mm_scr, s_scr):
    i = pl.program_id(1)
    nq = NSA_HEADS * SEL_TQ
    qa_scr[0:HEAD_DIM, :] = _stack_heads(q_ref, NSA_HEADS)
    qa_scr[HEAD_DIM:LANES, :] = jnp.zeros((LANES - HEAD_DIM, nq), BF16)
    mm = mm_ref[...]
    for h in range(NSA_HEADS):
        mm_scr[:, h * SEL_TQ:(h + 1) * SEL_TQ] = mm

    gkeys = SEL_GROUP * LANES

    def load_query(gi):
        rg = (gi * SEL_GROUP) // 8
        qa_scr[HEAD_DIM:HEAD_DIM + 16, :] = mm_scr[pl.ds(pl.multiple_of(rg * 16, 16), 16), :]
        return qa_scr[...]

    def chunk_scores(gi, u, qa):
        row = pl.multiple_of((gi * SEL_GROUP + u) * LANES, LANES)
        return jnp.dot(ks_ref[pl.ds(row, LANES), :], qa, preferred_element_type=F32)

    def weighted_values(gi, ps):
        vcat = jnp.concatenate([v_ref[gi * SEL_GROUP + u] for u in range(SEL_GROUP)], axis=1)
        return jnp.dot(vcat, jnp.concatenate(ps, axis=0), preferred_element_type=F32)

    def body(gi, carry):
        m, acc, mg = carry
        m_new = jnp.maximum(m, mg)
        qa = load_query(gi + 1)
        ps, mx = [], None
        for u in range(SEL_GROUP):
            rows = slice(u * LANES, (u + 1) * LANES)
            ps.append(jnp.exp(s_scr[rows, :] - m_new).astype(BF16))
            nxt = chunk_scores(gi + 1, u, qa)
            s_scr[rows, :] = nxt
            mx = nxt if mx is None else jnp.maximum(mx, nxt)
        acc = acc * jnp.exp(m - m_new) + weighted_values(gi, ps)
        return m_new, acc, jnp.max(mx, axis=0, keepdims=True)

    qa = load_query(0)
    mx = None
    for u in range(SEL_GROUP):
        s0 = chunk_scores(0, u, qa)
        s_scr[u * LANES:(u + 1) * LANES, :] = s0
        mx = s0 if mx is None else jnp.maximum(mx, s0)
    c0 = i * (SEL_TQ // LANES)
    last = c0 // SEL_GROUP
    m, acc, _ = lax.fori_loop(0, last, body, (jnp.full((1, nq), NEG, F32), jnp.zeros((VROWS, nq), F32),
                                              jnp.max(mx, axis=0, keepdims=True)))
    drow = pl.multiple_of((c0 % SEL_GROUP) * LANES, SEL_TQ)
    s_scr[pl.ds(drow, SEL_TQ), :] += causal_ref[...]
    s = s_scr[...]
    m_new = jnp.maximum(m, jnp.max(s, axis=0, keepdims=True))
    acc = acc * jnp.exp(m - m_new) + weighted_values(last, [jnp.exp(s - m_new).astype(BF16)])
    o = acc[0:HEAD_DIM] * (1.0 / acc[HEAD_DIM:HEAD_DIM + 1])

    g = gate_ref[...]
    for h in range(NSA_HEADS):
        rs = slice(HEAD_DIM * h, HEAD_DIM * (h + 1))
        out = (g[3 * h:3 * h + 1] * ocmp_ref[rs, :] + g[3 * h + 1:3 * h + 2] * o[:, h * SEL_TQ:(h + 1) * SEL_TQ]
               + g[3 * h + 2:3 * h + 3] * owin_ref[rs, :])
        out_ref[rs, :] = out.astype(BF16)


def _sel_attend(qT, mm, nk, v4, causal4, ocmp, owin, gates):
    B, _, S = qT.shape
    nstep = S // SEL_TQ
    n_sel = S // SEL_BLOCK
    nq = NSA_HEADS * SEL_TQ
    ar = NSA_HEADS * HEAD_DIM
    return pl.pallas_call(
        _sel_kernel,
        grid=(B, nstep),
        in_specs=[pl.BlockSpec((None, ar, SEL_TQ), lambda b, i: (b, A_Q_BLK, i)),
                  pl.BlockSpec((None, n_sel, SEL_TQ), lambda b, i: (b, 0, i)),
                  pl.BlockSpec((None, S, LANES), lambda b, i: (b, 0, 0)),
                  pl.BlockSpec((None, S // LANES, VROWS, LANES), lambda b, i: (b, 0, 0, 0)),
                  _const_spec((SEL_TQ, nq), lambda b, i: (0, 0)),
                  pl.BlockSpec((None, ar, SEL_TQ), lambda b, i: (b, 0, i)),
                  pl.BlockSpec((None, ar, SEL_TQ), lambda b, i: (b, 0, i)),
                  pl.BlockSpec((None, G_ROWS, SEL_TQ), lambda b, i: (b, 0, i))],
        out_specs=pl.BlockSpec((None, ar, SEL_TQ), lambda b, i: (b, 0, i)),
        out_shape=jax.ShapeDtypeStruct((B, ar, S), BF16),
        scratch_shapes=[pltpu.VMEM((LANES, nq), BF16), pltpu.VMEM((n_sel, nq), BF16),
                        pltpu.VMEM((SEL_GROUP * LANES, nq), F32)],
        compiler_params=_cp(("parallel", "parallel")),
        name="nsa_sel",
    )(qT, mm, nk, v4, causal4, ocmp, owin, gates)


def _banded_kernel(*refs, hkv, grp, nprev, zero_bias, has_sink, want_lse):
    q_ref, k_ref, v_ref, bias_ref = refs[:4]
    pos = 4
    sink_ref = None
    if has_sink:
        sink_ref = refs[pos]
        pos += 1
    o_ref = refs[pos]
    lse_ref = refs[pos + 1] if want_lse else None
    j = pl.program_id(1)

    def scores(s, g, first):
        lanes = slice(s * LANES, (s + 1) * LANES)
        qg = jnp.concatenate([q_ref[HEAD_DIM * (g * grp + u):HEAD_DIM * (g * grp + u + 1), lanes]
                              for u in range(grp)], axis=1)
        parts = []
        for ci in range(nprev + 1):
            if first:
                kc = s - nprev + ci
                if kc < 0:
                    continue
                row = kc * LANES
            else:
                kc = j * BAND_NSUB + s - nprev + ci
                row = pl.multiple_of(kc * LANES, LANES)
            sc = jnp.dot(k_ref[pl.ds(row, LANES), HEAD_DIM * g:HEAD_DIM * (g + 1)], qg,
                         preferred_element_type=F32)
            if not zero_bias[ci]:
                sc = sc + bias_ref[ci]
            parts.append((kc, sc))
        return parts

    def finish(s, g, parts):
        lanes = slice(s * LANES, (s + 1) * LANES)
        m = None
        for _, sc in parts:
            mc = jnp.max(sc, axis=0, keepdims=True)
            m = mc if m is None else jnp.maximum(m, mc)
        if has_sink:
            sk = sink_ref[g, 0:1, :]
            m = jnp.maximum(m, sk)
        acc = None
        for kcc, sc in parts:
            t = jnp.dot(v_ref[kcc, VROWS * g:VROWS * (g + 1), :], jnp.exp(sc - m).astype(BF16),
                        preferred_element_type=F32)
            acc = t if acc is None else acc + t
        l = acc[HEAD_DIM:HEAD_DIM + 1]
        if has_sink:
            l = l + jnp.exp(sk - m)
        o = acc[0:HEAD_DIM] * (1.0 / l)
        for u in range(grp):
            hq = g * grp + u
            o_ref[HEAD_DIM * hq:HEAD_DIM * (hq + 1), lanes] = o[:, u * LANES:(u + 1) * LANES].astype(o_ref.dtype)
        if want_lse:
            lse_ref[8 * g:8 * (g + 1), lanes] = jnp.broadcast_to(m + jnp.log(l), (8, LANES))

    def run(first):
        pending = None
        for s in range(BAND_NSUB):
            for g in range(hkv):
                parts = scores(s, g, first)
                if pending is not None:
                    finish(*pending)
                pending = (s, g, parts)
        finish(*pending)

    pl.when(j == 0)(lambda: run(True))
    pl.when(j > 0)(lambda: run(False))


def _banded(qT, q_blk, nk, k_blk, v4, v_blk, bias_and_zero, *, hkv, grp, nprev, out_dtype, sinks=None,
            want_lse=False, name):
    bias, zero_bias = bias_and_zero
    assert nprev <= BAND_NSUB
    B, _, S = qT.shape
    nt = S // BAND_TILE
    qrows = hkv * grp * HEAD_DIM
    in_specs = [pl.BlockSpec((None, qrows, BAND_TILE), lambda b, j: (b, q_blk, j)),
                pl.BlockSpec((None, S, LANES), lambda b, j: (b, 0, k_blk)),
                pl.BlockSpec((None, S // LANES, hkv * VROWS, LANES), lambda b, j: (b, 0, v_blk, 0)),
                pl.BlockSpec(bias.shape, lambda b, j: (0, 0, 0))]
    args = [qT, nk, v4, bias]
    if sinks is not None:
        in_specs.append(pl.BlockSpec(sinks.shape, lambda b, j: (0, 0, 0)))
        args.append(sinks)
    out_specs = [pl.BlockSpec((None, qrows, BAND_TILE), lambda b, j: (b, 0, j))]
    out_shape = [jax.ShapeDtypeStruct((B, qrows, S), out_dtype)]
    if want_lse:
        out_specs.append(pl.BlockSpec((None, 8 * hkv, BAND_TILE), lambda b, j: (b, 0, j)))
        out_shape.append(jax.ShapeDtypeStruct((B, 8 * hkv, S), F32))
    return pl.pallas_call(
        functools.partial(_banded_kernel, hkv=hkv, grp=grp, nprev=nprev, zero_bias=zero_bias,
                          has_sink=sinks is not None, want_lse=want_lse),
        grid=(B, nt),
        in_specs=in_specs,
        out_specs=out_specs,
        out_shape=out_shape,
        compiler_params=_cp(("parallel", "parallel")),
        name=name,
    )(*args)


def _folded_kernel(q_ref, k_ref, v_ref, bias_ref, o_ref, lse_ref):
    j = pl.program_id(2)
    nsub = q_ref.shape[0] // LANES
    ones = jnp.ones((VROWS - HEAD_DIM, LANES), BF16)
    row = lax.broadcasted_iota(jnp.int32, (LANES, LANES), 0)

    def scores(s):
        qt = q_ref[s * LANES:(s + 1) * LANES, :].astype(F32).T.astype(BF16)
        parts = []
        for ci in range(2):
            kc = j * nsub + s - 1 + ci
            kcc = jnp.maximum(kc, 0)
            rows = pl.ds(pl.multiple_of(kcc * LANES, LANES), LANES)
            bias = jnp.where(kc >= 0, bias_ref[ci], NEG)
            vt = v_ref[rows, :].astype(F32).T.astype(BF16)
            for g in range(2):
                hs = slice(HEAD_DIM * g, HEAD_DIM * (g + 1))
                sc = jnp.dot(k_ref[rows, hs], qt[hs], preferred_element_type=F32) + bias
                parts.append((g, jnp.concatenate([vt[hs], ones], axis=0), sc))
        return parts

    def finish(s, parts):
        outs, lses = [], []
        for g in range(2):
            mine = [(v, sc) for gg, v, sc in parts if gg == g]
            m = functools.reduce(jnp.maximum, [jnp.max(sc, axis=0, keepdims=True) for _, sc in mine])
            acc = None
            for v, sc in mine:
                t = jnp.dot(v, jnp.exp(sc - m).astype(BF16), preferred_element_type=F32)
                acc = t if acc is None else acc + t
            l = acc[HEAD_DIM:HEAD_DIM + 1]
            outs.append(acc[0:HEAD_DIM] * (1.0 / l))
            lses.append(m + jnp.log(l))
        rows = slice(s * LANES, (s + 1) * LANES)
        o_ref[rows, :] = jnp.concatenate(outs, axis=0).T
        lse_ref[rows, :] = jnp.where(row == 0, lses[0], jnp.where(row == 1, lses[1], 0.0)).T

    pending = None
    for s in range(nsub):
        parts = scores(s)
        if pending is not None:
            finish(*pending)
        pending = (s, parts)
    finish(*pending)


def _folded_dilated(fq, fk, fv, bias, dil):
    B, n, _ = fq.shape
    tq = min(n, BAND_TILE)
    return pl.pallas_call(
        _folded_kernel,
        grid=(B, dil, n // tq),
        in_specs=[pl.BlockSpec((None, tq, LANES), lambda b, r, j: (b, j, r)),
                  pl.BlockSpec((None, n, LANES), lambda b, r, j: (b, 0, r)),
                  pl.BlockSpec((None, n, LANES), lambda b, r, j: (b, 0, r)),
                  pl.BlockSpec(bias.shape, lambda b, r, j: (0, 0, 0))],
        out_specs=[pl.BlockSpec((None, tq, LANES), lambda b, r, j: (b, j, r)),
                   pl.BlockSpec((None, tq, LANES), lambda b, r, j: (b, j, r))],
        out_shape=[jax.ShapeDtypeStruct((B, n, dil * LANES), F32),
                   jax.ShapeDtypeStruct((B, n, dil * LANES), F32)],
        compiler_params=_cp(("parallel", "parallel", "parallel")),
        name="dil%d_folded" % dil,
    )(fq, fk, fv, bias)


def _post_norm_residual(y, x_ref, gw_ref, o_ref, token_major=False):
    ms = jnp.mean(y * y, axis=0, keepdims=True)
    yn = y * lax.rsqrt(ms + RMS_EPS)
    gw = gw_ref[...]
    for c in range(NSUB):
        sl = slice(c * LANES, (c + 1) * LANES)
        out = x_ref[:, sl] + gw * yn[:, sl]
        if token_major:
            o_ref[sl, :] = out.T
        else:
            o_ref[:, sl] = out


def _mix_ffn_kernel(a_ref, b0_ref, b1_ref, b2_ref, l0_ref, l1_ref, l2_ref, c_ref, wo_ref, x_ref, gw1_ref,
                    a2_ref, sh2_ref, wg_ref, wu_ref, wd_ref, cv_ref, gw2_ref, o_ref,
                    mix_scr, x1_scr, h_scr, carry_scr, act_scr, unf_scr, *, fchunk, token_major_out):
    s = pl.program_id(1)
    cur = s % 2
    prv = 1 - cur
    d_ff = wg_ref.shape[0]

    @pl.when(s == 0)
    def _():
        carry_scr[...] = jnp.zeros(carry_scr.shape, F32)
        h_scr[1] = jnp.zeros(h_scr.shape[1:], BF16)
        x1_scr[1] = jnp.zeros(x1_scr.shape[1:], F32)

    h = h_scr[prv]
    lane = lax.broadcasted_iota(jnp.int32, (fchunk, LANES), 1)
    for c in range(d_ff // fchunk):
        rs = slice(c * fchunk, (c + 1) * fchunk)
        g = jnp.dot(wg_ref[rs, :], h, preferred_element_type=F32)
        prev = carry_scr[rs, :]
        carry_scr[rs, :] = g[:, TILE - LANES:TILE]
        g1 = pltpu.roll(g, 1, 1)
        g2 = pltpu.roll(g, 2, 1)
        fix1 = jnp.where(lane < 1, pltpu.roll(prev, 1, 1), g1[:, 0:LANES])
        fix2 = jnp.where(lane < 2, pltpu.roll(prev, 2, 1), g2[:, 0:LANES])
        g1 = jnp.concatenate([fix1, g1[:, LANES:]], axis=1)
        g2 = jnp.concatenate([fix2, g2[:, LANES:]], axis=1)
        w0 = jnp.concatenate([cv_ref[0, rs, :]] * NSUB, axis=1)
        w1 = jnp.concatenate([cv_ref[1, rs, :]] * NSUB, axis=1)
        w2 = jnp.concatenate([cv_ref[2, rs, :]] * NSUB, axis=1)
        cb = jnp.concatenate([cv_ref[3, rs, :]] * NSUB, axis=1)
        acc = g2 * w0 + g1 * w1 + g * w2 + cb
        up = jnp.dot(wu_ref[rs, :], h, preferred_element_type=F32)
        act_scr[rs, :] = (jax.nn.gelu(acc, approximate=True) * up).astype(BF16)

    def unfold(src_ref, fold):
        for res in range(fold):
            unf_scr[pl.ds(res, TILE // fold, stride=fold), :] = src_ref[:, res * LANES:(res + 1) * LANES]
        return unf_scr[...].T

    d1, d2 = DIL_PATTERNS[1][1], DIL_PATTERNS[2][1]
    b_vals = (b0_ref[...], unfold(b1_ref, d1), unfold(b2_ref, d2))
    l1t, l2t = unfold(l1_ref, d1), unfold(l2_ref, d2)
    for hh in range(2):
        ls = [l0_ref[8 * hh:8 * hh + 1, :], l1t[hh:hh + 1, :], l2t[hh:hh + 1, :]]
        mx = jnp.maximum(jnp.maximum(ls[0], ls[1]), ls[2])
        es = [jnp.exp(v - mx) for v in ls]
        inv = 1.0 / (es[0] + es[1] + es[2])
        for g in range(3):
            rs = slice(HEAD_DIM * hh, HEAD_DIM * (hh + 1))
            mix_scr[128 * g + HEAD_DIM * hh:128 * g + HEAD_DIM * (hh + 1), :] = (
                b_vals[g][rs, :] * (es[g] * inv)).astype(BF16)
    y1 = jnp.dot(wo_ref[:, 0:256], a_ref[...], preferred_element_type=F32)
    y1 = y1 + jnp.dot(wo_ref[:, 256:640], mix_scr[...], preferred_element_type=F32)
    y1 = y1 + jnp.dot(wo_ref[:, 640:1024], c_ref[...], preferred_element_type=F32)

    y2 = jnp.dot(wd_ref[...], act_scr[...], preferred_element_type=F32)

    _post_norm_residual(y1, x_ref, gw1_ref, x1_scr.at[cur])
    _norm_mod_to_scratch(x1_scr.at[cur], a2_ref, sh2_ref, h_scr.at[cur])
    _post_norm_residual(y2, x1_scr.at[prv], gw2_ref, o_ref, token_major_out)


def _mix_ffn(aT, bs, lses, cT, w_outT, xT, gw1, a2, sh2, wgT, wuT, wdT, cv, gw2, layer, token_major_out):
    B, D, S = xT.shape
    nt = S // TILE
    d_ff = wgT.shape[1]
    tile = lambda rows: pl.BlockSpec((None, rows, TILE), lambda b, s: (b, 0, jnp.minimum(s, nt - 1)))
    tok = lambda d: pl.BlockSpec((None, TILE // d, d * LANES), lambda b, s: (b, jnp.minimum(s, nt - 1), 0))
    t1, t2 = tok(DIL_PATTERNS[1][1]), tok(DIL_PATTERNS[2][1])
    vec = pl.BlockSpec((None, D, LANES), lambda b, s: (b, 0, 0))
    return pl.pallas_call(
        functools.partial(_mix_ffn_kernel, fchunk=256, token_major_out=token_major_out),
        grid=(B, nt + 1),
        in_specs=[tile(256), tile(128), t1, t2, tile(16), t1, t2, tile(384),
                  _const_spec((None, D, D), lambda b, s: (layer, 0, 0)),
                  tile(D), vec, vec, vec,
                  _const_spec((None, d_ff, D), lambda b, s: (layer, 0, 0)),
                  _const_spec((None, d_ff, D), lambda b, s: (layer, 0, 0)),
                  _const_spec((None, D, d_ff), lambda b, s: (layer, 0, 0)),
                  _const_spec((None, 4, d_ff, LANES), lambda b, s: (layer, 0, 0, 0)),
                  vec],
        out_specs=(pl.BlockSpec((None, TILE, D), lambda b, s: (b, jnp.maximum(s - 1, 0), 0)) if token_major_out
                   else pl.BlockSpec((None, D, TILE), lambda b, s: (b, 0, jnp.maximum(s - 1, 0)))),
        out_shape=jax.ShapeDtypeStruct((B, S, D) if token_major_out else (B, D, S), F32),
        scratch_shapes=[pltpu.VMEM((384, TILE), BF16), pltpu.VMEM((2, D, TILE), F32),
                        pltpu.VMEM((2, D, TILE), BF16), pltpu.VMEM((d_ff, LANES), F32),
                        pltpu.VMEM((d_ff, TILE), BF16), pltpu.VMEM((TILE, LANES), F32)],
        compiler_params=_cp(("arbitrary", "arbitrary")),
        name="mix_ffn",
    )(aT, bs[0], bs[1], bs[2], lses[0], lses[1], lses[2], cT, w_outT, xT, gw1,
      a2, sh2, wgT, wuT, wdT, cv, gw2)


def _band_bias(old_edge, reps, nprev):
    kk = np.arange(LANES)[:, None]
    qq = np.arange(LANES)[None, :]
    tabs = [kk - qq >= old_edge] + [np.ones((LANES, LANES), bool)] * (nprev - 1) + [kk <= qq]
    zero = tuple(bool(t.all()) for t in tabs)
    out = np.stack([np.where(t, 0.0, NEG) for t in tabs]).astype(np.float32)
    return jnp.asarray(np.tile(out, (1, 1, reps))), zero


def _cmp_bias():
    nn = np.arange(LANES)[:, None]
    qq = np.arange(LANES)[None, :]
    tabs = [np.where(CMP_STRIDE * nn + CMP_BLOCK - 1 <= LANES * r + qq, 0.0, NEG) for r in range(16)]
    return jnp.asarray(np.stack(tabs).astype(np.float32))


def _overlap_rows():
    jj = np.arange(40)[:, None]
    nn = np.arange(LANES)[None, :]
    return jnp.asarray(((nn >= 4 * jj - 1) & (nn <= 4 * jj + 3)).astype(np.float32), dtype=BF16)


_IN_COL_ORDER = ((1804, 2188), (652, 1036), (0, 256),
                 (256, 320), (320, 384), (384, 448), (512, 576), (1036, 1420), (2188, 2316),
                 (448, 512), (576, 640), (1420, 1804), (2316, 2444),
                 (640, 652))


def _prep_w_in(w_in):
    wt = jnp.swapaxes(jnp.concatenate([w_in[:, :, a:b] for a, b in _IN_COL_ORDER], axis=2), 1, 2)
    scale = np.ones((wt.shape[1], 1), np.float32)
    scale[:Q_ROWS] = HEAD_DIM ** -0.5
    wt = wt * scale
    wt = jnp.pad(wt, ((0, 0), (0, W_ROWS - wt.shape[1]), (0, 0)))
    return wt.astype(BF16)


def _prep_compress(w_ck, w_cv, pe_k, pe_v):
    L = w_ck.shape[0]
    half = CMP_BLOCK // 2

    def big(lo):
        wk = w_ck[:, lo:lo + half]
        wv = w_cv[:, lo:lo + half]
        z = jnp.zeros_like(wk)
        top = jnp.concatenate([wk, z], axis=-1)
        bot = jnp.concatenate([z, wv], axis=-1)
        return jnp.concatenate([top, bot], axis=2).reshape(L, half * LANES, LANES)

    wbig = jnp.concatenate([big(0), big(half)], axis=-1).astype(BF16)
    pe = jnp.concatenate([pe_k, pe_v], axis=-1)
    pe2 = jnp.zeros((L, 16, half * LANES), F32)
    pe2 = pe2.at[:, 0].set(pe[:, :half].reshape(L, -1)).at[:, 8].set(pe[:, half:].reshape(L, -1))
    return wbig, pe2.astype(BF16)


def _lane_bcast(v):
    return jnp.broadcast_to(v[..., None], v.shape + (LANES,))


def kernel(x, c, positions, w_in, w_out, w_ada, b_ada, norm_w, cmp_w_k, cmp_w_v, cmp_pe_k, cmp_pe_v,
           sinks, w_gate, w_up, conv_w, conv_b, w_down):
    B, S, D = x.shape
    depth = w_in.shape[0]
    assert S % 2048 == 0 and D == 1024 and w_in.shape[2] == 2444

    inv = ROPE_THETA ** (-jnp.arange(0, HEAD_DIM, 2, dtype=F32) / HEAD_DIM)
    ang = positions.astype(F32)[:, None, :] * inv[None, :, None]
    cosT, sinT = jnp.cos(ang), jnp.sin(ang)

    c8 = jnp.pad(c, ((0, 8 - B), (0, 0)))
    ada = _adaln(c8, w_ada, b_ada)[:, :B]
    sh1, sc1, g1, sh2, sc2, g2 = [ada[:, :, k * D:(k + 1) * D] for k in range(6)]
    nw = norm_w[:, :, None, :]
    a1 = _lane_bcast(nw[:, 0] * (1 + sc1))
    gw1 = _lane_bcast(g1 * nw[:, 1])
    a2 = _lane_bcast(nw[:, 2] * (1 + sc2))
    gw2 = _lane_bcast(g2 * nw[:, 3])
    sh1b, sh2b = _lane_bcast(sh1), _lane_bcast(sh2)

    w_inT = _prep_w_in(w_in)
    w_outT = jnp.swapaxes(w_out, 1, 2).astype(BF16)
    wgT = jnp.swapaxes(w_gate, 1, 2).astype(BF16)
    wuT = jnp.swapaxes(w_up, 1, 2).astype(BF16)
    wdT = jnp.swapaxes(w_down, 1, 2).astype(BF16)
    cv = _lane_bcast(jnp.concatenate([conv_w, conv_b[:, None, :]], axis=1))
    wbig, pe2 = _prep_compress(cmp_w_k, cmp_w_v, cmp_pe_k, cmp_pe_v)
    sink_tab = jnp.broadcast_to(
        jnp.repeat(sinks.reshape(depth, SWA_KV_HEADS, SWA_HEADS // SWA_KV_HEADS), LANES, axis=-1)[:, :, None, :],
        (depth, SWA_KV_HEADS, 8, LANES * (SWA_HEADS // SWA_KV_HEADS)))

    ov = _overlap_rows()
    cmp_bias = _cmp_bias()
    kk = np.arange(SEL_TQ)[:, None]
    causal4 = jnp.asarray(np.tile(np.where(kk <= kk.T, 0.0, NEG).astype(np.float32), (1, NSA_HEADS)))
    bias_win = _band_bias(1, NSA_HEADS, NSA_WINDOW // LANES)
    bias_swa = _band_bias(1, SWA_HEADS // SWA_KV_HEADS, SWA_WINDOW // LANES)
    bias_dil = _band_bias(0, 1, 1)
    assert all(w // d == LANES for w, d in DIL_PATTERNS)

    xT = x
    for l in range(depth):
        first, last_layer = l == 0, l == depth - 1
        outs = _inproj(xT, a1[l], sh1b[l], w_inT, l, cosT, sinT, token_major=first)
        qT, kcvc, nk, v4, gates = outs[:5]
        folded = {g: outs[5 + 3 * k:8 + 3 * k] for k, (g, _) in enumerate(FOLDED)}
        if first:
            xT = outs[5 + 3 * len(FOLDED)]
        kc4, vc4 = _compress(kcvc, wbig, pe2, l)
        ocmp, mm = _cmp_topk(qT, kc4, vc4, ov, cmp_bias)
        owin, = _banded(qT, A_Q_BLK, nk, 1, v4, 1, bias_win, hkv=1, grp=NSA_HEADS, nprev=NSA_WINDOW // LANES,
                        out_dtype=F32, name="nsa_win")
        aT = _sel_attend(qT, mm, nk, v4, causal4, ocmp, owin, gates)
        o, lse = _banded(qT, 3, nk, 2, v4, 1, bias_dil, hkv=2, grp=1, nprev=1, out_dtype=F32, want_lse=True,
                         name="dil1")
        bs, lses = [o], [lse]
        for g, _ in FOLDED:
            o, lse = _folded_dilated(*folded[g], bias_dil[0], DIL_PATTERNS[g][1])
            bs.append(o)
            lses.append(lse)
        cT, = _banded(qT, 0, nk, 5, v4, 4, bias_swa, hkv=SWA_KV_HEADS, grp=SWA_HEADS // SWA_KV_HEADS,
                      nprev=SWA_WINDOW // LANES, out_dtype=BF16,
                      sinks=sink_tab[l], name="swa")
        xT = _mix_ffn(aT, bs, lses, cT, w_outT, xT, gw1[l], a2[l], sh2b[l], wgT, wuT, wdT, cv, gw2[l], l,
                      token_major_out=last_layer)
    return xT
```

```python
import functools

import numpy as np
import jax
import jax.numpy as jnp
from jax import lax
from jax.experimental import pallas as pl
from jax.experimental.pallas import tpu as pltpu

F32 = jnp.float32
BF16 = jnp.bfloat16

HEAD_DIM = 64
HALF = HEAD_DIM // 2
NSA_HEADS = 4
CMP_BLOCK = 32
CMP_STRIDE = 16
SEL_BLOCK = 64
SEL_TOPK = 16
NSA_WINDOW = 512
DIL_PATTERNS = ((128, 1), (512, 4), (2048, 16))
SWA_HEADS = 6
SWA_KV_HEADS = 2
SWA_WINDOW = 128
ROPE_THETA = 10000.0
RMS_EPS = 1e-6
NEG = -1e30
FORCE = 1e4

LANES = 128
TILE = 512
NSUB = TILE // LANES
BAND_TILE = 1024
BAND_NSUB = BAND_TILE // LANES
VROWS = HEAD_DIM + 16
MEMBER_BIG = 2.0 ** 100
VMEM_LIMIT = 56 * 1024 * 1024

Q_ROWS = 1024
A_Q_BLK = 3
NK_ROWS = 768
NK_LANES = 768
V_ROWS = 640
G_ROWS = 16
W_ROWS = Q_ROWS + NK_ROWS + V_ROWS + G_ROWS
N_VPIECES = V_ROWS // HEAD_DIM
FOLDED = ((1, slice(128, 256)), (2, slice(256, 384)))
FOLD = CMP_STRIDE
assert DIL_PATTERNS[2][1] == FOLD


def _cp(sem):
    return pltpu.CompilerParams(dimension_semantics=sem, vmem_limit_bytes=VMEM_LIMIT)


def _const_spec(shape, index_map):
    return pl.BlockSpec(shape, index_map, pipeline_mode=pl.Buffered(1))


def _adaln_kernel(c_ref, w_ref, b_ref, o_ref):
    c = c_ref[...]
    cond = c * jax.nn.sigmoid(c)
    o_ref[...] = jnp.dot(cond, w_ref[...], preferred_element_type=F32,
                         precision=lax.Precision.HIGHEST) + b_ref[...]


def _adaln(c8, w_ada, b_ada):
    depth, d, six_d = w_ada.shape
    nblk = six_d // d
    return pl.pallas_call(
        _adaln_kernel,
        grid=(depth, nblk),
        in_specs=[pl.BlockSpec((8, d), lambda l, n: (0, 0)),
                  pl.BlockSpec((None, d, d), lambda l, n: (l, 0, n)),
                  pl.BlockSpec((None, 1, d), lambda l, n: (l, 0, n))],
        out_specs=pl.BlockSpec((None, 8, d), lambda l, n: (l, 0, n)),
        out_shape=jax.ShapeDtypeStruct((depth, 8, six_d), F32),
        compiler_params=_cp(("parallel", "parallel")),
        name="adaln",
    )(c8, w_ada, b_ada.reshape(depth, 1, six_d))


def _norm_mod_to_scratch(x_ref, a_ref, sh_ref, h_scr):
    for c in range(NSUB):
        sl = slice(c * LANES, (c + 1) * LANES)
        xs = x_ref[:, sl]
        ms = jnp.mean(xs * xs, axis=0, keepdims=True)
        h_scr[:, sl] = ((xs * lax.rsqrt(ms + RMS_EPS)) * a_ref[...] + sh_ref[...]).astype(BF16)


def _inproj_kernel(x_ref, a_ref, sh_ref, w_ref, cos_ref, sin_ref,
                   q_ref, kcvc_ref, nk_ref, v4_ref, gate_ref, fq1_ref, fk1_ref, fv1_ref, fq2_ref, fk2_ref, fv2_ref,
                   *rest, token_major):
    j = pl.program_id(1)
    if token_major:
        xt_ref, h_scr, fold_scr = rest
        for c in range(NSUB):
            xt_ref[:, c * LANES:(c + 1) * LANES] = x_ref[c * LANES:(c + 1) * LANES, :].T
        x_ref = xt_ref
    else:
        h_scr, fold_scr = rest
    _norm_mod_to_scratch(x_ref, a_ref, sh_ref, h_scr)
    h = h_scr[...]
    cos = cos_ref[...]
    sin = sin_ref[...]

    def proj(r0, r1):
        return jnp.dot(w_ref[r0:r1, :], h, preferred_element_type=F32)

    def fold_store(t, dst_ref, fold=FOLD):
        fold_scr[...] = t
        for res in range(fold):
            dst_ref[:, res * LANES:(res + 1) * LANES] = fold_scr[pl.ds(res, TILE // fold, stride=fold), :].astype(BF16)

    folded_refs = {1: (fq1_ref, fk1_ref, fv1_ref), 2: (fq2_ref, fk2_ref, fv2_ref)}

    def rope(r, nh):
        outs = []
        for hh in range(nh):
            t1 = r[HEAD_DIM * hh:HEAD_DIM * hh + HALF]
            t2 = r[HEAD_DIM * hh + HALF:HEAD_DIM * (hh + 1)]
            outs.append(t1 * cos - t2 * sin)
            outs.append(t2 * cos + t1 * sin)
        return jnp.concatenate(outs, axis=0)

    for r0, r1 in ((0, 384), (384, 768), (768, 1024)):
        r = rope(proj(r0, r1), (r1 - r0) // HEAD_DIM)
        q_ref[r0:r1, :] = r.astype(BF16)
        if r0 == 384:
            for grp_i, rows in FOLDED:
                fold_store(r[rows].T, folded_refs[grp_i][0], DIL_PATTERNS[grp_i][1])

    base = Q_ROWS
    r = proj(base, base + 128)
    kcvc = jnp.concatenate([rope(r[0:64], 1), r[64:128]], axis=0)
    fold_store(kcvc.T, kcvc_ref)

    r = proj(base + 128, base + 192)
    tok = j * TILE + lax.broadcasted_iota(jnp.int32, (HEAD_DIM, TILE), 1)
    row = lax.broadcasted_iota(jnp.int32, (HEAD_DIM, TILE), 0)
    member_cols = jnp.where(row == ((tok >> 6) & 15), MEMBER_BIG, 0.0).astype(F32)
    nk_ref[:, 0:128] = jnp.concatenate([rope(r, 1), member_cols], axis=0).T.astype(BF16)

    r = proj(base + 192, base + 256)
    nk_ref[:, 128:256] = jnp.concatenate([rope(r, 1), jnp.zeros((HEAD_DIM, TILE), F32)], axis=0).T.astype(BF16)

    r = proj(base + 256, base + 640)
    bk = rope(r, 6).T
    nk_ref[:, 256:640] = bk.astype(BF16)
    for grp_i, rows in FOLDED:
        fold_store(bk[:, rows], folded_refs[grp_i][1], DIL_PATTERNS[grp_i][1])

    r = proj(base + 640, base + 768)
    nk_ref[:, 640:768] = rope(r, 2).T.astype(BF16)

    base = Q_ROWS + NK_ROWS
    r = proj(base, base + V_ROWS)
    for grp_i, rows in FOLDED:
        fold_store(r[128 + rows.start:128 + rows.stop].T, folded_refs[grp_i][2], DIL_PATTERNS[grp_i][1])
    r = r.astype(BF16)
    ones = jnp.ones((VROWS - HEAD_DIM, LANES), BF16)
    for c in range(NSUB):
        for p in range(N_VPIECES):
            v4_ref[c, VROWS * p:VROWS * p + HEAD_DIM, :] = r[HEAD_DIM * p:HEAD_DIM * (p + 1),
                                                             c * LANES:(c + 1) * LANES]
            v4_ref[c, VROWS * p + HEAD_DIM:VROWS * (p + 1), :] = ones

    base = Q_ROWS + NK_ROWS + V_ROWS
    gate_ref[...] = jax.nn.sigmoid(proj(base, base + G_ROWS))


def _inproj(x, a1, sh1, w_inT, layer, cosT, sinT, token_major):
    if token_major:
        B, S, D = x.shape
        x_spec = pl.BlockSpec((None, TILE, D), lambda b, j: (b, j, 0))
    else:
        B, D, S = x.shape
        x_spec = pl.BlockSpec((None, D, TILE), lambda b, j: (b, 0, j))
    nt = S // TILE
    fold_specs = {d: pl.BlockSpec((None, TILE // d, d * LANES), lambda b, j: (b, j, 0)) for _, d in DIL_PATTERNS}
    fold_spec = fold_specs[FOLD]
    out_specs = [pl.BlockSpec((None, Q_ROWS, TILE), lambda b, j: (b, 0, j)),
                 fold_spec,
                 pl.BlockSpec((None, TILE, NK_LANES), lambda b, j: (b, j, 0)),
                 pl.BlockSpec((None, NSUB, N_VPIECES * VROWS, LANES), lambda b, j: (b, j, 0, 0)),
                 pl.BlockSpec((None, G_ROWS, TILE), lambda b, j: (b, 0, j)),
                 ] + [fold_specs[DIL_PATTERNS[g][1]] for g, _ in FOLDED for _ in range(3)]
    out_shape = [jax.ShapeDtypeStruct((B, Q_ROWS, S), BF16),
                 jax.ShapeDtypeStruct((B, S // FOLD, FOLD * LANES), BF16),
                 jax.ShapeDtypeStruct((B, S, NK_LANES), BF16),
                 jax.ShapeDtypeStruct((B, S // LANES, N_VPIECES * VROWS, LANES), BF16),
                 jax.ShapeDtypeStruct((B, G_ROWS, S), F32)] + [
                     jax.ShapeDtypeStruct((B, S // DIL_PATTERNS[g][1], DIL_PATTERNS[g][1] * LANES), BF16)
                     for g, _ in FOLDED for _ in range(3)]
    if token_major:
        out_specs.append(pl.BlockSpec((None, D, TILE), lambda b, j: (b, 0, j)))
        out_shape.append(jax.ShapeDtypeStruct((B, D, S), F32))
    return pl.pallas_call(
        functools.partial(_inproj_kernel, token_major=token_major),
        grid=(B, nt),
        in_specs=[x_spec,
                  pl.BlockSpec((None, D, LANES), lambda b, j: (b, 0, 0)),
                  pl.BlockSpec((None, D, LANES), lambda b, j: (b, 0, 0)),
                  _const_spec((None, W_ROWS, D), lambda b, j: (layer, 0, 0)),
                  pl.BlockSpec((None, HALF, TILE), lambda b, j: (b, 0, j)),
                  pl.BlockSpec((None, HALF, TILE), lambda b, j: (b, 0, j))],
        out_specs=out_specs,
        out_shape=out_shape,
        scratch_shapes=[pltpu.VMEM((D, TILE), BF16), pltpu.VMEM((TILE, LANES), F32)],
        compiler_params=_cp(("parallel", "parallel")),
        name="inproj",
    )(x, a1, sh1, w_inT, cosT, sinT)


def _compress_kernel(t_ref, w_ref, pe_ref, kc_ref, vc_ref):
    n = t_ref.shape[0]
    a = jnp.dot(t_ref[...], w_ref[...], preferred_element_type=F32)
    pc = jnp.dot(pe_ref[...], w_ref[...], preferred_element_type=F32)
    const = pc[0:1, 0:LANES] + pc[8:9, LANES:2 * LANES]
    cmp = a[:, 0:LANES] + pltpu.roll(a[:, LANES:2 * LANES], n - 1, 0) + const
    cmp_t = cmp.T
    ones = jnp.ones((VROWS - HEAD_DIM, LANES), BF16)
    for c in range(n // LANES):
        kc_ref[c] = cmp[c * LANES:(c + 1) * LANES].astype(BF16)
        vc_ref[c, 0:HEAD_DIM, :] = cmp_t[HEAD_DIM:2 * HEAD_DIM, c * LANES:(c + 1) * LANES].astype(BF16)
        vc_ref[c, HEAD_DIM:VROWS, :] = ones


def _compress(tview, wbig, pe2, layer):
    B, n, _ = tview.shape
    nch = n // LANES
    return pl.pallas_call(
        _compress_kernel,
        grid=(B,),
        in_specs=[pl.BlockSpec((None, n, CMP_STRIDE * LANES), lambda b: (b, 0, 0)),
                  pl.BlockSpec((None, CMP_STRIDE * LANES, 2 * LANES), lambda b: (layer, 0, 0)),
                  pl.BlockSpec((None, 16, CMP_STRIDE * LANES), lambda b: (layer, 0, 0))],
        out_specs=[pl.BlockSpec((None, nch, LANES, LANES), lambda b: (b, 0, 0, 0)),
                   pl.BlockSpec((None, nch, VROWS, LANES), lambda b: (b, 0, 0, 0))],
        out_shape=[jax.ShapeDtypeStruct((B, nch, LANES, LANES), BF16),
                   jax.ShapeDtypeStruct((B, nch, VROWS, LANES), BF16)],
        compiler_params=_cp(("parallel",)),
        name="compress",
    )(tview, wbig, pe2)


def _stack_heads(q_ref, nh, lane_slice=slice(None)):
    return jnp.concatenate([q_ref[HEAD_DIM * h:HEAD_DIM * (h + 1), lane_slice] for h in range(nh)], axis=1)


CMP_UNITS = 4
CMP_TQ = CMP_UNITS * LANES


def _cmp_kernel(q_ref, kc_ref, vc_ref, ov_ref, bias_ref, o_ref, mm_ref, imp_scr, *, n_sel, nstep):
    i = pl.program_id(1)
    nch_all = kc_ref.shape[0]
    parts = 4 if nch_all % 4 == 0 else (2 if nch_all % 2 == 0 else 1)
    for v in range(parts):
        pl.when(i // (nstep // parts) == v)(functools.partial(
            _cmp_body, q_ref, kc_ref, vc_ref, ov_ref, bias_ref, o_ref, mm_ref, imp_scr,
            nch=nch_all * (v + 1) // parts, nrows=n_sel * (v + 1) // parts, n_sel=n_sel))


def _cmp_body(q_ref, kc_ref, vc_ref, ov_ref, bias_ref, o_ref, mm_ref, imp_scr, *, nch, nrows, n_sel):
    i = pl.program_id(1)
    cd = (i * CMP_UNITS) // 16

    def scores(u):
        q = _stack_heads(q_ref, NSA_HEADS, slice(u * LANES, (u + 1) * LANES))
        edge = bias_ref[u]
        ss = []
        for c in range(nch):
            b = jnp.where(c < cd, 0.0, jnp.where(c == cd, edge, NEG))
            s = jnp.dot(kc_ref[c, :, 0:HEAD_DIM], q, preferred_element_type=F32)
            ss.append(s + jnp.concatenate([b] * NSA_HEADS, axis=1))
        return ss

    def finish(u, ss):
        lanes = slice(u * LANES, (u + 1) * LANES)
        m = jnp.max(functools.reduce(jnp.maximum, ss), axis=0, keepdims=True)
        valid = m > 0.5 * NEG
        imp_scr[u, 0:32 * nch + 8, :] = jnp.zeros((32 * nch + 8, imp_scr.shape[2]), F32)
        acc = None
        for c in range(nch):
            p = jnp.exp(ss[c] - m).astype(BF16)
            t = jnp.dot(vc_ref[c], p, preferred_element_type=F32)
            acc = t if acc is None else acc + t
            imp_scr[u, 32 * c:32 * c + 40, :] += jnp.dot(ov_ref[...], p, preferred_element_type=F32)
        inv = jnp.where(valid, 1.0 / acc[HEAD_DIM:HEAD_DIM + 1], 0.0)
        o = acc[0:HEAD_DIM] * inv
        for h in range(NSA_HEADS):
            o_ref[HEAD_DIM * h:HEAD_DIM * (h + 1), lanes] = o[:, h * LANES:(h + 1) * LANES]
        imp = jnp.zeros((nrows, LANES), F32)
        for h in range(NSA_HEADS):
            sl = slice(h * LANES, (h + 1) * LANES)
            imp = imp + imp_scr[u, 0:nrows, sl] * inv[:, sl]
        return imp

    pending = scores(0)
    imps = []
    for u in range(1, CMP_UNITS):
        nxt = scores(u)
        imps.append(finish(u - 1, pending))
        pending = nxt
    imps.append(finish(CMP_UNITS - 1, pending))

    blk = lax.broadcasted_iota(jnp.int32, (nrows, LANES), 0).astype(F32)

    def pick_one(imp):
        mx = jnp.max(imp, axis=0, keepdims=True)
        first = jnp.min(jnp.where(imp == mx, blk, float(nrows)), axis=0, keepdims=True)
        return jnp.where(blk == first, -jnp.inf, imp)

    curs, cands = [], []
    for u in range(CMP_UNITS):
        t = i * CMP_TQ + u * LANES + lax.broadcasted_iota(jnp.int32, (nrows, LANES), 1)
        cur = (t >> 6).astype(F32)
        forced = (blk == 0.0) | (blk == cur) | (blk == cur - 1.0)
        imp = jnp.where(forced, FORCE, imps[u])
        curs.append(cur)
        cands.append(jnp.where(blk <= cur, imp, NEG))
    cands = lax.fori_loop(0, min(SEL_TOPK, nrows), lambda _, c: tuple(pick_one(x) for x in c), tuple(cands))
    for u in range(CMP_UNITS):
        member = (cands[u] == -jnp.inf) & (blk <= curs[u])
        mm_ref[0:nrows, u * LANES:(u + 1) * LANES] = jnp.where(member, 0.0, -1.0).astype(BF16)
    if nrows < n_sel:
        mm_ref[nrows:n_sel, :] = jnp.full((n_sel - nrows, CMP_TQ), -1.0, BF16)


def _cmp_topk(qT, kc4, vc4, ov, cmp_bias):
    B, _, S = qT.shape
    nstep = S // CMP_TQ
    nch = kc4.shape[1]
    n_sel = S // SEL_BLOCK
    nq = NSA_HEADS * LANES
    nbias = cmp_bias.shape[0] // CMP_UNITS
    return pl.pallas_call(
        functools.partial(_cmp_kernel, n_sel=n_sel, nstep=nstep),
        grid=(B, nstep),
        in_specs=[pl.BlockSpec((None, NSA_HEADS * HEAD_DIM, CMP_TQ), lambda b, i: (b, A_Q_BLK, i)),
                  pl.BlockSpec((None, nch, LANES, LANES), lambda b, i: (b, 0, 0, 0)),
                  pl.BlockSpec((None, nch, VROWS, LANES), lambda b, i: (b, 0, 0, 0)),
                  pl.BlockSpec((40, LANES), lambda b, i: (0, 0)),
                  pl.BlockSpec((None, CMP_UNITS, LANES, LANES), lambda b, i: (i % nbias, 0, 0, 0))],
        out_specs=[pl.BlockSpec((None, NSA_HEADS * HEAD_DIM, CMP_TQ), lambda b, i: (b, 0, i)),
                   pl.BlockSpec((None, n_sel, CMP_TQ), lambda b, i: (b, 0, i))],
        out_shape=[jax.ShapeDtypeStruct((B, NSA_HEADS * HEAD_DIM, S), F32),
                   jax.ShapeDtypeStruct((B, n_sel, S), BF16)],
        scratch_shapes=[pltpu.VMEM((CMP_UNITS, 32 * nch + 64, nq), F32)],
        compiler_params=_cp(("parallel", "parallel")),
        name="nsa_cmp_topk",
    )(qT, kc4, vc4, ov, cmp_bias.reshape(nbias, CMP_UNITS, LANES, LANES))


SEL_GROUP = 8
SEL_TQ = 2 * LANES


def _sel_kernel(q_ref, mm_ref, ks_ref, v_ref, causal_ref, ocmp_ref, owin_ref, gate_ref, out_ref,
                qa_scr, mm_scr, s_scr):
    i = pl.program_id(1)
    nq = NSA_HEADS * SEL_TQ
    qa_scr[0:HEAD_DIM, :] = _stack_heads(q_ref, NSA_HEADS)
    qa_scr[HEAD_DIM:LANES, :] = jnp.zeros((LANES - HEAD_DIM, nq), BF16)
    mm = mm_ref[...]
    for h in range(NSA_HEADS):
        mm_scr[:, h * SEL_TQ:(h + 1) * SEL_TQ] = mm

    gkeys = SEL_GROUP * LANES

    def load_query(gi):
        rg = (gi * SEL_GROUP) // 8
        qa_scr[HEAD_DIM:HEAD_DIM + 16, :] = mm_scr[pl.ds(pl.multiple_of(rg * 16, 16), 16), :]
        return qa_scr[...]

    def chunk_scores(gi, u, qa):
        row = pl.multiple_of((gi * SEL_GROUP + u) * LANES, LANES)
        return jnp.dot(ks_ref[pl.ds(row, LANES), :], qa, preferred_element_type=F32)

    def weighted_values(gi, ps):
        vcat = jnp.concatenate([v_ref[gi * SEL_GROUP + u] for u in range(SEL_GROUP)], axis=1)
        return jnp.dot(vcat, jnp.concatenate(ps, axis=0), preferred_element_type=F32)

    def body(gi, carry):
        m, acc, mg = carry
        m_new = jnp.maximum(m, mg)
        qa = load_query(gi + 1)
        ps, mx = [], None
        for u in range(SEL_GROUP):
            rows = slice(u * LANES, (u + 1) * LANES)
            ps.append(jnp.exp(s_scr[rows, :] - m_new).astype(BF16))
            nxt = chunk_scores(gi + 1, u, qa)
            s_scr[rows, :] = nxt
            mx = nxt if mx is None else jnp.maximum(mx, nxt)
        acc = acc * jnp.exp(m - m_new) + weighted_values(gi, ps)
        return m_new, acc, jnp.max(mx, axis=0, keepdims=True)

    qa = load_query(0)
    mx = None
    for u in range(SEL_GROUP):
        s0 = chunk_scores(0, u, qa)
        s_scr[u * LANES:(u + 1) * LANES, :] = s0
        mx = s0 if mx is None else jnp.maximum(mx, s0)
    c0 = i * (SEL_TQ // LANES)
    last = c0 // SEL_GROUP
    m, acc, _ = lax.fori_loop(0, last, body, (jnp.full((1, nq), NEG, F32), jnp.zeros((VROWS, nq), F32),
                                              jnp.max(mx, axis=0, keepdims=True)))
    drow = pl.multiple_of((c0 % SEL_GROUP) * LANES, SEL_TQ)
    s_scr[pl.ds(drow, SEL_TQ), :] += causal_ref[...]
    s = s_scr[...]
    m_new = jnp.maximum(m, jnp.max(s, axis=0, keepdims=True))
    acc = acc * jnp.exp(m - m_new) + weighted_values(last, [jnp.exp(s - m_new).astype(BF16)])
    o = acc[0:HEAD_DIM] * (1.0 / acc[HEAD_DIM:HEAD_DIM + 1])

    g = gate_ref[...]
    for h in range(NSA_HEADS):
        rs = slice(HEAD_DIM * h, HEAD_DIM * (h + 1))
        out = (g[3 * h:3 * h + 1] * ocmp_ref[rs, :] + g[3 * h + 1:3 * h + 2] * o[:, h * SEL_TQ:(h + 1) * SEL_TQ]
               + g[3 * h + 2:3 * h + 3] * owin_ref[rs, :])
        out_ref[rs, :] = out.astype(BF16)


def _sel_attend(qT, mm, nk, v4, causal4, ocmp, owin, gates):
    B, _, S = qT.shape
    nstep = S // SEL_TQ
    n_sel = S // SEL_BLOCK
    nq = NSA_HEADS * SEL_TQ
    ar = NSA_HEADS * HEAD_DIM
    return pl.pallas_call(
        _sel_kernel,
        grid=(B, nstep),
        in_specs=[pl.BlockSpec((None, ar, SEL_TQ), lambda b, i: (b, A_Q_BLK, i)),
                  pl.BlockSpec((None, n_sel, SEL_TQ), lambda b, i: (b, 0, i)),
                  pl.BlockSpec((None, S, LANES), lambda b, i: (b, 0, 0)),
                  pl.BlockSpec((None, S // LANES, VROWS, LANES), lambda b, i: (b, 0, 0, 0)),
                  _const_spec((SEL_TQ, nq), lambda b, i: (0, 0)),
                  pl.BlockSpec((None, ar, SEL_TQ), lambda b, i: (b, 0, i)),
                  pl.BlockSpec((None, ar, SEL_TQ), lambda b, i: (b, 0, i)),
                  pl.BlockSpec((None, G_ROWS, SEL_TQ), lambda b, i: (b, 0, i))],
        out_specs=pl.BlockSpec((None, ar, SEL_TQ), lambda b, i: (b, 0, i)),
        out_shape=jax.ShapeDtypeStruct((B, ar, S), BF16),
        scratch_shapes=[pltpu.VMEM((LANES, nq), BF16), pltpu.VMEM((n_sel, nq), BF16),
                        pltpu.VMEM((SEL_GROUP * LANES, nq), F32)],
        compiler_params=_cp(("parallel", "parallel")),
        name="nsa_sel",
    )(qT, mm, nk, v4, causal4, ocmp, owin, gates)


def _banded_kernel(*refs, hkv, grp, nprev, zero_bias, has_sink, want_lse):
    q_ref, k_ref, v_ref, bias_ref = refs[:4]
    pos = 4
    sink_ref = None
    if has_sink:
        sink_ref = refs[pos]
        pos += 1
    o_ref = refs[pos]
    lse_ref = refs[pos + 1] if want_lse else None
    j = pl.program_id(1)

    def scores(s, g, first):
        lanes = slice(s * LANES, (s + 1) * LANES)
        qg = jnp.concatenate([q_ref[HEAD_DIM * (g * grp + u):HEAD_DIM * (g * grp + u + 1), lanes]
                              for u in range(grp)], axis=1)
        parts = []
        for ci in range(nprev + 1):
            if first:
                kc = s - nprev + ci
                if kc < 0:
                    continue
                row = kc * LANES
            else:
                kc = j * BAND_NSUB + s - nprev + ci
                row = pl.multiple_of(kc * LANES, LANES)
            sc = jnp.dot(k_ref[pl.ds(row, LANES), HEAD_DIM * g:HEAD_DIM * (g + 1)], qg,
                         preferred_element_type=F32)
            if not zero_bias[ci]:
                sc = sc + bias_ref[ci]
            parts.append((kc, sc))
        return parts

    def finish(s, g, parts):
        lanes = slice(s * LANES, (s + 1) * LANES)
        m = None
        for _, sc in parts:
            mc = jnp.max(sc, axis=0, keepdims=True)
            m = mc if m is None else jnp.maximum(m, mc)
        if has_sink:
            sk = sink_ref[g, 0:1, :]
            m = jnp.maximum(m, sk)
        acc = None
        for kcc, sc in parts:
            t = jnp.dot(v_ref[kcc, VROWS * g:VROWS * (g + 1), :], jnp.exp(sc - m).astype(BF16),
                        preferred_element_type=F32)
            acc = t if acc is None else acc + t
        l = acc[HEAD_DIM:HEAD_DIM + 1]
        if has_sink:
            l = l + jnp.exp(sk - m)
        o = acc[0:HEAD_DIM] * (1.0 / l)
        for u in range(grp):
            hq = g * grp + u
            o_ref[HEAD_DIM * hq:HEAD_DIM * (hq + 1), lanes] = o[:, u * LANES:(u + 1) * LANES].astype(o_ref.dtype)
        if want_lse:
            lse_ref[8 * g:8 * (g + 1), lanes] = jnp.broadcast_to(m + jnp.log(l), (8, LANES))

    def run(first):
        pending = None
        for s in range(BAND_NSUB):
            for g in range(hkv):
                parts = scores(s, g, first)
                if pending is not None:
                    finish(*pending)
                pending = (s, g, parts)
        finish(*pending)

    pl.when(j == 0)(lambda: run(True))
    pl.when(j > 0)(lambda: run(False))


def _banded(qT, q_blk, nk, k_blk, v4, v_blk, bias_and_zero, *, hkv, grp, nprev, out_dtype, sinks=None,
            want_lse=False, name):
    bias, zero_bias = bias_and_zero
    assert nprev <= BAND_NSUB
    B, _, S = qT.shape
    nt = S // BAND_TILE
    qrows = hkv * grp * HEAD_DIM
    in_specs = [pl.BlockSpec((None, qrows, BAND_TILE), lambda b, j: (b, q_blk, j)),
                pl.BlockSpec((None, S, LANES), lambda b, j: (b, 0, k_blk)),
                pl.BlockSpec((None, S // LANES, hkv * VROWS, LANES), lambda b, j: (b, 0, v_blk, 0)),
                pl.BlockSpec(bias.shape, lambda b, j: (0, 0, 0))]
    args = [qT, nk, v4, bias]
    if sinks is not None:
        in_specs.append(pl.BlockSpec(sinks.shape, lambda b, j: (0, 0, 0)))
        args.append(sinks)
    out_specs = [pl.BlockSpec((None, qrows, BAND_TILE), lambda b, j: (b, 0, j))]
    out_shape = [jax.ShapeDtypeStruct((B, qrows, S), out_dtype)]
    if want_lse:
        out_specs.append(pl.BlockSpec((None, 8 * hkv, BAND_TILE), lambda b, j: (b, 0, j)))
        out_shape.append(jax.ShapeDtypeStruct((B, 8 * hkv, S), F32))
    return pl.pallas_call(
        functools.partial(_banded_kernel, hkv=hkv, grp=grp, nprev=nprev, zero_bias=zero_bias,
                          has_sink=sinks is not None, want_lse=want_lse),
        grid=(B, nt),
        in_specs=in_specs,
        out_specs=out_specs,
        out_shape=out_shape,
        compiler_params=_cp(("parallel", "parallel")),
        name=name,
    )(*args)


def _folded_kernel(q_ref, k_ref, v_ref, bias_ref, o_ref, lse_ref):
    j = pl.program_id(2)
    nsub = q_ref.shape[0] // LANES
    ones = jnp.ones((VROWS - HEAD_DIM, LANES), BF16)
    row = lax.broadcasted_iota(jnp.int32, (LANES, LANES), 0)

    def scores(s):
        qt = q_ref[s * LANES:(s + 1) * LANES, :].astype(F32).T.astype(BF16)
        parts = []
        for ci in range(2):
            kc = j * nsub + s - 1 + ci
            kcc = jnp.maximum(kc, 0)
            rows = pl.ds(pl.multiple_of(kcc * LANES, LANES), LANES)
            bias = jnp.where(kc >= 0, bias_ref[ci], NEG)
            vt = v_ref[rows, :].astype(F32).T.astype(BF16)
            for g in range(2):
                hs = slice(HEAD_DIM * g, HEAD_DIM * (g + 1))
                sc = jnp.dot(k_ref[rows, hs], qt[hs], preferred_element_type=F32) + bias
                parts.append((g, jnp.concatenate([vt[hs], ones], axis=0), sc))
        return parts

    def finish(s, parts):
        outs, lses = [], []
        for g in range(2):
            mine = [(v, sc) for gg, v, sc in parts if gg == g]
            m = functools.reduce(jnp.maximum, [jnp.max(sc, axis=0, keepdims=True) for _, sc in mine])
            acc = None
            for v, sc in mine:
                t = jnp.dot(v, jnp.exp(sc - m).astype(BF16), preferred_element_type=F32)
                acc = t if acc is None else acc + t
            l = acc[HEAD_DIM:HEAD_DIM + 1]
            outs.append(acc[0:HEAD_DIM] * (1.0 / l))
            lses.append(m + jnp.log(l))
        rows = slice(s * LANES, (s + 1) * LANES)
        o_ref[rows, :] = jnp.concatenate(outs, axis=0).T
        lse_ref[rows, :] = jnp.where(row == 0, lses[0], jnp.where(row == 1, lses[1], 0.0)).T

    pending = None
    for s in range(nsub):
        parts = scores(s)
        if pending is not None:
            finish(*pending)
        pending = (s, parts)
    finish(*pending)


def _folded_dilated(fq, fk, fv, bias, dil):
    B, n, _ = fq.shape
    tq = min(n, BAND_TILE)
    return pl.pallas_call(
        _folded_kernel,
        grid=(B, dil, n // tq),
        in_specs=[pl.BlockSpec((None, tq, LANES), lambda b, r, j: (b, j, r)),
                  pl.BlockSpec((None, n, LANES), lambda b, r, j: (b, 0, r)),
                  pl.BlockSpec((None, n, LANES), lambda b, r, j: (b, 0, r)),
                  pl.BlockSpec(bias.shape, lambda b, r, j: (0, 0, 0))],
        out_specs=[pl.BlockSpec((None, tq, LANES), lambda b, r, j: (b, j, r)),
                   pl.BlockSpec((None, tq, LANES), lambda b, r, j: (b, j, r))],
        out_shape=[jax.ShapeDtypeStruct((B, n, dil * LANES), F32),
                   jax.ShapeDtypeStruct((B, n, dil * LANES), F32)],
        compiler_params=_cp(("parallel", "parallel", "parallel")),
        name="dil%d_folded" % dil,
    )(fq, fk, fv, bias)


def _post_norm_residual(y, x_ref, gw_ref, o_ref, token_major=False):
    ms = jnp.mean(y * y, axis=0, keepdims=True)
    yn = y * lax.rsqrt(ms + RMS_EPS)
    gw = gw_ref[...]
    for c in range(NSUB):
        sl = slice(c * LANES, (c + 1) * LANES)
        out = x_ref[:, sl] + gw * yn[:, sl]
        if token_major:
            o_ref[sl, :] = out.T
        else:
            o_ref[:, sl] = out


def _mix_ffn_kernel(a_ref, b0_ref, b1_ref, b2_ref, l0_ref, l1_ref, l2_ref, c_ref, wo_ref, x_ref, gw1_ref,
                    a2_ref, sh2_ref, wg_ref, wu_ref, wd_ref, cv_ref, gw2_ref, o_ref,
                    mix_scr, x1_scr, h_scr, carry_scr, act_scr, unf_scr, *, fchunk, token_major_out):
    s = pl.program_id(1)
    cur = s % 2
    prv = 1 - cur
    d_ff = wg_ref.shape[0]

    @pl.when(s == 0)
    def _():
        carry_scr[...] = jnp.zeros(carry_scr.shape, F32)
        h_scr[1] = jnp.zeros(h_scr.shape[1:], BF16)
        x1_scr[1] = jnp.zeros(x1_scr.shape[1:], F32)

    h = h_scr[prv]
    lane = lax.broadcasted_iota(jnp.int32, (fchunk, LANES), 1)
    for c in range(d_ff // fchunk):
        rs = slice(c * fchunk, (c + 1) * fchunk)
        g = jnp.dot(wg_ref[rs, :], h, preferred_element_type=F32)
        prev = carry_scr[rs, :]
        carry_scr[rs, :] = g[:, TILE - LANES:TILE]
        g1 = pltpu.roll(g, 1, 1)
        g2 = pltpu.roll(g, 2, 1)
        fix1 = jnp.where(lane < 1, pltpu.roll(prev, 1, 1), g1[:, 0:LANES])
        fix2 = jnp.where(lane < 2, pltpu.roll(prev, 2, 1), g2[:, 0:LANES])
        g1 = jnp.concatenate([fix1, g1[:, LANES:]], axis=1)
        g2 = jnp.concatenate([fix2, g2[:, LANES:]], axis=1)
        w0 = jnp.concatenate([cv_ref[0, rs, :]] * NSUB, axis=1)
        w1 = jnp.concatenate([cv_ref[1, rs, :]] * NSUB, axis=1)
        w2 = jnp.concatenate([cv_ref[2, rs, :]] * NSUB, axis=1)
        cb = jnp.concatenate([cv_ref[3, rs, :]] * NSUB, axis=1)
        acc = g2 * w0 + g1 * w1 + g * w2 + cb
        up = jnp.dot(wu_ref[rs, :], h, preferred_element_type=F32)
        act_scr[rs, :] = (jax.nn.gelu(acc, approximate=True) * up).astype(BF16)

    def unfold(src_ref, fold):
        for res in range(fold):
            unf_scr[pl.ds(res, TILE // fold, stride=fold), :] = src_ref[:, res * LANES:(res + 1) * LANES]
        return unf_scr[...].T

    d1, d2 = DIL_PATTERNS[1][1], DIL_PATTERNS[2][1]
    b_vals = (b0_ref[...], unfold(b1_ref, d1), unfold(b2_ref, d2))
    l1t, l2t = unfold(l1_ref, d1), unfold(l2_ref, d2)
    for hh in range(2):
        ls = [l0_ref[8 * hh:8 * hh + 1, :], l1t[hh:hh + 1, :], l2t[hh:hh + 1, :]]
        mx = jnp.maximum(jnp.maximum(ls[0], ls[1]), ls[2])
        es = [jnp.exp(v - mx) for v in ls]
        inv = 1.0 / (es[0] + es[1] + es[2])
        for g in range(3):
            rs = slice(HEAD_DIM * hh, HEAD_DIM * (hh + 1))
            mix_scr[128 * g + HEAD_DIM * hh:128 * g + HEAD_DIM * (hh + 1), :] = (
                b_vals[g][rs, :] * (es[g] * inv)).astype(BF16)
    y1 = jnp.dot(wo_ref[:, 0:256], a_ref[...], preferred_element_type=F32)
    y1 = y1 + jnp.dot(wo_ref[:, 256:640], mix_scr[...], preferred_element_type=F32)
    y1 = y1 + jnp.dot(wo_ref[:, 640:1024], c_ref[...], preferred_element_type=F32)

    y2 = jnp.dot(wd_ref[...], act_scr[...], preferred_element_type=F32)

    _post_norm_residual(y1, x_ref, gw1_ref, x1_scr.at[cur])
    _norm_mod_to_scratch(x1_scr.at[cur], a2_ref, sh2_ref, h_scr.at[cur])
    _post_norm_residual(y2, x1_scr.at[prv], gw2_ref, o_ref, token_major_out)


def _mix_ffn(aT, bs, lses, cT, w_outT, xT, gw1, a2, sh2, wgT, wuT, wdT, cv, gw2, layer, token_major_out):
    B, D, S = xT.shape
    nt = S // TILE
    d_ff = wgT.shape[1]
    tile = lambda rows: pl.BlockSpec((None, rows, TILE), lambda b, s: (b, 0, jnp.minimum(s, nt - 1)))
    tok = lambda d: pl.BlockSpec((None, TILE // d, d * LANES), lambda b, s: (b, jnp.minimum(s, nt - 1), 0))
    t1, t2 = tok(DIL_PATTERNS[1][1]), tok(DIL_PATTERNS[2][1])
    vec = pl.BlockSpec((None, D, LANES), lambda b, s: (b, 0, 0))
    return pl.pallas_call(
        functools.partial(_mix_ffn_kernel, fchunk=256, token_major_out=token_major_out),
        grid=(B, nt + 1),
        in_specs=[tile(256), tile(128), t1, t2, tile(16), t1, t2, tile(384),
                  _const_spec((None, D, D), lambda b, s: (layer, 0, 0)),
                  tile(D), vec, vec, vec,
                  _const_spec((None, d_ff, D), lambda b, s: (layer, 0, 0)),
                  _const_spec((None, d_ff, D), lambda b, s: (layer, 0, 0)),
                  _const_spec((None, D, d_ff), lambda b, s: (layer, 0, 0)),
                  _const_spec((None, 4, d_ff, LANES), lambda b, s: (layer, 0, 0, 0)),
                  vec],
        out_specs=(pl.BlockSpec((None, TILE, D), lambda b, s: (b, jnp.maximum(s - 1, 0), 0)) if token_major_out
                   else pl.BlockSpec((None, D, TILE), lambda b, s: (b, 0, jnp.maximum(s - 1, 0)))),
        out_shape=jax.ShapeDtypeStruct((B, S, D) if token_major_out else (B, D, S), F32),
        scratch_shapes=[pltpu.VMEM((384, TILE), BF16), pltpu.VMEM((2, D, TILE), F32),
                        pltpu.VMEM((2, D, TILE), BF16), pltpu.VMEM((d_ff, LANES), F32),
                        pltpu.VMEM((d_ff, TILE), BF16), pltpu.VMEM((TILE, LANES), F32)],
        compiler_params=_cp(("arbitrary", "arbitrary")),
        name="mix_ffn",
    )(aT, bs[0], bs[1], bs[2], lses[0], lses[1], lses[2], cT, w_outT, xT, gw1,
      a2, sh2, wgT, wuT, wdT, cv, gw2)


def _band_bias(old_edge, reps, nprev):
    kk = np.arange(LANES)[:, None]
    qq = np.arange(LANES)[None, :]
    tabs = [kk - qq >= old_edge] + [np.ones((LANES, LANES), bool)] * (nprev - 1) + [kk <= qq]
    zero = tuple(bool(t.all()) for t in tabs)
    out = np.stack([np.where(t, 0.0, NEG) for t in tabs]).astype(np.float32)
    return jnp.asarray(np.tile(out, (1, 1, reps))), zero


def _cmp_bias():
    nn = np.arange(LANES)[:, None]
    qq = np.arange(LANES)[None, :]
    tabs = [np.where(CMP_STRIDE * nn + CMP_BLOCK - 1 <= LANES * r + qq, 0.0, NEG) for r in range(16)]
    return jnp.asarray(np.stack(tabs).astype(np.float32))


def _overlap_rows():
    jj = np.arange(40)[:, None]
    nn = np.arange(LANES)[None, :]
    return jnp.asarray(((nn >= 4 * jj - 1) & (nn <= 4 * jj + 3)).astype(np.float32), dtype=BF16)


_IN_COL_ORDER = ((1804, 2188), (652, 1036), (0, 256),
                 (256, 320), (320, 384), (384, 448), (512, 576), (1036, 1420), (2188, 2316),
                 (448, 512), (576, 640), (1420, 1804), (2316, 2444),
                 (640, 652))


def _prep_w_in(w_in):
    wt = jnp.swapaxes(jnp.concatenate([w_in[:, :, a:b] for a, b in _IN_COL_ORDER], axis=2), 1, 2)
    scale = np.ones((wt.shape[1], 1), np.float32)
    scale[:Q_ROWS] = HEAD_DIM ** -0.5
    wt = wt * scale
    wt = jnp.pad(wt, ((0, 0), (0, W_ROWS - wt.shape[1]), (0, 0)))
    return wt.astype(BF16)


def _prep_compress(w_ck, w_cv, pe_k, pe_v):
    L = w_ck.shape[0]
    half = CMP_BLOCK // 2

    def big(lo):
        wk = w_ck[:, lo:lo + half]
        wv = w_cv[:, lo:lo + half]
        z = jnp.zeros_like(wk)
        top = jnp.concatenate([wk, z], axis=-1)
        bot = jnp.concatenate([z, wv], axis=-1)
        return jnp.concatenate([top, bot], axis=2).reshape(L, half * LANES, LANES)

    wbig = jnp.concatenate([big(0), big(half)], axis=-1).astype(BF16)
    pe = jnp.concatenate([pe_k, pe_v], axis=-1)
    pe2 = jnp.zeros((L, 16, half * LANES), F32)
    pe2 = pe2.at[:, 0].set(pe[:, :half].reshape(L, -1)).at[:, 8].set(pe[:, half:].reshape(L, -1))
    return wbig, pe2.astype(BF16)


def _lane_bcast(v):
    return jnp.broadcast_to(v[..., None], v.shape + (LANES,))


def kernel(x, c, positions, w_in, w_out, w_ada, b_ada, norm_w, cmp_w_k, cmp_w_v, cmp_pe_k, cmp_pe_v,
           sinks, w_gate, w_up, conv_w, conv_b, w_down):
    B, S, D = x.shape
    depth = w_in.shape[0]
    assert S % 2048 == 0 and D == 1024 and w_in.shape[2] == 2444

    inv = ROPE_THETA ** (-jnp.arange(0, HEAD_DIM, 2, dtype=F32) / HEAD_DIM)
    ang = positions.astype(F32)[:, None, :] * inv[None, :, None]
    cosT, sinT = jnp.cos(ang), jnp.sin(ang)

    c8 = jnp.pad(c, ((0, 8 - B), (0, 0)))
    ada = _adaln(c8, w_ada, b_ada)[:, :B]
    sh1, sc1, g1, sh2, sc2, g2 = [ada[:, :, k * D:(k + 1) * D] for k in range(6)]
    nw = norm_w[:, :, None, :]
    a1 = _lane_bcast(nw[:, 0] * (1 + sc1))
    gw1 = _lane_bcast(g1 * nw[:, 1])
    a2 = _lane_bcast(nw[:, 2] * (1 + sc2))
    gw2 = _lane_bcast(g2 * nw[:, 3])
    sh1b, sh2b = _lane_bcast(sh1), _lane_bcast(sh2)

    w_inT = _prep_w_in(w_in)
    w_outT = jnp.swapaxes(w_out, 1, 2).astype(BF16)
    wgT = jnp.swapaxes(w_gate, 1, 2).astype(BF16)
    wuT = jnp.swapaxes(w_up, 1, 2).astype(BF16)
    wdT = jnp.swapaxes(w_down, 1, 2).astype(BF16)
    cv = _lane_bcast(jnp.concatenate([conv_w, conv_b[:, None, :]], axis=1))
    wbig, pe2 = _prep_compress(cmp_w_k, cmp_w_v, cmp_pe_k, cmp_pe_v)
    sink_tab = jnp.broadcast_to(
        jnp.repeat(sinks.reshape(depth, SWA_KV_HEADS, SWA_HEADS // SWA_KV_HEADS), LANES, axis=-1)[:, :, None, :],
        (depth, SWA_KV_HEADS, 8, LANES * (SWA_HEADS // SWA_KV_HEADS)))

    ov = _overlap_rows()
    cmp_bias = _cmp_bias()
    kk = np.arange(SEL_TQ)[:, None]
    causal4 = jnp.asarray(np.tile(np.where(kk <= kk.T, 0.0, NEG).astype(np.float32), (1, NSA_HEADS)))
    bias_win = _band_bias(1, NSA_HEADS, NSA_WINDOW // LANES)
    bias_swa = _band_bias(1, SWA_HEADS // SWA_KV_HEADS, SWA_WINDOW // LANES)
    bias_dil = _band_bias(0, 1, 1)
    assert all(w // d == LANES for w, d in DIL_PATTERNS)

    xT = x
    for l in range(depth):
        first, last_layer = l == 0, l == depth - 1
        outs = _inproj(xT, a1[l], sh1b[l], w_inT, l, cosT, sinT, token_major=first)
        qT, kcvc, nk, v4, gates = outs[:5]
        folded = {g: outs[5 + 3 * k:8 + 3 * k] for k, (g, _) in enumerate(FOLDED)}
        if first:
            xT = outs[5 + 3 * len(FOLDED)]
        kc4, vc4 = _compress(kcvc, wbig, pe2, l)
        ocmp, mm = _cmp_topk(qT, kc4, vc4, ov, cmp_bias)
        owin, = _banded(qT, A_Q_BLK, nk, 1, v4, 1, bias_win, hkv=1, grp=NSA_HEADS, nprev=NSA_WINDOW // LANES,
                        out_dtype=F32, name="nsa_win")
        aT = _sel_attend(qT, mm, nk, v4, causal4, ocmp, owin, gates)
        o, lse = _banded(qT, 3, nk, 2, v4, 1, bias_dil, hkv=2, grp=1, nprev=1, out_dtype=F32, want_lse=True,
                         name="dil1")
        bs, lses = [o], [lse]
        for g, _ in FOLDED:
            o, lse = _folded_dilated(*folded[g], bias_dil[0], DIL_PATTERNS[g][1])
            bs.append(o)
            lses.append(lse)
        cT, = _banded(qT, 0, nk, 5, v4, 4, bias_swa, hkv=SWA_KV_HEADS, grp=SWA_HEADS // SWA_KV_HEADS,
                      nprev=SWA_WINDOW // LANES, out_dtype=BF16,
                      sinks=sink_tab[l], name="swa")
        xT = _mix_ffn(aT, bs, lses, cT, w_outT, xT, gw1[l], a2[l], sh2b[l], wgT, wuT, wdT, cv, gw2[l], l,
                      token_major_out=last_layer)
    return xT
```

```python
import functools

import numpy as np
import jax
import jax.numpy as jnp
from jax import lax
from jax.experimental import pallas as pl
from jax.experimental.pallas import tpu as pltpu

F32 = jnp.float32
BF16 = jnp.bfloat16

HEAD_DIM = 64
HALF = HEAD_DIM // 2
NSA_HEADS = 4
CMP_BLOCK = 32
CMP_STRIDE = 16
SEL_BLOCK = 64
SEL_TOPK = 16
NSA_WINDOW = 512
DIL_PATTERNS = ((128, 1), (512, 4), (2048, 16))
SWA_HEADS = 6
SWA_KV_HEADS = 2
SWA_WINDOW = 128
ROPE_THETA = 10000.0
RMS_EPS = 1e-6
NEG = -1e30
FORCE = 1e4

LANES = 128
TILE = 512
NSUB = TILE // LANES
BAND_TILE = 2048
BAND_NSUB = BAND_TILE // LANES
VROWS = HEAD_DIM + 16
MEMBER_BIG = 2.0 ** 100
VMEM_LIMIT = 56 * 1024 * 1024

Q_ROWS = 1024
A_Q_BLK = 3
NK_ROWS = 768
NK_LANES = 768
V_ROWS = 640
G_ROWS = 16
W_ROWS = Q_ROWS + NK_ROWS + V_ROWS + G_ROWS
N_VPIECES = V_ROWS // HEAD_DIM
FOLDED = ((1, slice(128, 256)), (2, slice(256, 384)))
FOLD = CMP_STRIDE
assert DIL_PATTERNS[2][1] == FOLD


def _cp(sem):
    return pltpu.CompilerParams(dimension_semantics=sem, vmem_limit_bytes=VMEM_LIMIT)


def _const_spec(shape, index_map):
    return pl.BlockSpec(shape, index_map, pipeline_mode=pl.Buffered(1))


def _adaln_kernel(c_ref, w_ref, b_ref, o_ref):
    c = c_ref[...]
    cond = c * jax.nn.sigmoid(c)
    o_ref[...] = jnp.dot(cond, w_ref[...], preferred_element_type=F32,
                         precision=lax.Precision.HIGHEST) + b_ref[...]


def _adaln(c8, w_ada, b_ada):
    depth, d, six_d = w_ada.shape
    nblk = six_d // d
    return pl.pallas_call(
        _adaln_kernel,
        grid=(depth, nblk),
        in_specs=[pl.BlockSpec((8, d), lambda l, n: (0, 0)),
                  pl.BlockSpec((None, d, d), lambda l, n: (l, 0, n)),
                  pl.BlockSpec((None, 1, d), lambda l, n: (l, 0, n))],
        out_specs=pl.BlockSpec((None, 8, d), lambda l, n: (l, 0, n)),
        out_shape=jax.ShapeDtypeStruct((depth, 8, six_d), F32),
        compiler_params=_cp(("parallel", "parallel")),
        name="adaln",
    )(c8, w_ada, b_ada.reshape(depth, 1, six_d))


def _norm_mod_to_scratch(x_ref, a_ref, sh_ref, h_scr):
    for c in range(NSUB):
        sl = slice(c * LANES, (c + 1) * LANES)
        xs = x_ref[:, sl]
        ms = jnp.mean(xs * xs, axis=0, keepdims=True)
        h_scr[:, sl] = ((xs * lax.rsqrt(ms + RMS_EPS)) * a_ref[...] + sh_ref[...]).astype(BF16)


def _inproj_kernel(x_ref, a_ref, sh_ref, w_ref, cos_ref, sin_ref,
                   q_ref, kcvc_ref, nk_ref, v4_ref, gate_ref, fq1_ref, fk1_ref, fv1_ref, fq2_ref, fk2_ref, fv2_ref,
                   *rest, token_major):
    j = pl.program_id(1)
    if token_major:
        xt_ref, h_scr, fold_scr = rest
        for c in range(NSUB):
            xt_ref[:, c * LANES:(c + 1) * LANES] = x_ref[c * LANES:(c + 1) * LANES, :].T
        x_ref = xt_ref
    else:
        h_scr, fold_scr = rest
    _norm_mod_to_scratch(x_ref, a_ref, sh_ref, h_scr)
    h = h_scr[...]
    cos = cos_ref[...]
    sin = sin_ref[...]

    def proj(r0, r1):
        return jnp.dot(w_ref[r0:r1, :], h, preferred_element_type=F32)

    def fold_store(t, dst_ref, fold=FOLD):
        fold_scr[...] = t
        for res in range(fold):
            dst_ref[:, res * LANES:(res + 1) * LANES] = fold_scr[pl.ds(res, TILE // fold, stride=fold), :].astype(BF16)

    folded_refs = {1: (fq1_ref, fk1_ref, fv1_ref), 2: (fq2_ref, fk2_ref, fv2_ref)}

    def rope(r, nh):
        outs = []
        for hh in range(nh):
            t1 = r[HEAD_DIM * hh:HEAD_DIM * hh + HALF]
            t2 = r[HEAD_DIM * hh + HALF:HEAD_DIM * (hh + 1)]
            outs.append(t1 * cos - t2 * sin)
            outs.append(t2 * cos + t1 * sin)
        return jnp.concatenate(outs, axis=0)

    for r0, r1 in ((0, 384), (384, 768), (768, 1024)):
        r = rope(proj(r0, r1), (r1 - r0) // HEAD_DIM)
        q_ref[r0:r1, :] = r.astype(BF16)
        if r0 == 384:
            for grp_i, rows in FOLDED:
                fold_store(r[rows].T, folded_refs[grp_i][0], DIL_PATTERNS[grp_i][1])

    base = Q_ROWS
    r = proj(base, base + 128)
    kcvc = jnp.concatenate([rope(r[0:64], 1), r[64:128]], axis=0)
    fold_store(kcvc.T, kcvc_ref)

    r = proj(base + 128, base + 192)
    tok = j * TILE + lax.broadcasted_iota(jnp.int32, (HEAD_DIM, TILE), 1)
    row = lax.broadcasted_iota(jnp.int32, (HEAD_DIM, TILE), 0)
    member_cols = jnp.where(row == ((tok >> 6) & 15), MEMBER_BIG, 0.0).astype(F32)
    nk_ref[:, 0:128] = jnp.concatenate([rope(r, 1), member_cols], axis=0).T.astype(BF16)

    r = proj(base + 192, base + 256)
    nk_ref[:, 128:256] = jnp.concatenate([rope(r, 1), jnp.zeros((HEAD_DIM, TILE), F32)], axis=0).T.astype(BF16)

    r = proj(base + 256, base + 640)
    bk = rope(r, 6).T
    nk_ref[:, 256:640] = bk.astype(BF16)
    for grp_i, rows in FOLDED:
        fold_store(bk[:, rows], folded_refs[grp_i][1], DIL_PATTERNS[grp_i][1])

    r = proj(base + 640, base + 768)
    nk_ref[:, 640:768] = rope(r, 2).T.astype(BF16)

    base = Q_ROWS + NK_ROWS
    r = proj(base, base + V_ROWS)
    for grp_i, rows in FOLDED:
        fold_store(r[128 + rows.start:128 + rows.stop].T, folded_refs[grp_i][2], DIL_PATTERNS[grp_i][1])
    r = r.astype(BF16)
    ones = jnp.ones((VROWS - HEAD_DIM, LANES), BF16)
    for c in range(NSUB):
        for p in range(N_VPIECES):
            v4_ref[c, VROWS * p:VROWS * p + HEAD_DIM, :] = r[HEAD_DIM * p:HEAD_DIM * (p + 1),
                                                             c * LANES:(c + 1) * LANES]
            v4_ref[c, VROWS * p + HEAD_DIM:VROWS * (p + 1), :] = ones

    base = Q_ROWS + NK_ROWS + V_ROWS
    gate_ref[...] = jax.nn.sigmoid(proj(base, base + G_ROWS))


def _inproj(x, a1, sh1, w_inT, layer, cosT, sinT, token_major):
    if token_major:
        B, S, D = x.shape
        x_spec = pl.BlockSpec((None, TILE, D), lambda b, j: (b, j, 0))
    else:
        B, D, S = x.shape
        x_spec = pl.BlockSpec((None, D, TILE), lambda b, j: (b, 0, j))
    nt = S // TILE
    fold_specs = {d: pl.BlockSpec((None, TILE // d, d * LANES), lambda b, j: (b, j, 0)) for _, d in DIL_PATTERNS}
    fold_spec = fold_specs[FOLD]
    out_specs = [pl.BlockSpec((None, Q_ROWS, TILE), lambda b, j: (b, 0, j)),
                 fold_spec,
                 pl.BlockSpec((None, TILE, NK_LANES), lambda b, j: (b, j, 0)),
                 pl.BlockSpec((None, NSUB, N_VPIECES * VROWS, LANES), lambda b, j: (b, j, 0, 0)),
                 pl.BlockSpec((None, G_ROWS, TILE), lambda b, j: (b, 0, j)),
                 ] + [fold_specs[DIL_PATTERNS[g][1]] for g, _ in FOLDED for _ in range(3)]
    out_shape = [jax.ShapeDtypeStruct((B, Q_ROWS, S), BF16),
                 jax.ShapeDtypeStruct((B, S // FOLD, FOLD * LANES), BF16),
                 jax.ShapeDtypeStruct((B, S, NK_LANES), BF16),
                 jax.ShapeDtypeStruct((B, S // LANES, N_VPIECES * VROWS, LANES), BF16),
                 jax.ShapeDtypeStruct((B, G_ROWS, S), F32)] + [
                     jax.ShapeDtypeStruct((B, S // DIL_PATTERNS[g][1], DIL_PATTERNS[g][1] * LANES), BF16)
                     for g, _ in FOLDED for _ in range(3)]
    if token_major:
        out_specs.append(pl.BlockSpec((None, D, TILE), lambda b, j: (b, 0, j)))
        out_shape.append(jax.ShapeDtypeStruct((B, D, S), F32))
    return pl.pallas_call(
        functools.partial(_inproj_kernel, token_major=token_major),
        grid=(B, nt),
        in_specs=[x_spec,
                  pl.BlockSpec((None, D, LANES), lambda b, j: (b, 0, 0)),
                  pl.BlockSpec((None, D, LANES), lambda b, j: (b, 0, 0)),
                  _const_spec((None, W_ROWS, D), lambda b, j: (layer, 0, 0)),
                  pl.BlockSpec((None, HALF, TILE), lambda b, j: (b, 0, j)),
                  pl.BlockSpec((None, HALF, TILE), lambda b, j: (b, 0, j))],
        out_specs=out_specs,
        out_shape=out_shape,
        scratch_shapes=[pltpu.VMEM((D, TILE), BF16), pltpu.VMEM((TILE, LANES), F32)],
        compiler_params=_cp(("parallel", "parallel")),
        name="inproj",
    )(x, a1, sh1, w_inT, cosT, sinT)


def _compress_kernel(t_ref, w_ref, pe_ref, kc_ref, vc_ref):
    n = t_ref.shape[0]
    a = jnp.dot(t_ref[...], w_ref[...], preferred_element_type=F32)
    pc = jnp.dot(pe_ref[...], w_ref[...], preferred_element_type=F32)
    const = pc[0:1, 0:LANES] + pc[8:9, LANES:2 * LANES]
    cmp = a[:, 0:LANES] + pltpu.roll(a[:, LANES:2 * LANES], n - 1, 0) + const
    cmp_t = cmp.T
    ones = jnp.ones((VROWS - HEAD_DIM, LANES), BF16)
    for c in range(n // LANES):
        kc_ref[c] = cmp[c * LANES:(c + 1) * LANES].astype(BF16)
        vc_ref[c, 0:HEAD_DIM, :] = cmp_t[HEAD_DIM:2 * HEAD_DIM, c * LANES:(c + 1) * LANES].astype(BF16)
        vc_ref[c, HEAD_DIM:VROWS, :] = ones


def _compress(tview, wbig, pe2, layer):
    B, n, _ = tview.shape
    nch = n // LANES
    return pl.pallas_call(
        _compress_kernel,
        grid=(B,),
        in_specs=[pl.BlockSpec((None, n, CMP_STRIDE * LANES), lambda b: (b, 0, 0)),
                  pl.BlockSpec((None, CMP_STRIDE * LANES, 2 * LANES), lambda b: (layer, 0, 0)),
                  pl.BlockSpec((None, 16, CMP_STRIDE * LANES), lambda b: (layer, 0, 0))],
        out_specs=[pl.BlockSpec((None, nch, LANES, LANES), lambda b: (b, 0, 0, 0)),
                   pl.BlockSpec((None, nch, VROWS, LANES), lambda b: (b, 0, 0, 0))],
        out_shape=[jax.ShapeDtypeStruct((B, nch, LANES, LANES), BF16),
                   jax.ShapeDtypeStruct((B, nch, VROWS, LANES), BF16)],
        compiler_params=_cp(("parallel",)),
        name="compress",
    )(tview, wbig, pe2)


def _stack_heads(q_ref, nh, lane_slice=slice(None)):
    return jnp.concatenate([q_ref[HEAD_DIM * h:HEAD_DIM * (h + 1), lane_slice] for h in range(nh)], axis=1)


CMP_UNITS = 8
CMP_TQ = CMP_UNITS * LANES


def _cmp_kernel(q_ref, kc_ref, vc_ref, ov_ref, bias_ref, o_ref, mm_ref, imp_scr, *, n_sel, nstep):
    i = pl.program_id(1)
    nch_all = kc_ref.shape[0]
    parts = 4 if nch_all % 4 == 0 else (2 if nch_all % 2 == 0 else 1)
    for v in range(parts):
        pl.when(i // (nstep // parts) == v)(functools.partial(
            _cmp_body, q_ref, kc_ref, vc_ref, ov_ref, bias_ref, o_ref, mm_ref, imp_scr,
            nch=nch_all * (v + 1) // parts, nrows=n_sel * (v + 1) // parts, n_sel=n_sel))


def _cmp_body(q_ref, kc_ref, vc_ref, ov_ref, bias_ref, o_ref, mm_ref, imp_scr, *, nch, nrows, n_sel):
    i = pl.program_id(1)
    cd = (i * CMP_UNITS) // 16

    def scores(u):
        q = _stack_heads(q_ref, NSA_HEADS, slice(u * LANES, (u + 1) * LANES))
        edge = bias_ref[u]
        ss = []
        for c in range(nch):
            b = jnp.where(c < cd, 0.0, jnp.where(c == cd, edge, NEG))
            s = jnp.dot(kc_ref[c, :, 0:HEAD_DIM], q, preferred_element_type=F32)
            ss.append(s + jnp.concatenate([b] * NSA_HEADS, axis=1))
        return ss

    def finish(u, ss):
        lanes = slice(u * LANES, (u + 1) * LANES)
        m = jnp.max(functools.reduce(jnp.maximum, ss), axis=0, keepdims=True)
        valid = m > 0.5 * NEG
        imp_scr[u, 0:32 * nch + 8, :] = jnp.zeros((32 * nch + 8, imp_scr.shape[2]), F32)
        acc = None
        for c in range(nch):
            p = jnp.exp(ss[c] - m).astype(BF16)
            t = jnp.dot(vc_ref[c], p, preferred_element_type=F32)
            acc = t if acc is None else acc + t
            imp_scr[u, 32 * c:32 * c + 40, :] += jnp.dot(ov_ref[...], p, preferred_element_type=F32)
        inv = jnp.where(valid, 1.0 / acc[HEAD_DIM:HEAD_DIM + 1], 0.0)
        o = acc[0:HEAD_DIM] * inv
        for h in range(NSA_HEADS):
            o_ref[HEAD_DIM * h:HEAD_DIM * (h + 1), lanes] = o[:, h * LANES:(h + 1) * LANES]
        imp = jnp.zeros((nrows, LANES), F32)
        for h in range(NSA_HEADS):
            sl = slice(h * LANES, (h + 1) * LANES)
            imp = imp + imp_scr[u, 0:nrows, sl] * inv[:, sl]
        return imp

    pending = scores(0)
    imps = []
    for u in range(1, CMP_UNITS):
        nxt = scores(u)
        imps.append(finish(u - 1, pending))
        pending = nxt
    imps.append(finish(CMP_UNITS - 1, pending))

    blk = lax.broadcasted_iota(jnp.int32, (nrows, LANES), 0).astype(F32)

    def pick_one(imp):
        mx = jnp.max(imp, axis=0, keepdims=True)
        first = jnp.min(jnp.where(imp == mx, blk, float(nrows)), axis=0, keepdims=True)
        return jnp.where(blk == first, -jnp.inf, imp)

    curs, cands = [], []
    for u in range(CMP_UNITS):
        t = i * CMP_TQ + u * LANES + lax.broadcasted_iota(jnp.int32, (nrows, LANES), 1)
        cur = (t >> 6).astype(F32)
        forced = (blk == 0.0) | (blk == cur) | (blk == cur - 1.0)
        imp = jnp.where(forced, FORCE, imps[u])
        curs.append(cur)
        cands.append(jnp.where(blk <= cur, imp, NEG))
    cands = lax.fori_loop(0, min(SEL_TOPK, nrows), lambda _, c: tuple(pick_one(x) for x in c), tuple(cands))
    for u in range(CMP_UNITS):
        member = (cands[u] == -jnp.inf) & (blk <= curs[u])
        mm_ref[0:nrows, u * LANES:(u + 1) * LANES] = jnp.where(member, 0.0, -1.0).astype(BF16)
    if nrows < n_sel:
        mm_ref[nrows:n_sel, :] = jnp.full((n_sel - nrows, CMP_TQ), -1.0, BF16)


def _cmp_topk(qT, kc4, vc4, ov, cmp_bias):
    B, _, S = qT.shape
    nstep = S // CMP_TQ
    nch = kc4.shape[1]
    n_sel = S // SEL_BLOCK
    nq = NSA_HEADS * LANES
    nbias = cmp_bias.shape[0] // CMP_UNITS
    return pl.pallas_call(
        functools.partial(_cmp_kernel, n_sel=n_sel, nstep=nstep),
        grid=(B, nstep),
        in_specs=[pl.BlockSpec((None, NSA_HEADS * HEAD_DIM, CMP_TQ), lambda b, i: (b, A_Q_BLK, i)),
                  pl.BlockSpec((None, nch, LANES, LANES), lambda b, i: (b, 0, 0, 0)),
                  pl.BlockSpec((None, nch, VROWS, LANES), lambda b, i: (b, 0, 0, 0)),
                  pl.BlockSpec((40, LANES), lambda b, i: (0, 0)),
                  pl.BlockSpec((None, CMP_UNITS, LANES, LANES), lambda b, i: (i % nbias, 0, 0, 0))],
        out_specs=[pl.BlockSpec((None, NSA_HEADS * HEAD_DIM, CMP_TQ), lambda b, i: (b, 0, i)),
                   pl.BlockSpec((None, n_sel, CMP_TQ), lambda b, i: (b, 0, i))],
        out_shape=[jax.ShapeDtypeStruct((B, NSA_HEADS * HEAD_DIM, S), F32),
                   jax.ShapeDtypeStruct((B, n_sel, S), BF16)],
        scratch_shapes=[pltpu.VMEM((CMP_UNITS, 32 * nch + 64, nq), F32)],
        compiler_params=_cp(("parallel", "parallel")),
        name="nsa_cmp_topk",
    )(qT, kc4, vc4, ov, cmp_bias.reshape(nbias, CMP_UNITS, LANES, LANES))


SEL_GROUP = 8
SEL_TQ = 2 * LANES


def _sel_kernel(q_ref, mm_ref, ks_ref, v_ref, causal_ref, ocmp_ref, owin_ref, gate_ref, out_ref,
                qa_scr, mm_scr, s_scr):
    i = pl.program_id(1)
    nq = NSA_HEADS * SEL_TQ
    qa_scr[0:HEAD_DIM, :] = _stack_heads(q_ref, NSA_HEADS)
    qa_scr[HEAD_DIM:LANES, :] = jnp.zeros((LANES - HEAD_DIM, nq), BF16)
    mm = mm_ref[...]
    for h in range(NSA_HEADS):
        mm_scr[:, h * SEL_TQ:(h + 1) * SEL_TQ] = mm

    gkeys = SEL_GROUP * LANES

    def load_query(gi):
        rg = (gi * SEL_GROUP) // 8
        qa_scr[HEAD_DIM:HEAD_DIM + 16, :] = mm_scr[pl.ds(pl.multiple_of(rg * 16, 16), 16), :]
        return qa_scr[...]

    def chunk_scores(gi, u, qa):
        row = pl.multiple_of((gi * SEL_GROUP + u) * LANES, LANES)
        return jnp.dot(ks_ref[pl.ds(row, LANES), :], qa, preferred_element_type=F32)

    def weighted_values(gi, ps):
        vcat = jnp.concatenate([v_ref[gi * SEL_GROUP + u] for u in range(SEL_GROUP)], axis=1)
        return jnp.dot(vcat, jnp.concatenate(ps, axis=0), preferred_element_type=F32)

    def body(gi, carry):
        m, acc, mg = carry
        m_new = jnp.maximum(m, mg)
        qa = load_query(gi + 1)
        ps, mx = [], None
        for u in range(SEL_GROUP):
            rows = slice(u * LANES, (u + 1) * LANES)
            ps.append(jnp.exp(s_scr[rows, :] - m_new).astype(BF16))
            nxt = chunk_scores(gi + 1, u, qa)
            s_scr[rows, :] = nxt
            mx = nxt if mx is None else jnp.maximum(mx, nxt)
        acc = acc * jnp.exp(m - m_new) + weighted_values(gi, ps)
        return m_new, acc, jnp.max(mx, axis=0, keepdims=True)

    qa = load_query(0)
    mx = None
    for u in range(SEL_GROUP):
        s0 = chunk_scores(0, u, qa)
        s_scr[u * LANES:(u + 1) * LANES, :] = s0
        mx = s0 if mx is None else jnp.maximum(mx, s0)
    c0 = i * (SEL_TQ // LANES)
    last = c0 // SEL_GROUP
    m, acc, _ = lax.fori_loop(0, last, body, (jnp.full((1, nq), NEG, F32), jnp.zeros((VROWS, nq), F32),
                                              jnp.max(mx, axis=0, keepdims=True)))
    drow = pl.multiple_of((c0 % SEL_GROUP) * LANES, SEL_TQ)
    s_scr[pl.ds(drow, SEL_TQ), :] += causal_ref[...]
    s = s_scr[...]
    m_new = jnp.maximum(m, jnp.max(s, axis=0, keepdims=True))
    acc = acc * jnp.exp(m - m_new) + weighted_values(last, [jnp.exp(s - m_new).astype(BF16)])
    o = acc[0:HEAD_DIM] * (1.0 / acc[HEAD_DIM:HEAD_DIM + 1])

    g = gate_ref[...]
    for h in range(NSA_HEADS):
        rs = slice(HEAD_DIM * h, HEAD_DIM * (h + 1))
        out = (g[3 * h:3 * h + 1] * ocmp_ref[rs, :] + g[3 * h + 1:3 * h + 2] * o[:, h * SEL_TQ:(h + 1) * SEL_TQ]
               + g[3 * h + 2:3 * h + 3] * owin_ref[rs, :])
        out_ref[rs, :] = out.astype(BF16)


def _sel_attend(qT, mm, nk, v4, causal4, ocmp, owin, gates):
    B, _, S = qT.shape
    nstep = S // SEL_TQ
    n_sel = S // SEL_BLOCK
    nq = NSA_HEADS * SEL_TQ
    ar = NSA_HEADS * HEAD_DIM
    return pl.pallas_call(
        _sel_kernel,
        grid=(B, nstep),
        in_specs=[pl.BlockSpec((None, ar, SEL_TQ), lambda b, i: (b, A_Q_BLK, i)),
                  pl.BlockSpec((None, n_sel, SEL_TQ), lambda b, i: (b, 0, i)),
                  pl.BlockSpec((None, S, LANES), lambda b, i: (b, 0, 0)),
                  pl.BlockSpec((None, S // LANES, VROWS, LANES), lambda b, i: (b, 0, 0, 0)),
                  _const_spec((SEL_TQ, nq), lambda b, i: (0, 0)),
                  pl.BlockSpec((None, ar, SEL_TQ), lambda b, i: (b, 0, i)),
                  pl.BlockSpec((None, ar, SEL_TQ), lambda b, i: (b, 0, i)),
                  pl.BlockSpec((None, G_ROWS, SEL_TQ), lambda b, i: (b, 0, i))],
        out_specs=pl.BlockSpec((None, ar, SEL_TQ), lambda b, i: (b, 0, i)),
        out_shape=jax.ShapeDtypeStruct((B, ar, S), BF16),
        scratch_shapes=[pltpu.VMEM((LANES, nq), BF16), pltpu.VMEM((n_sel, nq), BF16),
                        pltpu.VMEM((SEL_GROUP * LANES, nq), F32)],
        compiler_params=_cp(("parallel", "parallel")),
        name="nsa_sel",
    )(qT, mm, nk, v4, causal4, ocmp, owin, gates)


def _banded_kernel(*refs, hkv, grp, nprev, zero_bias, has_sink, want_lse):
    q_ref, k_ref, v_ref, bias_ref = refs[:4]
    pos = 4
    sink_ref = None
    if has_sink:
        sink_ref = refs[pos]
        pos += 1
    o_ref = refs[pos]
    lse_ref = refs[pos + 1] if want_lse else None
    j = pl.program_id(1)

    def scores(s, g, first):
        lanes = slice(s * LANES, (s + 1) * LANES)
        qg = jnp.concatenate([q_ref[HEAD_DIM * (g * grp + u):HEAD_DIM * (g * grp + u + 1), lanes]
                              for u in range(grp)], axis=1)
        parts = []
        for ci in range(nprev + 1):
            if first:
                kc = s - nprev + ci
                if kc < 0:
                    continue
                row = kc * LANES
            else:
                kc = j * BAND_NSUB + s - nprev + ci
                row = pl.multiple_of(kc * LANES, LANES)
            sc = jnp.dot(k_ref[pl.ds(row, LANES), HEAD_DIM * g:HEAD_DIM * (g + 1)], qg,
                         preferred_element_type=F32)
            if not zero_bias[ci]:
                sc = sc + bias_ref[ci]
            parts.append((kc, sc))
        return parts

    def finish(s, g, parts):
        lanes = slice(s * LANES, (s + 1) * LANES)
        m = None
        for _, sc in parts:
            mc = jnp.max(sc, axis=0, keepdims=True)
            m = mc if m is None else jnp.maximum(m, mc)
        if has_sink:
            sk = sink_ref[g, 0:1, :]
            m = jnp.maximum(m, sk)
        acc = None
        for kcc, sc in parts:
            t = jnp.dot(v_ref[kcc, VROWS * g:VROWS * (g + 1), :], jnp.exp(sc - m).astype(BF16),
                        preferred_element_type=F32)
            acc = t if acc is None else acc + t
        l = acc[HEAD_DIM:HEAD_DIM + 1]
        if has_sink:
            l = l + jnp.exp(sk - m)
        o = acc[0:HEAD_DIM] * (1.0 / l)
        for u in range(grp):
            hq = g * grp + u
            o_ref[HEAD_DIM * hq:HEAD_DIM * (hq + 1), lanes] = o[:, u * LANES:(u + 1) * LANES].astype(o_ref.dtype)
        if want_lse:
            lse_ref[8 * g:8 * (g + 1), lanes] = jnp.broadcast_to(m + jnp.log(l), (8, LANES))

    def run(first):
        pending = None
        for s in range(BAND_NSUB):
            for g in range(hkv):
                parts = scores(s, g, first)
                if pending is not None:
                    finish(*pending)
                pending = (s, g, parts)
        finish(*pending)

    pl.when(j == 0)(lambda: run(True))
    pl.when(j > 0)(lambda: run(False))


def _banded(qT, q_blk, nk, k_blk, v4, v_blk, bias_and_zero, *, hkv, grp, nprev, out_dtype, sinks=None,
            want_lse=False, name):
    bias, zero_bias = bias_and_zero
    assert nprev <= BAND_NSUB
    B, _, S = qT.shape
    nt = S // BAND_TILE
    qrows = hkv * grp * HEAD_DIM
    in_specs = [pl.BlockSpec((None, qrows, BAND_TILE), lambda b, j: (b, q_blk, j)),
                pl.BlockSpec((None, S, LANES), lambda b, j: (b, 0, k_blk)),
                pl.BlockSpec((None, S // LANES, hkv * VROWS, LANES), lambda b, j: (b, 0, v_blk, 0)),
                pl.BlockSpec(bias.shape, lambda b, j: (0, 0, 0))]
    args = [qT, nk, v4, bias]
    if sinks is not None:
        in_specs.append(pl.BlockSpec(sinks.shape, lambda b, j: (0, 0, 0)))
        args.append(sinks)
    out_specs = [pl.BlockSpec((None, qrows, BAND_TILE), lambda b, j: (b, 0, j))]
    out_shape = [jax.ShapeDtypeStruct((B, qrows, S), out_dtype)]
    if want_lse:
        out_specs.append(pl.BlockSpec((None, 8 * hkv, BAND_TILE), lambda b, j: (b, 0, j)))
        out_shape.append(jax.ShapeDtypeStruct((B, 8 * hkv, S), F32))
    return pl.pallas_call(
        functools.partial(_banded_kernel, hkv=hkv, grp=grp, nprev=nprev, zero_bias=zero_bias,
                          has_sink=sinks is not None, want_lse=want_lse),
        grid=(B, nt),
        in_specs=in_specs,
        out_specs=out_specs,
        out_shape=out_shape,
        compiler_params=_cp(("parallel", "parallel")),
        name=name,
    )(*args)


def _folded_kernel(q_ref, k_ref, v_ref, bias_ref, o_ref, lse_ref):
    j = pl.program_id(2)
    nsub = q_ref.shape[0] // LANES
    ones = jnp.ones((VROWS - HEAD_DIM, LANES), BF16)
    row = lax.broadcasted_iota(jnp.int32, (LANES, LANES), 0)

    def scores(s):
        qt = q_ref[s * LANES:(s + 1) * LANES, :].astype(F32).T.astype(BF16)
        parts = []
        for ci in range(2):
            kc = j * nsub + s - 1 + ci
            kcc = jnp.maximum(kc, 0)
            rows = pl.ds(pl.multiple_of(kcc * LANES, LANES), LANES)
            bias = jnp.where(kc >= 0, bias_ref[ci], NEG)
            vt = v_ref[rows, :].astype(F32).T.astype(BF16)
            for g in range(2):
                hs = slice(HEAD_DIM * g, HEAD_DIM * (g + 1))
                sc = jnp.dot(k_ref[rows, hs], qt[hs], preferred_element_type=F32) + bias
                parts.append((g, jnp.concatenate([vt[hs], ones], axis=0), sc))
        return parts

    def finish(s, parts):
        outs, lses = [], []
        for g in range(2):
            mine = [(v, sc) for gg, v, sc in parts if gg == g]
            m = functools.reduce(jnp.maximum, [jnp.max(sc, axis=0, keepdims=True) for _, sc in mine])
            acc = None
            for v, sc in mine:
                t = jnp.dot(v, jnp.exp(sc - m).astype(BF16), preferred_element_type=F32)
                acc = t if acc is None else acc + t
            l = acc[HEAD_DIM:HEAD_DIM + 1]
            outs.append(acc[0:HEAD_DIM] * (1.0 / l))
            lses.append(m + jnp.log(l))
        rows = slice(s * LANES, (s + 1) * LANES)
        o_ref[rows, :] = jnp.concatenate(outs, axis=0).T
        lse_ref[rows, :] = jnp.where(row == 0, lses[0], jnp.where(row == 1, lses[1], 0.0)).T

    pending = None
    for s in range(nsub):
        parts = scores(s)
        if pending is not None:
            finish(*pending)
        pending = (s, parts)
    finish(*pending)


def _folded_dilated(fq, fk, fv, bias, dil):
    B, n, _ = fq.shape
    tq = min(n, BAND_TILE)
    return pl.pallas_call(
        _folded_kernel,
        grid=(B, dil, n // tq),
        in_specs=[pl.BlockSpec((None, tq, LANES), lambda b, r, j: (b, j, r)),
                  pl.BlockSpec((None, n, LANES), lambda b, r, j: (b, 0, r)),
                  pl.BlockSpec((None, n, LANES), lambda b, r, j: (b, 0, r)),
                  pl.BlockSpec(bias.shape, lambda b, r, j: (0, 0, 0))],
        out_specs=[pl.BlockSpec((None, tq, LANES), lambda b, r, j: (b, j, r)),
                   pl.BlockSpec((None, tq, LANES), lambda b, r, j: (b, j, r))],
        out_shape=[jax.ShapeDtypeStruct((B, n, dil * LANES), F32),
                   jax.ShapeDtypeStruct((B, n, dil * LANES), F32)],
        compiler_params=_cp(("parallel", "parallel", "parallel")),
        name="dil%d_folded" % dil,
    )(fq, fk, fv, bias)


def _post_norm_residual(y, x_ref, gw_ref, o_ref, token_major=False):
    ms = jnp.mean(y * y, axis=0, keepdims=True)
    yn = y * lax.rsqrt(ms + RMS_EPS)
    gw = gw_ref[...]
    for c in range(NSUB):
        sl = slice(c * LANES, (c + 1) * LANES)
        out = x_ref[:, sl] + gw * yn[:, sl]
        if token_major:
            o_ref[sl, :] = out.T
        else:
            o_ref[:, sl] = out


def _mix_ffn_kernel(a_ref, b0_ref, b1_ref, b2_ref, l0_ref, l1_ref, l2_ref, c_ref, wo_ref, x_ref, gw1_ref,
                    a2_ref, sh2_ref, wg_ref, wu_ref, wd_ref, cv_ref, gw2_ref, o_ref,
                    mix_scr, x1_scr, h_scr, carry_scr, act_scr, unf_scr, *, fchunk, token_major_out):
    s = pl.program_id(1)
    cur = s % 2
    prv = 1 - cur
    d_ff = wg_ref.shape[0]

    @pl.when(s == 0)
    def _():
        carry_scr[...] = jnp.zeros(carry_scr.shape, F32)
        h_scr[1] = jnp.zeros(h_scr.shape[1:], BF16)
        x1_scr[1] = jnp.zeros(x1_scr.shape[1:], F32)

    h = h_scr[prv]
    lane = lax.broadcasted_iota(jnp.int32, (fchunk, LANES), 1)
    for c in range(d_ff // fchunk):
        rs = slice(c * fchunk, (c + 1) * fchunk)
        g = jnp.dot(wg_ref[rs, :], h, preferred_element_type=F32)
        prev = carry_scr[rs, :]
        carry_scr[rs, :] = g[:, TILE - LANES:TILE]
        g1 = pltpu.roll(g, 1, 1)
        g2 = pltpu.roll(g, 2, 1)
        fix1 = jnp.where(lane < 1, pltpu.roll(prev, 1, 1), g1[:, 0:LANES])
        fix2 = jnp.where(lane < 2, pltpu.roll(prev, 2, 1), g2[:, 0:LANES])
        g1 = jnp.concatenate([fix1, g1[:, LANES:]], axis=1)
        g2 = jnp.concatenate([fix2, g2[:, LANES:]], axis=1)
        w0 = jnp.concatenate([cv_ref[0, rs, :]] * NSUB, axis=1)
        w1 = jnp.concatenate([cv_ref[1, rs, :]] * NSUB, axis=1)
        w2 = jnp.concatenate([cv_ref[2, rs, :]] * NSUB, axis=1)
        cb = jnp.concatenate([cv_ref[3, rs, :]] * NSUB, axis=1)
        acc = g2 * w0 + g1 * w1 + g * w2 + cb
        up = jnp.dot(wu_ref[rs, :], h, preferred_element_type=F32)
        act_scr[rs, :] = (jax.nn.gelu(acc, approximate=True) * up).astype(BF16)

    def unfold(src_ref, fold):
        for res in range(fold):
            unf_scr[pl.ds(res, TILE // fold, stride=fold), :] = src_ref[:, res * LANES:(res + 1) * LANES]
        return unf_scr[...].T

    d1, d2 = DIL_PATTERNS[1][1], DIL_PATTERNS[2][1]
    b_vals = (b0_ref[...], unfold(b1_ref, d1), unfold(b2_ref, d2))
    l1t, l2t = unfold(l1_ref, d1), unfold(l2_ref, d2)
    for hh in range(2):
        ls = [l0_ref[8 * hh:8 * hh + 1, :], l1t[hh:hh + 1, :], l2t[hh:hh + 1, :]]
        mx = jnp.maximum(jnp.maximum(ls[0], ls[1]), ls[2])
        es = [jnp.exp(v - mx) for v in ls]
        inv = 1.0 / (es[0] + es[1] + es[2])
        for g in range(3):
            rs = slice(HEAD_DIM * hh, HEAD_DIM * (hh + 1))
            mix_scr[128 * g + HEAD_DIM * hh:128 * g + HEAD_DIM * (hh + 1), :] = (
                b_vals[g][rs, :] * (es[g] * inv)).astype(BF16)
    y1 = jnp.dot(wo_ref[:, 0:256], a_ref[...], preferred_element_type=F32)
    y1 = y1 + jnp.dot(wo_ref[:, 256:640], mix_scr[...], preferred_element_type=F32)
    y1 = y1 + jnp.dot(wo_ref[:, 640:1024], c_ref[...], preferred_element_type=F32)

    y2 = jnp.dot(wd_ref[...], act_scr[...], preferred_element_type=F32)

    _post_norm_residual(y1, x_ref, gw1_ref, x1_scr.at[cur])
    _norm_mod_to_scratch(x1_scr.at[cur], a2_ref, sh2_ref, h_scr.at[cur])
    _post_norm_residual(y2, x1_scr.at[prv], gw2_ref, o_ref, token_major_out)


def _mix_ffn(aT, bs, lses, cT, w_outT, xT, gw1, a2, sh2, wgT, wuT, wdT, cv, gw2, layer, token_major_out):
    B, D, S = xT.shape
    nt = S // TILE
    d_ff = wgT.shape[1]
    tile = lambda rows: pl.BlockSpec((None, rows, TILE), lambda b, s: (b, 0, jnp.minimum(s, nt - 1)))
    tok = lambda d: pl.BlockSpec((None, TILE // d, d * LANES), lambda b, s: (b, jnp.minimum(s, nt - 1), 0))
    t1, t2 = tok(DIL_PATTERNS[1][1]), tok(DIL_PATTERNS[2][1])
    vec = pl.BlockSpec((None, D, LANES), lambda b, s: (b, 0, 0))
    return pl.pallas_call(
        functools.partial(_mix_ffn_kernel, fchunk=256, token_major_out=token_major_out),
        grid=(B, nt + 1),
        in_specs=[tile(256), tile(128), t1, t2, tile(16), t1, t2, tile(384),
                  _const_spec((None, D, D), lambda b, s: (layer, 0, 0)),
                  tile(D), vec, vec, vec,
                  _const_spec((None, d_ff, D), lambda b, s: (layer, 0, 0)),
                  _const_spec((None, d_ff, D), lambda b, s: (layer, 0, 0)),
                  _const_spec((None, D, d_ff), lambda b, s: (layer, 0, 0)),
                  _const_spec((None, 4, d_ff, LANES), lambda b, s: (layer, 0, 0, 0)),
                  vec],
        out_specs=(pl.BlockSpec((None, TILE, D), lambda b, s: (b, jnp.maximum(s - 1, 0), 0)) if token_major_out
                   else pl.BlockSpec((None, D, TILE), lambda b, s: (b, 0, jnp.maximum(s - 1, 0)))),
        out_shape=jax.ShapeDtypeStruct((B, S, D) if token_major_out else (B, D, S), F32),
        scratch_shapes=[pltpu.VMEM((384, TILE), BF16), pltpu.VMEM((2, D, TILE), F32),
                        pltpu.VMEM((2, D, TILE), BF16), pltpu.VMEM((d_ff, LANES), F32),
                        pltpu.VMEM((d_ff, TILE), BF16), pltpu.VMEM((TILE, LANES), F32)],
        compiler_params=_cp(("arbitrary", "arbitrary")),
        name="mix_ffn",
    )(aT, bs[0], bs[1], bs[2], lses[0], lses[1], lses[2], cT, w_outT, xT, gw1,
      a2, sh2, wgT, wuT, wdT, cv, gw2)


def _band_bias(old_edge, reps, nprev):
    kk = np.arange(LANES)[:, None]
    qq = np.arange(LANES)[None, :]
    tabs = [kk - qq >= old_edge] + [np.ones((LANES, LANES), bool)] * (nprev - 1) + [kk <= qq]
    zero = tuple(bool(t.all()) for t in tabs)
    out = np.stack([np.where(t, 0.0, NEG) for t in tabs]).astype(np.float32)
    return jnp.asarray(np.tile(out, (1, 1, reps))), zero


def _cmp_bias():
    nn = np.arange(LANES)[:, None]
    qq = np.arange(LANES)[None, :]
    tabs = [np.where(CMP_STRIDE * nn + CMP_BLOCK - 1 <= LANES * r + qq, 0.0, NEG) for r in range(16)]
    return jnp.asarray(np.stack(tabs).astype(np.float32))


def _overlap_rows():
    jj = np.arange(40)[:, None]
    nn = np.arange(LANES)[None, :]
    return jnp.asarray(((nn >= 4 * jj - 1) & (nn <= 4 * jj + 3)).astype(np.float32), dtype=BF16)


_IN_COL_ORDER = ((1804, 2188), (652, 1036), (0, 256),
                 (256, 320), (320, 384), (384, 448), (512, 576), (1036, 1420), (2188, 2316),
                 (448, 512), (576, 640), (1420, 1804), (2316, 2444),
                 (640, 652))


def _prep_w_in(w_in):
    wt = jnp.swapaxes(jnp.concatenate([w_in[:, :, a:b] for a, b in _IN_COL_ORDER], axis=2), 1, 2)
    scale = np.ones((wt.shape[1], 1), np.float32)
    scale[:Q_ROWS] = HEAD_DIM ** -0.5
    wt = wt * scale
    wt = jnp.pad(wt, ((0, 0), (0, W_ROWS - wt.shape[1]), (0, 0)))
    return wt.astype(BF16)


def _prep_compress(w_ck, w_cv, pe_k, pe_v):
    L = w_ck.shape[0]
    half = CMP_BLOCK // 2

    def big(lo):
        wk = w_ck[:, lo:lo + half]
        wv = w_cv[:, lo:lo + half]
        z = jnp.zeros_like(wk)
        top = jnp.concatenate([wk, z], axis=-1)
        bot = jnp.concatenate([z, wv], axis=-1)
        return jnp.concatenate([top, bot], axis=2).reshape(L, half * LANES, LANES)

    wbig = jnp.concatenate([big(0), big(half)], axis=-1).astype(BF16)
    pe = jnp.concatenate([pe_k, pe_v], axis=-1)
    pe2 = jnp.zeros((L, 16, half * LANES), F32)
    pe2 = pe2.at[:, 0].set(pe[:, :half].reshape(L, -1)).at[:, 8].set(pe[:, half:].reshape(L, -1))
    return wbig, pe2.astype(BF16)


def _lane_bcast(v):
    return jnp.broadcast_to(v[..., None], v.shape + (LANES,))


def kernel(x, c, positions, w_in, w_out, w_ada, b_ada, norm_w, cmp_w_k, cmp_w_v, cmp_pe_k, cmp_pe_v,
           sinks, w_gate, w_up, conv_w, conv_b, w_down):
    B, S, D = x.shape
    depth = w_in.shape[0]
    assert S % 2048 == 0 and D == 1024 and w_in.shape[2] == 2444

    inv = ROPE_THETA ** (-jnp.arange(0, HEAD_DIM, 2, dtype=F32) / HEAD_DIM)
    ang = positions.astype(F32)[:, None, :] * inv[None, :, None]
    cosT, sinT = jnp.cos(ang), jnp.sin(ang)

    c8 = jnp.pad(c, ((0, 8 - B), (0, 0)))
    ada = _adaln(c8, w_ada, b_ada)[:, :B]
    sh1, sc1, g1, sh2, sc2, g2 = [ada[:, :, k * D:(k + 1) * D] for k in range(6)]
    nw = norm_w[:, :, None, :]
    a1 = _lane_bcast(nw[:, 0] * (1 + sc1))
    gw1 = _lane_bcast(g1 * nw[:, 1])
    a2 = _lane_bcast(nw[:, 2] * (1 + sc2))
    gw2 = _lane_bcast(g2 * nw[:, 3])
    sh1b, sh2b = _lane_bcast(sh1), _lane_bcast(sh2)

    w_inT = _prep_w_in(w_in)
    w_outT = jnp.swapaxes(w_out, 1, 2).astype(BF16)
    wgT = jnp.swapaxes(w_gate, 1, 2).astype(BF16)
    wuT = jnp.swapaxes(w_up, 1, 2).astype(BF16)
    wdT = jnp.swapaxes(w_down, 1, 2).astype(BF16)
    cv = _lane_bcast(jnp.concatenate([conv_w, conv_b[:, None, :]], axis=1))
    wbig, pe2 = _prep_compress(cmp_w_k, cmp_w_v, cmp_pe_k, cmp_pe_v)
    sink_tab = jnp.broadcast_to(
        jnp.repeat(sinks.reshape(depth, SWA_KV_HEADS, SWA_HEADS // SWA_KV_HEADS), LANES, axis=-1)[:, :, None, :],
        (depth, SWA_KV_HEADS, 8, LANES * (SWA_HEADS // SWA_KV_HEADS)))

    ov = _overlap_rows()
    cmp_bias = _cmp_bias()
    kk = np.arange(SEL_TQ)[:, None]
    causal4 = jnp.asarray(np.tile(np.where(kk <= kk.T, 0.0, NEG).astype(np.float32), (1, NSA_HEADS)))
    bias_win = _band_bias(1, NSA_HEADS, NSA_WINDOW // LANES)
    bias_swa = _band_bias(1, SWA_HEADS // SWA_KV_HEADS, SWA_WINDOW // LANES)
    bias_dil = _band_bias(0, 1, 1)
    assert all(w // d == LANES for w, d in DIL_PATTERNS)

    xT = x
    for l in range(depth):
        first, last_layer = l == 0, l == depth - 1
        outs = _inproj(xT, a1[l], sh1b[l], w_inT, l, cosT, sinT, token_major=first)
        qT, kcvc, nk, v4, gates = outs[:5]
        folded = {g: outs[5 + 3 * k:8 + 3 * k] for k, (g, _) in enumerate(FOLDED)}
        if first:
            xT = outs[5 + 3 * len(FOLDED)]
        kc4, vc4 = _compress(kcvc, wbig, pe2, l)
        ocmp, mm = _cmp_topk(qT, kc4, vc4, ov, cmp_bias)
        owin, = _banded(qT, A_Q_BLK, nk, 1, v4, 1, bias_win, hkv=1, grp=NSA_HEADS, nprev=NSA_WINDOW // LANES,
                        out_dtype=F32, name="nsa_win")
        aT = _sel_attend(qT, mm, nk, v4, causal4, ocmp, owin, gates)
        o, lse = _banded(qT, 3, nk, 2, v4, 1, bias_dil, hkv=2, grp=1, nprev=1, out_dtype=F32, want_lse=True,
                         name="dil1")
        bs, lses = [o], [lse]
        for g, _ in FOLDED:
            o, lse = _folded_dilated(*folded[g], bias_dil[0], DIL_PATTERNS[g][1])
            bs.append(o)
            lses.append(lse)
        cT, = _banded(qT, 0, nk, 5, v4, 4, bias_swa, hkv=SWA_KV_HEADS, grp=SWA_HEADS // SWA_KV_HEADS,
                      nprev=SWA_WINDOW // LANES, out_dtype=BF16,
                      sinks=sink_tab[l], name="swa")
        xT = _mix_ffn(aT, bs, lses, cT, w_outT, xT, gw1[l], a2[l], sh2b[l], wgT, wuT, wdT, cv, gw2[l], l,
                      token_major_out=last_layer)
    return xT
```

```python
import functools

import numpy as np
import jax
import jax.numpy as jnp
from jax import lax
from jax.experimental import pallas as pl
from jax.experimental.pallas import tpu as pltpu

F32 = jnp.float32
BF16 = jnp.bfloat16

HEAD_DIM = 64
HALF = HEAD_DIM // 2
NSA_HEADS = 4
CMP_BLOCK = 32
CMP_STRIDE = 16
SEL_BLOCK = 64
SEL_TOPK = 16
NSA_WINDOW = 512
DIL_PATTERNS = ((128, 1), (512, 4), (2048, 16))
SWA_HEADS = 6
SWA_KV_HEADS = 2
SWA_WINDOW = 128
ROPE_THETA = 10000.0
RMS_EPS = 1e-6
NEG = -1e30
FORCE = 1e4

LANES = 128
TILE = 512
NSUB = TILE // LANES
BAND_TILE = 2048
BAND_NSUB = BAND_TILE // LANES
VROWS = HEAD_DIM + 16
MEMBER_BIG = 2.0 ** 100
VMEM_LIMIT = 56 * 1024 * 1024

Q_ROWS = 1024
A_Q_BLK = 3
NK_ROWS = 768
NK_LANES = 768
V_ROWS = 640
G_ROWS = 16
W_ROWS = Q_ROWS + NK_ROWS + V_ROWS + G_ROWS
N_VPIECES = V_ROWS // HEAD_DIM
FOLDED = ((1, slice(128, 256)), (2, slice(256, 384)))
FOLD = CMP_STRIDE
assert DIL_PATTERNS[2][1] == FOLD


def _cp(sem):
    return pltpu.CompilerParams(dimension_semantics=sem, vmem_limit_bytes=VMEM_LIMIT)


def _const_spec(shape, index_map):
    return pl.BlockSpec(shape, index_map, pipeline_mode=pl.Buffered(1))


def _adaln_kernel(c_ref, w_ref, b_ref, o_ref):
    c = c_ref[...]
    cond = c * jax.nn.sigmoid(c)
    o_ref[...] = jnp.dot(cond, w_ref[...], preferred_element_type=F32,
                         precision=lax.Precision.HIGHEST) + b_ref[...]


def _adaln(c8, w_ada, b_ada):
    depth, d, six_d = w_ada.shape
    nblk = six_d // d
    return pl.pallas_call(
        _adaln_kernel,
        grid=(depth, nblk),
        in_specs=[pl.BlockSpec((8, d), lambda l, n: (0, 0)),
                  pl.BlockSpec((None, d, d), lambda l, n: (l, 0, n)),
                  pl.BlockSpec((None, 1, d), lambda l, n: (l, 0, n))],
        out_specs=pl.BlockSpec((None, 8, d), lambda l, n: (l, 0, n)),
        out_shape=jax.ShapeDtypeStruct((depth, 8, six_d), F32),
        compiler_params=_cp(("parallel", "parallel")),
        name="adaln",
    )(c8, w_ada, b_ada.reshape(depth, 1, six_d))


def _norm_mod_to_scratch(x_ref, a_ref, sh_ref, h_scr):
    for c in range(NSUB):
        sl = slice(c * LANES, (c + 1) * LANES)
        xs = x_ref[:, sl]
        ms = jnp.mean(xs * xs, axis=0, keepdims=True)
        h_scr[:, sl] = ((xs * lax.rsqrt(ms + RMS_EPS)) * a_ref[...] + sh_ref[...]).astype(BF16)


def _inproj_kernel(x_ref, a_ref, sh_ref, w_ref, cos_ref, sin_ref,
                   q_ref, kcvc_ref, nk_ref, v4_ref, gate_ref, fq1_ref, fk1_ref, fv1_ref, fq2_ref, fk2_ref, fv2_ref,
                   *rest, token_major):
    j = pl.program_id(1)
    if token_major:
        xt_ref, h_scr, fold_scr = rest
        for c in range(NSUB):
            xt_ref[:, c * LANES:(c + 1) * LANES] = x_ref[c * LANES:(c + 1) * LANES, :].T
        x_ref = xt_ref
    else:
        h_scr, fold_scr = rest
    _norm_mod_to_scratch(x_ref, a_ref, sh_ref, h_scr)
    h = h_scr[...]
    cos = cos_ref[...]
    sin = sin_ref[...]

    def proj(r0, r1):
        return jnp.dot(w_ref[r0:r1, :], h, preferred_element_type=F32)

    def fold_store(t, dst_ref, fold=FOLD):
        fold_scr[...] = t
        for res in range(fold):
            dst_ref[:, res * LANES:(res + 1) * LANES] = fold_scr[pl.ds(res, TILE // fold, stride=fold), :].astype(BF16)

    folded_refs = {1: (fq1_ref, fk1_ref, fv1_ref), 2: (fq2_ref, fk2_ref, fv2_ref)}

    def rope(r, nh):
        outs = []
        for hh in range(nh):
            t1 = r[HEAD_DIM * hh:HEAD_DIM * hh + HALF]
            t2 = r[HEAD_DIM * hh + HALF:HEAD_DIM * (hh + 1)]
            outs.append(t1 * cos - t2 * sin)
            outs.append(t2 * cos + t1 * sin)
        return jnp.concatenate(outs, axis=0)

    for r0, r1 in ((0, 384), (384, 768), (768, 1024)):
        r = rope(proj(r0, r1), (r1 - r0) // HEAD_DIM)
        q_ref[r0:r1, :] = r.astype(BF16)
        if r0 == 384:
            for grp_i, rows in FOLDED:
                fold_store(r[rows].T, folded_refs[grp_i][0], DIL_PATTERNS[grp_i][1])

    base = Q_ROWS
    r = proj(base, base + 128)
    kcvc = jnp.concatenate([rope(r[0:64], 1), r[64:128]], axis=0)
    fold_store(kcvc.T, kcvc_ref)

    r = proj(base + 128, base + 192)
    tok = j * TILE + lax.broadcasted_iota(jnp.int32, (HEAD_DIM, TILE), 1)
    row = lax.broadcasted_iota(jnp.int32, (HEAD_DIM, TILE), 0)
    member_cols = jnp.where(row == ((tok >> 6) & 15), MEMBER_BIG, 0.0).astype(F32)
    nk_ref[:, 0:128] = jnp.concatenate([rope(r, 1), member_cols], axis=0).T.astype(BF16)

    r = proj(base + 192, base + 256)
    nk_ref[:, 128:256] = jnp.concatenate([rope(r, 1), jnp.zeros((HEAD_DIM, TILE), F32)], axis=0).T.astype(BF16)

    r = proj(base + 256, base + 640)
    bk = rope(r, 6).T
    nk_ref[:, 256:640] = bk.astype(BF16)
    for grp_i, rows in FOLDED:
        fold_store(bk[:, rows], folded_refs[grp_i][1], DIL_PATTERNS[grp_i][1])

    r = proj(base + 640, base + 768)
    nk_ref[:, 640:768] = rope(r, 2).T.astype(BF16)

    base = Q_ROWS + NK_ROWS
    r = proj(base, base + V_ROWS)
    for grp_i, rows in FOLDED:
        fold_store(r[128 + rows.start:128 + rows.stop].T, folded_refs[grp_i][2], DIL_PATTERNS[grp_i][1])
    r = r.astype(BF16)
    ones = jnp.ones((VROWS - HEAD_DIM, LANES), BF16)
    for c in range(NSUB):
        for p in range(N_VPIECES):
            v4_ref[c, VROWS * p:VROWS * p + HEAD_DIM, :] = r[HEAD_DIM * p:HEAD_DIM * (p + 1),
                                                             c * LANES:(c + 1) * LANES]
            v4_ref[c, VROWS * p + HEAD_DIM:VROWS * (p + 1), :] = ones

    base = Q_ROWS + NK_ROWS + V_ROWS
    gate_ref[...] = jax.nn.sigmoid(proj(base, base + G_ROWS))


def _inproj(x, a1, sh1, w_inT, layer, cosT, sinT, token_major):
    if token_major:
        B, S, D = x.shape
        x_spec = pl.BlockSpec((None, TILE, D), lambda b, j: (b, j, 0))
    else:
        B, D, S = x.shape
        x_spec = pl.BlockSpec((None, D, TILE), lambda b, j: (b, 0, j))
    nt = S // TILE
    fold_specs = {d: pl.BlockSpec((None, TILE // d, d * LANES), lambda b, j: (b, j, 0)) for _, d in DIL_PATTERNS}
    fold_spec = fold_specs[FOLD]
    out_specs = [pl.BlockSpec((None, Q_ROWS, TILE), lambda b, j: (b, 0, j)),
                 fold_spec,
                 pl.BlockSpec((None, TILE, NK_LANES), lambda b, j: (b, j, 0)),
                 pl.BlockSpec((None, NSUB, N_VPIECES * VROWS, LANES), lambda b, j: (b, j, 0, 0)),
                 pl.BlockSpec((None, G_ROWS, TILE), lambda b, j: (b, 0, j)),
                 ] + [fold_specs[DIL_PATTERNS[g][1]] for g, _ in FOLDED for _ in range(3)]
    out_shape = [jax.ShapeDtypeStruct((B, Q_ROWS, S), BF16),
                 jax.ShapeDtypeStruct((B, S // FOLD, FOLD * LANES), BF16),
                 jax.ShapeDtypeStruct((B, S, NK_LANES), BF16),
                 jax.ShapeDtypeStruct((B, S // LANES, N_VPIECES * VROWS, LANES), BF16),
                 jax.ShapeDtypeStruct((B, G_ROWS, S), F32)] + [
                     jax.ShapeDtypeStruct((B, S // DIL_PATTERNS[g][1], DIL_PATTERNS[g][1] * LANES), BF16)
                     for g, _ in FOLDED for _ in range(3)]
    if token_major:
        out_specs.append(pl.BlockSpec((None, D, TILE), lambda b, j: (b, 0, j)))
        out_shape.append(jax.ShapeDtypeStruct((B, D, S), F32))
    return pl.pallas_call(
        functools.partial(_inproj_kernel, token_major=token_major),
        grid=(B, nt),
        in_specs=[x_spec,
                  pl.BlockSpec((None, D, LANES), lambda b, j: (b, 0, 0)),
                  pl.BlockSpec((None, D, LANES), lambda b, j: (b, 0, 0)),
                  _const_spec((None, W_ROWS, D), lambda b, j: (layer, 0, 0)),
                  pl.BlockSpec((None, HALF, TILE), lambda b, j: (b, 0, j)),
                  pl.BlockSpec((None, HALF, TILE), lambda b, j: (b, 0, j))],
        out_specs=out_specs,
        out_shape=out_shape,
        scratch_shapes=[pltpu.VMEM((D, TILE), BF16), pltpu.VMEM((TILE, LANES), F32)],
        compiler_params=_cp(("parallel", "parallel")),
        name="inproj",
    )(x, a1, sh1, w_inT, cosT, sinT)


def _compress_kernel(t_ref, w_ref, pe_ref, kc_ref, vc_ref):
    n = t_ref.shape[0]
    a = jnp.dot(t_ref[...], w_ref[...], preferred_element_type=F32)
    pc = jnp.dot(pe_ref[...], w_ref[...], preferred_element_type=F32)
    const = pc[0:1, 0:LANES] + pc[8:9, LANES:2 * LANES]
    cmp = a[:, 0:LANES] + pltpu.roll(a[:, LANES:2 * LANES], n - 1, 0) + const
    cmp_t = cmp.T
    ones = jnp.ones((VROWS - HEAD_DIM, LANES), BF16)
    for c in range(n // LANES):
        kc_ref[c] = cmp[c * LANES:(c + 1) * LANES].astype(BF16)
        vc_ref[c, 0:HEAD_DIM, :] = cmp_t[HEAD_DIM:2 * HEAD_DIM, c * LANES:(c + 1) * LANES].astype(BF16)
        vc_ref[c, HEAD_DIM:VROWS, :] = ones


def _compress(tview, wbig, pe2, layer):
    B, n, _ = tview.shape
    nch = n // LANES
    return pl.pallas_call(
        _compress_kernel,
        grid=(B,),
        in_specs=[pl.BlockSpec((None, n, CMP_STRIDE * LANES), lambda b: (b, 0, 0)),
                  pl.BlockSpec((None, CMP_STRIDE * LANES, 2 * LANES), lambda b: (layer, 0, 0)),
                  pl.BlockSpec((None, 16, CMP_STRIDE * LANES), lambda b: (layer, 0, 0))],
        out_specs=[pl.BlockSpec((None, nch, LANES, LANES), lambda b: (b, 0, 0, 0)),
                   pl.BlockSpec((None, nch, VROWS, LANES), lambda b: (b, 0, 0, 0))],
        out_shape=[jax.ShapeDtypeStruct((B, nch, LANES, LANES), BF16),
                   jax.ShapeDtypeStruct((B, nch, VROWS, LANES), BF16)],
        compiler_params=_cp(("parallel",)),
        name="compress",
    )(tview, wbig, pe2)


def _stack_heads(q_ref, nh, lane_slice=slice(None)):
    return jnp.concatenate([q_ref[HEAD_DIM * h:HEAD_DIM * (h + 1), lane_slice] for h in range(nh)], axis=1)


CMP_UNITS = 8
CMP_TQ = CMP_UNITS * LANES


def _cmp_kernel(q_ref, kc_ref, vc_ref, ov_ref, bias_ref, o_ref, mm_ref, imp_scr, *, n_sel, nstep):
    i = pl.program_id(1)
    nch_all = kc_ref.shape[0]
    parts = 4 if nch_all % 4 == 0 else (2 if nch_all % 2 == 0 else 1)
    for v in range(parts):
        pl.when(i // (nstep // parts) == v)(functools.partial(
            _cmp_body, q_ref, kc_ref, vc_ref, ov_ref, bias_ref, o_ref, mm_ref, imp_scr,
            nch=nch_all * (v + 1) // parts, nrows=n_sel * (v + 1) // parts, n_sel=n_sel))


def _cmp_body(q_ref, kc_ref, vc_ref, ov_ref, bias_ref, o_ref, mm_ref, imp_scr, *, nch, nrows, n_sel):
    i = pl.program_id(1)
    cd = (i * CMP_UNITS) // 16

    def scores(u):
        q = _stack_heads(q_ref, NSA_HEADS, slice(u * LANES, (u + 1) * LANES))
        edge = bias_ref[u]
        ss = []
        for c in range(nch):
            b = jnp.where(c < cd, 0.0, jnp.where(c == cd, edge, NEG))
            s = jnp.dot(kc_ref[c, :, 0:HEAD_DIM], q, preferred_element_type=F32)
            ss.append(s + jnp.concatenate([b] * NSA_HEADS, axis=1))
        return ss

    def finish(u, ss):
        lanes = slice(u * LANES, (u + 1) * LANES)
        m = jnp.max(functools.reduce(jnp.maximum, ss), axis=0, keepdims=True)
        valid = m > 0.5 * NEG
        imp_scr[u, 0:32 * nch + 8, :] = jnp.zeros((32 * nch + 8, imp_scr.shape[2]), F32)
        acc = None
        for c in range(nch):
            p = jnp.exp(ss[c] - m).astype(BF16)
            t = jnp.dot(vc_ref[c], p, preferred_element_type=F32)
            acc = t if acc is None else acc + t
            imp_scr[u, 32 * c:32 * c + 40, :] += jnp.dot(ov_ref[...], p, preferred_element_type=F32)
        inv = jnp.where(valid, 1.0 / acc[HEAD_DIM:HEAD_DIM + 1], 0.0)
        o = acc[0:HEAD_DIM] * inv
        for h in range(NSA_HEADS):
            o_ref[HEAD_DIM * h:HEAD_DIM * (h + 1), lanes] = o[:, h * LANES:(h + 1) * LANES]
        imp = jnp.zeros((nrows, LANES), F32)
        for h in range(NSA_HEADS):
            sl = slice(h * LANES, (h + 1) * LANES)
            imp = imp + imp_scr[u, 0:nrows, sl] * inv[:, sl]
        return imp

    pending = scores(0)
    imps = []
    for u in range(1, CMP_UNITS):
        nxt = scores(u)
        imps.append(finish(u - 1, pending))
        pending = nxt
    imps.append(finish(CMP_UNITS - 1, pending))

    blk = lax.broadcasted_iota(jnp.int32, (nrows, LANES), 0).astype(F32)

    def pick_one(imp):
        mx = jnp.max(imp, axis=0, keepdims=True)
        first = jnp.min(jnp.where(imp == mx, blk, float(nrows)), axis=0, keepdims=True)
        return jnp.where(blk == first, -jnp.inf, imp)

    curs, cands = [], []
    for u in range(CMP_UNITS):
        t = i * CMP_TQ + u * LANES + lax.broadcasted_iota(jnp.int32, (nrows, LANES), 1)
        cur = (t >> 6).astype(F32)
        forced = (blk == 0.0) | (blk == cur) | (blk == cur - 1.0)
        imp = jnp.where(forced, FORCE, imps[u])
        curs.append(cur)
        cands.append(jnp.where(blk <= cur, imp, NEG))
    cands = lax.fori_loop(0, min(SEL_TOPK, nrows), lambda _, c: tuple(pick_one(x) for x in c), tuple(cands))
    for u in range(CMP_UNITS):
        member = (cands[u] == -jnp.inf) & (blk <= curs[u])
        mm_ref[0:nrows, u * LANES:(u + 1) * LANES] = jnp.where(member, 0.0, -1.0).astype(BF16)
    if nrows < n_sel:
        mm_ref[nrows:n_sel, :] = jnp.full((n_sel - nrows, CMP_TQ), -1.0, BF16)


def _cmp_topk(qT, kc4, vc4, ov, cmp_bias):
    B, _, S = qT.shape
    nstep = S // CMP_TQ
    nch = kc4.shape[1]
    n_sel = S // SEL_BLOCK
    nq = NSA_HEADS * LANES
    nbias = cmp_bias.shape[0] // CMP_UNITS
    return pl.pallas_call(
        functools.partial(_cmp_kernel, n_sel=n_sel, nstep=nstep),
        grid=(B, nstep),
        in_specs=[pl.BlockSpec((None, NSA_HEADS * HEAD_DIM, CMP_TQ), lambda b, i: (b, A_Q_BLK, i)),
                  pl.BlockSpec((None, nch, LANES, LANES), lambda b, i: (b, 0, 0, 0)),
                  pl.BlockSpec((None, nch, VROWS, LANES), lambda b, i: (b, 0, 0, 0)),
                  pl.BlockSpec((40, LANES), lambda b, i: (0, 0)),
                  pl.BlockSpec((None, CMP_UNITS, LANES, LANES), lambda b, i: (i % nbias, 0, 0, 0))],
        out_specs=[pl.BlockSpec((None, NSA_HEADS * HEAD_DIM, CMP_TQ), lambda b, i: (b, 0, i)),
                   pl.BlockSpec((None, n_sel, CMP_TQ), lambda b, i: (b, 0, i))],
        out_shape=[jax.ShapeDtypeStruct((B, NSA_HEADS * HEAD_DIM, S), F32),
                   jax.ShapeDtypeStruct((B, n_sel, S), BF16)],
        scratch_shapes=[pltpu.VMEM((CMP_UNITS, 32 * nch + 64, nq), F32)],
        compiler_params=_cp(("parallel", "parallel")),
        name="nsa_cmp_topk",
    )(qT, kc4, vc4, ov, cmp_bias.reshape(nbias, CMP_UNITS, LANES, LANES))


SEL_GROUP = 8
SEL_TQ = 2 * LANES


def _sel_kernel(q_ref, mm_ref, ks_ref, v_ref, causal_ref, ocmp_ref, owin_ref, gate_ref, out_ref,
                qa_scr, mm_scr, s_scr):
    i = pl.program_id(1)
    nq = NSA_HEADS * SEL_TQ
    qa_scr[0:HEAD_DIM, :] = _stack_heads(q_ref, NSA_HEADS)
    qa_scr[HEAD_DIM:LANES, :] = jnp.zeros((LANES - HEAD_DIM, nq), BF16)
    mm = mm_ref[...]
    for h in range(NSA_HEADS):
        mm_scr[:, h * SEL_TQ:(h + 1) * SEL_TQ] = mm

    gkeys = SEL_GROUP * LANES

    def load_query(gi):
        rg = (gi * SEL_GROUP) // 8
        qa_scr[HEAD_DIM:HEAD_DIM + 16, :] = mm_scr[pl.ds(pl.multiple_of(rg * 16, 16), 16), :]
        return qa_scr[...]

    def chunk_scores(gi, u, qa):
        row = pl.multiple_of((gi * SEL_GROUP + u) * LANES, LANES)
        return jnp.dot(ks_ref[pl.ds(row, LANES), :], qa, preferred_element_type=F32)

    def weighted_values(gi, ps):
        vcat = jnp.concatenate([v_ref[gi * SEL_GROUP + u] for u in range(SEL_GROUP)], axis=1)
        return jnp.dot(vcat, jnp.concatenate(ps, axis=0), preferred_element_type=F32)

    def body(gi, carry):
        m, acc, mg = carry
        m_new = jnp.maximum(m, mg)
        qa = load_query(gi + 1)
        ps, mx = [], None
        for u in range(SEL_GROUP):
            rows = slice(u * LANES, (u + 1) * LANES)
            ps.append(jnp.exp(s_scr[rows, :].astype(F32) - m_new).astype(BF16))
            nxt = chunk_scores(gi + 1, u, qa)
            s_scr[rows, :] = nxt.astype(BF16)
            mx = nxt if mx is None else jnp.maximum(mx, nxt)
        acc = acc * jnp.exp(m - m_new) + weighted_values(gi, ps)
        return m_new, acc, jnp.max(mx, axis=0, keepdims=True)

    qa = load_query(0)
    mx = None
    for u in range(SEL_GROUP):
        s0 = chunk_scores(0, u, qa)
        s_scr[u * LANES:(u + 1) * LANES, :] = s0.astype(BF16)
        mx = s0 if mx is None else jnp.maximum(mx, s0)
    c0 = i * (SEL_TQ // LANES)
    last = c0 // SEL_GROUP
    m, acc, _ = lax.fori_loop(0, last, body, (jnp.full((1, nq), NEG, F32), jnp.zeros((VROWS, nq), F32),
                                              jnp.max(mx, axis=0, keepdims=True)))
    drow = pl.multiple_of((c0 % SEL_GROUP) * LANES, SEL_TQ)
    s_scr[pl.ds(drow, SEL_TQ), :] = (s_scr[pl.ds(drow, SEL_TQ), :].astype(F32) + causal_ref[...]).astype(BF16)
    s = s_scr[...].astype(F32)
    m_new = jnp.maximum(m, jnp.max(s, axis=0, keepdims=True))
    acc = acc * jnp.exp(m - m_new) + weighted_values(last, [jnp.exp(s - m_new).astype(BF16)])
    o = acc[0:HEAD_DIM] * (1.0 / acc[HEAD_DIM:HEAD_DIM + 1])

    g = gate_ref[...]
    for h in range(NSA_HEADS):
        rs = slice(HEAD_DIM * h, HEAD_DIM * (h + 1))
        out = (g[3 * h:3 * h + 1] * ocmp_ref[rs, :] + g[3 * h + 1:3 * h + 2] * o[:, h * SEL_TQ:(h + 1) * SEL_TQ]
               + g[3 * h + 2:3 * h + 3] * owin_ref[rs, :])
        out_ref[rs, :] = out.astype(BF16)


def _sel_attend(qT, mm, nk, v4, causal4, ocmp, owin, gates):
    B, _, S = qT.shape
    nstep = S // SEL_TQ
    n_sel = S // SEL_BLOCK
    nq = NSA_HEADS * SEL_TQ
    ar = NSA_HEADS * HEAD_DIM
    return pl.pallas_call(
        _sel_kernel,
        grid=(B, nstep),
        in_specs=[pl.BlockSpec((None, ar, SEL_TQ), lambda b, i: (b, A_Q_BLK, i)),
                  pl.BlockSpec((None, n_sel, SEL_TQ), lambda b, i: (b, 0, i)),
                  pl.BlockSpec((None, S, LANES), lambda b, i: (b, 0, 0)),
                  pl.BlockSpec((None, S // LANES, VROWS, LANES), lambda b, i: (b, 0, 0, 0)),
                  _const_spec((SEL_TQ, nq), lambda b, i: (0, 0)),
                  pl.BlockSpec((None, ar, SEL_TQ), lambda b, i: (b, 0, i)),
                  pl.BlockSpec((None, ar, SEL_TQ), lambda b, i: (b, 0, i)),
                  pl.BlockSpec((None, G_ROWS, SEL_TQ), lambda b, i: (b, 0, i))],
        out_specs=pl.BlockSpec((None, ar, SEL_TQ), lambda b, i: (b, 0, i)),
        out_shape=jax.ShapeDtypeStruct((B, ar, S), BF16),
        scratch_shapes=[pltpu.VMEM((LANES, nq), BF16), pltpu.VMEM((n_sel, nq), BF16),
                        pltpu.VMEM((SEL_GROUP * LANES, nq), BF16)],
        compiler_params=_cp(("parallel", "parallel")),
        name="nsa_sel",
    )(qT, mm, nk, v4, causal4, ocmp, owin, gates)


def _banded_kernel(*refs, hkv, grp, nprev, zero_bias, has_sink, want_lse):
    q_ref, k_ref, v_ref, bias_ref = refs[:4]
    pos = 4
    sink_ref = None
    if has_sink:
        sink_ref = refs[pos]
        pos += 1
    o_ref = refs[pos]
    lse_ref = refs[pos + 1] if want_lse else None
    j = pl.program_id(1)

    def scores(s, g, first):
        lanes = slice(s * LANES, (s + 1) * LANES)
        qg = jnp.concatenate([q_ref[HEAD_DIM * (g * grp + u):HEAD_DIM * (g * grp + u + 1), lanes]
                              for u in range(grp)], axis=1)
        parts = []
        for ci in range(nprev + 1):
            if first:
                kc = s - nprev + ci
                if kc < 0:
                    continue
                row = kc * LANES
            else:
                kc = j * BAND_NSUB + s - nprev + ci
                row = pl.multiple_of(kc * LANES, LANES)
            sc = jnp.dot(k_ref[pl.ds(row, LANES), HEAD_DIM * g:HEAD_DIM * (g + 1)], qg,
                         preferred_element_type=F32)
            if not zero_bias[ci]:
                sc = sc + bias_ref[ci]
            parts.append((kc, sc))
        return parts

    def finish(s, g, parts):
        lanes = slice(s * LANES, (s + 1) * LANES)
        m = None
        for _, sc in parts:
            mc = jnp.max(sc, axis=0, keepdims=True)
            m = mc if m is None else jnp.maximum(m, mc)
        if has_sink:
            sk = sink_ref[g, 0:1, :]
            m = jnp.maximum(m, sk)
        acc = None
        for kcc, sc in parts:
            t = jnp.dot(v_ref[kcc, VROWS * g:VROWS * (g + 1), :], jnp.exp(sc - m).astype(BF16),
                        preferred_element_type=F32)
            acc = t if acc is None else acc + t
        l = acc[HEAD_DIM:HEAD_DIM + 1]
        if has_sink:
            l = l + jnp.exp(sk - m)
        o = acc[0:HEAD_DIM] * (1.0 / l)
        for u in range(grp):
            hq = g * grp + u
            o_ref[HEAD_DIM * hq:HEAD_DIM * (hq + 1), lanes] = o[:, u * LANES:(u + 1) * LANES].astype(o_ref.dtype)
        if want_lse:
            lse_ref[8 * g:8 * (g + 1), lanes] = jnp.broadcast_to(m + jnp.log(l), (8, LANES))

    def run(first):
        pending = None
        for s in range(BAND_NSUB):
            for g in range(hkv):
                parts = scores(s, g, first)
                if pending is not None:
                    finish(*pending)
                pending = (s, g, parts)
        finish(*pending)

    pl.when(j == 0)(lambda: run(True))
    pl.when(j > 0)(lambda: run(False))


def _banded(qT, q_blk, nk, k_blk, v4, v_blk, bias_and_zero, *, hkv, grp, nprev, out_dtype, sinks=None,
            want_lse=False, name):
    bias, zero_bias = bias_and_zero
    assert nprev <= BAND_NSUB
    B, _, S = qT.shape
    nt = S // BAND_TILE
    qrows = hkv * grp * HEAD_DIM
    in_specs = [pl.BlockSpec((None, qrows, BAND_TILE), lambda b, j: (b, q_blk, j)),
                pl.BlockSpec((None, S, LANES), lambda b, j: (b, 0, k_blk)),
                pl.BlockSpec((None, S // LANES, hkv * VROWS, LANES), lambda b, j: (b, 0, v_blk, 0)),
                pl.BlockSpec(bias.shape, lambda b, j: (0, 0, 0))]
    args = [qT, nk, v4, bias]
    if sinks is not None:
        in_specs.append(pl.BlockSpec(sinks.shape, lambda b, j: (0, 0, 0)))
        args.append(sinks)
    out_specs = [pl.BlockSpec((None, qrows, BAND_TILE), lambda b, j: (b, 0, j))]
    out_shape = [jax.ShapeDtypeStruct((B, qrows, S), out_dtype)]
    if want_lse:
        out_specs.append(pl.BlockSpec((None, 8 * hkv, BAND_TILE), lambda b, j: (b, 0, j)))
        out_shape.append(jax.ShapeDtypeStruct((B, 8 * hkv, S), F32))
    return pl.pallas_call(
        functools.partial(_banded_kernel, hkv=hkv, grp=grp, nprev=nprev, zero_bias=zero_bias,
                          has_sink=sinks is not None, want_lse=want_lse),
        grid=(B, nt),
        in_specs=in_specs,
        out_specs=out_specs,
        out_shape=out_shape,
        compiler_params=_cp(("parallel", "parallel")),
        name=name,
    )(*args)


def _folded_kernel(q_ref, k_ref, v_ref, bias_ref, o_ref, lse_ref):
    j = pl.program_id(2)
    nsub = q_ref.shape[0] // LANES
    ones = jnp.ones((VROWS - HEAD_DIM, LANES), BF16)
    row = lax.broadcasted_iota(jnp.int32, (LANES, LANES), 0)

    def scores(s):
        qt = q_ref[s * LANES:(s + 1) * LANES, :].astype(F32).T.astype(BF16)
        parts = []
        for ci in range(2):
            kc = j * nsub + s - 1 + ci
            kcc = jnp.maximum(kc, 0)
            rows = pl.ds(pl.multiple_of(kcc * LANES, LANES), LANES)
            bias = jnp.where(kc >= 0, bias_ref[ci], NEG)
            vt = v_ref[rows, :].astype(F32).T.astype(BF16)
            for g in range(2):
                hs = slice(HEAD_DIM * g, HEAD_DIM * (g + 1))
                sc = jnp.dot(k_ref[rows, hs], qt[hs], preferred_element_type=F32) + bias
                parts.append((g, jnp.concatenate([vt[hs], ones], axis=0), sc))
        return parts

    def finish(s, parts):
        outs, lses = [], []
        for g in range(2):
            mine = [(v, sc) for gg, v, sc in parts if gg == g]
            m = functools.reduce(jnp.maximum, [jnp.max(sc, axis=0, keepdims=True) for _, sc in mine])
            acc = None
            for v, sc in mine:
                t = jnp.dot(v, jnp.exp(sc - m).astype(BF16), preferred_element_type=F32)
                acc = t if acc is None else acc + t
            l = acc[HEAD_DIM:HEAD_DIM + 1]
            outs.append(acc[0:HEAD_DIM] * (1.0 / l))
            lses.append(m + jnp.log(l))
        rows = slice(s * LANES, (s + 1) * LANES)
        o_ref[rows, :] = jnp.concatenate(outs, axis=0).T
        lse_ref[rows, :] = jnp.where(row == 0, lses[0], jnp.where(row == 1, lses[1], 0.0)).T

    pending = None
    for s in range(nsub):
        parts = scores(s)
        if pending is not None:
            finish(*pending)
        pending = (s, parts)
    finish(*pending)


def _folded_dilated(fq, fk, fv, bias, dil):
    B, n, _ = fq.shape
    tq = min(n, BAND_TILE)
    return pl.pallas_call(
        _folded_kernel,
        grid=(B, dil, n // tq),
        in_specs=[pl.BlockSpec((None, tq, LANES), lambda b, r, j: (b, j, r)),
                  pl.BlockSpec((None, n, LANES), lambda b, r, j: (b, 0, r)),
                  pl.BlockSpec((None, n, LANES), lambda b, r, j: (b, 0, r)),
                  pl.BlockSpec(bias.shape, lambda b, r, j: (0, 0, 0))],
        out_specs=[pl.BlockSpec((None, tq, LANES), lambda b, r, j: (b, j, r)),
                   pl.BlockSpec((None, tq, LANES), lambda b, r, j: (b, j, r))],
        out_shape=[jax.ShapeDtypeStruct((B, n, dil * LANES), F32),
                   jax.ShapeDtypeStruct((B, n, dil * LANES), F32)],
        compiler_params=_cp(("parallel", "parallel", "parallel")),
        name="dil%d_folded" % dil,
    )(fq, fk, fv, bias)


def _post_norm_residual(y, x_ref, gw_ref, o_ref, token_major=False):
    ms = jnp.mean(y * y, axis=0, keepdims=True)
    yn = y * lax.rsqrt(ms + RMS_EPS)
    gw = gw_ref[...]
    for c in range(NSUB):
        sl = slice(c * LANES, (c + 1) * LANES)
        out = x_ref[:, sl] + gw * yn[:, sl]
        if token_major:
            o_ref[sl, :] = out.T
        else:
            o_ref[:, sl] = out


def _mix_ffn_kernel(a_ref, b0_ref, b1_ref, b2_ref, l0_ref, l1_ref, l2_ref, c_ref, wo_ref, x_ref, gw1_ref,
                    a2_ref, sh2_ref, wg_ref, wu_ref, wd_ref, cv_ref, gw2_ref, o_ref,
                    mix_scr, x1_scr, h_scr, carry_scr, act_scr, unf_scr, *, fchunk, token_major_out):
    s = pl.program_id(1)
    cur = s % 2
    prv = 1 - cur
    d_ff = wg_ref.shape[0]

    @pl.when(s == 0)
    def _():
        carry_scr[...] = jnp.zeros(carry_scr.shape, F32)
        h_scr[1] = jnp.zeros(h_scr.shape[1:], BF16)
        x1_scr[1] = jnp.zeros(x1_scr.shape[1:], F32)

    h = h_scr[prv]
    lane = lax.broadcasted_iota(jnp.int32, (fchunk, LANES), 1)
    for c in range(d_ff // fchunk):
        rs = slice(c * fchunk, (c + 1) * fchunk)
        g = jnp.dot(wg_ref[rs, :], h, preferred_element_type=F32)
        prev = carry_scr[rs, :]
        carry_scr[rs, :] = g[:, TILE - LANES:TILE]
        g1 = pltpu.roll(g, 1, 1)
        g2 = pltpu.roll(g, 2, 1)
        fix1 = jnp.where(lane < 1, pltpu.roll(prev, 1, 1), g1[:, 0:LANES])
        fix2 = jnp.where(lane < 2, pltpu.roll(prev, 2, 1), g2[:, 0:LANES])
        g1 = jnp.concatenate([fix1, g1[:, LANES:]], axis=1)
        g2 = jnp.concatenate([fix2, g2[:, LANES:]], axis=1)
        w0 = jnp.concatenate([cv_ref[0, rs, :]] * NSUB, axis=1)
        w1 = jnp.concatenate([cv_ref[1, rs, :]] * NSUB, axis=1)
        w2 = jnp.concatenate([cv_ref[2, rs, :]] * NSUB, axis=1)
        cb = jnp.concatenate([cv_ref[3, rs, :]] * NSUB, axis=1)
        acc = g2 * w0 + g1 * w1 + g * w2 + cb
        up = jnp.dot(wu_ref[rs, :], h, preferred_element_type=F32)
        act_scr[rs, :] = (jax.nn.gelu(acc, approximate=True) * up).astype(BF16)

    def unfold(src_ref, fold):
        for res in range(fold):
            unf_scr[pl.ds(res, TILE // fold, stride=fold), :] = src_ref[:, res * LANES:(res + 1) * LANES]
        return unf_scr[...].T

    d1, d2 = DIL_PATTERNS[1][1], DIL_PATTERNS[2][1]
    b_vals = (b0_ref[...], unfold(b1_ref, d1), unfold(b2_ref, d2))
    l1t, l2t = unfold(l1_ref, d1), unfold(l2_ref, d2)
    for hh in range(2):
        ls = [l0_ref[8 * hh:8 * hh + 1, :], l1t[hh:hh + 1, :], l2t[hh:hh + 1, :]]
        mx = jnp.maximum(jnp.maximum(ls[0], ls[1]), ls[2])
        es = [jnp.exp(v - mx) for v in ls]
        inv = 1.0 / (es[0] + es[1] + es[2])
        for g in range(3):
            rs = slice(HEAD_DIM * hh, HEAD_DIM * (hh + 1))
            mix_scr[128 * g + HEAD_DIM * hh:128 * g + HEAD_DIM * (hh + 1), :] = (
                b_vals[g][rs, :] * (es[g] * inv)).astype(BF16)
    y1 = jnp.dot(wo_ref[:, 0:256], a_ref[...], preferred_element_type=F32)
    y1 = y1 + jnp.dot(wo_ref[:, 256:640], mix_scr[...], preferred_element_type=F32)
    y1 = y1 + jnp.dot(wo_ref[:, 640:1024], c_ref[...], preferred_element_type=F32)

    y2 = jnp.dot(wd_ref[...], act_scr[...], preferred_element_type=F32)

    _post_norm_residual(y1, x_ref, gw1_ref, x1_scr.at[cur])
    _norm_mod_to_scratch(x1_scr.at[cur], a2_ref, sh2_ref, h_scr.at[cur])
    _post_norm_residual(y2, x1_scr.at[prv], gw2_ref, o_ref, token_major_out)


def _mix_ffn(aT, bs, lses, cT, w_outT, xT, gw1, a2, sh2, wgT, wuT, wdT, cv, gw2, layer, token_major_out):
    B, D, S = xT.shape
    nt = S // TILE
    d_ff = wgT.shape[1]
    tile = lambda rows: pl.BlockSpec((None, rows, TILE), lambda b, s: (b, 0, jnp.minimum(s, nt - 1)))
    tok = lambda d: pl.BlockSpec((None, TILE // d, d * LANES), lambda b, s: (b, jnp.minimum(s, nt - 1), 0))
    t1, t2 = tok(DIL_PATTERNS[1][1]), tok(DIL_PATTERNS[2][1])
    vec = pl.BlockSpec((None, D, LANES), lambda b, s: (b, 0, 0))
    return pl.pallas_call(
        functools.partial(_mix_ffn_kernel, fchunk=256, token_major_out=token_major_out),
        grid=(B, nt + 1),
        in_specs=[tile(256), tile(128), t1, t2, tile(16), t1, t2, tile(384),
                  _const_spec((None, D, D), lambda b, s: (layer, 0, 0)),
                  tile(D), vec, vec, vec,
                  _const_spec((None, d_ff, D), lambda b, s: (layer, 0, 0)),
                  _const_spec((None, d_ff, D), lambda b, s: (layer, 0, 0)),
                  _const_spec((None, D, d_ff), lambda b, s: (layer, 0, 0)),
                  _const_spec((None, 4, d_ff, LANES), lambda b, s: (layer, 0, 0, 0)),
                  vec],
        out_specs=(pl.BlockSpec((None, TILE, D), lambda b, s: (b, jnp.maximum(s - 1, 0), 0)) if token_major_out
                   else pl.BlockSpec((None, D, TILE), lambda b, s: (b, 0, jnp.maximum(s - 1, 0)))),
        out_shape=jax.ShapeDtypeStruct((B, S, D) if token_major_out else (B, D, S), F32),
        scratch_shapes=[pltpu.VMEM((384, TILE), BF16), pltpu.VMEM((2, D, TILE), F32),
                        pltpu.VMEM((2, D, TILE), BF16), pltpu.VMEM((d_ff, LANES), F32),
                        pltpu.VMEM((d_ff, TILE), BF16), pltpu.VMEM((TILE, LANES), F32)],
        compiler_params=_cp(("arbitrary", "arbitrary")),
        name="mix_ffn",
    )(aT, bs[0], bs[1], bs[2], lses[0], lses[1], lses[2], cT, w_outT, xT, gw1,
      a2, sh2, wgT, wuT, wdT, cv, gw2)


def _band_bias(old_edge, reps, nprev):
    kk = np.arange(LANES)[:, None]
    qq = np.arange(LANES)[None, :]
    tabs = [kk - qq >= old_edge] + [np.ones((LANES, LANES), bool)] * (nprev - 1) + [kk <= qq]
    zero = tuple(bool(t.all()) for t in tabs)
    out = np.stack([np.where(t, 0.0, NEG) for t in tabs]).astype(np.float32)
    return jnp.asarray(np.tile(out, (1, 1, reps))), zero


def _cmp_bias():
    nn = np.arange(LANES)[:, None]
    qq = np.arange(LANES)[None, :]
    tabs = [np.where(CMP_STRIDE * nn + CMP_BLOCK - 1 <= LANES * r + qq, 0.0, NEG) for r in range(16)]
    return jnp.asarray(np.stack(tabs).astype(np.float32))


def _overlap_rows():
    jj = np.arange(40)[:, None]
    nn = np.arange(LANES)[None, :]
    return jnp.asarray(((nn >= 4 * jj - 1) & (nn <= 4 * jj + 3)).astype(np.float32), dtype=BF16)


_IN_COL_ORDER = ((1804, 2188), (652, 1036), (0, 256),
                 (256, 320), (320, 384), (384, 448), (512, 576), (1036, 1420), (2188, 2316),
                 (448, 512), (576, 640), (1420, 1804), (2316, 2444),
                 (640, 652))


def _prep_w_in(w_in):
    wt = jnp.swapaxes(jnp.concatenate([w_in[:, :, a:b] for a, b in _IN_COL_ORDER], axis=2), 1, 2)
    scale = np.ones((wt.shape[1], 1), np.float32)
    scale[:Q_ROWS] = HEAD_DIM ** -0.5
    wt = wt * scale
    wt = jnp.pad(wt, ((0, 0), (0, W_ROWS - wt.shape[1]), (0, 0)))
    return wt.astype(BF16)


def _prep_compress(w_ck, w_cv, pe_k, pe_v):
    L = w_ck.shape[0]
    half = CMP_BLOCK // 2

    def big(lo):
        wk = w_ck[:, lo:lo + half]
        wv = w_cv[:, lo:lo + half]
        z = jnp.zeros_like(wk)
        top = jnp.concatenate([wk, z], axis=-1)
        bot = jnp.concatenate([z, wv], axis=-1)
        return jnp.concatenate([top, bot], axis=2).reshape(L, half * LANES, LANES)

    wbig = jnp.concatenate([big(0), big(half)], axis=-1).astype(BF16)
    pe = jnp.concatenate([pe_k, pe_v], axis=-1)
    pe2 = jnp.zeros((L, 16, half * LANES), F32)
    pe2 = pe2.at[:, 0].set(pe[:, :half].reshape(L, -1)).at[:, 8].set(pe[:, half:].reshape(L, -1))
    return wbig, pe2.astype(BF16)


def _lane_bcast(v):
    return jnp.broadcast_to(v[..., None], v.shape + (LANES,))


def kernel(x, c, positions, w_in, w_out, w_ada, b_ada, norm_w, cmp_w_k, cmp_w_v, cmp_pe_k, cmp_pe_v,
           sinks, w_gate, w_up, conv_w, conv_b, w_down):
    B, S, D = x.shape
    depth = w_in.shape[0]
    assert S % 2048 == 0 and D == 1024 and w_in.shape[2] == 2444

    inv = ROPE_THETA ** (-jnp.arange(0, HEAD_DIM, 2, dtype=F32) / HEAD_DIM)
    ang = positions.astype(F32)[:, None, :] * inv[None, :, None]
    cosT, sinT = jnp.cos(ang), jnp.sin(ang)

    c8 = jnp.pad(c, ((0, 8 - B), (0, 0)))
    ada = _adaln(c8, w_ada, b_ada)[:, :B]
    sh1, sc1, g1, sh2, sc2, g2 = [ada[:, :, k * D:(k + 1) * D] for k in range(6)]
    nw = norm_w[:, :, None, :]
    a1 = _lane_bcast(nw[:, 0] * (1 + sc1))
    gw1 = _lane_bcast(g1 * nw[:, 1])
    a2 = _lane_bcast(nw[:, 2] * (1 + sc2))
    gw2 = _lane_bcast(g2 * nw[:, 3])
    sh1b, sh2b = _lane_bcast(sh1), _lane_bcast(sh2)

    w_inT = _prep_w_in(w_in)
    w_outT = jnp.swapaxes(w_out, 1, 2).astype(BF16)
    wgT = jnp.swapaxes(w_gate, 1, 2).astype(BF16)
    wuT = jnp.swapaxes(w_up, 1, 2).astype(BF16)
    wdT = jnp.swapaxes(w_down, 1, 2).astype(BF16)
    cv = _lane_bcast(jnp.concatenate([conv_w, conv_b[:, None, :]], axis=1))
    wbig, pe2 = _prep_compress(cmp_w_k, cmp_w_v, cmp_pe_k, cmp_pe_v)
    sink_tab = jnp.broadcast_to(
        jnp.repeat(sinks.reshape(depth, SWA_KV_HEADS, SWA_HEADS // SWA_KV_HEADS), LANES, axis=-1)[:, :, None, :],
        (depth, SWA_KV_HEADS, 8, LANES * (SWA_HEADS // SWA_KV_HEADS)))

    ov = _overlap_rows()
    cmp_bias = _cmp_bias()
    kk = np.arange(SEL_TQ)[:, None]
    causal4 = jnp.asarray(np.tile(np.where(kk <= kk.T, 0.0, NEG).astype(np.float32), (1, NSA_HEADS)))
    bias_win = _band_bias(1, NSA_HEADS, NSA_WINDOW // LANES)
    bias_swa = _band_bias(1, SWA_HEADS // SWA_KV_HEADS, SWA_WINDOW // LANES)
    bias_dil = _band_bias(0, 1, 1)
    assert all(w // d == LANES for w, d in DIL_PATTERNS)

    xT = x
    for l in range(depth):
        first, last_layer = l == 0, l == depth - 1
        outs = _inproj(xT, a1[l], sh1b[l], w_inT, l, cosT, sinT, token_major=first)
        qT, kcvc, nk, v4, gates = outs[:5]
        folded = {g: outs[5 + 3 * k:8 + 3 * k] for k, (g, _) in enumerate(FOLDED)}
        if first:
            xT = outs[5 + 3 * len(FOLDED)]
        kc4, vc4 = _compress(kcvc, wbig, pe2, l)
        ocmp, mm = _cmp_topk(qT, kc4, vc4, ov, cmp_bias)
        owin, = _banded(qT, A_Q_BLK, nk, 1, v4, 1, bias_win, hkv=1, grp=NSA_HEADS, nprev=NSA_WINDOW // LANES,
                        out_dtype=F32, name="nsa_win")
        aT = _sel_attend(qT, mm, nk, v4, causal4, ocmp, owin, gates)
        o, lse = _banded(qT, 3, nk, 2, v4, 1, bias_dil, hkv=2, grp=1, nprev=1, out_dtype=F32, want_lse=True,
                         name="dil1")
        bs, lses = [o], [lse]
        for g, _ in FOLDED:
            o, lse = _folded_dilated(*folded[g], bias_dil[0], DIL_PATTERNS[g][1])
            bs.append(o)
            lses.append(lse)
        cT, = _banded(qT, 0, nk, 5, v4, 4, bias_swa, hkv=SWA_KV_HEADS, grp=SWA_HEADS // SWA_KV_HEADS,
                      nprev=SWA_WINDOW // LANES, out_dtype=BF16,
                      sinks=sink_tab[l], name="swa")
        xT = _mix_ffn(aT, bs, lses, cT, w_outT, xT, gw1[l], a2[l], sh2b[l], wgT, wuT, wdT, cv, gw2[l], l,
                      token_major_out=last_layer)
    return xT
```
